```python
import math
import jax, jax.numpy as jnp
from jax import lax
import numpy as np

D_MODEL = 1024
BATCH = 8
SEQ = 2048
DEPTH = 2

GRID_W = 64
CTX_LEN = 256
F32 = jnp.float32
RMS_EPS = 1e-6
HEAD_DIM = 64
ATTN_HEADS = 8
ATTN_KV_HEADS = 2
Q_BLOCK = 128
ROPE_THETA = 10000.0
SSD_HEADS = 8
SSD_HEAD_DIM = 64
SSD_GROUPS = 2
SSD_STATE = 64
SSD_CONV = 3
SSD_CHUNK = 128
S5_GROUPS = 24
S5_GROUP_DIM = 16
S5_STATE = 64
SC_WIDTH = 384
SC_CONV = 3
N_EXPERTS = 16
EXPERT_FF = 1024
EC_CAPACITY = 2
N_BRANCHES = 4

ATTN_Q_DIM = ATTN_HEADS * HEAD_DIM
ATTN_KV_DIM = ATTN_KV_HEADS * HEAD_DIM
SSD_INNER = SSD_HEADS * SSD_HEAD_DIM
SSD_BC_DIM = SSD_GROUPS * SSD_STATE
SSD_CONV_DIM = SSD_INNER + 2 * SSD_BC_DIM
S5_WIDTH = S5_GROUPS * S5_GROUP_DIM
IN_SPLITS = (ATTN_Q_DIM, ATTN_KV_DIM, ATTN_KV_DIM, SSD_INNER, SSD_CONV_DIM, SSD_HEADS,
             S5_WIDTH, SC_WIDTH, SC_WIDTH, SC_WIDTH, N_BRANCHES * D_MODEL)
N_IN = sum(IN_SPLITS)

kernel_name = "hybrid_parallel_dit_block"


def rmsnorm(x, g):
    xf = x.astype(F32)
    y = xf * lax.rsqrt(jnp.mean(xf * xf, axis=-1, keepdims=True) + RMS_EPS)
    return (y * g.astype(F32)).astype(x.dtype)


def modulate(h, shift, scale):
    return h * (1.0 + scale) + shift


def split_cols(z, sizes):
    return jnp.split(z, np.cumsum(sizes)[:-1].tolist(), axis=-1)


def dwconv_centred(x, w):
    k = w.shape[0]
    pad = k // 2
    T = x.shape[1]
    xp = jnp.pad(x, ((0, 0), (pad, pad), (0, 0)))
    return sum(w[i] * xp[:, i:i + T] for i in range(k))


def axial_angles(T):
    rows = T // GRID_W
    row = jnp.repeat(jnp.arange(rows, dtype=F32), GRID_W)
    col = jnp.tile(jnp.arange(GRID_W, dtype=F32), rows)
    half = HEAD_DIM // 2
    inv = ROPE_THETA ** (-jnp.arange(0, half, 2, dtype=F32) / half)
    return row[:, None] * inv, col[:, None] * inv


def rope_1d(x, ang):
    x1, x2 = jnp.split(x, 2, axis=-1)
    cos = jnp.cos(ang)[None, :, None, :].astype(x.dtype)
    sin = jnp.sin(ang)[None, :, None, :].astype(x.dtype)
    return jnp.concatenate([x1 * cos - x2 * sin, x2 * cos + x1 * sin], axis=-1)


def axial_rope(x, row_ang, col_ang):
    half = HEAD_DIM // 2
    return jnp.concatenate([rope_1d(x[..., :half], row_ang), rope_1d(x[..., half:], col_ang)], axis=-1)


def gqa_attend(q, k, v):
    b, T, H, d = q.shape
    G = k.shape[2]
    qg = q.reshape(b, T, G, H // G, d)
    s = jnp.einsum('btgrd,bsgd->bgrts', qg, k).astype(F32) * (d ** -0.5)
    p = jax.nn.softmax(s, axis=-1).astype(v.dtype)
    o = jnp.einsum('bgrts,bsgd->btgrd', p, v)
    return o.reshape(b, T, H * d)


def attention_mixer(q_l, k_l, v_l, q_c, k_c, v_c, q_norm_g, k_norm_g, ctx_out):
    b, T = q_l.shape[:2]

    def heads(t, h):
        return t.reshape(t.shape[0], t.shape[1], h, HEAD_DIM)

    row_ang, col_ang = axial_angles(T)
    ql = axial_rope(rmsnorm(heads(q_l, ATTN_HEADS), q_norm_g), row_ang, col_ang)
    kl = axial_rope(rmsnorm(heads(k_l, ATTN_KV_HEADS), k_norm_g), row_ang, col_ang)
    vl = heads(v_l, ATTN_KV_HEADS)
    kc = rmsnorm(heads(k_c, ATTN_KV_HEADS), k_norm_g)
    vc = heads(v_c, ATTN_KV_HEADS)
    k_all = jnp.concatenate([kc, kl], axis=1)
    v_all = jnp.concatenate([vc, vl], axis=1)
    nb = T // Q_BLOCK
    qb = jnp.moveaxis(ql.reshape(b, nb, Q_BLOCK, ATTN_HEADS, HEAD_DIM), 1, 0)
    ob = lax.map(lambda qq: gqa_attend(qq, k_all, v_all), qb)
    y_l = jnp.moveaxis(ob, 0, 1).reshape(b, T, ATTN_Q_DIM)
    if not ctx_out:
        return y_l, None
    y_c = gqa_attend(rmsnorm(heads(q_c, ATTN_HEADS), q_norm_g), kc, vc)
    return y_l, y_c


def segsum(a):
    T = a.shape[-1]
    x = jnp.broadcast_to(a[..., :, None], a.shape + (T,))
    x = jnp.where(jnp.tril(jnp.ones((T, T), bool), -1), x, 0.0)
    cs = jnp.cumsum(x, axis=-2)
    return jnp.where(jnp.tril(jnp.ones((T, T), bool)), cs, -jnp.inf)


def ssd_scan(x, dt, a, bm, cm, h0):
    b, T, H, P = x.shape
    N = bm.shape[-1]
    l = SSD_CHUNK
    nc = T // l
    xd = (x.astype(F32) * dt[..., None]).reshape(b, nc, l, H, P)
    bc = bm.astype(F32).reshape(b, nc, l, H, N)
    cc = cm.astype(F32).reshape(b, nc, l, H, N)
    adt = jnp.moveaxis((dt * a).reshape(b, nc, l, H), -1, 1)
    a_cs = jnp.cumsum(adt, axis=-1)
    scores = jnp.einsum('bclhn,bcshn->bhcls', cc, bc) * jnp.exp(segsum(adt))
    y_diag = jnp.einsum('bhcls,bcshp->bclhp', scores, xd)
    decay_to_end = jnp.exp(a_cs[..., -1:] - a_cs)
    chunk_states = jnp.einsum('bclhn,bhcl,bclhp->bchpn', bc, decay_to_end, xd)
    states = jnp.concatenate([h0[:, None].astype(F32), chunk_states], axis=1)
    decay_chunk = jnp.exp(segsum(jnp.pad(a_cs[..., -1], ((0, 0), (0, 0), (1, 0)))))
    states = jnp.einsum('bhzc,bchpn->bzhpn', decay_chunk, states)
    y_off = jnp.einsum('bclhn,bchpn,bhcl->bclhp', cc, states[:, :-1], jnp.exp(a_cs))
    return (y_diag + y_off).reshape(b, T, H, P), states[:, -1]


def ssd_split(xbc):
    xs, bs, cs = split_cols(xbc, (SSD_INNER, SSD_BC_DIM, SSD_BC_DIM))
    b, T = xs.shape[:2]
    rep = SSD_HEADS // SSD_GROUPS
    xs = xs.reshape(b, T, SSD_HEADS, SSD_HEAD_DIM)
    bs = jnp.repeat(bs.reshape(b, T, SSD_GROUPS, SSD_STATE), rep, axis=2)
    cs = jnp.repeat(cs.reshape(b, T, SSD_GROUPS, SSD_STATE), rep, axis=2)
    return xs, bs, cs


def ssd_direction(x, dt, bm, cm, a, h0, rev):
    if rev:
        x, dt, bm, cm = (jnp.flip(t, axis=1) for t in (x, dt, bm, cm))
    y, h = ssd_scan(x, dt, a, bm, cm, h0)
    return (jnp.flip(y, axis=1) if rev else y), h


def ssd_mixer(z_l, xbc_l, dt_l, z_c, xbc_c, dt_c, conv_w, conv_b, dt_bias, a_log, d_skip, norm_g, ctx_out):
    xl, bl, cl = ssd_split(jax.nn.silu(dwconv_centred(xbc_l, conv_w) + conv_b))
    xc, bc, cc = ssd_split(jax.nn.silu(dwconv_centred(xbc_c, conv_w) + conv_b))
    h0 = jnp.zeros((xl.shape[0], SSD_HEADS, SSD_HEAD_DIM, SSD_STATE), F32)
    y_l = d_skip.astype(F32)[:, None] * xl.astype(F32)
    y_c = d_skip.astype(F32)[:, None] * xc.astype(F32)
    for d in range(2):
        a = -jnp.exp(a_log[d].astype(F32))
        dtl = jax.nn.softplus((dt_l + dt_bias[d]).astype(F32))
        dtc = jax.nn.softplus((dt_c + dt_bias[d]).astype(F32))
        yc_d, hc = ssd_direction(xc, dtc, bc, cc, a, h0, d == 1)
        yl_d, _ = ssd_direction(xl, dtl, bl, cl, a, hc, d == 1)
        y_l = y_l + yl_d
        y_c = y_c + yc_d

    def finish(y, z):
        y = y.reshape(z.shape).astype(z.dtype) * jax.nn.silu(z)
        return rmsnorm(y, norm_g)

    return finish(y_l, z_l), (finish(y_c, z_c) if ctx_out else None)


def s5_discretise(lam_re, lam_im, log_step, b_re, b_im):
    step = jnp.exp(log_step.astype(F32))[:, None]
    lr, li = lam_re.astype(F32), lam_im.astype(F32)
    mag = jnp.exp(lr * step)
    ab_re, ab_im = mag * jnp.cos(li * step), mag * jnp.sin(li * step)
    den = lr * lr + li * li
    nr, ni = ab_re - 1.0, ab_im
    coef_re = (nr * lr + ni * li) / den
    coef_im = (ni * lr - nr * li) / den
    br, bi = b_re.astype(F32), b_im.astype(F32)
    bb_re = coef_re[..., None] * br - coef_im[..., None] * bi
    bb_im = coef_re[..., None] * bi + coef_im[..., None] * br
    return ab_re, ab_im, bb_re, bb_im


def complex_linear_scan(ab_re, ab_im, bu_re, bu_im, h0_re, h0_im):
    a_re = jnp.broadcast_to(ab_re, bu_re.shape)
    a_im = jnp.broadcast_to(ab_im, bu_re.shape)

    def combine(e1, e2):
        a1r, a1i, b1r, b1i = e1
        a2r, a2i, b2r, b2i = e2
        return (a2r * a1r - a2i * a1i, a2r * a1i + a2i * a1r,
                a2r * b1r - a2i * b1i + b2r, a2r * b1i + a2i * b1r + b2i)

    pr, pi, hr, hi = lax.associative_scan(combine, (a_re, a_im, bu_re, bu_im), axis=1)
    h0r, h0i = h0_re[:, None], h0_im[:, None]
    return hr + pr * h0r - pi * h0i, hi + pr * h0i + pi * h0r


def s5_drive(u, bb):
    return jnp.einsum('btgi,gpi->btgp', u.astype(F32), bb)


def s5_readout(u, s_re, s_im, c_re, c_im, d_skip, w_glu, b_glu):
    b, T = u.shape[:2]
    y = (jnp.einsum('btgp,gip->btgi', s_re, c_re.astype(F32))
         - jnp.einsum('btgp,gip->btgi', s_im, c_im.astype(F32))
         + d_skip.reshape(S5_GROUPS, S5_GROUP_DIM).astype(F32) * u.astype(F32))
    y = jax.nn.gelu(y.reshape(b, T, S5_WIDTH)).astype(u.dtype)
    return y * jax.nn.sigmoid(y @ w_glu + b_glu)


def s5_mixer(u_l, u_c, lam_re, lam_im, log_step, b_re, b_im, c_re, c_im, d_skip, w_glu, b_glu, ctx_out):
    def groups(u):
        return u.reshape(u.shape[0], u.shape[1], S5_GROUPS, S5_GROUP_DIM)

    ul, uc = groups(u_l), groups(u_c)
    zero = jnp.zeros((ul.shape[0], S5_GROUPS, S5_STATE), F32)
    sl_re = sl_im = sc_re = sc_im = 0.0
    for d in range(2):
        ab_re, ab_im, bb_re, bb_im = s5_discretise(lam_re[d], lam_im[d], log_step[d], b_re, b_im)
        uc_d = jnp.flip(uc, axis=1) if d else uc
        ul_d = jnp.flip(ul, axis=1) if d else ul
        hc_re, hc_im = complex_linear_scan(ab_re, ab_im, s5_drive(uc_d, bb_re), s5_drive(uc_d, bb_im), zero, zero)
        hl_re, hl_im = complex_linear_scan(ab_re, ab_im, s5_drive(ul_d, bb_re), s5_drive(ul_d, bb_im),
                                           hc_re[:, -1], hc_im[:, -1])
        if d:
            hc_re, hc_im, hl_re, hl_im = (jnp.flip(t, axis=1) for t in (hc_re, hc_im, hl_re, hl_im))
        sl_re, sl_im = sl_re + hl_re, sl_im + hl_im
        sc_re, sc_im = sc_re + hc_re, sc_im + hc_im
    y_l = s5_readout(ul, sl_re, sl_im, c_re, c_im, d_skip, w_glu, b_glu)
    y_c = s5_readout(uc, sc_re, sc_im, c_re, c_im, d_skip, w_glu, b_glu) if ctx_out else None
    return y_l, y_c


def shortconv_mixer(bg, cg, hh, conv_w):
    return bg * dwconv_centred(cg * hh, conv_w)


def hybrid_mixer(h_l, h_c, p, ctx_out):
    proj_l = h_l @ p['w_in']
    proj_c = h_c @ p['w_in']
    q_l, k_l, v_l, z_l, xbc_l, dt_l, u_l, sb_l, sg_l, sh_l, g_l = split_cols(proj_l, IN_SPLITS)
    q_c, k_c, v_c, z_c, xbc_c, dt_c, u_c, sb_c, sg_c, sh_c, g_c = split_cols(proj_c, IN_SPLITS)
    ya_l, ya_c = attention_mixer(q_l, k_l, v_l, q_c, k_c, v_c, p['q_norm_g'], p['k_norm_g'], ctx_out)
    yb_l, yb_c = ssd_mixer(z_l, xbc_l, dt_l, z_c, xbc_c, dt_c, p['ssd_conv_w'], p['ssd_conv_b'],
                           p['ssd_dt_bias'], p['ssd_a_log'], p['ssd_d'], p['ssd_norm_g'], ctx_out)
    yc_l, yc_c = s5_mixer(u_l, u_c, p['s5_lambda_re'], p['s5_lambda_im'], p['s5_log_step'],
                          p['s5_b_re'], p['s5_b_im'], p['s5_c_re'], p['s5_c_im'], p['s5_d'],
                          p['s5_w_glu'], p['s5_b_glu'], ctx_out)
    yd_l = shortconv_mixer(sb_l, sg_l, sh_l, p['sc_conv_w'])

    def merge(ya, yb, yc, yd, g):
        gates = jnp.split(jax.nn.sigmoid(g), N_BRANCHES, axis=-1)
        merged = (gates[0] * (ya @ p['w_br_attn']) + gates[1] * (yb @ p['w_br_ssd'])
                  + gates[2] * (yc @ p['w_br_s5']) + gates[3] * (yd @ p['w_br_sc']))
        return merged @ p['w_out']

    y_l = merge(ya_l, yb_l, yc_l, yd_l, g_l)
    if not ctx_out:
        return y_l, None
    yd_c = shortconv_mixer(sb_c, sg_c, sh_c, p['sc_conv_w'])
    return y_l, merge(ya_c, yb_c, yc_c, yd_c, g_c)


def expert_choice_ffn(h, w_router, w_gate, w_up, w_down):
    b, T, _ = h.shape
    cap = EC_CAPACITY * T // N_EXPERTS
    aff = jax.nn.softmax((h @ w_router).astype(F32), axis=-1)
    gval, idx = lax.top_k(jnp.swapaxes(aff, 1, 2), cap)
    bidx = jnp.arange(b)[:, None, None]
    xg = h[bidx, idx]
    hid = jax.nn.silu(jnp.einsum('becd,edf->becf', xg, w_gate)) * jnp.einsum('becd,edf->becf', xg, w_up)
    ye = jnp.einsum('becf,efd->becd', hid, w_down) * gval[..., None].astype(h.dtype)
    return jnp.zeros_like(h).at[bidx, idx].add(ye)


def setup_inputs(seed: int = 0) -> dict:
    key = jax.random.key(seed)
    ks = iter(jax.random.split(key, 48))

    def nrm(shape, scale=1.0):
        return scale * jax.random.normal(next(ks), shape, F32)

    def unif(shape, lo, hi):
        return jax.random.uniform(next(ks), shape, F32, lo, hi)

    L, D = DEPTH, D_MODEL
    dt0 = jnp.exp(unif((L, 2, SSD_HEADS), math.log(1e-3), math.log(1e-1)))
    return {
        'x': nrm((BATCH, SEQ, D)),
        'c': nrm((BATCH, D)),
        'ctx': nrm((BATCH, CTX_LEN, D)),
        'c_ctx': nrm((D,)),
        'w_mod': nrm((L, D, 6 * D), 0.5 * D ** -0.5),
        'b_mod': nrm((L, 6 * D), 0.01),
        'norm_mix_g': 1.0 + nrm((L, D), 0.01),
        'norm_ffn_g': 1.0 + nrm((L, D), 0.01),
        'w_in': nrm((L, D, N_IN), D ** -0.5),
        'q_norm_g': 1.0 + nrm((L, HEAD_DIM), 0.01),
        'k_norm_g': 1.0 + nrm((L, HEAD_DIM), 0.01),
        'ssd_conv_w': nrm((L, SSD_CONV, SSD_CONV_DIM), SSD_CONV ** -0.5),
        'ssd_conv_b': nrm((L, SSD_CONV_DIM), 0.01),
        'ssd_dt_bias': dt0 + jnp.log(-jnp.expm1(-dt0)),
        'ssd_a_log': jnp.log(unif((L, 2, SSD_HEADS), 1.0, 16.0)),
        'ssd_d': 1.0 + nrm((L, SSD_HEADS), 0.01),
        'ssd_norm_g': 1.0 + nrm((L, SSD_INNER), 0.01),
        's5_lambda_re': -0.5 + nrm((L, 2, S5_GROUPS, S5_STATE), 0.01),
        's5_lambda_im': jnp.pi * jnp.arange(S5_STATE, dtype=F32) + nrm((L, 2, S5_GROUPS, S5_STATE), 0.01),
        's5_log_step': unif((L, 2, S5_GROUPS), math.log(1e-3), math.log(1e-1)),
        's5_b_re': nrm((L, S5_GROUPS, S5_STATE, S5_GROUP_DIM), S5_GROUP_DIM ** -0.5),
        's5_b_im': nrm((L, S5_GROUPS, S5_STATE, S5_GROUP_DIM), S5_GROUP_DIM ** -0.5),
        's5_c_re': nrm((L, S5_GROUPS, S5_GROUP_DIM, S5_STATE), S5_STATE ** -0.5),
        's5_c_im': nrm((L, S5_GROUPS, S5_GROUP_DIM, S5_STATE), S5_STATE ** -0.5),
        's5_d': nrm((L, S5_WIDTH)),
        's5_w_glu': nrm((L, S5_WIDTH, S5_WIDTH), S5_WIDTH ** -0.5),
        's5_b_glu': nrm((L, S5_WIDTH), 0.01),
        'sc_conv_w': nrm((L, SC_CONV, SC_WIDTH), SC_CONV ** -0.5),
        'w_br_attn': nrm((L, ATTN_Q_DIM, D), ATTN_Q_DIM ** -0.5),
        'w_br_ssd': nrm((L, SSD_INNER, D), SSD_INNER ** -0.5),
        'w_br_s5': nrm((L, S5_WIDTH, D), S5_WIDTH ** -0.5),
        'w_br_sc': nrm((L, SC_WIDTH, D), SC_WIDTH ** -0.5),
        'w_out': nrm((L, D, D), D ** -0.5),
        'w_router': nrm((L, D, N_EXPERTS), D ** -0.5),
        'w_exp_gate': nrm((L, N_EXPERTS, D, EXPERT_FF), D ** -0.5),
        'w_exp_up': nrm((L, N_EXPERTS, D, EXPERT_FF), D ** -0.5),
        'w_exp_down': nrm((L, N_EXPERTS, EXPERT_FF, D), EXPERT_FF ** -0.5),
        'final_norm_g': 1.0 + nrm((D,), 0.01),
    }


def reference(x, c, ctx, c_ctx, w_mod, b_mod, norm_mix_g, norm_ffn_g, w_in, q_norm_g, k_norm_g,
              ssd_conv_w, ssd_conv_b, ssd_dt_bias, ssd_a_log, ssd_d, ssd_norm_g,
              s5_lambda_re, s5_lambda_im, s5_log_step, s5_b_re, s5_b_im, s5_c_re, s5_c_im,
              s5_d, s5_w_glu, s5_b_glu, sc_conv_w, w_br_attn, w_br_ssd, w_br_s5, w_br_sc, w_out,
              w_router, w_exp_gate, w_exp_up, w_exp_down, final_norm_g):
    silu_c = jax.nn.silu(c)
    silu_cc = jax.nn.silu(c_ctx)
    for i in range(DEPTH):
        ctx_out = i < DEPTH - 1
        p = {
            'w_in': w_in[i], 'q_norm_g': q_norm_g[i], 'k_norm_g': k_norm_g[i],
            'ssd_conv_w': ssd_conv_w[i], 'ssd_conv_b': ssd_conv_b[i], 'ssd_dt_bias': ssd_dt_bias[i],
            'ssd_a_log': ssd_a_log[i], 'ssd_d': ssd_d[i], 'ssd_norm_g': ssd_norm_g[i],
            's5_lambda_re': s5_lambda_re[i], 's5_lambda_im': s5_lambda_im[i], 's5_log_step': s5_log_step[i],
            's5_b_re': s5_b_re[i], 's5_b_im': s5_b_im[i], 's5_c_re': s5_c_re[i], 's5_c_im': s5_c_im[i],
            's5_d': s5_d[i], 's5_w_glu': s5_w_glu[i], 's5_b_glu': s5_b_glu[i], 'sc_conv_w': sc_conv_w[i],
            'w_br_attn': w_br_attn[i], 'w_br_ssd': w_br_ssd[i], 'w_br_s5': w_br_s5[i], 'w_br_sc': w_br_sc[i],
            'w_out': w_out[i],
        }
        mod_l = jnp.split((silu_c @ w_mod[i] + b_mod[i])[:, None, :], 6, axis=-1)
        mod_c = jnp.split(silu_cc @ w_mod[i] + b_mod[i], 6, axis=-1)
        h_l = modulate(rmsnorm(x, norm_mix_g[i]), mod_l[0], mod_l[1])
        h_c = modulate(rmsnorm(ctx, norm_mix_g[i]), mod_c[0], mod_c[1])
        y_l, y_c = hybrid_mixer(h_l, h_c, p, ctx_out)
        x = x + mod_l[2] * y_l
        x = x + mod_l[5] * expert_choice_ffn(modulate(rmsnorm(x, norm_ffn_g[i]), mod_l[3], mod_l[4]),
                                             w_router[i], w_exp_gate[i], w_exp_up[i], w_exp_down[i])
        if ctx_out:
            ctx = ctx + mod_c[2] * y_c
            ctx = ctx + mod_c[5] * expert_choice_ffn(modulate(rmsnorm(ctx, norm_ffn_g[i]), mod_c[3], mod_c[4]),
                                                     w_router[i], w_exp_gate[i], w_exp_up[i], w_exp_down[i])
    return rmsnorm(x, final_norm_g)
```

```python
import functools
import math

import jax
import jax.numpy as jnp
import numpy as np
from jax import lax
from jax.experimental import pallas as pl
from jax.experimental.pallas import tpu as pltpu

F32 = jnp.float32
BF16 = jnp.bfloat16
I32 = jnp.int32
HI = lax.Precision.HIGHEST

RMS_EPS = 1e-6
GRID_W = 64
ROPE_THETA = 10000.0
HEAD_DIM = 64
ATTN_HEADS = 8
ATTN_KV_HEADS = 2
SSD_HEADS = 8
SSD_HEAD_DIM = 64
SSD_GROUPS = 2
SSD_STATE = 64
SSD_CHUNK = 128
S5_GROUPS = 24
S5_GROUP_DIM = 16
S5_STATE = 64
S5_CHUNK = 16
SC_WIDTH = 384
N_EXPERTS = 16
EC_CAPACITY = 2
MAX_ROW_BLOCK = 256

ATTN_Q_DIM = ATTN_HEADS * HEAD_DIM
ATTN_KV_DIM = ATTN_KV_HEADS * HEAD_DIM
SSD_INNER = SSD_HEADS * SSD_HEAD_DIM
SSD_BC_DIM = SSD_GROUPS * SSD_STATE
SSD_CONV_DIM = SSD_INNER + 2 * SSD_BC_DIM
S5_WIDTH = S5_GROUPS * S5_GROUP_DIM

PA_U, PA_SB, PA_SG, PA_SH = 0, 384, 768, 1152
PA_XBC, PA_K, PA_V, PA_Q, PA_Z, PA_W = 1536, 2304, 2432, 2560, 3072, 3584


def _sds(shape, dtype):
    return jax.ShapeDtypeStruct(shape, dtype)


def _cp(sem, vmem_mb=None):
    kw = dict(dimension_semantics=sem)
    if vmem_mb is not None:
        kw["vmem_limit_bytes"] = vmem_mb << 20
    return pltpu.CompilerParams(**kw)


def _silu(x):
    return x * jax.nn.sigmoid(x)


def _mods_body(c_ref, w_ref, b_ref, o_ref):
    s = _silu(c_ref[...])
    o_ref[...] = jnp.dot(s.astype(BF16), w_ref[...].astype(BF16), preferred_element_type=F32) + b_ref[...]


def _mods(cc, w, b):
    R, D = cc.shape
    N = w.shape[1]
    tn = 1536
    return pl.pallas_call(
        _mods_body,
        grid=(N // tn,),
        in_specs=[pl.BlockSpec((R, D), lambda j: (0, 0)),
                  pl.BlockSpec((D, tn), lambda j: (0, j)),
                  pl.BlockSpec((1, tn), lambda j: (0, j))],
        out_specs=pl.BlockSpec((R, tn), lambda j: (0, j)),
        out_shape=_sds((R, N), F32),
        compiler_params=_cp(("parallel",), 40),
        name="mods",
    )(cc, w, b)


def _row_cfg(S, Sc):
    tm = math.gcd(math.gcd(Sc, S - Sc), MAX_ROW_BLOCK)
    return tm, S // tm, Sc // tm


def _mod_spec(D, rc, B, chunk):
    _, nb, nbc = rc
    return pl.BlockSpec((None, 1, D), lambda i: (jnp.where(i % nb < nbc, B, i // nb), 0, chunk))


def _normmod_body(x_ref, g_ref, sh_ref, sc_ref, o_ref):
    x = x_ref[...]
    ms = jnp.mean(x * x, axis=-1, keepdims=True)
    y = x * lax.rsqrt(ms + RMS_EPS) * g_ref[...]
    o_ref[...] = (y * (1.0 + sc_ref[...]) + sh_ref[...]).astype(o_ref.dtype)


def _normmod_res_body(x_ref, r_ref, g_ref, sh_ref, sc_ref, xo_ref, o_ref):
    x = x_ref[...] + r_ref[...]
    xo_ref[...] = x
    ms = jnp.mean(x * x, axis=-1, keepdims=True)
    y = x * lax.rsqrt(ms + RMS_EPS) * g_ref[...]
    o_ref[...] = (y * (1.0 + sc_ref[...]) + sh_ref[...]).astype(o_ref.dtype)


def _normmod(xa, res, g, mods3, B, rc, shift_chunk, scale_chunk):
    N, D = xa.shape
    tm = rc[0]
    row = pl.BlockSpec((tm, D), lambda i: (i, 0))
    tail = [pl.BlockSpec((1, D), lambda i: (0, 0)), _mod_spec(D, rc, B, shift_chunk), _mod_spec(D, rc, B, scale_chunk)]
    if res is None:
        hn = pl.pallas_call(
            _normmod_body, grid=(N // tm,), in_specs=[row] + tail, out_specs=row,
            out_shape=_sds((N, D), BF16), compiler_params=_cp(("parallel",)), name="normmod",
        )(xa, g, mods3, mods3)
        return xa, hn
    return pl.pallas_call(
        _normmod_res_body, grid=(N // tm,), in_specs=[row, row] + tail, out_specs=[row, row],
        out_shape=[_sds((N, D), F32), _sds((N, D), BF16)], compiler_params=_cp(("parallel",)), name="normmod_res",
    )(xa, res, g, mods3, mods3)


def _mm_body(a_ref, w_ref, o_ref):
    o_ref[...] = jnp.dot(a_ref[...], w_ref[...], preferred_element_type=F32).astype(o_ref.dtype)


def _matmul(a, w, out_dtype, tm, tn, name):
    M, K = a.shape
    N = w.shape[1]
    return pl.pallas_call(
        _mm_body,
        grid=(M // tm, N // tn),
        in_specs=[pl.BlockSpec((tm, K), lambda i, j: (i, 0)),
                  pl.BlockSpec((K, tn), lambda i, j: (0, j))],
        out_specs=pl.BlockSpec((tm, tn), lambda i, j: (i, j)),
        out_shape=_sds((M, N), out_dtype),
        compiler_params=_cp(("parallel", "arbitrary"), 40),
        name=name,
    )(a, w)


def _shifted(x, Sc):
    S = x.shape[0]
    t = lax.broadcasted_iota(I32, (S, 1), 0)
    prev = jnp.where((t == 0) | (t == Sc), 0.0, pltpu.roll(x, 1, 0))
    nxt = jnp.where((t == Sc - 1) | (t == S - 1), 0.0, pltpu.roll(x, S - 1, 0))
    return prev, nxt


def _ssdconv_body(x_ref, w_ref, b_ref, o_ref, *, Sc):
    x = x_ref[...].astype(F32)
    prev, nxt = _shifted(x, Sc)
    w = w_ref[...]
    y = w[0:1, :] * prev + w[1:2, :] * x + w[2:3, :] * nxt + b_ref[...]
    o_ref[...] = _silu(y).astype(o_ref.dtype)


def _ssdconv(proj_a, w, b, B, S, Sc):
    C = 384
    j0 = PA_XBC // C
    return pl.pallas_call(
        functools.partial(_ssdconv_body, Sc=Sc),
        grid=(B, SSD_CONV_DIM // C),
        in_specs=[pl.BlockSpec((S, C), lambda b_, j: (b_, j0 + j)),
                  pl.BlockSpec((3, C), lambda b_, j: (0, j)),
                  pl.BlockSpec((1, C), lambda b_, j: (0, j))],
        out_specs=pl.BlockSpec((S, C), lambda b_, j: (b_, j)),
        out_shape=_sds((B * S, SSD_CONV_DIM), BF16),
        compiler_params=_cp(("parallel", "parallel"), 40),
        name="ssdconv",
    )(proj_a, w, b)


def _shortconv_body(sb_ref, sg_ref, sh_ref, w_ref, o_ref, *, Sc):
    x = sg_ref[...].astype(F32) * sh_ref[...].astype(F32)
    prev, nxt = _shifted(x, Sc)
    w = w_ref[...]
    y = w[0:1, :] * prev + w[1:2, :] * x + w[2:3, :] * nxt
    o_ref[...] = (sb_ref[...].astype(F32) * y).astype(o_ref.dtype)


def _shortconv(proj_a, w, B, S, Sc):
    C = SC_WIDTH
    return pl.pallas_call(
        functools.partial(_shortconv_body, Sc=Sc),
        grid=(B,),
        in_specs=[pl.BlockSpec((S, C), lambda b_: (b_, PA_SB // C)),
                  pl.BlockSpec((S, C), lambda b_: (b_, PA_SG // C)),
                  pl.BlockSpec((S, C), lambda b_: (b_, PA_SH // C)),
                  pl.BlockSpec((3, C), lambda b_: (0, 0))],
        out_specs=pl.BlockSpec((S, C), lambda b_: (b_, 0)),
        out_shape=_sds((B * S, C), BF16),
        compiler_params=_cp(("parallel",), 40),
        name="shortconv",
    )(proj_a, proj_a, proj_a, w)


def _norm_rope(x, g, cos, sin, bd, scale):
    W = x.shape[1]
    sq = x * x
    hi = sq.astype(BF16)
    lo = (sq - hi.astype(F32)).astype(BF16)
    ssq = jnp.dot(hi, bd, preferred_element_type=F32) + jnp.dot(lo, bd, preferred_element_type=F32)
    y = x * lax.rsqrt(ssq * (1.0 / HEAD_DIM) + RMS_EPS) * g
    lane = lax.broadcasted_iota(I32, (1, W), 1)
    first = (lane % 32) < 16
    partner = jnp.where(first, pltpu.roll(y, W - 16, 1), pltpu.roll(y, 16, 1))
    return (y * cos + partner * sin) * scale


def _attn_body(q_ref, k_ref, v_ref, cq_ref, sq_ref, ck_ref, sk_ref, gq_ref, gk_ref, bdq_ref, bdk_ref,
               o_ref, kh_ref, *, Sc, S, tq):
    qb = pl.program_id(1)

    @pl.when(qb == 0)
    def _():
        k = k_ref[...].astype(F32)
        kh_ref[...] = _norm_rope(k, gk_ref[...], ck_ref[...], sk_ref[...], bdk_ref[...], 1.0).astype(BF16)

    q = q_ref[...].astype(F32)
    qh = _norm_rope(q, gq_ref[...], cq_ref[...], sq_ref[...], bdq_ref[...], HEAD_DIM ** -0.5).astype(BF16)
    rep = ATTN_HEADS // ATTN_KV_HEADS

    def attend(nk):
        for g in range(ATTN_KV_HEADS):
            kg = kh_ref[0:nk, g * HEAD_DIM:(g + 1) * HEAD_DIM]
            vg = v_ref[0:nk, g * HEAD_DIM:(g + 1) * HEAD_DIM]
            qg = jnp.concatenate(
                [qh[:, (g * rep + r) * HEAD_DIM:(g * rep + r + 1) * HEAD_DIM] for r in range(rep)], axis=0)
            s = lax.dot_general(qg, kg, (((1,), (1,)), ((), ())), preferred_element_type=F32)
            m = jnp.max(s, axis=-1, keepdims=True)
            p = jnp.exp(s - m)
            l = jnp.sum(p, axis=-1, keepdims=True)
            o = jnp.dot(p.astype(BF16), vg, preferred_element_type=F32) / l
            for r in range(rep):
                h = g * rep + r
                o_ref[:, h * HEAD_DIM:(h + 1) * HEAD_DIM] = o[r * tq:(r + 1) * tq].astype(o_ref.dtype)

    @pl.when(qb < Sc // tq)
    def _():
        attend(Sc)

    @pl.when(qb >= Sc // tq)
    def _():
        attend(S)


def _rope_tables(T, Sc):
    rows = T // GRID_W
    row = np.repeat(np.arange(rows, dtype=np.float32), GRID_W)
    col = np.tile(np.arange(GRID_W, dtype=np.float32), rows)
    half = HEAD_DIM // 2
    inv = jnp.asarray(ROPE_THETA, F32) ** (-jnp.arange(0, half, 2, dtype=F32) / half)
    ra = jnp.asarray(row)[:, None] * inv
    ca = jnp.asarray(col)[:, None] * inv
    cos = jnp.concatenate([jnp.cos(ra), jnp.cos(ra), jnp.cos(ca), jnp.cos(ca)], axis=1)
    sin = jnp.concatenate([-jnp.sin(ra), jnp.sin(ra), -jnp.sin(ca), jnp.sin(ca)], axis=1)
    cos = jnp.concatenate([jnp.ones((Sc, HEAD_DIM), F32), cos], axis=0)
    sin = jnp.concatenate([jnp.zeros((Sc, HEAD_DIM), F32), sin], axis=0)
    return cos, sin


def _block_diag_ones(W):
    i = np.arange(W) // HEAD_DIM
    return jnp.asarray((i[:, None] == i[None, :]).astype(np.float32), BF16)


def _attention(proj_a, tabs, q_norm_g, k_norm_g, B, S, Sc):
    cos_q, sin_q, cos_k, sin_k, bdq, bdk = tabs
    tq = 128
    nq = S // tq
    gq = jnp.tile(q_norm_g, ATTN_HEADS)[None, :]
    gk = jnp.tile(k_norm_g, ATTN_KV_HEADS)[None, :]
    const = lambda b_, i: (0, 0)
    return pl.pallas_call(
        functools.partial(_attn_body, Sc=Sc, S=S, tq=tq),
        grid=(B, nq),
        in_specs=[pl.BlockSpec((tq, ATTN_Q_DIM), lambda b_, i: (b_ * nq + i, PA_Q // ATTN_Q_DIM)),
                  pl.BlockSpec((S, ATTN_KV_DIM), lambda b_, i: (b_, PA_K // ATTN_KV_DIM)),
                  pl.BlockSpec((S, ATTN_KV_DIM), lambda b_, i: (b_, PA_V // ATTN_KV_DIM)),
                  pl.BlockSpec((tq, ATTN_Q_DIM), lambda b_, i: (i, 0)),
                  pl.BlockSpec((tq, ATTN_Q_DIM), lambda b_, i: (i, 0)),
                  pl.BlockSpec((S, ATTN_KV_DIM), const),
                  pl.BlockSpec((S, ATTN_KV_DIM), const),
                  pl.BlockSpec((1, ATTN_Q_DIM), const),
                  pl.BlockSpec((1, ATTN_KV_DIM), const),
                  pl.BlockSpec((ATTN_Q_DIM, ATTN_Q_DIM), const),
                  pl.BlockSpec((ATTN_KV_DIM, ATTN_KV_DIM), const)],
        out_specs=pl.BlockSpec((tq, ATTN_Q_DIM), lambda b_, i: (b_ * nq + i, 0)),
        out_shape=_sds((B * S, ATTN_Q_DIM), BF16),
        scratch_shapes=[pltpu.VMEM((S, ATTN_KV_DIM), BF16)],
        compiler_params=_cp(("parallel", "arbitrary"), 48),
        name="attention",
    )(proj_a, proj_a, proj_a, cos_q, sin_q, cos_k, sin_k, gq, gk, bdq, bdk)


def _ssd_dir(xbc_ref, dt_ref, bias, a_neg, h_ref, y_ref, rev):
    L = SSD_CHUNK
    P, Nst = SSD_HEAD_DIM, SSD_STATE
    xbc = xbc_ref[...]
    x = xbc[:, 0:SSD_INNER]
    bm = xbc[:, SSD_INNER:SSD_INNER + SSD_BC_DIM]
    cm = xbc[:, SSD_INNER + SSD_BC_DIM:SSD_CONV_DIM]
    dt = jax.nn.softplus(dt_ref[...] + bias)
    a = dt * a_neg
    row = lax.broadcasted_iota(I32, (L, L), 0)
    col = lax.broadcasted_iota(I32, (L, L), 1)
    tri = (row >= col).astype(F32)
    cum = jnp.dot(tri, a, precision=HI, preferred_element_type=F32)
    e = cum - a if rev else cum
    eT = e.T
    dtT = dt.T
    bT = bm.astype(F32).T.astype(BF16)
    mask = (col >= row) if rev else (row >= col)
    rep = SSD_HEADS // SSD_GROUPS
    for g in range(SSD_GROUPS):
        cg = cm[:, g * Nst:(g + 1) * Nst]
        bTg = bT[g * Nst:(g + 1) * Nst, :]
        sg = jnp.dot(cg, bTg, preferred_element_type=F32)
        for r in range(rep):
            h = g * rep + r
            ecol = e[:, h:h + 1]
            erow = eT[h:h + 1, :]
            tot = cum[L - 1:L, h:h + 1]
            diff = (erow - ecol) if rev else (ecol - erow)
            dec = jnp.where(mask, jnp.exp(jnp.minimum(diff, 0.0)), 0.0)
            m = (sg * dec * dtT[h:h + 1, :]).astype(BF16)
            xh = x[:, h * P:(h + 1) * P]
            y = jnp.dot(m, xh, preferred_element_type=F32)
            hh = h_ref[h]
            win = jnp.exp(tot - ecol) if rev else jnp.exp(ecol)
            y = y + win * jnp.dot(cg, hh.astype(BF16), preferred_element_type=F32)
            wrow = (jnp.exp(erow) if rev else jnp.exp(tot - erow)) * dtT[h:h + 1, :]
            bw = (bTg.astype(F32) * wrow).astype(BF16)
            h_ref[h] = jnp.exp(tot) * hh + jnp.dot(bw, xh, preferred_element_type=F32)
            y_ref[:, h * P:(h + 1) * P] = y.astype(y_ref.dtype)


def _ssd_body(xf_ref, dtf_ref, xb_ref, dtb_ref, bias_ref, alog_ref, yf_ref, yb_ref, hf_ref, hb_ref):
    @pl.when(pl.program_id(1) == 0)
    def _():
        hf_ref[...] = jnp.zeros_like(hf_ref)
        hb_ref[...] = jnp.zeros_like(hb_ref)

    lane = lax.broadcasted_iota(I32, (1, 128), 1)
    a_neg = jnp.where(lane < SSD_HEADS, -jnp.exp(alog_ref[...]), 0.0)
    bias = bias_ref[...]
    _ssd_dir(xf_ref, dtf_ref, bias[0:1, :], a_neg[0:1, :], hf_ref, yf_ref, False)
    _ssd_dir(xb_ref, dtb_ref, bias[1:2, :], a_neg[1:2, :], hb_ref, yb_ref, True)


def _ssd(xbc_c, dt, dt_bias, a_log, B, S, Sc):
    L = SSD_CHUNK
    nc, ncc = S // L, Sc // L
    padh = 128 - SSD_HEADS
    bias = jnp.pad(dt_bias, ((0, 0), (0, padh)))
    alog = jnp.pad(a_log, ((0, 0), (0, padh)))

    def fwd(b_, i):
        return (b_ * nc + i, 0)

    def bwd(b_, i):
        return (b_ * nc + jnp.where(i < ncc, ncc - 1 - i, nc + ncc - 1 - i), 0)

    const = lambda b_, i: (0, 0)
    out = _sds((B * S, SSD_INNER), BF16)
    return pl.pallas_call(
        _ssd_body,
        grid=(B, nc),
        in_specs=[pl.BlockSpec((L, SSD_CONV_DIM), fwd), pl.BlockSpec((L, 128), fwd),
                  pl.BlockSpec((L, SSD_CONV_DIM), bwd), pl.BlockSpec((L, 128), bwd),
                  pl.BlockSpec((2, 128), const), pl.BlockSpec((2, 128), const)],
        out_specs=[pl.BlockSpec((L, SSD_INNER), fwd), pl.BlockSpec((L, SSD_INNER), bwd)],
        out_shape=[out, out],
        scratch_shapes=[pltpu.VMEM((SSD_HEADS, SSD_STATE, SSD_HEAD_DIM), F32),
                        pltpu.VMEM((SSD_HEADS, SSD_STATE, SSD_HEAD_DIM), F32)],
        compiler_params=_cp(("parallel", "arbitrary")),
        name="ssd",
    )(xbc_c, dt, xbc_c, dt, bias, alog)


def _cpow(n, lr, li, st):
    mag = jnp.exp(n * lr * st)
    ang = n * li * st
    return mag * jnp.cos(ang), mag * jnp.sin(ang)


def _zoh_coef(lr, li, st):
    ar, ai = _cpow(1.0, lr, li, st)
    nr, ni = ar - 1.0, ai
    den = lr * lr + li * li
    return (nr * lr + ni * li) / den, (ni * lr - nr * li) / den


def _s5gen_body(colp_ref, rowp_ref, ctr_ref, cti_ref, btr_ref, bti_ref, blr_ref, bli_ref, d_ref,
                w_ref, so_ref, ar_ref):
    Lc, G, P = S5_CHUNK, S5_GROUP_DIM, S5_STATE
    NS = 2 * Lc + 1
    colp = colp_ref[...]
    rowp = rowp_ref[...]
    lrf, lif, stf = colp[:, 0:1], colp[:, 1:2], jnp.exp(colp[:, 2:3])
    lrb, lib, stb = colp[:, 3:4], colp[:, 4:5], jnp.exp(colp[:, 5:6])
    lane = lax.broadcasted_iota(I32, (1, NS * G), 1)
    slot = lane // G
    isb = slot < Lc
    lag = jnp.abs(slot - Lc).astype(F32)
    pr, pi = _cpow(lag, jnp.where(isb, lrb, lrf), jnp.where(isb, lib, lif), jnp.where(isb, stb, stf))
    ctr, cti = ctr_ref[...], cti_ref[...]
    er = ctr * pr - cti * pi
    ei = ctr * pi + cti * pr
    rlrf, rlif, rstf = rowp[0:1, :], rowp[1:2, :], jnp.exp(rowp[2:3, :])
    rlrb, rlib, rstb = rowp[3:4, :], rowp[4:5, :], jnp.exp(rowp[5:6, :])
    btr, bti = btr_ref[...], bti_ref[...]
    cfr, cfi = _zoh_coef(rlrf, rlif, rstf)
    cbr, cbi = _zoh_coef(rlrb, rlib, rstb)
    bbf_r, bbf_i = cfr * btr - cfi * bti, cfr * bti + cfi * btr
    bbb_r, bbb_i = cbr * btr - cbi * bti, cbr * bti + cbi * btr

    def kt(br, bi):
        return (jnp.dot(br, er, precision=HI, preferred_element_type=F32)
                - jnp.dot(bi, ei, precision=HI, preferred_element_type=F32))

    ktf, ktb = kt(bbf_r, bbf_i), kt(bbb_r, bbb_i)
    ii = lax.broadcasted_iota(I32, (G, NS * G), 0)
    dmat = jnp.where((slot == Lc) & (ii == lane - Lc * G), d_ref[...], 0.0)
    strip = jnp.where(slot == Lc, ktf + ktb, jnp.where(isb, ktb, ktf)) + dmat
    for s in range(Lc):
        off = (Lc - s) * G
        w_ref[s * G:(s + 1) * G, :] = strip[:, off:off + Lc * G].astype(w_ref.dtype)
    fo = (Lc + 1) * G
    w_ref[Lc * G:Lc * G + P, :] = er[:, fo:fo + Lc * G].astype(w_ref.dtype)
    w_ref[Lc * G + P:Lc * G + 2 * P, :] = er[:, 0:Lc * G].astype(w_ref.dtype)
    w_ref[Lc * G + 2 * P:Lc * G + 3 * P, :] = (-ei[:, fo:fo + Lc * G]).astype(w_ref.dtype)
    w_ref[Lc * G + 3 * P:Lc * G + 4 * P, :] = (-ei[:, 0:Lc * G]).astype(w_ref.dtype)
    lane2 = lax.broadcasted_iota(I32, (1, Lc * G), 1)
    s_idx = (lane2 // G).astype(F32)
    qfr, qfi = _cpow((Lc - 1.0) - s_idx, lrf, lif, stf)
    qbr, qbi = _cpow(s_idx, lrb, lib, stb)
    ccfr, ccfi = _zoh_coef(lrf, lif, stf)
    ccbr, ccbi = _zoh_coef(lrb, lib, stb)
    blr, bli = blr_ref[...], bli_ref[...]
    bfr, bfi = ccfr * blr - ccfi * bli, ccfr * bli + ccfi * blr
    bbr, bbi = ccbr * blr - ccbi * bli, ccbr * bli + ccbi * blr
    so_ref[0:P, :] = (qfr * bfr - qfi * bfi).astype(so_ref.dtype)
    so_ref[P:2 * P, :] = (qbr * bbr - qbi * bbi).astype(so_ref.dtype)
    so_ref[2 * P:3 * P, :] = (qfr * bfi + qfi * bfr).astype(so_ref.dtype)
    so_ref[3 * P:4 * P, :] = (qbr * bbi + qbi * bbr).astype(so_ref.dtype)
    afr, afi = _cpow(float(Lc), rlrf, rlif, rstf)
    abr, abi = _cpow(float(Lc), rlrb, rlib, rstb)
    ar_ref[...] = jnp.zeros_like(ar_ref)
    ar_ref[0:1, 0:P] = afr
    ar_ref[0:1, P:2 * P] = abr
    ar_ref[1:2, 0:P] = afi
    ar_ref[1:2, P:2 * P] = abi


def _s5gen(lam_re, lam_im, log_step, b_re, b_im, c_re, c_im, d_skip):
    Gn, P, G, Lc = S5_GROUPS, S5_STATE, S5_GROUP_DIM, S5_CHUNK
    NS = 2 * Lc + 1
    ls = jnp.broadcast_to(log_step[:, :, None], (2, Gn, P))
    z = jnp.zeros((Gn, P), F32)
    rowp = jnp.stack([lam_re[0], lam_im[0], ls[0], lam_re[1], lam_im[1], ls[1], z, z], axis=1)
    colp = jnp.swapaxes(rowp, 1, 2)
    ctr = jnp.tile(jnp.swapaxes(c_re, 1, 2), (1, 1, NS))
    cti = jnp.tile(jnp.swapaxes(c_im, 1, 2), (1, 1, NS))
    btr = jnp.swapaxes(b_re, 1, 2)
    bti = jnp.swapaxes(b_im, 1, 2)
    blr = jnp.tile(b_re, (1, 1, Lc))
    bli = jnp.tile(b_im, (1, 1, Lc))
    dt = jnp.tile(d_skip.reshape(Gn, 1, G), (1, 1, NS))
    g3 = lambda a, b: pl.BlockSpec((None, a, b), lambda g: (g, 0, 0))
    return pl.pallas_call(
        _s5gen_body,
        grid=(Gn,),
        in_specs=[g3(P, 8), g3(8, P), g3(P, NS * G), g3(P, NS * G), g3(G, P), g3(G, P),
                  g3(P, Lc * G), g3(P, Lc * G), g3(1, NS * G)],
        out_specs=[g3(Lc * G + 4 * P, Lc * G), g3(4 * P, Lc * G), g3(8, 2 * P)],
        out_shape=[_sds((Gn, Lc * G + 4 * P, Lc * G), BF16), _sds((Gn, 4 * P, Lc * G), BF16),
                   _sds((Gn, 8, 2 * P), F32)],
        compiler_params=_cp(("parallel",)),
        name="s5gen",
    )(colp, rowp, ctr, cti, btr, bti, blr, bli, dt)


def _s5_body(u_ref, w_ref, so_ref, ar_ref, y_ref, v_ref, h_ref, *, B, nc, ncc):
    P, LG = S5_STATE, S5_CHUNK * S5_GROUP_DIM
    u = u_ref[...]
    v_ref[...] = lax.dot_general(u, so_ref[...], (((1,), (1,)), ((), ())), preferred_element_type=F32)
    a_re = ar_ref[0:1, :]
    a_im = ar_ref[1:2, :]
    lane = lax.broadcasted_iota(I32, (1, 2 * P), 1)
    isf = lane < P

    def step(i, carry):
        hr, hi = carry
        cf = pl.multiple_of(i * B, B)
        cb = pl.multiple_of(jnp.where(i < ncc, ncc - 1 - i, nc + ncc - 1 - i) * B, B)
        h_ref[pl.ds(cf, B), 0:P] = hr[:, 0:P]
        h_ref[pl.ds(cf, B), 2 * P:3 * P] = hi[:, 0:P]
        h_ref[pl.ds(cb, B), P:2 * P] = hr[:, P:2 * P]
        h_ref[pl.ds(cb, B), 3 * P:4 * P] = hi[:, P:2 * P]
        vr = jnp.where(isf, v_ref[pl.ds(cf, B), 0:2 * P], v_ref[pl.ds(cb, B), 0:2 * P])
        vi = jnp.where(isf, v_ref[pl.ds(cf, B), 2 * P:4 * P], v_ref[pl.ds(cb, B), 2 * P:4 * P])
        return hr * a_re - hi * a_im + vr, hr * a_im + hi * a_re + vi

    z = jnp.zeros((B, 2 * P), F32)
    lax.fori_loop(0, nc, step, (z, z))
    y = jnp.dot(u, w_ref[0:LG, :], preferred_element_type=F32)
    y = y + jnp.dot(h_ref[...].astype(BF16), w_ref[LG:LG + 4 * P, :], preferred_element_type=F32)
    y_ref[...] = y.astype(y_ref.dtype)


def _s5(u_g, w, so, ar, B, nc, ncc):
    Gn, R, LG = u_g.shape
    P = S5_STATE
    g3 = lambda a, b: pl.BlockSpec((None, a, b), lambda g: (g, 0, 0))
    return pl.pallas_call(
        functools.partial(_s5_body, B=B, nc=nc, ncc=ncc),
        grid=(Gn,),
        in_specs=[g3(R, LG), g3(LG + 4 * P, LG), g3(4 * P, LG), g3(8, 2 * P)],
        out_specs=g3(R, LG),
        out_shape=_sds((Gn, R, LG), F32),
        scratch_shapes=[pltpu.VMEM((R, 4 * P), F32), pltpu.VMEM((R, 4 * P), F32)],
        compiler_params=_cp(("parallel",)),
        name="s5",
    )(u_g, w, so, ar)


def _merge_body(x_ref, ya_ref, yf_ref, yb_ref, xc_ref, z_ref, y5_ref, yd_ref, g_ref, gate_ref,
                wa_ref, wb_ref, wc_ref, wd_ref, wo_ref, wglu_ref, bglu_ref, ng_ref, dsk_ref, o_ref):
    D = o_ref.shape[1]
    y = yf_ref[...].astype(F32) + yb_ref[...].astype(F32) + dsk_ref[...] * xc_ref[...].astype(F32)
    y = y * _silu(z_ref[...].astype(F32))
    y = y * lax.rsqrt(jnp.mean(y * y, axis=-1, keepdims=True) + RMS_EPS) * ng_ref[...]
    c = jax.nn.gelu(y5_ref[...]).astype(BF16)
    glu = jax.nn.sigmoid(jnp.dot(c, wglu_ref[...], preferred_element_type=F32) + bglu_ref[...])
    c = (c.astype(F32) * glu).astype(BF16)
    g = g_ref[...]

    def gate(k):
        return jax.nn.sigmoid(g[:, k * D:(k + 1) * D].astype(F32))

    m = gate(0) * jnp.dot(ya_ref[...], wa_ref[...], preferred_element_type=F32)
    m = m + gate(1) * jnp.dot(y.astype(BF16), wb_ref[...], preferred_element_type=F32)
    m = m + gate(2) * jnp.dot(c, wc_ref[...], preferred_element_type=F32)
    m = m + gate(3) * jnp.dot(yd_ref[...], wd_ref[...], preferred_element_type=F32)
    out = jnp.dot(m.astype(BF16), wo_ref[...], preferred_element_type=F32)
    o_ref[...] = x_ref[...] + gate_ref[...] * out


def _merge(xa, ya, yf, yb, xbc_c, proj_a, y5, yd, proj_g, mods3, wa, wb, wc, wd, wo, wglu, bglu, ng, dsk, B, rc):
    N, D = xa.shape
    tm = rc[0]
    rows = lambda w, j=0: pl.BlockSpec((tm, w), lambda i: (i, j))
    full = lambda a: pl.BlockSpec(a.shape, lambda i: (0,) * a.ndim)
    return pl.pallas_call(
        _merge_body,
        grid=(N // tm,),
        in_specs=[rows(D), rows(ATTN_Q_DIM), rows(SSD_INNER), rows(SSD_INNER), rows(SSD_INNER),
                  rows(SSD_INNER, PA_Z // SSD_INNER), rows(S5_WIDTH), rows(SC_WIDTH), rows(4 * D),
                  _mod_spec(D, rc, B, 2),
                  full(wa), full(wb), full(wc), full(wd), full(wo), full(wglu), full(bglu), full(ng), full(dsk)],
        out_specs=rows(D),
        out_shape=_sds((N, D), F32),
        compiler_params=_cp(("parallel",), 48),
        name="merge",
    )(xa, ya, yf, yb, xbc_c, proj_a, y5, yd, proj_g, mods3, wa, wb, wc, wd, wo, wglu, bglu, ng, dsk)


def _prefix_excl(mask_f):
    R, T = mask_f.shape
    r = lax.broadcasted_iota(I32, (128, 128), 0)
    c = lax.broadcasted_iota(I32, (128, 128), 1)
    upper = (r <= c).astype(BF16)
    outs = []
    off = jnp.zeros((R, 1), F32)
    for k in range(T // 128):
        blk = mask_f[:, k * 128:(k + 1) * 128]
        inc = jnp.dot(blk.astype(BF16), upper, preferred_element_type=F32)
        outs.append(inc - blk + off)
        off = off + inc[:, 127:128]
    return jnp.concatenate(outs, axis=1)


def _topk_slots(aff, cap):
    E, T = aff.shape
    bits = pltpu.bitcast(aff, I32)

    def step(i, th):
        cand = th | (jnp.int32(1) << (30 - i))
        cnt = jnp.sum((bits >= cand).astype(I32), axis=1, keepdims=True)
        return jnp.where(cnt >= cap, cand, th)

    th = lax.fori_loop(0, 31, step, jnp.zeros((E, 1), I32))
    gt = bits > th
    eq = bits == th
    n_gt = jnp.sum(gt.astype(F32), axis=1, keepdims=True)
    sel = gt | (eq & (n_gt + _prefix_excl(eq.astype(F32)) < cap))
    pos = _prefix_excl(sel.astype(F32))
    return jnp.where(sel, pos, -1.0)


def _router_body(x_ref, g_ref, shl_ref, scl_ref, shc_ref, scc_ref, wr_ref, hm_ref, slot_ref, gate_ref,
                 *, Sc, cap_l, cap_c, ctx_out):
    x = x_ref[...]
    S = x.shape[0]
    ms = jnp.mean(x * x, axis=-1, keepdims=True)
    y = x * lax.rsqrt(ms + RMS_EPS) * g_ref[...]
    t = lax.broadcasted_iota(I32, (S, 1), 0)
    isc = t < Sc
    hm = y * (1.0 + jnp.where(isc, scc_ref[...], scl_ref[...])) + jnp.where(isc, shc_ref[...], shl_ref[...])
    hm_ref[...] = hm.astype(hm_ref.dtype)
    logits = jnp.dot(hm, wr_ref[...], precision=HI, preferred_element_type=F32)
    lt = logits.T[0:N_EXPERTS, :]
    mx = jnp.max(lt, axis=0, keepdims=True)
    ex = jnp.exp(lt - mx)
    aff = ex / jnp.sum(ex, axis=0, keepdims=True)
    gate_ref[...] = aff
    slot_l = _topk_slots(aff[:, Sc:], cap_l)
    if ctx_out:
        slot_c = _topk_slots(aff[:, 0:Sc], cap_c)
        slot_c = jnp.where(slot_c >= 0.0, slot_c + cap_l, -1.0)
    else:
        slot_c = jnp.full((N_EXPERTS, Sc), -1.0, F32)
    slot_ref[:, 0:Sc] = slot_c
    slot_ref[:, Sc:] = slot_l


def _router(xa, g, mods3, wr_pad, B, S, Sc, ctx_out):
    N, D = xa.shape
    T = S - Sc
    cap_l = EC_CAPACITY * T // N_EXPERTS
    cap_c = EC_CAPACITY * Sc // N_EXPERTS
    lat = lambda ch: pl.BlockSpec((None, 1, D), lambda b_: (b_, 0, ch))
    ctx = lambda ch: pl.BlockSpec((None, 1, D), lambda b_: (B, 0, ch))
    es = pl.BlockSpec((None, N_EXPERTS, S), lambda b_: (b_, 0, 0))
    return pl.pallas_call(
        functools.partial(_router_body, Sc=Sc, cap_l=cap_l, cap_c=cap_c, ctx_out=ctx_out),
        grid=(B,),
        in_specs=[pl.BlockSpec((S, D), lambda b_: (b_, 0)),
                  pl.BlockSpec((1, D), lambda b_: (0, 0)),
                  lat(3), lat(4), ctx(3), ctx(4),
                  pl.BlockSpec((D, 128), lambda b_: (0, 0))],
        out_specs=[pl.BlockSpec((S, D), lambda b_: (b_, 0)), es, es],
        out_shape=[_sds((N, D), BF16), _sds((B, N_EXPERTS, S), F32), _sds((B, N_EXPERTS, S), F32)],
        compiler_params=_cp(("parallel",), 56),
        name="router",
    )(xa, g, mods3, mods3, mods3, mods3, wr_pad)


def _experts_body(hm_ref, slot_ref, gate_ref, wg_ref, wu_ref, wd_ref, gl_ref, gc_ref, o_ref, *, Sc, ncap):
    e = pl.program_id(1)
    S = hm_ref.shape[0]

    @pl.when(e == 0)
    def _():
        o_ref[...] = jnp.zeros_like(o_ref)

    slot = slot_ref[pl.ds(e, 1), :]
    gate = gate_ref[pl.ds(e, 1), :]
    cidx = lax.broadcasted_iota(I32, (ncap, S), 0).astype(F32)
    hit = cidx == slot
    p = hit.astype(BF16)
    xg = jnp.dot(p, hm_ref[...], preferred_element_type=F32).astype(BF16)
    hid = _silu(jnp.dot(xg, wg_ref[...], preferred_element_type=F32)) * jnp.dot(
        xg, wu_ref[...], preferred_element_type=F32)
    gval = jnp.sum(jnp.where(hit, gate, 0.0), axis=1, keepdims=True)
    ye = jnp.dot(hid.astype(BF16), wd_ref[...], preferred_element_type=F32) * gval
    o_ref[...] += lax.dot_general(p, ye.astype(BF16), (((0,), (0,)), ((), ())), preferred_element_type=F32)

    @pl.when(e == pl.num_programs(1) - 1)
    def _():
        t = lax.broadcasted_iota(I32, (S, 1), 0)
        o_ref[...] = jnp.where(t < Sc, gc_ref[...], gl_ref[...]) * o_ref[...]


def _experts(hm, slot, gate, wg, wu, wd, mods3, B, S, Sc, ctx_out):
    N, D = hm.shape
    E, _, Fd = wg.shape
    T = S - Sc
    ncap = EC_CAPACITY * T // N_EXPERTS + (EC_CAPACITY * Sc // N_EXPERTS if ctx_out else 0)
    es = pl.BlockSpec((None, E, S), lambda b_, e: (b_, 0, 0))
    return pl.pallas_call(
        functools.partial(_experts_body, Sc=Sc, ncap=ncap),
        grid=(B, E),
        in_specs=[pl.BlockSpec((S, D), lambda b_, e: (b_, 0)), es, es,
                  pl.BlockSpec((None, D, Fd), lambda b_, e: (e, 0, 0)),
                  pl.BlockSpec((None, D, Fd), lambda b_, e: (e, 0, 0)),
                  pl.BlockSpec((None, Fd, D), lambda b_, e: (e, 0, 0)),
                  pl.BlockSpec((None, 1, D), lambda b_, e: (b_, 0, 5)),
                  pl.BlockSpec((None, 1, D), lambda b_, e: (B, 0, 5))],
        out_specs=pl.BlockSpec((S, D), lambda b_, e: (b_, 0)),
        out_shape=_sds((N, D), F32),
        compiler_params=_cp(("parallel", "arbitrary"), 56),
        name="experts",
    )(hm, slot, gate, wg, wu, wd, mods3, mods3)


def _final_body(x_ref, r_ref, g_ref, o_ref):
    x = x_ref[...] + r_ref[...]
    o_ref[...] = x * lax.rsqrt(jnp.mean(x * x, axis=-1, keepdims=True) + RMS_EPS) * g_ref[...]


def _final_norm(xa, res, g, B, S, Sc):
    N, D = xa.shape
    tm, nb, nbc = _row_cfg(S, Sc)
    nl = nb - nbc
    row = pl.BlockSpec((tm, D), lambda b_, i: (b_ * nb + nbc + i, 0))
    return pl.pallas_call(
        _final_body,
        grid=(B, nl),
        in_specs=[row, row, pl.BlockSpec((1, D), lambda b_, i: (0, 0))],
        out_specs=pl.BlockSpec((tm, D), lambda b_, i: (b_ * nl + i, 0)),
        out_shape=_sds((B * (S - Sc), D), F32),
        compiler_params=_cp(("parallel", "parallel")),
        name="final_norm",
    )(xa, res, g)


def _layer(xa, res, mods3, lp, tabs, B, S, Sc, ctx_out):
    N, D = xa.shape
    rc = _row_cfg(S, Sc)
    w_in = lp["w_in"]
    seg = lambda a, n: w_in[:, a:a + n]
    o_q, o_k, o_v, o_z = 0, ATTN_Q_DIM, ATTN_Q_DIM + ATTN_KV_DIM, ATTN_Q_DIM + 2 * ATTN_KV_DIM
    o_xbc = o_z + SSD_INNER
    o_dt = o_xbc + SSD_CONV_DIM
    o_u = o_dt + SSD_HEADS
    o_sb, o_sg, o_sh = o_u + S5_WIDTH, o_u + S5_WIDTH + SC_WIDTH, o_u + S5_WIDTH + 2 * SC_WIDTH
    o_g = o_sh + SC_WIDTH
    w_a = jnp.concatenate([seg(o_u, S5_WIDTH), seg(o_sb, SC_WIDTH), seg(o_sg, SC_WIDTH), seg(o_sh, SC_WIDTH),
                           seg(o_xbc, SSD_CONV_DIM), seg(o_k, ATTN_KV_DIM), seg(o_v, ATTN_KV_DIM),
                           seg(o_q, ATTN_Q_DIM), seg(o_z, SSD_INNER)], axis=1).astype(BF16)
    w_g = seg(o_g, 4 * D).astype(BF16)
    w_dt = jnp.pad(seg(o_dt, SSD_HEADS), ((0, 0), (0, 128 - SSD_HEADS))).astype(BF16)

    xa, hn = _normmod(xa, res, lp["norm_mix_g"][None, :], mods3, B, rc, 0, 1)
    tm = math.gcd(N, 1024)
    proj_a = _matmul(hn, w_a, BF16, tm, 512, "in_proj_a")
    proj_g = _matmul(hn, w_g, BF16, tm, 512, "in_proj_g")
    dt = _matmul(hn, w_dt, F32, tm, 128, "in_proj_dt")

    ya = _attention(proj_a, tabs, lp["q_norm_g"], lp["k_norm_g"], B, S, Sc)

    xbc_c = _ssdconv(proj_a, lp["ssd_conv_w"], lp["ssd_conv_b"][None, :], B, S, Sc)
    yf, yb = _ssd(xbc_c, dt, lp["ssd_dt_bias"], lp["ssd_a_log"], B, S, Sc)

    Lc, Gn, G = S5_CHUNK, S5_GROUPS, S5_GROUP_DIM
    nc5, ncc5 = S // Lc, Sc // Lc
    w5, so5, ar5 = _s5gen(lp["s5_lambda_re"], lp["s5_lambda_im"], lp["s5_log_step"], lp["s5_b_re"], lp["s5_b_im"],
                          lp["s5_c_re"], lp["s5_c_im"], lp["s5_d"])
    u = proj_a[:, PA_U:PA_U + S5_WIDTH].reshape(B, nc5, Lc, Gn, G)
    u_g = jnp.transpose(u, (3, 1, 0, 2, 4)).reshape(Gn, nc5 * B, Lc * G)
    y5g = _s5(u_g, w5, so5, ar5, B, nc5, ncc5)
    y5 = jnp.transpose(y5g.reshape(Gn, nc5, B, Lc, G), (2, 1, 3, 0, 4)).reshape(N, S5_WIDTH)

    yd = _shortconv(proj_a, lp["sc_conv_w"], B, S, Sc)

    dsk = jnp.repeat(lp["ssd_d"], SSD_HEAD_DIM)[None, :]
    x1 = _merge(xa, ya, yf, yb, xbc_c, proj_a, y5, yd, proj_g, mods3,
                lp["w_br_attn"].astype(BF16), lp["w_br_ssd"].astype(BF16), lp["w_br_s5"].astype(BF16),
                lp["w_br_sc"].astype(BF16), lp["w_out"].astype(BF16), lp["s5_w_glu"].astype(BF16),
                lp["s5_b_glu"][None, :], lp["ssd_norm_g"][None, :], dsk, B, rc)

    wr_pad = jnp.pad(lp["w_router"], ((0, 0), (0, 128 - N_EXPERTS)))
    hm, slot, gate = _router(x1, lp["norm_ffn_g"][None, :], mods3, wr_pad, B, S, Sc, ctx_out)
    moe = _experts(hm, slot, gate, lp["w_exp_gate"].astype(BF16), lp["w_exp_up"].astype(BF16),
                   lp["w_exp_down"].astype(BF16), mods3, B, S, Sc, ctx_out)
    return x1, moe


def kernel(x, c, ctx, c_ctx, w_mod, b_mod, norm_mix_g, norm_ffn_g, w_in, q_norm_g, k_norm_g, ssd_conv_w, ssd_conv_b, ssd_dt_bias, ssd_a_log, ssd_d, ssd_norm_g, s5_lambda_re, s5_lambda_im, s5_log_step, s5_b_re, s5_b_im, s5_c_re, s5_c_im, s5_d, s5_w_glu, s5_b_glu, sc_conv_w, w_br_attn, w_br_ssd, w_br_s5, w_br_sc, w_out, w_router, w_exp_gate, w_exp_up, w_exp_down, final_norm_g):
    B, T, D = x.shape
    Sc = ctx.shape[1]
    S = Sc + T
    depth = w_in.shape[0]
    xa = jnp.concatenate([ctx, x], axis=1).reshape(B * S, D)
    cc = jnp.zeros((16, D), F32).at[0:B].set(c).at[B].set(c_ctx)
    cos, sin = _rope_tables(T, Sc)
    tabs = (jnp.tile(cos, (1, ATTN_HEADS)), jnp.tile(sin, (1, ATTN_HEADS)),
            jnp.tile(cos, (1, ATTN_KV_HEADS)), jnp.tile(sin, (1, ATTN_KV_HEADS)),
            _block_diag_ones(ATTN_Q_DIM), _block_diag_ones(ATTN_KV_DIM))
    stacked = dict(
        w_in=w_in, norm_mix_g=norm_mix_g, norm_ffn_g=norm_ffn_g, q_norm_g=q_norm_g, k_norm_g=k_norm_g,
        ssd_conv_w=ssd_conv_w, ssd_conv_b=ssd_conv_b, ssd_dt_bias=ssd_dt_bias, ssd_a_log=ssd_a_log, ssd_d=ssd_d,
        ssd_norm_g=ssd_norm_g, s5_lambda_re=s5_lambda_re, s5_lambda_im=s5_lambda_im, s5_log_step=s5_log_step,
        s5_b_re=s5_b_re, s5_b_im=s5_b_im, s5_c_re=s5_c_re, s5_c_im=s5_c_im, s5_d=s5_d, s5_w_glu=s5_w_glu,
        s5_b_glu=s5_b_glu, sc_conv_w=sc_conv_w, w_br_attn=w_br_attn, w_br_ssd=w_br_ssd, w_br_s5=w_br_s5,
        w_br_sc=w_br_sc, w_out=w_out, w_router=w_router, w_exp_gate=w_exp_gate, w_exp_up=w_exp_up,
        w_exp_down=w_exp_down)
    res = None
    for i in range(depth):
        lp = {k: v[i] for k, v in stacked.items()}
        mods3 = _mods(cc, w_mod[i], b_mod[i][None, :]).reshape(16, 1, 6 * D)
        xa, res = _layer(xa, res, mods3, lp, tabs, B, S, Sc, ctx_out=i < depth - 1)
    return _final_norm(xa, res, final_norm_g[None, :], B, S, Sc).reshape(B, T, D)
```

```python
import functools
import math

import jax
import jax.numpy as jnp
import numpy as np
from jax import lax
from jax.experimental import pallas as pl
from jax.experimental.pallas import tpu as pltpu

F32 = jnp.float32
BF16 = jnp.bfloat16
I32 = jnp.int32
HI = lax.Precision.HIGHEST

RMS_EPS = 1e-6
GRID_W = 64
ROPE_THETA = 10000.0
HEAD_DIM = 64
ATTN_HEADS = 8
ATTN_KV_HEADS = 2
SSD_HEADS = 8
SSD_HEAD_DIM = 64
SSD_GROUPS = 2
SSD_STATE = 64
SSD_CHUNK = 128
S5_GROUPS = 24
S5_GROUP_DIM = 16
S5_STATE = 64
S5_CHUNK = 16
SC_WIDTH = 384
N_EXPERTS = 16
EC_CAPACITY = 2
MAX_ROW_BLOCK = 256
SLOT_WINDOW = 64
SLOT_ALIGN = 16

ATTN_Q_DIM = ATTN_HEADS * HEAD_DIM
ATTN_KV_DIM = ATTN_KV_HEADS * HEAD_DIM
SSD_INNER = SSD_HEADS * SSD_HEAD_DIM
SSD_BC_DIM = SSD_GROUPS * SSD_STATE
SSD_CONV_DIM = SSD_INNER + 2 * SSD_BC_DIM
S5_WIDTH = S5_GROUPS * S5_GROUP_DIM

PA_U, PA_SB, PA_SG, PA_SH = 0, 384, 768, 1152
PA_XBC, PA_K, PA_V, PA_Q, PA_Z, PA_W = 1536, 2304, 2432, 2560, 3072, 3584


def _sds(shape, dtype):
    return jax.ShapeDtypeStruct(shape, dtype)


def _cp(sem, vmem_mb=None):
    kw = dict(dimension_semantics=sem)
    if vmem_mb is not None:
        kw["vmem_limit_bytes"] = vmem_mb << 20
    return pltpu.CompilerParams(**kw)


def _silu(x):
    return x * jax.nn.sigmoid(x)


def _mods_body(c_ref, w_ref, b_ref, o_ref):
    s = _silu(c_ref[...])
    o_ref[...] = jnp.dot(s.astype(BF16), w_ref[...].astype(BF16), preferred_element_type=F32) + b_ref[...]


def _mods(cc, w, b):
    R, D = cc.shape
    N = w.shape[1]
    tn = 1536
    return pl.pallas_call(
        _mods_body,
        grid=(N // tn,),
        in_specs=[pl.BlockSpec((R, D), lambda j: (0, 0)),
                  pl.BlockSpec((D, tn), lambda j: (0, j)),
                  pl.BlockSpec((1, tn), lambda j: (0, j))],
        out_specs=pl.BlockSpec((R, tn), lambda j: (0, j)),
        out_shape=_sds((R, N), F32),
        compiler_params=_cp(("parallel",), 40),
        name="mods",
    )(cc, w, b)


def _row_cfg(S, Sc):
    tm = math.gcd(math.gcd(Sc, S - Sc), MAX_ROW_BLOCK)
    return tm, S // tm, Sc // tm


def _mod_spec(D, rc, B, chunk):
    _, nb, nbc = rc
    return pl.BlockSpec((None, 1, D), lambda i: (jnp.where(i % nb < nbc, B, i // nb), 0, chunk))


def _normmod_body(x_ref, g_ref, sh_ref, sc_ref, o_ref):
    x = x_ref[...]
    ms = jnp.mean(x * x, axis=-1, keepdims=True)
    y = x * lax.rsqrt(ms + RMS_EPS) * g_ref[...]
    o_ref[...] = (y * (1.0 + sc_ref[...]) + sh_ref[...]).astype(o_ref.dtype)


def _normmod_res_body(x_ref, r_ref, g_ref, sh_ref, sc_ref, xo_ref, o_ref):
    x = x_ref[...] + r_ref[...]
    xo_ref[...] = x
    ms = jnp.mean(x * x, axis=-1, keepdims=True)
    y = x * lax.rsqrt(ms + RMS_EPS) * g_ref[...]
    o_ref[...] = (y * (1.0 + sc_ref[...]) + sh_ref[...]).astype(o_ref.dtype)


def _normmod(xa, res, g, mods3, B, rc, shift_chunk, scale_chunk):
    N, D = xa.shape
    tm = rc[0]
    row = pl.BlockSpec((tm, D), lambda i: (i, 0))
    tail = [pl.BlockSpec((1, D), lambda i: (0, 0)), _mod_spec(D, rc, B, shift_chunk), _mod_spec(D, rc, B, scale_chunk)]
    if res is None:
        hn = pl.pallas_call(
            _normmod_body, grid=(N // tm,), in_specs=[row] + tail, out_specs=row,
            out_shape=_sds((N, D), BF16), compiler_params=_cp(("parallel",)), name="normmod",
        )(xa, g, mods3, mods3)
        return xa, hn
    return pl.pallas_call(
        _normmod_res_body, grid=(N // tm,), in_specs=[row, row] + tail, out_specs=[row, row],
        out_shape=[_sds((N, D), F32), _sds((N, D), BF16)], compiler_params=_cp(("parallel",)), name="normmod_res",
    )(xa, res, g, mods3, mods3)


def _mm_body(a_ref, w_ref, o_ref):
    o_ref[...] = jnp.dot(a_ref[...], w_ref[...], preferred_element_type=F32).astype(o_ref.dtype)


def _matmul(a, w, out_dtype, tm, tn, name):
    M, K = a.shape
    N = w.shape[1]
    return pl.pallas_call(
        _mm_body,
        grid=(M // tm, N // tn),
        in_specs=[pl.BlockSpec((tm, K), lambda i, j: (i, 0)),
                  pl.BlockSpec((K, tn), lambda i, j: (0, j))],
        out_specs=pl.BlockSpec((tm, tn), lambda i, j: (i, j)),
        out_shape=_sds((M, N), out_dtype),
        compiler_params=_cp(("parallel", "arbitrary"), 40),
        name=name,
    )(a, w)


def _shifted(x, Sc):
    S = x.shape[0]
    t = lax.broadcasted_iota(I32, (S, 1), 0)
    prev = jnp.where((t == 0) | (t == Sc), 0.0, pltpu.roll(x, 1, 0))
    nxt = jnp.where((t == Sc - 1) | (t == S - 1), 0.0, pltpu.roll(x, S - 1, 0))
    return prev, nxt


def _ssdconv_body(x_ref, w_ref, b_ref, o_ref, *, Sc):
    x = x_ref[...].astype(F32)
    prev, nxt = _shifted(x, Sc)
    w = w_ref[...]
    y = w[0:1, :] * prev + w[1:2, :] * x + w[2:3, :] * nxt + b_ref[...]
    o_ref[...] = _silu(y).astype(o_ref.dtype)


def _ssdconv(proj_a, w, b, B, S, Sc):
    C = 384
    j0 = PA_XBC // C
    return pl.pallas_call(
        functools.partial(_ssdconv_body, Sc=Sc),
        grid=(B, SSD_CONV_DIM // C),
        in_specs=[pl.BlockSpec((S, C), lambda b_, j: (b_, j0 + j)),
                  pl.BlockSpec((3, C), lambda b_, j: (0, j)),
                  pl.BlockSpec((1, C), lambda b_, j: (0, j))],
        out_specs=pl.BlockSpec((S, C), lambda b_, j: (b_, j)),
        out_shape=_sds((B * S, SSD_CONV_DIM), BF16),
        compiler_params=_cp(("parallel", "parallel"), 40),
        name="ssdconv",
    )(proj_a, w, b)


def _shortconv_body(sb_ref, sg_ref, sh_ref, w_ref, o_ref, *, Sc):
    x = sg_ref[...].astype(F32) * sh_ref[...].astype(F32)
    prev, nxt = _shifted(x, Sc)
    w = w_ref[...]
    y = w[0:1, :] * prev + w[1:2, :] * x + w[2:3, :] * nxt
    o_ref[...] = (sb_ref[...].astype(F32) * y).astype(o_ref.dtype)


def _shortconv(proj_a, w, B, S, Sc):
    C = SC_WIDTH
    return pl.pallas_call(
        functools.partial(_shortconv_body, Sc=Sc),
        grid=(B,),
        in_specs=[pl.BlockSpec((S, C), lambda b_: (b_, PA_SB // C)),
                  pl.BlockSpec((S, C), lambda b_: (b_, PA_SG // C)),
                  pl.BlockSpec((S, C), lambda b_: (b_, PA_SH // C)),
                  pl.BlockSpec((3, C), lambda b_: (0, 0))],
        out_specs=pl.BlockSpec((S, C), lambda b_: (b_, 0)),
        out_shape=_sds((B * S, C), BF16),
        compiler_params=_cp(("parallel",), 40),
        name="shortconv",
    )(proj_a, proj_a, proj_a, w)


def _norm_rope(x, g, cos, sin, bd, scale):
    W = x.shape[1]
    sq = x * x
    hi = sq.astype(BF16)
    lo = (sq - hi.astype(F32)).astype(BF16)
    ssq = jnp.dot(hi, bd, preferred_element_type=F32) + jnp.dot(lo, bd, preferred_element_type=F32)
    y = x * lax.rsqrt(ssq * (1.0 / HEAD_DIM) + RMS_EPS) * g
    lane = lax.broadcasted_iota(I32, (1, W), 1)
    first = (lane % 32) < 16
    partner = jnp.where(first, pltpu.roll(y, W - 16, 1), pltpu.roll(y, 16, 1))
    return (y * cos + partner * sin) * scale


def _attn_body(q_ref, k_ref, v_ref, cq_ref, sq_ref, ck_ref, sk_ref, gq_ref, gk_ref, bdq_ref, bdk_ref,
               o_ref, kh_ref, *, Sc, S, tq):
    qb = pl.program_id(1)

    @pl.when(qb == 0)
    def _():
        k = k_ref[...].astype(F32)
        kh_ref[...] = _norm_rope(k, gk_ref[...], ck_ref[...], sk_ref[...], bdk_ref[...], 1.0).astype(BF16)

    q = q_ref[...].astype(F32)
    qh = _norm_rope(q, gq_ref[...], cq_ref[...], sq_ref[...], bdq_ref[...], HEAD_DIM ** -0.5).astype(BF16)
    rep = ATTN_HEADS // ATTN_KV_HEADS

    def attend(nk):
        for g in range(ATTN_KV_HEADS):
            kg = kh_ref[0:nk, g * HEAD_DIM:(g + 1) * HEAD_DIM]
            vg = v_ref[0:nk, g * HEAD_DIM:(g + 1) * HEAD_DIM]
            qg = jnp.concatenate(
                [qh[:, (g * rep + r) * HEAD_DIM:(g * rep + r + 1) * HEAD_DIM] for r in range(rep)], axis=0)
            s = lax.dot_general(qg, kg, (((1,), (1,)), ((), ())), preferred_element_type=F32)
            m = jnp.max(s, axis=-1, keepdims=True)
            p = jnp.exp(s - m)
            l = jnp.sum(p, axis=-1, keepdims=True)
            o = jnp.dot(p.astype(BF16), vg, preferred_element_type=F32) / l
            for r in range(rep):
                h = g * rep + r
                o_ref[:, h * HEAD_DIM:(h + 1) * HEAD_DIM] = o[r * tq:(r + 1) * tq].astype(o_ref.dtype)

    @pl.when(qb < Sc // tq)
    def _():
        attend(Sc)

    @pl.when(qb >= Sc // tq)
    def _():
        attend(S)


def _rope_tables(T, Sc):
    rows = T // GRID_W
    row = np.repeat(np.arange(rows, dtype=np.float32), GRID_W)
    col = np.tile(np.arange(GRID_W, dtype=np.float32), rows)
    half = HEAD_DIM // 2
    inv = jnp.asarray(ROPE_THETA, F32) ** (-jnp.arange(0, half, 2, dtype=F32) / half)
    ra = jnp.asarray(row)[:, None] * inv
    ca = jnp.asarray(col)[:, None] * inv
    cos = jnp.concatenate([jnp.cos(ra), jnp.cos(ra), jnp.cos(ca), jnp.cos(ca)], axis=1)
    sin = jnp.concatenate([-jnp.sin(ra), jnp.sin(ra), -jnp.sin(ca), jnp.sin(ca)], axis=1)
    cos = jnp.concatenate([jnp.ones((Sc, HEAD_DIM), F32), cos], axis=0)
    sin = jnp.concatenate([jnp.zeros((Sc, HEAD_DIM), F32), sin], axis=0)
    return cos, sin


def _block_diag_ones(W):
    i = np.arange(W) // HEAD_DIM
    return jnp.asarray((i[:, None] == i[None, :]).astype(np.float32), BF16)


def _attention(proj_a, tabs, q_norm_g, k_norm_g, B, S, Sc):
    cos_q, sin_q, cos_k, sin_k, bdq, bdk = tabs
    tq = 128
    nq = S // tq
    gq = jnp.tile(q_norm_g, ATTN_HEADS)[None, :]
    gk = jnp.tile(k_norm_g, ATTN_KV_HEADS)[None, :]
    const = lambda b_, i: (0, 0)
    return pl.pallas_call(
        functools.partial(_attn_body, Sc=Sc, S=S, tq=tq),
        grid=(B, nq),
        in_specs=[pl.BlockSpec((tq, ATTN_Q_DIM), lambda b_, i: (b_ * nq + i, PA_Q // ATTN_Q_DIM)),
                  pl.BlockSpec((S, ATTN_KV_DIM), lambda b_, i: (b_, PA_K // ATTN_KV_DIM)),
                  pl.BlockSpec((S, ATTN_KV_DIM), lambda b_, i: (b_, PA_V // ATTN_KV_DIM)),
                  pl.BlockSpec((tq, ATTN_Q_DIM), lambda b_, i: (i, 0)),
                  pl.BlockSpec((tq, ATTN_Q_DIM), lambda b_, i: (i, 0)),
                  pl.BlockSpec((S, ATTN_KV_DIM), const),
                  pl.BlockSpec((S, ATTN_KV_DIM), const),
                  pl.BlockSpec((1, ATTN_Q_DIM), const),
                  pl.BlockSpec((1, ATTN_KV_DIM), const),
                  pl.BlockSpec((ATTN_Q_DIM, ATTN_Q_DIM), const),
                  pl.BlockSpec((ATTN_KV_DIM, ATTN_KV_DIM), const)],
        out_specs=pl.BlockSpec((tq, ATTN_Q_DIM), lambda b_, i: (b_ * nq + i, 0)),
        out_shape=_sds((B * S, ATTN_Q_DIM), BF16),
        scratch_shapes=[pltpu.VMEM((S, ATTN_KV_DIM), BF16)],
        compiler_params=_cp(("parallel", "arbitrary"), 48),
        name="attention",
    )(proj_a, proj_a, proj_a, cos_q, sin_q, cos_k, sin_k, gq, gk, bdq, bdk)


def _ssd_dir(xbc_ref, dt_ref, bias, a_neg, h_ref, y_ref, rev):
    L = SSD_CHUNK
    P, Nst = SSD_HEAD_DIM, SSD_STATE
    xbc = xbc_ref[...]
    x = xbc[:, 0:SSD_INNER]
    bm = xbc[:, SSD_INNER:SSD_INNER + SSD_BC_DIM]
    cm = xbc[:, SSD_INNER + SSD_BC_DIM:SSD_CONV_DIM]
    dt = jax.nn.softplus(dt_ref[...] + bias)
    a = dt * a_neg
    row = lax.broadcasted_iota(I32, (L, L), 0)
    col = lax.broadcasted_iota(I32, (L, L), 1)
    tri = (row >= col).astype(F32)
    cum = jnp.dot(tri, a, precision=HI, preferred_element_type=F32)
    e = cum - a if rev else cum
    eT = e.T
    dtT = dt.T
    bT = bm.astype(F32).T.astype(BF16)
    mask = (col >= row) if rev else (row >= col)
    rep = SSD_HEADS // SSD_GROUPS
    for g in range(SSD_GROUPS):
        cg = cm[:, g * Nst:(g + 1) * Nst]
        bTg = bT[g * Nst:(g + 1) * Nst, :]
        sg = jnp.dot(cg, bTg, preferred_element_type=F32)
        for r in range(rep):
            h = g * rep + r
            ecol = e[:, h:h + 1]
            erow = eT[h:h + 1, :]
            tot = cum[L - 1:L, h:h + 1]
            diff = (erow - ecol) if rev else (ecol - erow)
            dec = jnp.where(mask, jnp.exp(jnp.minimum(diff, 0.0)), 0.0)
            m = (sg * dec * dtT[h:h + 1, :]).astype(BF16)
            xh = x[:, h * P:(h + 1) * P]
            y = jnp.dot(m, xh, preferred_element_type=F32)
            hh = h_ref[h]
            win = jnp.exp(tot - ecol) if rev else jnp.exp(ecol)
            y = y + win * jnp.dot(cg, hh.astype(BF16), preferred_element_type=F32)
            wrow = (jnp.exp(erow) if rev else jnp.exp(tot - erow)) * dtT[h:h + 1, :]
            bw = (bTg.astype(F32) * wrow).astype(BF16)
            h_ref[h] = jnp.exp(tot) * hh + jnp.dot(bw, xh, preferred_element_type=F32)
            y_ref[:, h * P:(h + 1) * P] = y.astype(y_ref.dtype)


def _ssd_body(xf_ref, dtf_ref, xb_ref, dtb_ref, bias_ref, alog_ref, yf_ref, yb_ref, hf_ref, hb_ref):
    @pl.when(pl.program_id(1) == 0)
    def _():
        hf_ref[...] = jnp.zeros_like(hf_ref)
        hb_ref[...] = jnp.zeros_like(hb_ref)

    lane = lax.broadcasted_iota(I32, (1, 128), 1)
    a_neg = jnp.where(lane < SSD_HEADS, -jnp.exp(alog_ref[...]), 0.0)
    bias = bias_ref[...]
    _ssd_dir(xf_ref, dtf_ref, bias[0:1, :], a_neg[0:1, :], hf_ref, yf_ref, False)
    _ssd_dir(xb_ref, dtb_ref, bias[1:2, :], a_neg[1:2, :], hb_ref, yb_ref, True)


def _ssd(xbc_c, dt, dt_bias, a_log, B, S, Sc):
    L = SSD_CHUNK
    nc, ncc = S // L, Sc // L
    padh = 128 - SSD_HEADS
    bias = jnp.pad(dt_bias, ((0, 0), (0, padh)))
    alog = jnp.pad(a_log, ((0, 0), (0, padh)))

    def fwd(b_, i):
        return (b_ * nc + i, 0)

    def bwd(b_, i):
        return (b_ * nc + jnp.where(i < ncc, ncc - 1 - i, nc + ncc - 1 - i), 0)

    const = lambda b_, i: (0, 0)
    out = _sds((B * S, SSD_INNER), BF16)
    return pl.pallas_call(
        _ssd_body,
        grid=(B, nc),
        in_specs=[pl.BlockSpec((L, SSD_CONV_DIM), fwd), pl.BlockSpec((L, 128), fwd),
                  pl.BlockSpec((L, SSD_CONV_DIM), bwd), pl.BlockSpec((L, 128), bwd),
                  pl.BlockSpec((2, 128), const), pl.BlockSpec((2, 128), const)],
        out_specs=[pl.BlockSpec((L, SSD_INNER), fwd), pl.BlockSpec((L, SSD_INNER), bwd)],
        out_shape=[out, out],
        scratch_shapes=[pltpu.VMEM((SSD_HEADS, SSD_STATE, SSD_HEAD_DIM), F32),
                        pltpu.VMEM((SSD_HEADS, SSD_STATE, SSD_HEAD_DIM), F32)],
        compiler_params=_cp(("parallel", "arbitrary")),
        name="ssd",
    )(xbc_c, dt, xbc_c, dt, bias, alog)


def _cpow(n, lr, li, st):
    mag = jnp.exp(n * lr * st)
    ang = n * li * st
    return mag * jnp.cos(ang), mag * jnp.sin(ang)


def _zoh_coef(lr, li, st):
    ar, ai = _cpow(1.0, lr, li, st)
    nr, ni = ar - 1.0, ai
    den = lr * lr + li * li
    return (nr * lr + ni * li) / den, (ni * lr - nr * li) / den


def _s5gen_body(colp_ref, rowp_ref, ctr_ref, cti_ref, btr_ref, bti_ref, blr_ref, bli_ref, d_ref,
                wt_ref, so_ref, ar_ref, ws_ref, sos_ref):
    Lc, G, P = S5_CHUNK, S5_GROUP_DIM, S5_STATE
    NS = 2 * Lc + 1
    colp = colp_ref[...]
    rowp = rowp_ref[...]
    lrf, lif, stf = colp[:, 0:1], colp[:, 1:2], jnp.exp(colp[:, 2:3])
    lrb, lib, stb = colp[:, 3:4], colp[:, 4:5], jnp.exp(colp[:, 5:6])
    lane = lax.broadcasted_iota(I32, (1, NS * G), 1)
    slot = lane // G
    isb = slot < Lc
    lag = jnp.abs(slot - Lc).astype(F32)
    pr, pi = _cpow(lag, jnp.where(isb, lrb, lrf), jnp.where(isb, lib, lif), jnp.where(isb, stb, stf))
    ctr, cti = ctr_ref[...], cti_ref[...]
    er = ctr * pr - cti * pi
    ei = ctr * pi + cti * pr
    rlrf, rlif, rstf = rowp[0:1, :], rowp[1:2, :], jnp.exp(rowp[2:3, :])
    rlrb, rlib, rstb = rowp[3:4, :], rowp[4:5, :], jnp.exp(rowp[5:6, :])
    btr, bti = btr_ref[...], bti_ref[...]
    cfr, cfi = _zoh_coef(rlrf, rlif, rstf)
    cbr, cbi = _zoh_coef(rlrb, rlib, rstb)
    bbf_r, bbf_i = cfr * btr - cfi * bti, cfr * bti + cfi * btr
    bbb_r, bbb_i = cbr * btr - cbi * bti, cbr * bti + cbi * btr

    def kt(br, bi):
        return (jnp.dot(br, er, precision=HI, preferred_element_type=F32)
                - jnp.dot(bi, ei, precision=HI, preferred_element_type=F32))

    ktf, ktb = kt(bbf_r, bbf_i), kt(bbb_r, bbb_i)
    ii = lax.broadcasted_iota(I32, (G, NS * G), 0)
    dmat = jnp.where((slot == Lc) & (ii == lane - Lc * G), d_ref[...], 0.0)
    strip = jnp.where(slot == Lc, ktf + ktb, jnp.where(isb, ktb, ktf)) + dmat
    for s in range(Lc):
        off = (Lc - s) * G
        ws_ref[s * G:(s + 1) * G, :] = strip[:, off:off + Lc * G]
    fo = (Lc + 1) * G
    ws_ref[Lc * G:Lc * G + P, :] = er[:, fo:fo + Lc * G]
    ws_ref[Lc * G + P:Lc * G + 2 * P, :] = er[:, 0:Lc * G]
    ws_ref[Lc * G + 2 * P:Lc * G + 3 * P, :] = -ei[:, fo:fo + Lc * G]
    ws_ref[Lc * G + 3 * P:Lc * G + 4 * P, :] = -ei[:, 0:Lc * G]
    wt_ref[:, 0:Lc * G] = ws_ref[0:Lc * G, :].T.astype(wt_ref.dtype)
    wt_ref[:, Lc * G:Lc * G + 4 * P] = ws_ref[Lc * G:Lc * G + 4 * P, :].T.astype(wt_ref.dtype)
    lane2 = lax.broadcasted_iota(I32, (1, Lc * G), 1)
    s_idx = (lane2 // G).astype(F32)
    qfr, qfi = _cpow((Lc - 1.0) - s_idx, lrf, lif, stf)
    qbr, qbi = _cpow(s_idx, lrb, lib, stb)
    ccfr, ccfi = _zoh_coef(lrf, lif, stf)
    ccbr, ccbi = _zoh_coef(lrb, lib, stb)
    blr, bli = blr_ref[...], bli_ref[...]
    bfr, bfi = ccfr * blr - ccfi * bli, ccfr * bli + ccfi * blr
    bbr, bbi = ccbr * blr - ccbi * bli, ccbr * bli + ccbi * blr
    sos_ref[0:P, :] = qfr * bfr - qfi * bfi
    sos_ref[P:2 * P, :] = qbr * bbr - qbi * bbi
    sos_ref[2 * P:3 * P, :] = qfr * bfi + qfi * bfr
    sos_ref[3 * P:4 * P, :] = qbr * bbi + qbi * bbr
    so_ref[...] = sos_ref[...].T.astype(so_ref.dtype)
    afr, afi = _cpow(float(Lc), rlrf, rlif, rstf)
    abr, abi = _cpow(float(Lc), rlrb, rlib, rstb)
    ar_ref[...] = jnp.zeros_like(ar_ref)
    ar_ref[0:1, 0:P] = afr
    ar_ref[0:1, P:2 * P] = abr
    ar_ref[1:2, 0:P] = afi
    ar_ref[1:2, P:2 * P] = abi


def _s5gen(lam_re, lam_im, log_step, b_re, b_im, c_re, c_im, d_skip):
    Gn, P, G, Lc = S5_GROUPS, S5_STATE, S5_GROUP_DIM, S5_CHUNK
    NS = 2 * Lc + 1
    ls = jnp.broadcast_to(log_step[:, :, None], (2, Gn, P))
    z = jnp.zeros((Gn, P), F32)
    rowp = jnp.stack([lam_re[0], lam_im[0], ls[0], lam_re[1], lam_im[1], ls[1], z, z], axis=1)
    colp = jnp.swapaxes(rowp, 1, 2)
    ctr = jnp.tile(jnp.swapaxes(c_re, 1, 2), (1, 1, NS))
    cti = jnp.tile(jnp.swapaxes(c_im, 1, 2), (1, 1, NS))
    btr = jnp.swapaxes(b_re, 1, 2)
    bti = jnp.swapaxes(b_im, 1, 2)
    blr = jnp.tile(b_re, (1, 1, Lc))
    bli = jnp.tile(b_im, (1, 1, Lc))
    dt = jnp.tile(d_skip.reshape(Gn, 1, G), (1, 1, NS))
    g3 = lambda a, b: pl.BlockSpec((None, a, b), lambda g: (g, 0, 0))
    return pl.pallas_call(
        _s5gen_body,
        grid=(Gn,),
        in_specs=[g3(P, 8), g3(8, P), g3(P, NS * G), g3(P, NS * G), g3(G, P), g3(G, P),
                  g3(P, Lc * G), g3(P, Lc * G), g3(1, NS * G)],
        out_specs=[g3(Lc * G, Lc * G + 4 * P), g3(Lc * G, 4 * P), g3(8, 2 * P)],
        out_shape=[_sds((Gn, Lc * G, Lc * G + 4 * P), BF16), _sds((Gn, Lc * G, 4 * P), BF16),
                   _sds((Gn, 8, 2 * P), F32)],
        scratch_shapes=[pltpu.VMEM((Lc * G + 4 * P, Lc * G), F32), pltpu.VMEM((4 * P, Lc * G), F32)],
        compiler_params=_cp(("parallel",)),
        name="s5gen",
    )(colp, rowp, ctr, cti, btr, bti, blr, bli, dt)


def _s5_body(xt_ref, wt_ref, so_ref, ar_ref, y_ref, v_ref, h_ref, *, B, nc, ncc):
    P, Lc, G = S5_STATE, S5_CHUNK, S5_GROUP_DIM
    LG = Lc * G
    R = nc * B
    ucol = jnp.concatenate([xt_ref[:, l * R:(l + 1) * R] for l in range(Lc)], axis=0)
    v_ref[...] = lax.dot_general(ucol, so_ref[...], (((0,), (0,)), ((), ())), preferred_element_type=F32)
    a_re = ar_ref[0:1, :]
    a_im = ar_ref[1:2, :]
    lane = lax.broadcasted_iota(I32, (1, 2 * P), 1)
    isf = lane < P

    def step(i, carry):
        hr, hi = carry
        cf = pl.multiple_of(i * B, B)
        cb = pl.multiple_of(jnp.where(i < ncc, ncc - 1 - i, nc + ncc - 1 - i) * B, B)
        h_ref[pl.ds(cf, B), 0:P] = hr[:, 0:P]
        h_ref[pl.ds(cf, B), 2 * P:3 * P] = hi[:, 0:P]
        h_ref[pl.ds(cb, B), P:2 * P] = hr[:, P:2 * P]
        h_ref[pl.ds(cb, B), 3 * P:4 * P] = hi[:, P:2 * P]
        vr = jnp.where(isf, v_ref[pl.ds(cf, B), 0:2 * P], v_ref[pl.ds(cb, B), 0:2 * P])
        vi = jnp.where(isf, v_ref[pl.ds(cf, B), 2 * P:4 * P], v_ref[pl.ds(cb, B), 2 * P:4 * P])
        return hr * a_re - hi * a_im + vr, hr * a_im + hi * a_re + vi

    z = jnp.zeros((B, 2 * P), F32)
    lax.fori_loop(0, nc, step, (z, z))
    yt = jnp.dot(wt_ref[:, 0:LG], ucol, preferred_element_type=F32)
    yt = yt + lax.dot_general(wt_ref[:, LG:LG + 4 * P], h_ref[...].astype(BF16), (((1,), (1,)), ((), ())),
                              preferred_element_type=F32)
    for t in range(Lc):
        y_ref[:, t * R:(t + 1) * R] = yt[t * G:(t + 1) * G, :]


def _s5(xt, wt, so, ar, B, nc, ncc):
    P, Lc, G, Gn = S5_STATE, S5_CHUNK, S5_GROUP_DIM, S5_GROUPS
    N = xt.shape[1]
    R = nc * B
    g3 = lambda a, b: pl.BlockSpec((None, a, b), lambda g: (g, 0, 0))
    return pl.pallas_call(
        functools.partial(_s5_body, B=B, nc=nc, ncc=ncc),
        grid=(Gn,),
        in_specs=[pl.BlockSpec((G, N), lambda g: (g, 0)), g3(Lc * G, Lc * G + 4 * P), g3(Lc * G, 4 * P),
                  g3(8, 2 * P)],
        out_specs=pl.BlockSpec((G, N), lambda g: (g, 0)),
        out_shape=_sds((Gn * G, N), F32),
        scratch_shapes=[pltpu.VMEM((R, 4 * P), F32), pltpu.VMEM((R, 4 * P), F32)],
        compiler_params=_cp(("parallel",)),
        name="s5",
    )(xt, wt, so, ar)


def _t2d_body(x_ref, o_ref):
    o_ref[...] = x_ref[...].astype(F32).T.astype(o_ref.dtype)


def _transpose2d(x, tr, name):
    M, C = x.shape
    return pl.pallas_call(
        _t2d_body,
        grid=(M // tr,),
        in_specs=[pl.BlockSpec((tr, C), lambda i: (i, 0))],
        out_specs=pl.BlockSpec((C, tr), lambda i: (0, i)),
        out_shape=_sds((C, M), x.dtype),
        compiler_params=_cp(("parallel",)),
        name=name,
    )(x)


def _untranspose2d(xt, tr, name):
    C, M = xt.shape
    return pl.pallas_call(
        _t2d_body,
        grid=(M // tr,),
        in_specs=[pl.BlockSpec((C, tr), lambda i: (0, i))],
        out_specs=pl.BlockSpec((tr, C), lambda i: (i, 0)),
        out_shape=_sds((M, C), xt.dtype),
        compiler_params=_cp(("parallel",)),
        name=name,
    )(xt)


def _merge_body(x_ref, ya_ref, yf_ref, yb_ref, xc_ref, z_ref, y5_ref, yd_ref, g_ref, gate_ref,
                wa_ref, wb_ref, wc_ref, wd_ref, wo_ref, wglu_ref, bglu_ref, ng_ref, dsk_ref, o_ref):
    D = o_ref.shape[1]
    y = yf_ref[...].astype(F32) + yb_ref[...].astype(F32) + dsk_ref[...] * xc_ref[...].astype(F32)
    y = y * _silu(z_ref[...].astype(F32))
    y = y * lax.rsqrt(jnp.mean(y * y, axis=-1, keepdims=True) + RMS_EPS) * ng_ref[...]
    c = jax.nn.gelu(y5_ref[...]).astype(BF16)
    glu = jax.nn.sigmoid(jnp.dot(c, wglu_ref[...], preferred_element_type=F32) + bglu_ref[...])
    c = (c.astype(F32) * glu).astype(BF16)
    g = g_ref[...]

    def gate(k):
        return jax.nn.sigmoid(g[:, k * D:(k + 1) * D].astype(F32))

    m = gate(0) * jnp.dot(ya_ref[...], wa_ref[...], preferred_element_type=F32)
    m = m + gate(1) * jnp.dot(y.astype(BF16), wb_ref[...], preferred_element_type=F32)
    m = m + gate(2) * jnp.dot(c, wc_ref[...], preferred_element_type=F32)
    m = m + gate(3) * jnp.dot(yd_ref[...], wd_ref[...], preferred_element_type=F32)
    out = jnp.dot(m.astype(BF16), wo_ref[...], preferred_element_type=F32)
    o_ref[...] = x_ref[...] + gate_ref[...] * out


def _merge(xa, ya, yf, yb, xbc_c, proj_a, y5, yd, proj_g, mods3, wa, wb, wc, wd, wo, wglu, bglu, ng, dsk, B, rc):
    N, D = xa.shape
    tm = rc[0]
    rows = lambda w, j=0: pl.BlockSpec((tm, w), lambda i: (i, j))
    full = lambda a: pl.BlockSpec(a.shape, lambda i: (0,) * a.ndim)
    return pl.pallas_call(
        _merge_body,
        grid=(N // tm,),
        in_specs=[rows(D), rows(ATTN_Q_DIM), rows(SSD_INNER), rows(SSD_INNER), rows(SSD_INNER),
                  rows(SSD_INNER, PA_Z // SSD_INNER), rows(S5_WIDTH), rows(SC_WIDTH), rows(4 * D),
                  _mod_spec(D, rc, B, 2),
                  full(wa), full(wb), full(wc), full(wd), full(wo), full(wglu), full(bglu), full(ng), full(dsk)],
        out_specs=rows(D),
        out_shape=_sds((N, D), F32),
        compiler_params=_cp(("parallel",), 48),
        name="merge",
    )(xa, ya, yf, yb, xbc_c, proj_a, y5, yd, proj_g, mods3, wa, wb, wc, wd, wo, wglu, bglu, ng, dsk)


def _prefix_excl(mask_f):
    R, T = mask_f.shape
    r = lax.broadcasted_iota(I32, (128, 128), 0)
    c = lax.broadcasted_iota(I32, (128, 128), 1)
    upper = (r <= c).astype(BF16)
    outs = []
    off = jnp.zeros((R, 1), F32)
    for k in range(T // 128):
        blk = mask_f[:, k * 128:(k + 1) * 128]
        inc = jnp.dot(blk.astype(BF16), upper, preferred_element_type=F32)
        outs.append(inc - blk + off)
        off = off + inc[:, 127:128]
    return jnp.concatenate(outs, axis=1)


def _topk_slots(aff, cap):
    E, T = aff.shape
    bits = pltpu.bitcast(aff, I32)

    def step(i, th):
        cand = th | (jnp.int32(1) << (30 - i))
        cnt = jnp.sum((bits >= cand).astype(I32), axis=1, keepdims=True)
        return jnp.where(cnt >= cap, cand, th)

    th = lax.fori_loop(0, 31, step, jnp.zeros((E, 1), I32))
    gt = bits > th
    eq = bits == th
    n_gt = jnp.sum(gt.astype(F32), axis=1, keepdims=True)
    sel = gt | (eq & (n_gt + _prefix_excl(eq.astype(F32)) < cap))
    pos = _prefix_excl(sel.astype(F32))
    return jnp.where(sel, pos, -1.0)


def _router_body(x_ref, g_ref, shl_ref, scl_ref, shc_ref, scc_ref, wr_ref, hm_ref, slot_ref, gate_ref, rng_ref,
                 *, Sc, cap_l, cap_c, ctx_out, tb):
    x = x_ref[...]
    S = x.shape[0]
    ms = jnp.mean(x * x, axis=-1, keepdims=True)
    y = x * lax.rsqrt(ms + RMS_EPS) * g_ref[...]
    t = lax.broadcasted_iota(I32, (S, 1), 0)
    isc = t < Sc
    hm = y * (1.0 + jnp.where(isc, scc_ref[...], scl_ref[...])) + jnp.where(isc, shc_ref[...], shl_ref[...])
    hm_ref[...] = hm.astype(hm_ref.dtype)
    logits = jnp.dot(hm, wr_ref[...], precision=HI, preferred_element_type=F32)
    lt = logits.T[0:N_EXPERTS, :]
    mx = jnp.max(lt, axis=0, keepdims=True)
    ex = jnp.exp(lt - mx)
    aff = ex / jnp.sum(ex, axis=0, keepdims=True)
    gate_ref[...] = aff
    slot_l = _topk_slots(aff[:, Sc:], cap_l)
    if ctx_out:
        slot_c = _topk_slots(aff[:, 0:Sc], cap_c)
        slot_c = jnp.where(slot_c >= 0.0, slot_c + cap_l, -1.0)
    else:
        slot_c = jnp.full((N_EXPERTS, Sc), -1.0, F32)
    slot_ref[:, 0:Sc] = slot_c
    slot_ref[:, Sc:] = slot_l
    lane = lax.broadcasted_iota(I32, (1, 128), 1)
    lo_a = jnp.zeros((N_EXPERTS, 128), F32)
    nwin = jnp.zeros((N_EXPERTS, 1), F32)
    for seg0, seg1, base, slots in ((0, Sc, float(cap_l), slot_c), (Sc, S, 0.0, slot_l)):
        lo = jnp.full((N_EXPERTS, 1), base, F32)
        for k in range(seg0 // tb, seg1 // tb):
            blk = slots[:, k * tb - seg0:(k + 1) * tb - seg0]
            hi = lo + jnp.sum((blk >= 0.0).astype(F32), axis=1, keepdims=True)
            lo_al = jnp.floor(lo * (1.0 / SLOT_ALIGN)) * SLOT_ALIGN
            need = jnp.where(hi > lo, jnp.floor((hi - lo_al + (SLOT_WINDOW - 1)) * (1.0 / SLOT_WINDOW)), 0.0)
            nwin = jnp.maximum(nwin, need)
            lo_a = jnp.where(lane == k, lo, lo_a)
            lo = hi
    rng_ref[...] = jnp.where(lane == 127, jnp.max(nwin, axis=0, keepdims=True), lo_a).astype(I32)


def _token_block(S, Sc):
    return math.gcd(math.gcd(Sc, S - Sc), 256)


def _router(xa, g, mods3, wr_pad, B, S, Sc, ctx_out):
    N, D = xa.shape
    T = S - Sc
    cap_l = EC_CAPACITY * T // N_EXPERTS
    cap_c = EC_CAPACITY * Sc // N_EXPERTS
    tb = _token_block(S, Sc)
    lat = lambda ch: pl.BlockSpec((None, 1, D), lambda b_: (b_, 0, ch))
    ctx = lambda ch: pl.BlockSpec((None, 1, D), lambda b_: (B, 0, ch))
    es = pl.BlockSpec((None, N_EXPERTS, S), lambda b_: (b_, 0, 0))
    return pl.pallas_call(
        functools.partial(_router_body, Sc=Sc, cap_l=cap_l, cap_c=cap_c, ctx_out=ctx_out, tb=tb),
        grid=(B,),
        in_specs=[pl.BlockSpec((S, D), lambda b_: (b_, 0)),
                  pl.BlockSpec((1, D), lambda b_: (0, 0)),
                  lat(3), lat(4), ctx(3), ctx(4),
                  pl.BlockSpec((D, 128), lambda b_: (0, 0))],
        out_specs=[pl.BlockSpec((S, D), lambda b_: (b_, 0)), es, es,
                   pl.BlockSpec((None, N_EXPERTS, 128), lambda b_: (b_, 0, 0))],
        out_shape=[_sds((N, D), BF16), _sds((B, N_EXPERTS, S), F32), _sds((B, N_EXPERTS, S), F32),
                   _sds((B, N_EXPERTS, 128), I32)],
        compiler_params=_cp(("parallel",), 56),
        name="router",
    )(xa, g, mods3, mods3, mods3, mods3, wr_pad)


def _window_rows(rng_ref, k, j, ncap):
    rows = []
    for e in range(N_EXPERTS):
        lo = rng_ref[e, k]
        lo_al = lax.shift_left(lax.shift_right_logical(lo, SLOT_ALIGN.bit_length() - 1), SLOT_ALIGN.bit_length() - 1)
        rows.append(pl.multiple_of(jnp.minimum(lo_al + j * SLOT_WINDOW, ncap), SLOT_ALIGN))
    return rows


def _window_hits(rows, slot_ref):
    wrow = lax.broadcasted_iota(I32, (SLOT_WINDOW, 1), 0)
    return [(wrow + rows[e]).astype(F32) == slot_ref[e:e + 1, :] for e in range(N_EXPERTS)]


def _moe_gather_body(rng_ref, hm_ref, slot_ref, gate_ref, xg_ref, gv_ref, *, ncap):
    k = pl.program_id(1)
    W = SLOT_WINDOW

    @pl.when(k == 0)
    def _():
        xg_ref[...] = jnp.zeros_like(xg_ref)
        gv_ref[...] = jnp.zeros_like(gv_ref)

    def window(j, carry):
        rows = _window_rows(rng_ref, k, j, ncap)
        hits = _window_hits(rows, slot_ref)
        hit_all = jnp.concatenate([h.astype(BF16) for h in hits], axis=0)
        xgw = jnp.dot(hit_all, hm_ref[...], preferred_element_type=F32).astype(BF16)
        for e in range(N_EXPERTS):
            r = pl.ds(rows[e], W)
            xg_ref[e, r, :] += xgw[e * W:(e + 1) * W, :]
            gv_ref[e, r, :] += jnp.sum(jnp.where(hits[e], gate_ref[e:e + 1, :], 0.0), axis=1, keepdims=True)
        return carry

    lax.fori_loop(0, rng_ref[0, 127], window, 0)


def _moe_ffn_body(xg_ref, gv_ref, wg_ref, wu_ref, wd_ref, yw_ref, wgs_ref, wus_ref, wds_ref, *, ncap):
    @pl.when(pl.program_id(1) == 0)
    def _():
        wgs_ref[...] = wg_ref[...].astype(BF16)
        wus_ref[...] = wu_ref[...].astype(BF16)
        wds_ref[...] = wd_ref[...].astype(BF16)

    xg = xg_ref[0:ncap, :]
    hid = _silu(jnp.dot(xg, wgs_ref[...], preferred_element_type=F32)) * jnp.dot(
        xg, wus_ref[...], preferred_element_type=F32)
    ye = jnp.dot(hid.astype(BF16), wds_ref[...], preferred_element_type=F32) * gv_ref[0:ncap, :]
    yw_ref[0:ncap, :] = ye.astype(yw_ref.dtype)
    yw_ref[ncap:, :] = jnp.zeros((yw_ref.shape[0] - ncap, yw_ref.shape[1]), yw_ref.dtype)


def _moe_scatter_body(rng_ref, slot_ref, yw_ref, gl_ref, gc_ref, o_ref, *, ncap, nbc):
    k = pl.program_id(1)
    W = SLOT_WINDOW

    def window(j, acc):
        rows = _window_rows(rng_ref, k, j, ncap)
        hit_all = jnp.concatenate([h.astype(BF16) for h in _window_hits(rows, slot_ref)], axis=0)
        yw = jnp.concatenate([yw_ref[e, pl.ds(rows[e], W), :] for e in range(N_EXPERTS)], axis=0)
        return acc + lax.dot_general(hit_all, yw, (((0,), (0,)), ((), ())), preferred_element_type=F32)

    acc = lax.fori_loop(0, rng_ref[0, 127], window, jnp.zeros(o_ref.shape, F32))
    o_ref[...] = jnp.where(k < nbc, gc_ref[...], gl_ref[...]) * acc


def _experts(hm, slot, gate, rng, wg, wu, wd, mods3, B, S, Sc, ctx_out):
    N, D = hm.shape
    E, _, Fd = wg.shape
    T = S - Sc
    tb = _token_block(S, Sc)
    nblk, nbc = S // tb, Sc // tb
    ncap = EC_CAPACITY * T // N_EXPERTS + (EC_CAPACITY * Sc // N_EXPERTS if ctx_out else 0)
    assert ncap % SLOT_ALIGN == 0
    rows = ncap + SLOT_WINDOW
    smem = pl.BlockSpec((None, E, 128), lambda b_, k: (b_, 0, 0), memory_space=pltpu.SMEM)
    es = pl.BlockSpec((None, E, tb), lambda b_, k: (b_, 0, k))
    xg, gv = pl.pallas_call(
        functools.partial(_moe_gather_body, ncap=ncap),
        grid=(B, nblk),
        in_specs=[smem, pl.BlockSpec((tb, D), lambda b_, k: (b_ * nblk + k, 0)), es, es],
        out_specs=[pl.BlockSpec((None, E, rows, D), lambda b_, k: (b_, 0, 0, 0)),
                   pl.BlockSpec((None, E, rows, 1), lambda b_, k: (b_, 0, 0, 0))],
        out_shape=[_sds((B, E, rows, D), BF16), _sds((B, E, rows, 1), F32)],
        compiler_params=_cp(("parallel", "arbitrary"), 48),
        name="moe_gather",
    )(rng, hm, slot, gate)
    wspec = lambda a, b: pl.BlockSpec((None, a, b), lambda e, b_: (e, 0, 0))
    yw = pl.pallas_call(
        functools.partial(_moe_ffn_body, ncap=ncap),
        grid=(E, B),
        in_specs=[pl.BlockSpec((None, None, rows, D), lambda e, b_: (b_, e, 0, 0)),
                  pl.BlockSpec((None, None, rows, 1), lambda e, b_: (b_, e, 0, 0)),
                  wspec(D, Fd), wspec(D, Fd), wspec(Fd, D)],
        out_specs=pl.BlockSpec((None, None, rows, D), lambda e, b_: (b_, e, 0, 0)),
        out_shape=_sds((B, E, rows, D), BF16),
        scratch_shapes=[pltpu.VMEM((D, Fd), BF16), pltpu.VMEM((D, Fd), BF16), pltpu.VMEM((Fd, D), BF16)],
        compiler_params=_cp(("parallel", "arbitrary"), 48),
        name="moe_ffn",
    )(xg, gv, wg, wu, wd)
    return pl.pallas_call(
        functools.partial(_moe_scatter_body, ncap=ncap, nbc=nbc),
        grid=(B, nblk),
        in_specs=[smem, es, pl.BlockSpec((None, E, rows, D), lambda b_, k: (b_, 0, 0, 0)),
                  pl.BlockSpec((None, 1, D), lambda b_, k: (b_, 0, 5)),
                  pl.BlockSpec((None, 1, D), lambda b_, k: (B, 0, 5))],
        out_specs=pl.BlockSpec((tb, D), lambda b_, k: (b_ * nblk + k, 0)),
        out_shape=_sds((N, D), F32),
        compiler_params=_cp(("parallel", "arbitrary"), 48),
        name="moe_scatter",
    )(rng, slot, yw, mods3, mods3)


def _final_body(x_ref, r_ref, g_ref, o_ref):
    x = x_ref[...] + r_ref[...]
    o_ref[...] = x * lax.rsqrt(jnp.mean(x * x, axis=-1, keepdims=True) + RMS_EPS) * g_ref[...]


def _final_norm(xa, res, g, B, S, Sc):
    N, D = xa.shape
    tm, nb, nbc = _row_cfg(S, Sc)
    nl = nb - nbc
    row = pl.BlockSpec((tm, D), lambda b_, i: (b_ * nb + nbc + i, 0))
    return pl.pallas_call(
        _final_body,
        grid=(B, nl),
        in_specs=[row, row, pl.BlockSpec((1, D), lambda b_, i: (0, 0))],
        out_specs=pl.BlockSpec((tm, D), lambda b_, i: (b_ * nl + i, 0)),
        out_shape=_sds((B * (S - Sc), D), F32),
        compiler_params=_cp(("parallel", "parallel")),
        name="final_norm",
    )(xa, res, g)


def _layer(xa, res, mods3, lp, tabs, B, S, Sc, ctx_out):
    N, D = xa.shape
    rc = _row_cfg(S, Sc)
    w_in = lp["w_in"]
    seg = lambda a, n: w_in[:, a:a + n]
    o_q, o_k, o_v, o_z = 0, ATTN_Q_DIM, ATTN_Q_DIM + ATTN_KV_DIM, ATTN_Q_DIM + 2 * ATTN_KV_DIM
    o_xbc = o_z + SSD_INNER
    o_dt = o_xbc + SSD_CONV_DIM
    o_u = o_dt + SSD_HEADS
    o_sb, o_sg, o_sh = o_u + S5_WIDTH, o_u + S5_WIDTH + SC_WIDTH, o_u + S5_WIDTH + 2 * SC_WIDTH
    o_g = o_sh + SC_WIDTH
    w_a = jnp.concatenate([seg(o_u, S5_WIDTH), seg(o_sb, SC_WIDTH), seg(o_sg, SC_WIDTH), seg(o_sh, SC_WIDTH),
                           seg(o_xbc, SSD_CONV_DIM), seg(o_k, ATTN_KV_DIM), seg(o_v, ATTN_KV_DIM),
                           seg(o_q, ATTN_Q_DIM), seg(o_z, SSD_INNER)], axis=1).astype(BF16)
    w_g = seg(o_g, 4 * D).astype(BF16)
    w_dt = jnp.pad(seg(o_dt, SSD_HEADS), ((0, 0), (0, 128 - SSD_HEADS))).astype(BF16)

    xa, hn = _normmod(xa, res, lp["norm_mix_g"][None, :], mods3, B, rc, 0, 1)
    tm = math.gcd(N, 1024)
    proj_a = _matmul(hn, w_a, BF16, tm, 512, "in_proj_a")
    proj_g = _matmul(hn, w_g, BF16, tm, 512, "in_proj_g")
    dt = _matmul(hn, w_dt, F32, tm, 128, "in_proj_dt")

    ya = _attention(proj_a, tabs, lp["q_norm_g"], lp["k_norm_g"], B, S, Sc)

    xbc_c = _ssdconv(proj_a, lp["ssd_conv_w"], lp["ssd_conv_b"][None, :], B, S, Sc)
    yf, yb = _ssd(xbc_c, dt, lp["ssd_dt_bias"], lp["ssd_a_log"], B, S, Sc)

    Lc, Gn, G = S5_CHUNK, S5_GROUPS, S5_GROUP_DIM
    nc5, ncc5 = S // Lc, Sc // Lc
    w5, so5, ar5 = _s5gen(lp["s5_lambda_re"], lp["s5_lambda_im"], lp["s5_log_step"], lp["s5_b_re"], lp["s5_b_im"],
                          lp["s5_c_re"], lp["s5_c_im"], lp["s5_d"])
    u = proj_a[:, PA_U:PA_U + S5_WIDTH].reshape(B, nc5, Lc, S5_WIDTH)
    up = jnp.transpose(u, (2, 1, 0, 3)).reshape(N, S5_WIDTH)
    tr5 = nc5 * B
    y5t = _s5(_transpose2d(up, tr5, "s5_in_t"), w5, so5, ar5, B, nc5, ncc5)
    y5p = _untranspose2d(y5t, tr5, "s5_out_t").reshape(Lc, nc5, B, S5_WIDTH)
    y5 = jnp.transpose(y5p, (2, 1, 0, 3)).reshape(N, S5_WIDTH)

    yd = _shortconv(proj_a, lp["sc_conv_w"], B, S, Sc)

    dsk = jnp.repeat(lp["ssd_d"], SSD_HEAD_DIM)[None, :]
    x1 = _merge(xa, ya, yf, yb, xbc_c, proj_a, y5, yd, proj_g, mods3,
                lp["w_br_attn"].astype(BF16), lp["w_br_ssd"].astype(BF16), lp["w_br_s5"].astype(BF16),
                lp["w_br_sc"].astype(BF16), lp["w_out"].astype(BF16), lp["s5_w_glu"].astype(BF16),
                lp["s5_b_glu"][None, :], lp["ssd_norm_g"][None, :], dsk, B, rc)

    wr_pad = jnp.pad(lp["w_router"], ((0, 0), (0, 128 - N_EXPERTS)))
    hm, slot, gate, rng = _router(x1, lp["norm_ffn_g"][None, :], mods3, wr_pad, B, S, Sc, ctx_out)
    moe = _experts(hm, slot, gate, rng, lp["w_exp_gate"], lp["w_exp_up"], lp["w_exp_down"], mods3, B, S, Sc, ctx_out)
    return x1, moe


def kernel(x, c, ctx, c_ctx, w_mod, b_mod, norm_mix_g, norm_ffn_g, w_in, q_norm_g, k_norm_g, ssd_conv_w, ssd_conv_b, ssd_dt_bias, ssd_a_log, ssd_d, ssd_norm_g, s5_lambda_re, s5_lambda_im, s5_log_step, s5_b_re, s5_b_im, s5_c_re, s5_c_im, s5_d, s5_w_glu, s5_b_glu, sc_conv_w, w_br_attn, w_br_ssd, w_br_s5, w_br_sc, w_out, w_router, w_exp_gate, w_exp_up, w_exp_down, final_norm_g):
    B, T, D = x.shape
    Sc = ctx.shape[1]
    S = Sc + T
    depth = w_in.shape[0]
    xa = jnp.concatenate([ctx, x], axis=1).reshape(B * S, D)
    cc = jnp.zeros((16, D), F32).at[0:B].set(c).at[B].set(c_ctx)
    cos, sin = _rope_tables(T, Sc)
    tabs = (jnp.tile(cos, (1, ATTN_HEADS)), jnp.tile(sin, (1, ATTN_HEADS)),
            jnp.tile(cos, (1, ATTN_KV_HEADS)), jnp.tile(sin, (1, ATTN_KV_HEADS)),
            _block_diag_ones(ATTN_Q_DIM), _block_diag_ones(ATTN_KV_DIM))
    stacked = dict(
        w_in=w_in, norm_mix_g=norm_mix_g, norm_ffn_g=norm_ffn_g, q_norm_g=q_norm_g, k_norm_g=k_norm_g,
        ssd_conv_w=ssd_conv_w, ssd_conv_b=ssd_conv_b, ssd_dt_bias=ssd_dt_bias, ssd_a_log=ssd_a_log, ssd_d=ssd_d,
        ssd_norm_g=ssd_norm_g, s5_lambda_re=s5_lambda_re, s5_lambda_im=s5_lambda_im, s5_log_step=s5_log_step,
        s5_b_re=s5_b_re, s5_b_im=s5_b_im, s5_c_re=s5_c_re, s5_c_im=s5_c_im, s5_d=s5_d, s5_w_glu=s5_w_glu,
        s5_b_glu=s5_b_glu, sc_conv_w=sc_conv_w, w_br_attn=w_br_attn, w_br_ssd=w_br_ssd, w_br_s5=w_br_s5,
        w_br_sc=w_br_sc, w_out=w_out, w_router=w_router, w_exp_gate=w_exp_gate, w_exp_up=w_exp_up,
        w_exp_down=w_exp_down)
    res = None
    for i in range(depth):
        lp = {k: v[i] for k, v in stacked.items()}
        mods3 = _mods(cc, w_mod[i], b_mod[i][None, :]).reshape(16, 1, 6 * D)
        xa, res = _layer(xa, res, mods3, lp, tabs, B, S, Sc, ctx_out=i < depth - 1)
    return _final_norm(xa, res, final_norm_g[None, :], B, S, Sc).reshape(B, T, D)
```

```python
import functools
import math

import jax
import jax.numpy as jnp
import numpy as np
from jax import lax
from jax.experimental import pallas as pl
from jax.experimental.pallas import tpu as pltpu

F32 = jnp.float32
BF16 = jnp.bfloat16
I32 = jnp.int32
HI = lax.Precision.HIGHEST

RMS_EPS = 1e-6
GRID_W = 64
ROPE_THETA = 10000.0
HEAD_DIM = 64
ATTN_HEADS = 8
ATTN_KV_HEADS = 2
SSD_HEADS = 8
SSD_HEAD_DIM = 64
SSD_GROUPS = 2
SSD_STATE = 64
SSD_CHUNK = 128
S5_GROUPS = 24
S5_GROUP_DIM = 16
S5_STATE = 64
S5_CHUNK = 16
SC_WIDTH = 384
N_EXPERTS = 16
EC_CAPACITY = 2
MAX_ROW_BLOCK = 256
SLOT_WINDOW = 64
SLOT_ALIGN = 16

ATTN_Q_DIM = ATTN_HEADS * HEAD_DIM
ATTN_KV_DIM = ATTN_KV_HEADS * HEAD_DIM
SSD_INNER = SSD_HEADS * SSD_HEAD_DIM
SSD_BC_DIM = SSD_GROUPS * SSD_STATE
SSD_CONV_DIM = SSD_INNER + 2 * SSD_BC_DIM
S5_WIDTH = S5_GROUPS * S5_GROUP_DIM

PA_U, PA_SB, PA_SG, PA_SH = 0, 384, 768, 1152
PA_XBC, PA_K, PA_V, PA_Q, PA_Z, PA_W = 1536, 2304, 2432, 2560, 3072, 3584


def _sds(shape, dtype):
    return jax.ShapeDtypeStruct(shape, dtype)


def _cp(sem, vmem_mb=None):
    kw = dict(dimension_semantics=sem)
    if vmem_mb is not None:
        kw["vmem_limit_bytes"] = vmem_mb << 20
    return pltpu.CompilerParams(**kw)


def _silu(x):
    return x * jax.nn.sigmoid(x)


def _mods_body(c_ref, w_ref, b_ref, o_ref):
    s = _silu(c_ref[...])
    o_ref[...] = jnp.dot(s.astype(BF16), w_ref[...].astype(BF16), preferred_element_type=F32) + b_ref[...]


def _mods(cc, w, b, li):
    R, D = cc.shape
    N = w.shape[2]
    tn = 1536
    return pl.pallas_call(
        _mods_body,
        grid=(N // tn,),
        in_specs=[pl.BlockSpec((R, D), lambda j: (0, 0)),
                  pl.BlockSpec((None, D, tn), lambda j: (li, 0, j)),
                  pl.BlockSpec((None, 1, tn), lambda j: (li, 0, j))],
        out_specs=pl.BlockSpec((R, tn), lambda j: (0, j)),
        out_shape=_sds((R, N), F32),
        compiler_params=_cp(("parallel",), 40),
        name="mods",
    )(cc, w, b)


def _row_cfg(S, Sc):
    tm = math.gcd(math.gcd(Sc, S - Sc), MAX_ROW_BLOCK)
    return tm, S // tm, Sc // tm


def _mod_spec(D, rc, B, chunk):
    _, nb, nbc = rc
    return pl.BlockSpec((None, 1, D), lambda i: (jnp.where(i % nb < nbc, B, i // nb), 0, chunk))


def _normmod_body(x_ref, g_ref, sh_ref, sc_ref, o_ref):
    x = x_ref[...]
    ms = jnp.mean(x * x, axis=-1, keepdims=True)
    y = x * lax.rsqrt(ms + RMS_EPS) * g_ref[...]
    o_ref[...] = (y * (1.0 + sc_ref[...]) + sh_ref[...]).astype(o_ref.dtype)


def _normmod_res_body(x_ref, r_ref, g_ref, sh_ref, sc_ref, xo_ref, o_ref):
    x = x_ref[...] + r_ref[...]
    xo_ref[...] = x
    ms = jnp.mean(x * x, axis=-1, keepdims=True)
    y = x * lax.rsqrt(ms + RMS_EPS) * g_ref[...]
    o_ref[...] = (y * (1.0 + sc_ref[...]) + sh_ref[...]).astype(o_ref.dtype)


def _normmod(xa, res, g, mods3, B, rc, shift_chunk, scale_chunk):
    N, D = xa.shape
    tm = rc[0]
    row = pl.BlockSpec((tm, D), lambda i: (i, 0))
    tail = [pl.BlockSpec((1, D), lambda i: (0, 0)), _mod_spec(D, rc, B, shift_chunk), _mod_spec(D, rc, B, scale_chunk)]
    if res is None:
        hn = pl.pallas_call(
            _normmod_body, grid=(N // tm,), in_specs=[row] + tail, out_specs=row,
            out_shape=_sds((N, D), BF16), compiler_params=_cp(("parallel",)), name="normmod",
        )(xa, g, mods3, mods3)
        return xa, hn
    return pl.pallas_call(
        _normmod_res_body, grid=(N // tm,), in_specs=[row, row] + tail, out_specs=[row, row],
        out_shape=[_sds((N, D), F32), _sds((N, D), BF16)], compiler_params=_cp(("parallel",)), name="normmod_res",
    )(xa, res, g, mods3, mods3)


def _mm_body(a_ref, w_ref, o_ref):
    o_ref[...] = jnp.dot(a_ref[...], w_ref[...], preferred_element_type=F32).astype(o_ref.dtype)


def _matmul(a, w, out_dtype, tm, tn, name):
    M, K = a.shape
    N = w.shape[1]
    return pl.pallas_call(
        _mm_body,
        grid=(M // tm, N // tn),
        in_specs=[pl.BlockSpec((tm, K), lambda i, j: (i, 0)),
                  pl.BlockSpec((K, tn), lambda i, j: (0, j))],
        out_specs=pl.BlockSpec((tm, tn), lambda i, j: (i, j)),
        out_shape=_sds((M, N), out_dtype),
        compiler_params=_cp(("parallel", "arbitrary"), 40),
        name=name,
    )(a, w)


def _shifted(x, Sc):
    S = x.shape[0]
    t = lax.broadcasted_iota(I32, (S, 1), 0)
    prev = jnp.where((t == 0) | (t == Sc), 0.0, pltpu.roll(x, 1, 0))
    nxt = jnp.where((t == Sc - 1) | (t == S - 1), 0.0, pltpu.roll(x, S - 1, 0))
    return prev, nxt


def _ssdconv_body(x_ref, w_ref, b_ref, o_ref, *, Sc):
    x = x_ref[...].astype(F32)
    prev, nxt = _shifted(x, Sc)
    w = w_ref[...]
    y = w[0:1, :] * prev + w[1:2, :] * x + w[2:3, :] * nxt + b_ref[...]
    o_ref[...] = _silu(y).astype(o_ref.dtype)


def _ssdconv(proj_a, w, b, B, S, Sc):
    C = 384
    j0 = PA_XBC // C
    return pl.pallas_call(
        functools.partial(_ssdconv_body, Sc=Sc),
        grid=(B, SSD_CONV_DIM // C),
        in_specs=[pl.BlockSpec((S, C), lambda b_, j: (b_, j0 + j)),
                  pl.BlockSpec((3, C), lambda b_, j: (0, j)),
                  pl.BlockSpec((1, C), lambda b_, j: (0, j))],
        out_specs=pl.BlockSpec((S, C), lambda b_, j: (b_, j)),
        out_shape=_sds((B * S, SSD_CONV_DIM), BF16),
        compiler_params=_cp(("parallel", "parallel"), 40),
        name="ssdconv",
    )(proj_a, w, b)


def _shortconv_body(sb_ref, sg_ref, sh_ref, w_ref, o_ref, *, Sc):
    x = sg_ref[...].astype(F32) * sh_ref[...].astype(F32)
    prev, nxt = _shifted(x, Sc)
    w = w_ref[...]
    y = w[0:1, :] * prev + w[1:2, :] * x + w[2:3, :] * nxt
    o_ref[...] = (sb_ref[...].astype(F32) * y).astype(o_ref.dtype)


def _shortconv(proj_a, w, B, S, Sc):
    C = SC_WIDTH
    return pl.pallas_call(
        functools.partial(_shortconv_body, Sc=Sc),
        grid=(B,),
        in_specs=[pl.BlockSpec((S, C), lambda b_: (b_, PA_SB // C)),
                  pl.BlockSpec((S, C), lambda b_: (b_, PA_SG // C)),
                  pl.BlockSpec((S, C), lambda b_: (b_, PA_SH // C)),
                  pl.BlockSpec((3, C), lambda b_: (0, 0))],
        out_specs=pl.BlockSpec((S, C), lambda b_: (b_, 0)),
        out_shape=_sds((B * S, C), BF16),
        compiler_params=_cp(("parallel",), 40),
        name="shortconv",
    )(proj_a, proj_a, proj_a, w)


def _norm_rope(x, g, cos, sin, bd, scale):
    W = x.shape[1]
    sq = x * x
    hi = sq.astype(BF16)
    lo = (sq - hi.astype(F32)).astype(BF16)
    ssq = jnp.dot(hi, bd, preferred_element_type=F32) + jnp.dot(lo, bd, preferred_element_type=F32)
    y = x * lax.rsqrt(ssq * (1.0 / HEAD_DIM) + RMS_EPS) * g
    lane = lax.broadcasted_iota(I32, (1, W), 1)
    first = (lane % 32) < 16
    partner = jnp.where(first, pltpu.roll(y, W - 16, 1), pltpu.roll(y, 16, 1))
    return (y * cos + partner * sin) * scale


def _attn_body(q_ref, k_ref, v_ref, cq_ref, sq_ref, ck_ref, sk_ref, gq_ref, gk_ref, bdq_ref, bdk_ref,
               o_ref, kh_ref, ve_ref, *, Sc, S, tq):
    qb = pl.program_id(1)
    hd = HEAD_DIM

    @pl.when(qb == 0)
    def _():
        k = k_ref[...].astype(F32)
        kh_ref[...] = _norm_rope(k, gk_ref[...], ck_ref[...], sk_ref[...], bdk_ref[...], 1.0).astype(BF16)
        v = v_ref[...].astype(F32)
        lane = lax.broadcasted_iota(I32, (1, 2 * hd), 1)
        ve_ref[0] = jnp.where(lane < hd, v, 1.0).astype(BF16)
        ve_ref[1] = jnp.where(lane < hd, pltpu.roll(v, hd, 1), 1.0).astype(BF16)

    q = q_ref[...].astype(F32)
    qh = _norm_rope(q, gq_ref[...], cq_ref[...], sq_ref[...], bdq_ref[...],
                    HEAD_DIM ** -0.5 * math.log2(math.e)).astype(BF16)
    rep = ATTN_HEADS // ATTN_KV_HEADS
    nt = (((1,), (1,)), ((), ()))

    def attend(splits):
        scores = []
        for g in range(ATTN_KV_HEADS):
            qg = jnp.concatenate([qh[:, (g * rep + r) * hd:(g * rep + r + 1) * hd] for r in range(rep)], axis=0)
            scores.append([lax.dot_general(qg, kh_ref[a:b, g * hd:(g + 1) * hd], nt, preferred_element_type=F32)
                           for a, b in splits])
        for g in range(ATTN_KV_HEADS):
            m = functools.reduce(jnp.maximum, [jnp.max(s, axis=-1, keepdims=True) for s in scores[g]])
            acc = None
            for s, (a, b) in zip(scores[g], splits):
                p = jnp.exp2(s - m).astype(BF16)
                part = jnp.dot(p, ve_ref[g, a:b, :], preferred_element_type=F32)
                acc = part if acc is None else acc + part
            o = acc[:, 0:hd] / acc[:, hd:hd + 1]
            for r in range(rep):
                h = g * rep + r
                o_ref[:, h * hd:(h + 1) * hd] = o[r * tq:(r + 1) * tq].astype(o_ref.dtype)

    @pl.when(qb < Sc // tq)
    def _():
        attend([(0, Sc)])

    @pl.when(qb >= Sc // tq)
    def _():
        half = (S // 2 + 255) // 256 * 256 if S >= 512 else S
        attend([(0, half), (half, S)] if half < S else [(0, S)])


def _rope_tables(T, Sc):
    rows = T // GRID_W
    row = np.repeat(np.arange(rows, dtype=np.float32), GRID_W)
    col = np.tile(np.arange(GRID_W, dtype=np.float32), rows)
    half = HEAD_DIM // 2
    inv = jnp.asarray(ROPE_THETA, F32) ** (-jnp.arange(0, half, 2, dtype=F32) / half)
    ra = jnp.asarray(row)[:, None] * inv
    ca = jnp.asarray(col)[:, None] * inv
    cos = jnp.concatenate([jnp.cos(ra), jnp.cos(ra), jnp.cos(ca), jnp.cos(ca)], axis=1)
    sin = jnp.concatenate([-jnp.sin(ra), jnp.sin(ra), -jnp.sin(ca), jnp.sin(ca)], axis=1)
    cos = jnp.concatenate([jnp.ones((Sc, HEAD_DIM), F32), cos], axis=0)
    sin = jnp.concatenate([jnp.zeros((Sc, HEAD_DIM), F32), sin], axis=0)
    return cos, sin


def _block_diag_ones(W):
    i = np.arange(W) // HEAD_DIM
    return jnp.asarray((i[:, None] == i[None, :]).astype(np.float32), BF16)


def _attention(proj_a, tabs, q_norm_g, k_norm_g, B, S, Sc):
    cos_q, sin_q, cos_k, sin_k, bdq, bdk = tabs
    tq = 128
    nq = S // tq
    gq = jnp.tile(q_norm_g, ATTN_HEADS)[None, :]
    gk = jnp.tile(k_norm_g, ATTN_KV_HEADS)[None, :]
    const = lambda b_, i: (0, 0)
    return pl.pallas_call(
        functools.partial(_attn_body, Sc=Sc, S=S, tq=tq),
        grid=(B, nq),
        in_specs=[pl.BlockSpec((tq, ATTN_Q_DIM), lambda b_, i: (b_ * nq + i, PA_Q // ATTN_Q_DIM)),
                  pl.BlockSpec((S, ATTN_KV_DIM), lambda b_, i: (b_, PA_K // ATTN_KV_DIM)),
                  pl.BlockSpec((S, ATTN_KV_DIM), lambda b_, i: (b_, PA_V // ATTN_KV_DIM)),
                  pl.BlockSpec((tq, ATTN_Q_DIM), lambda b_, i: (i, 0)),
                  pl.BlockSpec((tq, ATTN_Q_DIM), lambda b_, i: (i, 0)),
                  pl.BlockSpec((S, ATTN_KV_DIM), const),
                  pl.BlockSpec((S, ATTN_KV_DIM), const),
                  pl.BlockSpec((1, ATTN_Q_DIM), const),
                  pl.BlockSpec((1, ATTN_KV_DIM), const),
                  pl.BlockSpec((ATTN_Q_DIM, ATTN_Q_DIM), const),
                  pl.BlockSpec((ATTN_KV_DIM, ATTN_KV_DIM), const)],
        out_specs=pl.BlockSpec((tq, ATTN_Q_DIM), lambda b_, i: (b_ * nq + i, 0)),
        out_shape=_sds((B * S, ATTN_Q_DIM), BF16),
        scratch_shapes=[pltpu.VMEM((S, ATTN_KV_DIM), BF16),
                        pltpu.VMEM((ATTN_KV_HEADS, S, 2 * HEAD_DIM), BF16)],
        compiler_params=_cp(("parallel", "arbitrary"), 48),
        name="attention",
    )(proj_a, proj_a, proj_a, cos_q, sin_q, cos_k, sin_k, gq, gk, bdq, bdk)


def _ssd_dir(xbc_ref, dt_ref, bias, a_neg, h_ref, y_ref, rev):
    L = SSD_CHUNK
    P, Nst = SSD_HEAD_DIM, SSD_STATE
    xbc = xbc_ref[...]
    x = xbc[:, 0:SSD_INNER]
    bm = xbc[:, SSD_INNER:SSD_INNER + SSD_BC_DIM]
    cm = xbc[:, SSD_INNER + SSD_BC_DIM:SSD_CONV_DIM]
    dt = jax.nn.softplus(dt_ref[...] + bias)
    a = dt * a_neg
    row = lax.broadcasted_iota(I32, (L, L), 0)
    col = lax.broadcasted_iota(I32, (L, L), 1)
    tri = (row >= col).astype(F32)
    cum = jnp.dot(tri, a, precision=HI, preferred_element_type=F32)
    e = cum - a if rev else cum
    eT = e.T
    dtT = dt.T
    bT = bm.astype(F32).T.astype(BF16)
    mask = (col >= row) if rev else (row >= col)
    rep = SSD_HEADS // SSD_GROUPS
    for g in range(SSD_GROUPS):
        cg = cm[:, g * Nst:(g + 1) * Nst]
        bTg = bT[g * Nst:(g + 1) * Nst, :]
        sg = jnp.dot(cg, bTg, preferred_element_type=F32)
        for r in range(rep):
            h = g * rep + r
            ecol = e[:, h:h + 1]
            erow = eT[h:h + 1, :]
            tot = cum[L - 1:L, h:h + 1]
            diff = (erow - ecol) if rev else (ecol - erow)
            dec = jnp.where(mask, jnp.exp(jnp.minimum(diff, 0.0)), 0.0)
            m = (sg * dec * dtT[h:h + 1, :]).astype(BF16)
            xh = x[:, h * P:(h + 1) * P]
            y = jnp.dot(m, xh, preferred_element_type=F32)
            hh = h_ref[h]
            win = jnp.exp(tot - ecol) if rev else jnp.exp(ecol)
            y = y + win * jnp.dot(cg, hh.astype(BF16), preferred_element_type=F32)
            wrow = (jnp.exp(erow) if rev else jnp.exp(tot - erow)) * dtT[h:h + 1, :]
            bw = (bTg.astype(F32) * wrow).astype(BF16)
            h_ref[h] = jnp.exp(tot) * hh + jnp.dot(bw, xh, preferred_element_type=F32)
            y_ref[:, h * P:(h + 1) * P] = y.astype(y_ref.dtype)


def _ssd_body(xf_ref, dtf_ref, xb_ref, dtb_ref, bias_ref, alog_ref, yf_ref, yb_ref, h_ref, *, nper):
    @pl.when(pl.program_id(1) == 0)
    def _():
        h_ref[...] = jnp.zeros_like(h_ref)

    lane = lax.broadcasted_iota(I32, (1, 128), 1)
    a_neg = jnp.where(lane < SSD_HEADS, -jnp.exp(alog_ref[...]), 0.0)
    bias = bias_ref[...]
    for p in range(nper):
        _ssd_dir(xf_ref.at[p], dtf_ref.at[p], bias[0:1, :], a_neg[0:1, :], h_ref.at[p, 0], yf_ref.at[p], False)
        _ssd_dir(xb_ref.at[p], dtb_ref.at[p], bias[1:2, :], a_neg[1:2, :], h_ref.at[p, 1], yb_ref.at[p], True)


def _ssd(xbc_c, dt, dt_bias, a_log, B, S, Sc):
    L = SSD_CHUNK
    nc, ncc = S // L, Sc // L
    nper = 2 if B % 2 == 0 else 1
    bg = B // nper
    padh = 128 - SSD_HEADS
    bias = jnp.pad(dt_bias, ((0, 0), (0, padh)))
    alog = jnp.pad(a_log, ((0, 0), (0, padh)))
    x3 = xbc_c.reshape(nper, bg * S, SSD_CONV_DIM)
    dt3 = dt.reshape(nper, bg * S, 128)

    def fwd(b_, i):
        return (0, b_ * nc + i, 0)

    def bwd(b_, i):
        return (0, b_ * nc + jnp.where(i < ncc, ncc - 1 - i, nc + ncc - 1 - i), 0)

    const = lambda b_, i: (0, 0)
    out = _sds((nper, bg * S, SSD_INNER), BF16)
    yf, yb = pl.pallas_call(
        functools.partial(_ssd_body, nper=nper),
        grid=(bg, nc),
        in_specs=[pl.BlockSpec((nper, L, SSD_CONV_DIM), fwd), pl.BlockSpec((nper, L, 128), fwd),
                  pl.BlockSpec((nper, L, SSD_CONV_DIM), bwd), pl.BlockSpec((nper, L, 128), bwd),
                  pl.BlockSpec((2, 128), const), pl.BlockSpec((2, 128), const)],
        out_specs=[pl.BlockSpec((nper, L, SSD_INNER), fwd), pl.BlockSpec((nper, L, SSD_INNER), bwd)],
        out_shape=[out, out],
        scratch_shapes=[pltpu.VMEM((nper, 2, SSD_HEADS, SSD_STATE, SSD_HEAD_DIM), F32)],
        compiler_params=_cp(("parallel", "arbitrary")),
        name="ssd",
    )(x3, dt3, x3, dt3, bias, alog)
    return yf.reshape(B * S, SSD_INNER), yb.reshape(B * S, SSD_INNER)


def _cpow(n, lr, li, st):
    mag = jnp.exp(n * lr * st)
    ang = n * li * st
    return mag * jnp.cos(ang), mag * jnp.sin(ang)


def _zoh_coef(lr, li, st):
    ar, ai = _cpow(1.0, lr, li, st)
    nr, ni = ar - 1.0, ai
    den = lr * lr + li * li
    return (nr * lr + ni * li) / den, (ni * lr - nr * li) / den


def _s5gen_body(colp_ref, rowp_ref, ctr_ref, cti_ref, btr_ref, bti_ref, blr_ref, bli_ref, d_ref,
                wt_ref, so_ref, ar_ref, ws_ref, sos_ref):
    Lc, G, P = S5_CHUNK, S5_GROUP_DIM, S5_STATE
    NS = 2 * Lc + 1
    colp = colp_ref[...]
    rowp = rowp_ref[...]
    lrf, lif, stf = colp[:, 0:1], colp[:, 1:2], jnp.exp(colp[:, 2:3])
    lrb, lib, stb = colp[:, 3:4], colp[:, 4:5], jnp.exp(colp[:, 5:6])
    lane = lax.broadcasted_iota(I32, (1, NS * G), 1)
    slot = lane // G
    isb = slot < Lc
    lag = jnp.abs(slot - Lc).astype(F32)
    pr, pi = _cpow(lag, jnp.where(isb, lrb, lrf), jnp.where(isb, lib, lif), jnp.where(isb, stb, stf))
    ctr, cti = ctr_ref[...], cti_ref[...]
    er = ctr * pr - cti * pi
    ei = ctr * pi + cti * pr
    rlrf, rlif, rstf = rowp[0:1, :], rowp[1:2, :], jnp.exp(rowp[2:3, :])
    rlrb, rlib, rstb = rowp[3:4, :], rowp[4:5, :], jnp.exp(rowp[5:6, :])
    btr, bti = btr_ref[...], bti_ref[...]
    cfr, cfi = _zoh_coef(rlrf, rlif, rstf)
    cbr, cbi = _zoh_coef(rlrb, rlib, rstb)
    bbf_r, bbf_i = cfr * btr - cfi * bti, cfr * bti + cfi * btr
    bbb_r, bbb_i = cbr * btr - cbi * bti, cbr * bti + cbi * btr

    def kt(br, bi):
        return (jnp.dot(br, er, precision=HI, preferred_element_type=F32)
                - jnp.dot(bi, ei, precision=HI, preferred_element_type=F32))

    ktf, ktb = kt(bbf_r, bbf_i), kt(bbb_r, bbb_i)
    ii = lax.broadcasted_iota(I32, (G, NS * G), 0)
    dmat = jnp.where((slot == Lc) & (ii == lane - Lc * G), d_ref[...], 0.0)
    strip = jnp.where(slot == Lc, ktf + ktb, jnp.where(isb, ktb, ktf)) + dmat
    for s in range(Lc):
        off = (Lc - s) * G
        ws_ref[s * G:(s + 1) * G, :] = strip[:, off:off + Lc * G]
    fo = (Lc + 1) * G
    ws_ref[Lc * G:Lc * G + P, :] = er[:, fo:fo + Lc * G]
    ws_ref[Lc * G + P:Lc * G + 2 * P, :] = er[:, 0:Lc * G]
    ws_ref[Lc * G + 2 * P:Lc * G + 3 * P, :] = -ei[:, fo:fo + Lc * G]
    ws_ref[Lc * G + 3 * P:Lc * G + 4 * P, :] = -ei[:, 0:Lc * G]
    wt_ref[:, 0:Lc * G] = ws_ref[0:Lc * G, :].T.astype(wt_ref.dtype)
    wt_ref[:, Lc * G:Lc * G + 4 * P] = ws_ref[Lc * G:Lc * G + 4 * P, :].T.astype(wt_ref.dtype)
    lane2 = lax.broadcasted_iota(I32, (1, Lc * G), 1)
    s_idx = (lane2 // G).astype(F32)
    qfr, qfi = _cpow((Lc - 1.0) - s_idx, lrf, lif, stf)
    qbr, qbi = _cpow(s_idx, lrb, lib, stb)
    ccfr, ccfi = _zoh_coef(lrf, lif, stf)
    ccbr, ccbi = _zoh_coef(lrb, lib, stb)
    blr, bli = blr_ref[...], bli_ref[...]
    bfr, bfi = ccfr * blr - ccfi * bli, ccfr * bli + ccfi * blr
    bbr, bbi = ccbr * blr - ccbi * bli, ccbr * bli + ccbi * blr
    sos_ref[0:P, :] = qfr * bfr - qfi * bfi
    sos_ref[P:2 * P, :] = qbr * bbr - qbi * bbi
    sos_ref[2 * P:3 * P, :] = qfr * bfi + qfi * bfr
    sos_ref[3 * P:4 * P, :] = qbr * bbi + qbi * bbr
    so_ref[...] = sos_ref[...].T.astype(so_ref.dtype)
    afr, afi = _cpow(float(Lc), rlrf, rlif, rstf)
    abr, abi = _cpow(float(Lc), rlrb, rlib, rstb)
    ar_ref[...] = jnp.zeros_like(ar_ref)
    ar_ref[0:1, 0:P] = afr
    ar_ref[0:1, P:2 * P] = abr
    ar_ref[1:2, 0:P] = afi
    ar_ref[1:2, P:2 * P] = abi


def _s5gen(lam_re, lam_im, log_step, b_re, b_im, c_re, c_im, d_skip):
    Gn, P, G, Lc = S5_GROUPS, S5_STATE, S5_GROUP_DIM, S5_CHUNK
    NS = 2 * Lc + 1
    ls = jnp.broadcast_to(log_step[:, :, None], (2, Gn, P))
    z = jnp.zeros((Gn, P), F32)
    rowp = jnp.stack([lam_re[0], lam_im[0], ls[0], lam_re[1], lam_im[1], ls[1], z, z], axis=1)
    colp = jnp.swapaxes(rowp, 1, 2)
    ctr = jnp.tile(jnp.swapaxes(c_re, 1, 2), (1, 1, NS))
    cti = jnp.tile(jnp.swapaxes(c_im, 1, 2), (1, 1, NS))
    btr = jnp.swapaxes(b_re, 1, 2)
    bti = jnp.swapaxes(b_im, 1, 2)
    blr = jnp.tile(b_re, (1, 1, Lc))
    bli = jnp.tile(b_im, (1, 1, Lc))
    dt = jnp.tile(d_skip.reshape(Gn, 1, G), (1, 1, NS))
    g3 = lambda a, b: pl.BlockSpec((None, a, b), lambda g: (g, 0, 0))
    return pl.pallas_call(
        _s5gen_body,
        grid=(Gn,),
        in_specs=[g3(P, 8), g3(8, P), g3(P, NS * G), g3(P, NS * G), g3(G, P), g3(G, P),
                  g3(P, Lc * G), g3(P, Lc * G), g3(1, NS * G)],
        out_specs=[g3(Lc * G, Lc * G + 4 * P), g3(Lc * G, 4 * P), g3(8, 2 * P)],
        out_shape=[_sds((Gn, Lc * G, Lc * G + 4 * P), BF16), _sds((Gn, Lc * G, 4 * P), BF16),
                   _sds((Gn, 8, 2 * P), F32)],
        scratch_shapes=[pltpu.VMEM((Lc * G + 4 * P, Lc * G), F32), pltpu.VMEM((4 * P, Lc * G), F32)],
        compiler_params=_cp(("parallel",)),
        name="s5gen",
    )(colp, rowp, ctr, cti, btr, bti, blr, bli, dt)


def _s5_body(xt_ref, wt_ref, so_ref, ar_ref, y_ref, v_ref, h_ref, *, B, nc, ncc):
    P, Lc, G = S5_STATE, S5_CHUNK, S5_GROUP_DIM
    LG = Lc * G
    R = nc * B
    ucol = jnp.concatenate([xt_ref[:, l * R:(l + 1) * R] for l in range(Lc)], axis=0)
    v_ref[...] = lax.dot_general(ucol, so_ref[...], (((0,), (0,)), ((), ())), preferred_element_type=F32)
    a_re = ar_ref[0:1, :]
    a_im = ar_ref[1:2, :]
    lane = lax.broadcasted_iota(I32, (1, 2 * P), 1)
    isf = lane < P

    def step(i, carry):
        hr, hi = carry
        cf = pl.multiple_of(i * B, B)
        cb = pl.multiple_of(jnp.where(i < ncc, ncc - 1 - i, nc + ncc - 1 - i) * B, B)
        h_ref[pl.ds(cf, B), 0:P] = hr[:, 0:P]
        h_ref[pl.ds(cf, B), 2 * P:3 * P] = hi[:, 0:P]
        h_ref[pl.ds(cb, B), P:2 * P] = hr[:, P:2 * P]
        h_ref[pl.ds(cb, B), 3 * P:4 * P] = hi[:, P:2 * P]
        vr = jnp.where(isf, v_ref[pl.ds(cf, B), 0:2 * P], v_ref[pl.ds(cb, B), 0:2 * P])
        vi = jnp.where(isf, v_ref[pl.ds(cf, B), 2 * P:4 * P], v_ref[pl.ds(cb, B), 2 * P:4 * P])
        return hr * a_re - hi * a_im + vr, hr * a_im + hi * a_re + vi

    z = jnp.zeros((B, 2 * P), F32)
    lax.fori_loop(0, nc, step, (z, z))
    yt = jnp.dot(wt_ref[:, 0:LG], ucol, preferred_element_type=F32)
    yt = yt + lax.dot_general(wt_ref[:, LG:LG + 4 * P], h_ref[...].astype(BF16), (((1,), (1,)), ((), ())),
                              preferred_element_type=F32)
    for t in range(Lc):
        y_ref[:, t * R:(t + 1) * R] = yt[t * G:(t + 1) * G, :]


def _s5(xt, wt, so, ar, B, nc, ncc):
    P, Lc, G, Gn = S5_STATE, S5_CHUNK, S5_GROUP_DIM, S5_GROUPS
    N = xt.shape[1]
    R = nc * B
    g3 = lambda a, b: pl.BlockSpec((None, a, b), lambda g: (g, 0, 0))
    return pl.pallas_call(
        functools.partial(_s5_body, B=B, nc=nc, ncc=ncc),
        grid=(Gn,),
        in_specs=[pl.BlockSpec((G, N), lambda g: (g, 0)), g3(Lc * G, Lc * G + 4 * P), g3(Lc * G, 4 * P),
                  g3(8, 2 * P)],
        out_specs=pl.BlockSpec((G, N), lambda g: (g, 0)),
        out_shape=_sds((Gn * G, N), F32),
        scratch_shapes=[pltpu.VMEM((R, 4 * P), F32), pltpu.VMEM((R, 4 * P), F32)],
        compiler_params=_cp(("parallel",)),
        name="s5",
    )(xt, wt, so, ar)


def _t2d_body(x_ref, o_ref):
    o_ref[...] = x_ref[...].astype(F32).T.astype(o_ref.dtype)


def _transpose2d(x, tr, name):
    M, C = x.shape
    return pl.pallas_call(
        _t2d_body,
        grid=(M // tr,),
        in_specs=[pl.BlockSpec((tr, C), lambda i: (i, 0))],
        out_specs=pl.BlockSpec((C, tr), lambda i: (0, i)),
        out_shape=_sds((C, M), x.dtype),
        compiler_params=_cp(("parallel",)),
        name=name,
    )(x)


def _untranspose2d(xt, tr, name):
    C, M = xt.shape
    return pl.pallas_call(
        _t2d_body,
        grid=(M // tr,),
        in_specs=[pl.BlockSpec((C, tr), lambda i: (0, i))],
        out_specs=pl.BlockSpec((tr, C), lambda i: (i, 0)),
        out_shape=_sds((M, C), xt.dtype),
        compiler_params=_cp(("parallel",)),
        name=name,
    )(xt)


def _merge_body(x_ref, ya_ref, yf_ref, yb_ref, xc_ref, z_ref, y5_ref, yd_ref, g_ref, gate_ref,
                wa_ref, wb_ref, wc_ref, wd_ref, wo_ref, wglu_ref, bglu_ref, ng_ref, dsk_ref, o_ref):
    D = o_ref.shape[1]
    y = yf_ref[...].astype(F32) + yb_ref[...].astype(F32) + dsk_ref[...] * xc_ref[...].astype(F32)
    y = y * _silu(z_ref[...].astype(F32))
    y = y * lax.rsqrt(jnp.mean(y * y, axis=-1, keepdims=True) + RMS_EPS) * ng_ref[...]
    c = jax.nn.gelu(y5_ref[...]).astype(BF16)
    glu = jax.nn.sigmoid(jnp.dot(c, wglu_ref[...], preferred_element_type=F32) + bglu_ref[...])
    c = (c.astype(F32) * glu).astype(BF16)
    g = g_ref[...]

    def gate(k):
        return jax.nn.sigmoid(g[:, k * D:(k + 1) * D].astype(F32))

    m = gate(0) * jnp.dot(ya_ref[...], wa_ref[...], preferred_element_type=F32)
    m = m + gate(1) * jnp.dot(y.astype(BF16), wb_ref[...], preferred_element_type=F32)
    m = m + gate(2) * jnp.dot(c, wc_ref[...], preferred_element_type=F32)
    m = m + gate(3) * jnp.dot(yd_ref[...], wd_ref[...], preferred_element_type=F32)
    out = jnp.dot(m.astype(BF16), wo_ref[...], preferred_element_type=F32)
    o_ref[...] = x_ref[...] + gate_ref[...] * out


def _merge(xa, ya, yf, yb, xbc_c, proj_a, y5, yd, proj_g, mods3, wa, wb, wc, wd, wo, wglu, bglu, ng, dsk, B, rc):
    N, D = xa.shape
    tm = rc[0]
    rows = lambda w, j=0: pl.BlockSpec((tm, w), lambda i: (i, j))
    full = lambda a: pl.BlockSpec(a.shape, lambda i: (0,) * a.ndim)
    return pl.pallas_call(
        _merge_body,
        grid=(N // tm,),
        in_specs=[rows(D), rows(ATTN_Q_DIM), rows(SSD_INNER), rows(SSD_INNER), rows(SSD_INNER),
                  rows(SSD_INNER, PA_Z // SSD_INNER), rows(S5_WIDTH), rows(SC_WIDTH), rows(4 * D),
                  _mod_spec(D, rc, B, 2),
                  full(wa), full(wb), full(wc), full(wd), full(wo), full(wglu), full(bglu), full(ng), full(dsk)],
        out_specs=rows(D),
        out_shape=_sds((N, D), F32),
        compiler_params=_cp(("parallel",), 48),
        name="merge",
    )(xa, ya, yf, yb, xbc_c, proj_a, y5, yd, proj_g, mods3, wa, wb, wc, wd, wo, wglu, bglu, ng, dsk)


def _prefix_excl(mask_f):
    R, T = mask_f.shape
    r = lax.broadcasted_iota(I32, (128, 128), 0)
    c = lax.broadcasted_iota(I32, (128, 128), 1)
    upper = (r <= c).astype(BF16)
    outs = []
    off = jnp.zeros((R, 1), F32)
    for k in range(T // 128):
        blk = mask_f[:, k * 128:(k + 1) * 128]
        inc = jnp.dot(blk.astype(BF16), upper, preferred_element_type=F32)
        outs.append(inc - blk + off)
        off = off + inc[:, 127:128]
    return jnp.concatenate(outs, axis=1)


def _topk_slots(affs, caps):
    E = affs[0].shape[0]
    bits = [pltpu.bitcast(a, I32) for a in affs]

    def step(i, ths):
        out = []
        for b, cap, th in zip(bits, caps, ths):
            cand = th | (jnp.int32(1) << (30 - i))
            cnt = jnp.sum((b >= cand).astype(I32), axis=1, keepdims=True)
            out.append(jnp.where(cnt >= cap, cand, th))
        return tuple(out)

    ths = lax.fori_loop(0, 31, step, tuple(jnp.zeros((E, 1), I32) for _ in affs))
    slots = []
    for b, cap, th in zip(bits, caps, ths):
        gt = b > th
        eq = b == th
        n_gt = jnp.sum(gt.astype(F32), axis=1, keepdims=True)
        sel = gt | (eq & (n_gt + _prefix_excl(eq.astype(F32)) < cap))
        slots.append(jnp.where(sel, _prefix_excl(sel.astype(F32)), -1.0))
    return slots


def _router_body(x_ref, g_ref, shl_ref, scl_ref, shc_ref, scc_ref, wr_ref, hm_ref, slot_ref, gate_ref, rng_ref,
                 *, Sc, cap_l, cap_c, ctx_out, tb):
    x = x_ref[...]
    S = x.shape[0]
    ms = jnp.mean(x * x, axis=-1, keepdims=True)
    y = x * lax.rsqrt(ms + RMS_EPS) * g_ref[...]
    t = lax.broadcasted_iota(I32, (S, 1), 0)
    isc = t < Sc
    hm = y * (1.0 + jnp.where(isc, scc_ref[...], scl_ref[...])) + jnp.where(isc, shc_ref[...], shl_ref[...])
    hm_ref[...] = hm.astype(hm_ref.dtype)
    wr = wr_ref[...]
    h_hi = hm.astype(BF16)
    h_lo = (hm - h_hi.astype(F32)).astype(BF16)
    w_hi = wr.astype(BF16)
    w_lo = (wr - w_hi.astype(F32)).astype(BF16)
    logits = (jnp.dot(h_hi, w_hi, preferred_element_type=F32) + jnp.dot(h_lo, w_hi, preferred_element_type=F32)
              + jnp.dot(h_hi, w_lo, preferred_element_type=F32))
    lt = logits.T[0:N_EXPERTS, :]
    mx = jnp.max(lt, axis=0, keepdims=True)
    ex = jnp.exp(lt - mx)
    aff = ex / jnp.sum(ex, axis=0, keepdims=True)
    gate_ref[...] = aff
    if ctx_out:
        slot_l, slot_c = _topk_slots([aff[:, Sc:], aff[:, 0:Sc]], [cap_l, cap_c])
        slot_c = jnp.where(slot_c >= 0.0, slot_c + cap_l, -1.0)
    else:
        slot_l, = _topk_slots([aff[:, Sc:]], [cap_l])
        slot_c = jnp.full((N_EXPERTS, Sc), -1.0, F32)
    slot_ref[:, 0:Sc] = slot_c
    slot_ref[:, Sc:] = slot_l
    lane = lax.broadcasted_iota(I32, (1, 128), 1)
    lo_a = jnp.zeros((N_EXPERTS, 128), F32)
    nwin = jnp.zeros((N_EXPERTS, 1), F32)
    for seg0, seg1, base, slots in ((0, Sc, float(cap_l), slot_c), (Sc, S, 0.0, slot_l)):
        lo = jnp.full((N_EXPERTS, 1), base, F32)
        for k in range(seg0 // tb, seg1 // tb):
            blk = slots[:, k * tb - seg0:(k + 1) * tb - seg0]
            hi = lo + jnp.sum((blk >= 0.0).astype(F32), axis=1, keepdims=True)
            lo_al = jnp.floor(lo * (1.0 / SLOT_ALIGN)) * SLOT_ALIGN
            need = jnp.where(hi > lo, jnp.floor((hi - lo_al + (SLOT_WINDOW - 1)) * (1.0 / SLOT_WINDOW)), 0.0)
            nwin = jnp.maximum(nwin, need)
            lo_a = jnp.where(lane == k, lo, lo_a)
            lo = hi
    rng_ref[...] = jnp.where(lane == 127, jnp.max(nwin, axis=0, keepdims=True), lo_a).astype(I32)


def _token_block(S, Sc):
    return math.gcd(math.gcd(Sc, S - Sc), 256)


def _router(xa, g, mods3, wr_pad, B, S, Sc, ctx_out):
    N, D = xa.shape
    T = S - Sc
    cap_l = EC_CAPACITY * T // N_EXPERTS
    cap_c = EC_CAPACITY * Sc // N_EXPERTS
    tb = _token_block(S, Sc)
    lat = lambda ch: pl.BlockSpec((None, 1, D), lambda b_: (b_, 0, ch))
    ctx = lambda ch: pl.BlockSpec((None, 1, D), lambda b_: (B, 0, ch))
    es = pl.BlockSpec((None, N_EXPERTS, S), lambda b_: (b_, 0, 0))
    return pl.pallas_call(
        functools.partial(_router_body, Sc=Sc, cap_l=cap_l, cap_c=cap_c, ctx_out=ctx_out, tb=tb),
        grid=(B,),
        in_specs=[pl.BlockSpec((S, D), lambda b_: (b_, 0)),
                  pl.BlockSpec((1, D), lambda b_: (0, 0)),
                  lat(3), lat(4), ctx(3), ctx(4),
                  pl.BlockSpec((D, 128), lambda b_: (0, 0))],
        out_specs=[pl.BlockSpec((S, D), lambda b_: (b_, 0)), es, es,
                   pl.BlockSpec((None, N_EXPERTS, 128), lambda b_: (b_, 0, 0))],
        out_shape=[_sds((N, D), BF16), _sds((B, N_EXPERTS, S), F32), _sds((B, N_EXPERTS, S), F32),
                   _sds((B, N_EXPERTS, 128), I32)],
        compiler_params=_cp(("parallel",), 56),
        name="router",
    )(xa, g, mods3, mods3, mods3, mods3, wr_pad)


def _window_rows(rng_ref, k, j, ncap):
    rows = []
    for e in range(N_EXPERTS):
        lo = rng_ref[e, k]
        lo_al = lax.shift_left(lax.shift_right_logical(lo, SLOT_ALIGN.bit_length() - 1), SLOT_ALIGN.bit_length() - 1)
        rows.append(pl.multiple_of(jnp.minimum(lo_al + j * SLOT_WINDOW, ncap), SLOT_ALIGN))
    return rows


def _window_hits(rows, slot_ref):
    wrow = lax.broadcasted_iota(I32, (SLOT_WINDOW, 1), 0)
    return [(wrow + rows[e]).astype(F32) == slot_ref[e:e + 1, :] for e in range(N_EXPERTS)]


def _moe_gather_body(rng_ref, hm_ref, slot_ref, gate_ref, xg_ref, gv_ref, *, ncap):
    k = pl.program_id(1)
    W = SLOT_WINDOW

    @pl.when(k == 0)
    def _():
        xg_ref[...] = jnp.zeros_like(xg_ref)
        gv_ref[...] = jnp.zeros_like(gv_ref)

    def window(j, carry):
        rows = _window_rows(rng_ref, k, j, ncap)
        hits = _window_hits(rows, slot_ref)
        hit_all = jnp.concatenate([h.astype(BF16) for h in hits], axis=0)
        xgw = jnp.dot(hit_all, hm_ref[...], preferred_element_type=F32).astype(BF16)
        for e in range(N_EXPERTS):
            r = pl.ds(rows[e], W)
            xg_ref[e, r, :] += xgw[e * W:(e + 1) * W, :]
            gv_ref[e, r, :] += jnp.sum(jnp.where(hits[e], gate_ref[e:e + 1, :], 0.0), axis=1, keepdims=True)
        return carry

    lax.fori_loop(0, rng_ref[0, 127], window, 0)


def _moe_ffn_body(xg_ref, gv_ref, wg_ref, wu_ref, wd_ref, yw_ref, wgs_ref, wus_ref, wds_ref, *, ncap):
    @pl.when(pl.program_id(1) == 0)
    def _():
        wgs_ref[...] = wg_ref[...].astype(BF16)
        wus_ref[...] = wu_ref[...].astype(BF16)
        wds_ref[...] = wd_ref[...].astype(BF16)

    xg = xg_ref[0:ncap, :]
    hid = _silu(jnp.dot(xg, wgs_ref[...], preferred_element_type=F32)) * jnp.dot(
        xg, wus_ref[...], preferred_element_type=F32)
    ye = jnp.dot(hid.astype(BF16), wds_ref[...], preferred_element_type=F32) * gv_ref[0:ncap, :]
    yw_ref[0:ncap, :] = ye.astype(yw_ref.dtype)
    yw_ref[ncap:, :] = jnp.zeros((yw_ref.shape[0] - ncap, yw_ref.shape[1]), yw_ref.dtype)


def _moe_scatter_body(rng_ref, slot_ref, yw_ref, gl_ref, gc_ref, o_ref, *, ncap, nbc):
    k = pl.program_id(1)
    W = SLOT_WINDOW

    def window(j, acc):
        rows = _window_rows(rng_ref, k, j, ncap)
        hit_all = jnp.concatenate([h.astype(BF16) for h in _window_hits(rows, slot_ref)], axis=0)
        yw = jnp.concatenate([yw_ref[e, pl.ds(rows[e], W), :] for e in range(N_EXPERTS)], axis=0)
        return acc + lax.dot_general(hit_all, yw, (((0,), (0,)), ((), ())), preferred_element_type=F32)

    acc = lax.fori_loop(0, rng_ref[0, 127], window, jnp.zeros(o_ref.shape, F32))
    o_ref[...] = jnp.where(k < nbc, gc_ref[...], gl_ref[...]) * acc


def _experts(hm, slot, gate, rng, wg, wu, wd, li, mods3, B, S, Sc, ctx_out):
    N, D = hm.shape
    _, E, _, Fd = wg.shape
    T = S - Sc
    tb = _token_block(S, Sc)
    nblk, nbc = S // tb, Sc // tb
    ncap = EC_CAPACITY * T // N_EXPERTS + (EC_CAPACITY * Sc // N_EXPERTS if ctx_out else 0)
    assert ncap % SLOT_ALIGN == 0
    rows = ncap + SLOT_WINDOW
    smem = pl.BlockSpec((None, E, 128), lambda b_, k: (b_, 0, 0), memory_space=pltpu.SMEM)
    es = pl.BlockSpec((None, E, tb), lambda b_, k: (b_, 0, k))
    xg, gv = pl.pallas_call(
        functools.partial(_moe_gather_body, ncap=ncap),
        grid=(B, nblk),
        in_specs=[smem, pl.BlockSpec((tb, D), lambda b_, k: (b_ * nblk + k, 0)), es, es],
        out_specs=[pl.BlockSpec((None, E, rows, D), lambda b_, k: (b_, 0, 0, 0)),
                   pl.BlockSpec((None, E, rows, 1), lambda b_, k: (b_, 0, 0, 0))],
        out_shape=[_sds((B, E, rows, D), BF16), _sds((B, E, rows, 1), F32)],
        compiler_params=_cp(("parallel", "arbitrary"), 48),
        name="moe_gather",
    )(rng, hm, slot, gate)
    wspec = lambda a, b: pl.BlockSpec((None, None, a, b), lambda e, b_: (li, e, 0, 0))
    yw = pl.pallas_call(
        functools.partial(_moe_ffn_body, ncap=ncap),
        grid=(E, B),
        in_specs=[pl.BlockSpec((None, None, rows, D), lambda e, b_: (b_, e, 0, 0)),
                  pl.BlockSpec((None, None, rows, 1), lambda e, b_: (b_, e, 0, 0)),
                  wspec(D, Fd), wspec(D, Fd), wspec(Fd, D)],
        out_specs=pl.BlockSpec((None, None, rows, D), lambda e, b_: (b_, e, 0, 0)),
        out_shape=_sds((B, E, rows, D), BF16),
        scratch_shapes=[pltpu.VMEM((D, Fd), BF16), pltpu.VMEM((D, Fd), BF16), pltpu.VMEM((Fd, D), BF16)],
        compiler_params=_cp(("parallel", "arbitrary"), 48),
        name="moe_ffn",
    )(xg, gv, wg, wu, wd)
    return pl.pallas_call(
        functools.partial(_moe_scatter_body, ncap=ncap, nbc=nbc),
        grid=(B, nblk),
        in_specs=[smem, es, pl.BlockSpec((None, E, rows, D), lambda b_, k: (b_, 0, 0, 0)),
                  pl.BlockSpec((None, 1, D), lambda b_, k: (b_, 0, 5)),
                  pl.BlockSpec((None, 1, D), lambda b_, k: (B, 0, 5))],
        out_specs=pl.BlockSpec((tb, D), lambda b_, k: (b_ * nblk + k, 0)),
        out_shape=_sds((N, D), F32),
        compiler_params=_cp(("parallel", "arbitrary"), 48),
        name="moe_scatter",
    )(rng, slot, yw, mods3, mods3)


def _final_body(x_ref, r_ref, g_ref, o_ref):
    x = x_ref[...] + r_ref[...]
    o_ref[...] = x * lax.rsqrt(jnp.mean(x * x, axis=-1, keepdims=True) + RMS_EPS) * g_ref[...]


def _final_norm(xa, res, g, B, S, Sc):
    N, D = xa.shape
    tm, nb, nbc = _row_cfg(S, Sc)
    nl = nb - nbc
    row = pl.BlockSpec((tm, D), lambda b_, i: (b_ * nb + nbc + i, 0))
    return pl.pallas_call(
        _final_body,
        grid=(B, nl),
        in_specs=[row, row, pl.BlockSpec((1, D), lambda b_, i: (0, 0))],
        out_specs=pl.BlockSpec((tm, D), lambda b_, i: (b_ * nl + i, 0)),
        out_shape=_sds((B * (S - Sc), D), F32),
        compiler_params=_cp(("parallel", "parallel")),
        name="final_norm",
    )(xa, res, g)


def _layer(xa, res, mods3, lp, ew, li, tabs, B, S, Sc, ctx_out):
    N, D = xa.shape
    rc = _row_cfg(S, Sc)
    w_in = lp["w_in"]
    seg = lambda a, n: w_in[:, a:a + n]
    o_q, o_k, o_v, o_z = 0, ATTN_Q_DIM, ATTN_Q_DIM + ATTN_KV_DIM, ATTN_Q_DIM + 2 * ATTN_KV_DIM
    o_xbc = o_z + SSD_INNER
    o_dt = o_xbc + SSD_CONV_DIM
    o_u = o_dt + SSD_HEADS
    o_sb, o_sg, o_sh = o_u + S5_WIDTH, o_u + S5_WIDTH + SC_WIDTH, o_u + S5_WIDTH + 2 * SC_WIDTH
    o_g = o_sh + SC_WIDTH
    w_a = jnp.concatenate([seg(o_u, S5_WIDTH), seg(o_sb, SC_WIDTH), seg(o_sg, SC_WIDTH), seg(o_sh, SC_WIDTH),
                           seg(o_xbc, SSD_CONV_DIM), seg(o_k, ATTN_KV_DIM), seg(o_v, ATTN_KV_DIM),
                           seg(o_q, ATTN_Q_DIM), seg(o_z, SSD_INNER)], axis=1).astype(BF16)
    w_g = seg(o_g, 4 * D).astype(BF16)
    w_dt = jnp.pad(seg(o_dt, SSD_HEADS), ((0, 0), (0, 128 - SSD_HEADS))).astype(BF16)

    xa, hn = _normmod(xa, res, lp["norm_mix_g"][None, :], mods3, B, rc, 0, 1)
    tm = math.gcd(N, 1024)
    proj_a = _matmul(hn, w_a, BF16, tm, PA_W // 2, "in_proj_a")
    proj_g = _matmul(hn, w_g, BF16, tm, 2 * D, "in_proj_g")
    dt = _matmul(hn, w_dt, F32, tm, 128, "in_proj_dt")

    ya = _attention(proj_a, tabs, lp["q_norm_g"], lp["k_norm_g"], B, S, Sc)

    xbc_c = _ssdconv(proj_a, lp["ssd_conv_w"], lp["ssd_conv_b"][None, :], B, S, Sc)
    yf, yb = _ssd(xbc_c, dt, lp["ssd_dt_bias"], lp["ssd_a_log"], B, S, Sc)

    Lc, Gn, G = S5_CHUNK, S5_GROUPS, S5_GROUP_DIM
    nc5, ncc5 = S // Lc, Sc // Lc
    w5, so5, ar5 = _s5gen(lp["s5_lambda_re"], lp["s5_lambda_im"], lp["s5_log_step"], lp["s5_b_re"], lp["s5_b_im"],
                          lp["s5_c_re"], lp["s5_c_im"], lp["s5_d"])
    u = proj_a[:, PA_U:PA_U + S5_WIDTH].reshape(B, nc5, Lc, S5_WIDTH)
    up = jnp.transpose(u, (2, 1, 0, 3)).reshape(N, S5_WIDTH)
    tr5 = nc5 * B
    y5t = _s5(_transpose2d(up, tr5, "s5_in_t"), w5, so5, ar5, B, nc5, ncc5)
    y5p = _untranspose2d(y5t, tr5, "s5_out_t").reshape(Lc, nc5, B, S5_WIDTH)
    y5 = jnp.transpose(y5p, (2, 1, 0, 3)).reshape(N, S5_WIDTH)

    yd = _shortconv(proj_a, lp["sc_conv_w"], B, S, Sc)

    dsk = jnp.repeat(lp["ssd_d"], SSD_HEAD_DIM)[None, :]
    x1 = _merge(xa, ya, yf, yb, xbc_c, proj_a, y5, yd, proj_g, mods3,
                lp["w_br_attn"].astype(BF16), lp["w_br_ssd"].astype(BF16), lp["w_br_s5"].astype(BF16),
                lp["w_br_sc"].astype(BF16), lp["w_out"].astype(BF16), lp["s5_w_glu"].astype(BF16),
                lp["s5_b_glu"][None, :], lp["ssd_norm_g"][None, :], dsk, B, rc)

    wr_pad = jnp.pad(lp["w_router"], ((0, 0), (0, 128 - N_EXPERTS)))
    hm, slot, gate, rng = _router(x1, lp["norm_ffn_g"][None, :], mods3, wr_pad, B, S, Sc, ctx_out)
    moe = _experts(hm, slot, gate, rng, ew[0], ew[1], ew[2], li, mods3, B, S, Sc, ctx_out)
    return x1, moe


def kernel(x, c, ctx, c_ctx, w_mod, b_mod, norm_mix_g, norm_ffn_g, w_in, q_norm_g, k_norm_g, ssd_conv_w, ssd_conv_b, ssd_dt_bias, ssd_a_log, ssd_d, ssd_norm_g, s5_lambda_re, s5_lambda_im, s5_log_step, s5_b_re, s5_b_im, s5_c_re, s5_c_im, s5_d, s5_w_glu, s5_b_glu, sc_conv_w, w_br_attn, w_br_ssd, w_br_s5, w_br_sc, w_out, w_router, w_exp_gate, w_exp_up, w_exp_down, final_norm_g):
    B, T, D = x.shape
    Sc = ctx.shape[1]
    S = Sc + T
    depth = w_in.shape[0]
    xa = jnp.concatenate([ctx, x], axis=1).reshape(B * S, D)
    cc = jnp.zeros((16, D), F32).at[0:B].set(c).at[B].set(c_ctx)
    cos, sin = _rope_tables(T, Sc)
    tabs = (jnp.tile(cos, (1, ATTN_HEADS)), jnp.tile(sin, (1, ATTN_HEADS)),
            jnp.tile(cos, (1, ATTN_KV_HEADS)), jnp.tile(sin, (1, ATTN_KV_HEADS)),
            _block_diag_ones(ATTN_Q_DIM), _block_diag_ones(ATTN_KV_DIM))
    stacked = dict(
        w_in=w_in, norm_mix_g=norm_mix_g, norm_ffn_g=norm_ffn_g, q_norm_g=q_norm_g, k_norm_g=k_norm_g,
        ssd_conv_w=ssd_conv_w, ssd_conv_b=ssd_conv_b, ssd_dt_bias=ssd_dt_bias, ssd_a_log=ssd_a_log, ssd_d=ssd_d,
        ssd_norm_g=ssd_norm_g, s5_lambda_re=s5_lambda_re, s5_lambda_im=s5_lambda_im, s5_log_step=s5_log_step,
        s5_b_re=s5_b_re, s5_b_im=s5_b_im, s5_c_re=s5_c_re, s5_c_im=s5_c_im, s5_d=s5_d, s5_w_glu=s5_w_glu,
        s5_b_glu=s5_b_glu, sc_conv_w=sc_conv_w, w_br_attn=w_br_attn, w_br_ssd=w_br_ssd, w_br_s5=w_br_s5,
        w_br_sc=w_br_sc, w_out=w_out, w_router=w_router)
    ew = (w_exp_gate, w_exp_up, w_exp_down)
    res = None
    for i in range(depth):
        lp = {k: v[i] for k, v in stacked.items()}
        mods3 = _mods(cc, w_mod, b_mod[:, None, :], i).reshape(16, 1, 6 * D)
        xa, res = _layer(xa, res, mods3, lp, ew, i, tabs, B, S, Sc, ctx_out=i < depth - 1)
    return _final_norm(xa, res, final_norm_g[None, :], B, S, Sc).reshape(B, T, D)
```

```python
import functools
import math

import jax
import jax.numpy as jnp
import numpy as np
from jax import lax
from jax.experimental import pallas as pl
from jax.experimental.pallas import tpu as pltpu

F32 = jnp.float32
BF16 = jnp.bfloat16
I32 = jnp.int32
HI = lax.Precision.HIGHEST

RMS_EPS = 1e-6
GRID_W = 64
ROPE_THETA = 10000.0
HEAD_DIM = 64
ATTN_HEADS = 8
ATTN_KV_HEADS = 2
SSD_HEADS = 8
SSD_HEAD_DIM = 64
SSD_GROUPS = 2
SSD_STATE = 64
SSD_CHUNK = 128
S5_GROUPS = 24
S5_GROUP_DIM = 16
S5_STATE = 64
S5_CHUNK = 16
SC_WIDTH = 384
N_EXPERTS = 16
EC_CAPACITY = 2
MAX_ROW_BLOCK = 256
SLOT_WINDOW = 64
SLOT_ALIGN = 16

ATTN_Q_DIM = ATTN_HEADS * HEAD_DIM
ATTN_KV_DIM = ATTN_KV_HEADS * HEAD_DIM
SSD_INNER = SSD_HEADS * SSD_HEAD_DIM
SSD_BC_DIM = SSD_GROUPS * SSD_STATE
SSD_CONV_DIM = SSD_INNER + 2 * SSD_BC_DIM
S5_WIDTH = S5_GROUPS * S5_GROUP_DIM

PA_G, PA_Q, PA_Z, PA_K, PA_V = 0, 4096, 4608, 5120, 5248
PA_U, PA_SB, PA_SG, PA_SH, PA_XBC, PA_W = 5376, 5760, 6144, 6528, 6912, 7680


def _sds(shape, dtype):
    return jax.ShapeDtypeStruct(shape, dtype)


def _cp(sem, vmem_mb=None):
    kw = dict(dimension_semantics=sem)
    if vmem_mb is not None:
        kw["vmem_limit_bytes"] = vmem_mb << 20
    return pltpu.CompilerParams(**kw)


def _sigmoid(x):
    return 0.5 * jnp.tanh(0.5 * x) + 0.5


def _silu(x):
    return x * _sigmoid(x)


def _mods_body(c_ref, w_ref, b_ref, o_ref):
    s = _silu(c_ref[...])
    o_ref[...] = jnp.dot(s.astype(BF16), w_ref[...].astype(BF16), preferred_element_type=F32) + b_ref[...]


def _mods(cc, w, b, li):
    R, D = cc.shape
    N = w.shape[2]
    tn = 1536
    return pl.pallas_call(
        _mods_body,
        grid=(N // tn,),
        in_specs=[pl.BlockSpec((R, D), lambda j: (0, 0)),
                  pl.BlockSpec((None, D, tn), lambda j: (li, 0, j)),
                  pl.BlockSpec((None, 1, tn), lambda j: (li, 0, j))],
        out_specs=pl.BlockSpec((R, tn), lambda j: (0, j)),
        out_shape=_sds((R, N), F32),
        compiler_params=_cp(("parallel",), 40),
        name="mods",
    )(cc, w, b)


def _row_cfg(S, Sc):
    tm = math.gcd(math.gcd(Sc, S - Sc), MAX_ROW_BLOCK)
    return tm, S // tm, Sc // tm


def _mod_spec(D, rc, B, chunk):
    _, nb, nbc = rc
    return pl.BlockSpec((None, 1, D), lambda i: (jnp.where(i % nb < nbc, B, i // nb), 0, chunk))


def _in_proj_body(x_ref, g_ref, shl_ref, scl_ref, shc_ref, scc_ref, w_ref, wdt_ref, o_ref, dt_ref, hn_ref,
                  *, tm, nb, Sc):
    @pl.when(pl.program_id(1) == 0)
    def _():
        x = x_ref[...]
        ms = jnp.mean(x * x, axis=-1, keepdims=True)
        y = x * lax.rsqrt(ms + RMS_EPS) * g_ref[...]
        t = (pl.program_id(0) % nb) * tm + lax.broadcasted_iota(I32, (tm, 1), 0)
        isc = t < Sc
        hn = y * (1.0 + jnp.where(isc, scc_ref[...], scl_ref[...])) + jnp.where(isc, shc_ref[...], shl_ref[...])
        hn_ref[...] = hn.astype(BF16)
        dt_ref[...] = jnp.dot(hn_ref[...], wdt_ref[...], preferred_element_type=F32)

    o_ref[...] = jnp.dot(hn_ref[...], w_ref[...], preferred_element_type=F32).astype(o_ref.dtype)


def _in_proj(xa, g, mods3, w_p, w_dt, B, S, Sc):
    N, D = xa.shape
    tm = S // 2 if (S // 2) % 128 == 0 else S
    nb = S // tm
    tn = 1536
    assert PA_W % tn == 0
    lat = lambda ch: pl.BlockSpec((None, 1, D), lambda i, j: (i // nb, 0, ch))
    ctx = lambda ch: pl.BlockSpec((None, 1, D), lambda i, j: (B, 0, ch))
    return pl.pallas_call(
        functools.partial(_in_proj_body, tm=tm, nb=nb, Sc=Sc),
        grid=(N // tm, PA_W // tn),
        in_specs=[pl.BlockSpec((tm, D), lambda i, j: (i, 0)),
                  pl.BlockSpec((1, D), lambda i, j: (0, 0)),
                  lat(0), lat(1), ctx(0), ctx(1),
                  pl.BlockSpec((D, tn), lambda i, j: (0, j)),
                  pl.BlockSpec((D, 128), lambda i, j: (0, 0))],
        out_specs=[pl.BlockSpec((tm, tn), lambda i, j: (i, j)), pl.BlockSpec((tm, 128), lambda i, j: (i, 0))],
        out_shape=[_sds((N, PA_W), BF16), _sds((N, 128), F32)],
        scratch_shapes=[pltpu.VMEM((tm, D), BF16)],
        compiler_params=_cp(("parallel", "arbitrary"), 48),
        name="in_proj",
    )(xa, g, mods3, mods3, mods3, mods3, w_p, w_dt)


def _shifted(x, Sc):
    S = x.shape[0]
    t = lax.broadcasted_iota(I32, (S, 1), 0)
    prev = jnp.where((t == 0) | (t == Sc), 0.0, pltpu.roll(x, 1, 0))
    nxt = jnp.where((t == Sc - 1) | (t == S - 1), 0.0, pltpu.roll(x, S - 1, 0))
    return prev, nxt


def _ssdconv_body(x_ref, w_ref, b_ref, o_ref, *, Sc):
    x = x_ref[...].astype(F32)
    prev, nxt = _shifted(x, Sc)
    w = w_ref[...]
    y = w[0:1, :] * prev + w[1:2, :] * x + w[2:3, :] * nxt + b_ref[...]
    o_ref[...] = _silu(y).astype(o_ref.dtype)


def _ssdconv(proj_a, w, b, B, S, Sc):
    C = 384
    j0 = PA_XBC // C
    return pl.pallas_call(
        functools.partial(_ssdconv_body, Sc=Sc),
        grid=(B, SSD_CONV_DIM // C),
        in_specs=[pl.BlockSpec((S, C), lambda b_, j: (b_, j0 + j)),
                  pl.BlockSpec((3, C), lambda b_, j: (0, j)),
                  pl.BlockSpec((1, C), lambda b_, j: (0, j))],
        out_specs=pl.BlockSpec((S, C), lambda b_, j: (b_, j)),
        out_shape=_sds((B * S, SSD_CONV_DIM), BF16),
        compiler_params=_cp(("parallel", "parallel"), 40),
        name="ssdconv",
    )(proj_a, w, b)


def _shortconv_body(sb_ref, sg_ref, sh_ref, w_ref, o_ref, *, Sc):
    x = sg_ref[...].astype(F32) * sh_ref[...].astype(F32)
    prev, nxt = _shifted(x, Sc)
    w = w_ref[...]
    y = w[0:1, :] * prev + w[1:2, :] * x + w[2:3, :] * nxt
    o_ref[...] = (sb_ref[...].astype(F32) * y).astype(o_ref.dtype)


def _shortconv(proj_a, w, B, S, Sc):
    C = SC_WIDTH
    return pl.pallas_call(
        functools.partial(_shortconv_body, Sc=Sc),
        grid=(B,),
        in_specs=[pl.BlockSpec((S, C), lambda b_: (b_, PA_SB // C)),
                  pl.BlockSpec((S, C), lambda b_: (b_, PA_SG // C)),
                  pl.BlockSpec((S, C), lambda b_: (b_, PA_SH // C)),
                  pl.BlockSpec((3, C), lambda b_: (0, 0))],
        out_specs=pl.BlockSpec((S, C), lambda b_: (b_, 0)),
        out_shape=_sds((B * S, C), BF16),
        compiler_params=_cp(("parallel",), 40),
        name="shortconv",
    )(proj_a, proj_a, proj_a, w)


def _norm_rope(x, g, cos, sin, bd, scale):
    W = x.shape[1]
    sq = x * x
    hi = sq.astype(BF16)
    lo = (sq - hi.astype(F32)).astype(BF16)
    ssq = jnp.dot(hi, bd, preferred_element_type=F32) + jnp.dot(lo, bd, preferred_element_type=F32)
    y = x * lax.rsqrt(ssq * (1.0 / HEAD_DIM) + RMS_EPS) * g
    lane = lax.broadcasted_iota(I32, (1, W), 1)
    first = (lane % 32) < 16
    partner = jnp.where(first, pltpu.roll(y, W - 16, 1), pltpu.roll(y, 16, 1))
    return (y * cos + partner * sin) * scale


def _attn_body(q_ref, k_ref, v_ref, cq_ref, sq_ref, ck_ref, sk_ref, gq_ref, gk_ref, bdq_ref, bdk_ref,
               o_ref, kh_ref, ve_ref, *, Sc, S, tq):
    qb = pl.program_id(1)
    hd = HEAD_DIM

    @pl.when(qb == 0)
    def _():
        k = k_ref[...].astype(F32)
        kh_ref[...] = _norm_rope(k, gk_ref[...], ck_ref[...], sk_ref[...], bdk_ref[...], 1.0).astype(BF16)
        v = v_ref[...].astype(F32)
        lane = lax.broadcasted_iota(I32, (1, 2 * hd), 1)
        ve_ref[0] = jnp.where(lane < hd, v, 1.0).astype(BF16)
        ve_ref[1] = jnp.where(lane < hd, pltpu.roll(v, hd, 1), 1.0).astype(BF16)

    q = q_ref[...].astype(F32)
    qh = _norm_rope(q, gq_ref[...], cq_ref[...], sq_ref[...], bdq_ref[...],
                    HEAD_DIM ** -0.5 * math.log2(math.e)).astype(BF16)
    rep = ATTN_HEADS // ATTN_KV_HEADS
    nt = (((1,), (1,)), ((), ()))

    def attend(splits):
        scores = []
        for g in range(ATTN_KV_HEADS):
            qg = jnp.concatenate([qh[:, (g * rep + r) * hd:(g * rep + r + 1) * hd] for r in range(rep)], axis=0)
            scores.append([lax.dot_general(qg, kh_ref[a:b, g * hd:(g + 1) * hd], nt, preferred_element_type=F32)
                           for a, b in splits])
        for g in range(ATTN_KV_HEADS):
            m = functools.reduce(jnp.maximum, [jnp.max(s, axis=-1, keepdims=True) for s in scores[g]])
            acc = None
            for s, (a, b) in zip(scores[g], splits):
                p = jnp.exp2(s - m).astype(BF16)
                part = jnp.dot(p, ve_ref[g, a:b, :], preferred_element_type=F32)
                acc = part if acc is None else acc + part
            o = acc[:, 0:hd] / acc[:, hd:hd + 1]
            for r in range(rep):
                h = g * rep + r
                o_ref[:, h * hd:(h + 1) * hd] = o[r * tq:(r + 1) * tq].astype(o_ref.dtype)

    @pl.when(qb < Sc // tq)
    def _():
        attend([(0, Sc)])

    @pl.when(qb >= Sc // tq)
    def _():
        half = (S // 2 + 255) // 256 * 256 if S >= 512 else S
        attend([(0, half), (half, S)] if half < S else [(0, S)])


def _rope_tables(T, Sc):
    rows = T // GRID_W
    row = np.repeat(np.arange(rows, dtype=np.float32), GRID_W)
    col = np.tile(np.arange(GRID_W, dtype=np.float32), rows)
    half = HEAD_DIM // 2
    inv = jnp.asarray(ROPE_THETA, F32) ** (-jnp.arange(0, half, 2, dtype=F32) / half)
    ra = jnp.asarray(row)[:, None] * inv
    ca = jnp.asarray(col)[:, None] * inv
    cos = jnp.concatenate([jnp.cos(ra), jnp.cos(ra), jnp.cos(ca), jnp.cos(ca)], axis=1)
    sin = jnp.concatenate([-jnp.sin(ra), jnp.sin(ra), -jnp.sin(ca), jnp.sin(ca)], axis=1)
    cos = jnp.concatenate([jnp.ones((Sc, HEAD_DIM), F32), cos], axis=0)
    sin = jnp.concatenate([jnp.zeros((Sc, HEAD_DIM), F32), sin], axis=0)
    return cos, sin


def _block_diag_ones(W):
    i = np.arange(W) // HEAD_DIM
    return jnp.asarray((i[:, None] == i[None, :]).astype(np.float32), BF16)


def _attention(proj_a, tabs, q_norm_g, k_norm_g, B, S, Sc):
    cos_q, sin_q, cos_k, sin_k, bdq, bdk = tabs
    tq = 128
    nq = S // tq
    gq = jnp.tile(q_norm_g, ATTN_HEADS)[None, :]
    gk = jnp.tile(k_norm_g, ATTN_KV_HEADS)[None, :]
    const = lambda b_, i: (0, 0)
    return pl.pallas_call(
        functools.partial(_attn_body, Sc=Sc, S=S, tq=tq),
        grid=(B, nq),
        in_specs=[pl.BlockSpec((tq, ATTN_Q_DIM), lambda b_, i: (b_ * nq + i, PA_Q // ATTN_Q_DIM)),
                  pl.BlockSpec((S, ATTN_KV_DIM), lambda b_, i: (b_, PA_K // ATTN_KV_DIM)),
                  pl.BlockSpec((S, ATTN_KV_DIM), lambda b_, i: (b_, PA_V // ATTN_KV_DIM)),
                  pl.BlockSpec((tq, ATTN_Q_DIM), lambda b_, i: (i, 0)),
                  pl.BlockSpec((tq, ATTN_Q_DIM), lambda b_, i: (i, 0)),
                  pl.BlockSpec((S, ATTN_KV_DIM), const),
                  pl.BlockSpec((S, ATTN_KV_DIM), const),
                  pl.BlockSpec((1, ATTN_Q_DIM), const),
                  pl.BlockSpec((1, ATTN_KV_DIM), const),
                  pl.BlockSpec((ATTN_Q_DIM, ATTN_Q_DIM), const),
                  pl.BlockSpec((ATTN_KV_DIM, ATTN_KV_DIM), const)],
        out_specs=pl.BlockSpec((tq, ATTN_Q_DIM), lambda b_, i: (b_ * nq + i, 0)),
        out_shape=_sds((B * S, ATTN_Q_DIM), BF16),
        scratch_shapes=[pltpu.VMEM((S, ATTN_KV_DIM), BF16),
                        pltpu.VMEM((ATTN_KV_HEADS, S, 2 * HEAD_DIM), BF16)],
        compiler_params=_cp(("parallel", "arbitrary"), 48),
        name="attention",
    )(proj_a, proj_a, proj_a, cos_q, sin_q, cos_k, sin_k, gq, gk, bdq, bdk)


def _ssd_dir(xbc_ref, dt_ref, bias, a_neg, h_ref, y_ref, rev):
    L = SSD_CHUNK
    P, Nst = SSD_HEAD_DIM, SSD_STATE
    xbc = xbc_ref[...]
    bm = xbc[:, SSD_INNER:SSD_INNER + SSD_BC_DIM]
    cm = xbc[:, SSD_INNER + SSD_BC_DIM:SSD_CONV_DIM]
    dt = jax.nn.softplus(dt_ref[...] + bias)
    a = dt * a_neg
    row = lax.broadcasted_iota(I32, (L, L), 0)
    col = lax.broadcasted_iota(I32, (L, L), 1)
    tri = (row >= col).astype(F32)
    cum = jnp.dot(tri, a, precision=HI, preferred_element_type=F32)
    e = cum - a if rev else cum
    eT = e.T
    dtT = dt.T
    bT = bm.astype(F32).T.astype(BF16)
    mask = (col >= row) if rev else (row >= col)
    lane = lax.broadcasted_iota(I32, (1, 2 * P), 1)
    lo_half = lane < P
    blockdiag = (row // Nst) == (col // P)
    tot_row = cum[L - 1:L, :]
    win_c = jnp.exp(tot_row - e) if rev else jnp.exp(e)
    win_hi = win_c.astype(BF16)
    win_lo = (win_c - win_hi.astype(F32)).astype(BF16)
    cm32 = cm.astype(F32)
    cm_rolled = pltpu.roll(cm32, P, 1)
    rep = SSD_HEADS // SSD_GROUPS
    for g in range(SSD_GROUPS):
        own = (lane // P) == g
        cg0 = jnp.where(own, cm, jnp.zeros_like(cm))
        sg = jnp.dot(cg0, bT, preferred_element_type=F32)
        cdup = jnp.where(own, cm32, cm_rolled)
        bTg = bT[g * Nst:(g + 1) * Nst, :].astype(F32)
        for q in range(rep // 2):
            pq = (g * rep) // 2 + q
            hs = (2 * pq, 2 * pq + 1)
            ms, bws, tots = [], [], []
            for h in hs:
                ecol = e[:, h:h + 1]
                erow = eT[h:h + 1, :]
                tot = cum[L - 1:L, h:h + 1]
                diff = (erow - ecol) if rev else (ecol - erow)
                dec = jnp.where(mask, jnp.exp(jnp.minimum(diff, 0.0)), 0.0)
                ms.append((sg * dec * dtT[h:h + 1, :]).astype(BF16))
                wrow = (jnp.exp(erow) if rev else jnp.exp(tot - erow)) * dtT[h:h + 1, :]
                bws.append(bTg * wrow)
                tots.append(jnp.exp(tot))
            xp = xbc[:, 2 * P * pq:2 * P * (pq + 1)]
            zero = jnp.zeros_like(xp)
            xbd = jnp.concatenate([jnp.where(lo_half, xp, zero), jnp.where(lo_half, zero, xp)], axis=0)
            y = jnp.dot(jnp.concatenate(ms, axis=1), xbd, preferred_element_type=F32)
            sel = (lax.broadcasted_iota(I32, (2 * P, 2 * P), 0) == hs[0] + col // P).astype(BF16)
            wexp = (jnp.dot(win_hi, sel, preferred_element_type=F32)
                    + jnp.dot(win_lo, sel, preferred_element_type=F32))
            hp = h_ref[pq]
            y = y + jnp.dot((wexp * cdup).astype(BF16), hp.astype(BF16), preferred_element_type=F32)
            upd = jnp.dot(jnp.concatenate(bws, axis=0).astype(BF16), xp, preferred_element_type=F32)
            keep = jnp.where(lax.broadcasted_iota(I32, (2 * Nst, 1), 0) < Nst, tots[0], tots[1])
            h_ref[pq] = jnp.where(blockdiag, keep * hp + upd, 0.0)
            y_ref[:, 2 * P * pq:2 * P * (pq + 1)] = y.astype(y_ref.dtype)


def _ssd_body(xf_ref, dtf_ref, xb_ref, dtb_ref, bias_ref, alog_ref, yf_ref, yb_ref, h_ref, *, nper):
    @pl.when(pl.program_id(1) == 0)
    def _():
        h_ref[...] = jnp.zeros_like(h_ref)

    lane = lax.broadcasted_iota(I32, (1, 128), 1)
    a_neg = jnp.where(lane < SSD_HEADS, -jnp.exp(alog_ref[...]), 0.0)
    bias = bias_ref[...]
    for p in range(nper):
        _ssd_dir(xf_ref.at[p], dtf_ref.at[p], bias[0:1, :], a_neg[0:1, :], h_ref.at[p, 0], yf_ref.at[p], False)
        _ssd_dir(xb_ref.at[p], dtb_ref.at[p], bias[1:2, :], a_neg[1:2, :], h_ref.at[p, 1], yb_ref.at[p], True)


def _ssd(xbc_c, dt, dt_bias, a_log, B, S, Sc):
    L = SSD_CHUNK
    nc, ncc = S // L, Sc // L
    nper = 2 if B % 2 == 0 else 1
    bg = B // nper
    padh = 128 - SSD_HEADS
    bias = jnp.pad(dt_bias, ((0, 0), (0, padh)))
    alog = jnp.pad(a_log, ((0, 0), (0, padh)))
    x3 = xbc_c.reshape(nper, bg * S, SSD_CONV_DIM)
    dt3 = dt.reshape(nper, bg * S, 128)

    def fwd(b_, i):
        return (0, b_ * nc + i, 0)

    def bwd(b_, i):
        return (0, b_ * nc + jnp.where(i < ncc, ncc - 1 - i, nc + ncc - 1 - i), 0)

    const = lambda b_, i: (0, 0)
    out = _sds((nper, bg * S, SSD_INNER), BF16)
    yf, yb = pl.pallas_call(
        functools.partial(_ssd_body, nper=nper),
        grid=(bg, nc),
        in_specs=[pl.BlockSpec((nper, L, SSD_CONV_DIM), fwd), pl.BlockSpec((nper, L, 128), fwd),
                  pl.BlockSpec((nper, L, SSD_CONV_DIM), bwd), pl.BlockSpec((nper, L, 128), bwd),
                  pl.BlockSpec((2, 128), const), pl.BlockSpec((2, 128), const)],
        out_specs=[pl.BlockSpec((nper, L, SSD_INNER), fwd), pl.BlockSpec((nper, L, SSD_INNER), bwd)],
        out_shape=[out, out],
        scratch_shapes=[pltpu.VMEM((nper, 2, SSD_HEADS // 2, 2 * SSD_STATE, 2 * SSD_HEAD_DIM), F32)],
        compiler_params=_cp(("parallel", "arbitrary")),
        name="ssd",
    )(x3, dt3, x3, dt3, bias, alog)
    return yf.reshape(B * S, SSD_INNER), yb.reshape(B * S, SSD_INNER)


def _cpow(n, lr, li, st):
    mag = jnp.exp(n * lr * st)
    ang = n * li * st
    return mag * jnp.cos(ang), mag * jnp.sin(ang)


def _zoh_coef(lr, li, st):
    ar, ai = _cpow(1.0, lr, li, st)
    nr, ni = ar - 1.0, ai
    den = lr * lr + li * li
    return (nr * lr + ni * li) / den, (ni * lr - nr * li) / den


def _s5gen_body(colp_ref, rowp_ref, ctr_ref, cti_ref, btr_ref, bti_ref, blr_ref, bli_ref, d_ref,
                wt_ref, so_ref, ar_ref, ws_ref, sos_ref):
    Lc, G, P = S5_CHUNK, S5_GROUP_DIM, S5_STATE
    NS = 2 * Lc + 1
    colp = colp_ref[...]
    rowp = rowp_ref[...]
    lrf, lif, stf = colp[:, 0:1], colp[:, 1:2], jnp.exp(colp[:, 2:3])
    lrb, lib, stb = colp[:, 3:4], colp[:, 4:5], jnp.exp(colp[:, 5:6])
    lane = lax.broadcasted_iota(I32, (1, NS * G), 1)
    slot = lane // G
    isb = slot < Lc
    lag = jnp.abs(slot - Lc).astype(F32)
    pr, pi = _cpow(lag, jnp.where(isb, lrb, lrf), jnp.where(isb, lib, lif), jnp.where(isb, stb, stf))
    ctr, cti = ctr_ref[...], cti_ref[...]
    er = ctr * pr - cti * pi
    ei = ctr * pi + cti * pr
    rlrf, rlif, rstf = rowp[0:1, :], rowp[1:2, :], jnp.exp(rowp[2:3, :])
    rlrb, rlib, rstb = rowp[3:4, :], rowp[4:5, :], jnp.exp(rowp[5:6, :])
    btr, bti = btr_ref[...], bti_ref[...]
    cfr, cfi = _zoh_coef(rlrf, rlif, rstf)
    cbr, cbi = _zoh_coef(rlrb, rlib, rstb)
    bbf_r, bbf_i = cfr * btr - cfi * bti, cfr * bti + cfi * btr
    bbb_r, bbb_i = cbr * btr - cbi * bti, cbr * bti + cbi * btr

    def kt(br, bi):
        return (jnp.dot(br, er, precision=HI, preferred_element_type=F32)
                - jnp.dot(bi, ei, precision=HI, preferred_element_type=F32))

    ktf, ktb = kt(bbf_r, bbf_i), kt(bbb_r, bbb_i)
    ii = lax.broadcasted_iota(I32, (G, NS * G), 0)
    dmat = jnp.where((slot == Lc) & (ii == lane - Lc * G), d_ref[...], 0.0)
    strip = jnp.where(slot == Lc, ktf + ktb, jnp.where(isb, ktb, ktf)) + dmat
    for s in range(Lc):
        off = (Lc - s) * G
        ws_ref[s * G:(s + 1) * G, :] = strip[:, off:off + Lc * G]
    fo = (Lc + 1) * G
    ws_ref[Lc * G:Lc * G + P, :] = er[:, fo:fo + Lc * G]
    ws_ref[Lc * G + P:Lc * G + 2 * P, :] = er[:, 0:Lc * G]
    ws_ref[Lc * G + 2 * P:Lc * G + 3 * P, :] = -ei[:, fo:fo + Lc * G]
    ws_ref[Lc * G + 3 * P:Lc * G + 4 * P, :] = -ei[:, 0:Lc * G]
    wt_ref[:, 0:Lc * G] = ws_ref[0:Lc * G, :].T.astype(wt_ref.dtype)
    wt_ref[:, Lc * G:Lc * G + 4 * P] = ws_ref[Lc * G:Lc * G + 4 * P, :].T.astype(wt_ref.dtype)
    lane2 = lax.broadcasted_iota(I32, (1, Lc * G), 1)
    s_idx = (lane2 // G).astype(F32)
    qfr, qfi = _cpow((Lc - 1.0) - s_idx, lrf, lif, stf)
    qbr, qbi = _cpow(s_idx, lrb, lib, stb)
    ccfr, ccfi = _zoh_coef(lrf, lif, stf)
    ccbr, ccbi = _zoh_coef(lrb, lib, stb)
    blr, bli = blr_ref[...], bli_ref[...]
    bfr, bfi = ccfr * blr - ccfi * bli, ccfr * bli + ccfi * blr
    bbr, bbi = ccbr * blr - ccbi * bli, ccbr * bli + ccbi * blr
    sos_ref[0:P, :] = qfr * bfr - qfi * bfi
    sos_ref[P:2 * P, :] = qbr * bbr - qbi * bbi
    sos_ref[2 * P:3 * P, :] = qfr * bfi + qfi * bfr
    sos_ref[3 * P:4 * P, :] = qbr * bbi + qbi * bbr
    so_ref[...] = sos_ref[...].T.astype(so_ref.dtype)
    afr, afi = _cpow(float(Lc), rlrf, rlif, rstf)
    abr, abi = _cpow(float(Lc), rlrb, rlib, rstb)
    ar_ref[...] = jnp.zeros_like(ar_ref)
    ar_ref[0:1, 0:P] = afr
    ar_ref[0:1, P:2 * P] = abr
    ar_ref[1:2, 0:P] = afi
    ar_ref[1:2, P:2 * P] = abi


def _s5gen(lam_re, lam_im, log_step, b_re, b_im, c_re, c_im, d_skip):
    Gn, P, G, Lc = S5_GROUPS, S5_STATE, S5_GROUP_DIM, S5_CHUNK
    NS = 2 * Lc + 1
    ls = jnp.broadcast_to(log_step[:, :, None], (2, Gn, P))
    z = jnp.zeros((Gn, P), F32)
    rowp = jnp.stack([lam_re[0], lam_im[0], ls[0], lam_re[1], lam_im[1], ls[1], z, z], axis=1)
    colp = jnp.swapaxes(rowp, 1, 2)
    ctr = jnp.tile(jnp.swapaxes(c_re, 1, 2), (1, 1, NS))
    cti = jnp.tile(jnp.swapaxes(c_im, 1, 2), (1, 1, NS))
    btr = jnp.swapaxes(b_re, 1, 2)
    bti = jnp.swapaxes(b_im, 1, 2)
    blr = jnp.tile(b_re, (1, 1, Lc))
    bli = jnp.tile(b_im, (1, 1, Lc))
    dt = jnp.tile(d_skip.reshape(Gn, 1, G), (1, 1, NS))
    g3 = lambda a, b: pl.BlockSpec((None, a, b), lambda g: (g, 0, 0))
    return pl.pallas_call(
        _s5gen_body,
        grid=(Gn,),
        in_specs=[g3(P, 8), g3(8, P), g3(P, NS * G), g3(P, NS * G), g3(G, P), g3(G, P),
                  g3(P, Lc * G), g3(P, Lc * G), g3(1, NS * G)],
        out_specs=[g3(Lc * G, Lc * G + 4 * P), g3(Lc * G, 4 * P), g3(8, 2 * P)],
        out_shape=[_sds((Gn, Lc * G, Lc * G + 4 * P), BF16), _sds((Gn, Lc * G, 4 * P), BF16),
                   _sds((Gn, 8, 2 * P), F32)],
        scratch_shapes=[pltpu.VMEM((Lc * G + 4 * P, Lc * G), F32), pltpu.VMEM((4 * P, Lc * G), F32)],
        compiler_params=_cp(("parallel",)),
        name="s5gen",
    )(colp, rowp, ctr, cti, btr, bti, blr, bli, dt)


def _s5_body(xt_ref, wt_ref, so_ref, ar_ref, y_ref, v_ref, h_ref, *, B, nc, ncc):
    P, Lc, G = S5_STATE, S5_CHUNK, S5_GROUP_DIM
    LG = Lc * G
    R = nc * B
    ucol = jnp.concatenate([xt_ref[:, l * R:(l + 1) * R] for l in range(Lc)], axis=0)
    v_ref[...] = lax.dot_general(ucol, so_ref[...], (((0,), (0,)), ((), ())), preferred_element_type=F32)
    a_re = ar_ref[0:1, :]
    a_im = ar_ref[1:2, :]
    lane = lax.broadcasted_iota(I32, (1, 2 * P), 1)
    isf = lane < P

    def step(i, carry):
        hr, hi = carry
        cf = pl.multiple_of(i * B, B)
        cb = pl.multiple_of(jnp.where(i < ncc, ncc - 1 - i, nc + ncc - 1 - i) * B, B)
        h_ref[pl.ds(cf, B), 0:P] = hr[:, 0:P]
        h_ref[pl.ds(cf, B), 2 * P:3 * P] = hi[:, 0:P]
        h_ref[pl.ds(cb, B), P:2 * P] = hr[:, P:2 * P]
        h_ref[pl.ds(cb, B), 3 * P:4 * P] = hi[:, P:2 * P]
        vr = jnp.where(isf, v_ref[pl.ds(cf, B), 0:2 * P], v_ref[pl.ds(cb, B), 0:2 * P])
        vi = jnp.where(isf, v_ref[pl.ds(cf, B), 2 * P:4 * P], v_ref[pl.ds(cb, B), 2 * P:4 * P])
        return hr * a_re - hi * a_im + vr, hr * a_im + hi * a_re + vi

    z = jnp.zeros((B, 2 * P), F32)
    lax.fori_loop(0, nc, step, (z, z))
    yt = jnp.dot(wt_ref[:, 0:LG], ucol, preferred_element_type=F32)
    yt = yt + lax.dot_general(wt_ref[:, LG:LG + 4 * P], h_ref[...].astype(BF16), (((1,), (1,)), ((), ())),
                              preferred_element_type=F32)
    for t in range(Lc):
        y_ref[:, t * R:(t + 1) * R] = yt[t * G:(t + 1) * G, :]


def _s5(xt, wt, so, ar, B, nc, ncc):
    P, Lc, G, Gn = S5_STATE, S5_CHUNK, S5_GROUP_DIM, S5_GROUPS
    N = xt.shape[1]
    R = nc * B
    g3 = lambda a, b: pl.BlockSpec((None, a, b), lambda g: (g, 0, 0))
    return pl.pallas_call(
        functools.partial(_s5_body, B=B, nc=nc, ncc=ncc),
        grid=(Gn,),
        in_specs=[pl.BlockSpec((G, N), lambda g: (g, 0)), g3(Lc * G, Lc * G + 4 * P), g3(Lc * G, 4 * P),
                  g3(8, 2 * P)],
        out_specs=pl.BlockSpec((G, N), lambda g: (g, 0)),
        out_shape=_sds((Gn * G, N), F32),
        scratch_shapes=[pltpu.VMEM((R, 4 * P), F32), pltpu.VMEM((R, 4 * P), F32)],
        compiler_params=_cp(("parallel",)),
        name="s5",
    )(xt, wt, so, ar)


def _t2d_body(x_ref, o_ref):
    o_ref[...] = x_ref[...].astype(F32).T.astype(o_ref.dtype)


def _transpose2d(x, tr, name):
    M, C = x.shape
    return pl.pallas_call(
        _t2d_body,
        grid=(M // tr,),
        in_specs=[pl.BlockSpec((tr, C), lambda i: (i, 0))],
        out_specs=pl.BlockSpec((C, tr), lambda i: (0, i)),
        out_shape=_sds((C, M), x.dtype),
        compiler_params=_cp(("parallel",)),
        name=name,
    )(x)


def _untranspose2d(xt, tr, name):
    C, M = xt.shape
    return pl.pallas_call(
        _t2d_body,
        grid=(M // tr,),
        in_specs=[pl.BlockSpec((C, tr), lambda i: (0, i))],
        out_specs=pl.BlockSpec((tr, C), lambda i: (i, 0)),
        out_shape=_sds((M, C), xt.dtype),
        compiler_params=_cp(("parallel",)),
        name=name,
    )(xt)


def _merge_body(x_ref, ya_ref, yf_ref, yb_ref, xc_ref, z_ref, y5_ref, yd_ref, g_ref, gate_ref,
                wa_ref, wb_ref, wc_ref, wd_ref, wo_ref, wglu_ref, bglu_ref, ng_ref, dsk_ref, o_ref):
    D = o_ref.shape[1]
    y = yf_ref[...].astype(F32) + yb_ref[...].astype(F32) + dsk_ref[...] * xc_ref[...].astype(F32)
    y = y * _silu(z_ref[...].astype(F32))
    y = y * lax.rsqrt(jnp.mean(y * y, axis=-1, keepdims=True) + RMS_EPS) * ng_ref[...]
    c = jax.nn.gelu(y5_ref[...]).astype(BF16)
    glu = _sigmoid(jnp.dot(c, wglu_ref[...], preferred_element_type=F32) + bglu_ref[...])
    c = (c.astype(F32) * glu).astype(BF16)
    g = g_ref[...]

    def gate(k):
        return _sigmoid(g[:, k * D:(k + 1) * D].astype(F32))

    m = gate(0) * jnp.dot(ya_ref[...], wa_ref[...], preferred_element_type=F32)
    m = m + gate(1) * jnp.dot(y.astype(BF16), wb_ref[...], preferred_element_type=F32)
    m = m + gate(2) * jnp.dot(c, wc_ref[...], preferred_element_type=F32)
    m = m + gate(3) * jnp.dot(yd_ref[...], wd_ref[...], preferred_element_type=F32)
    out = jnp.dot(m.astype(BF16), wo_ref[...], preferred_element_type=F32)
    o_ref[...] = x_ref[...] + gate_ref[...] * out


def _merge(xa, ya, yf, yb, xbc_c, proj, y5, yd, mods3, wa, wb, wc, wd, wo, wglu, bglu, ng, dsk, B, rc):
    N, D = xa.shape
    tm = rc[0]
    rows = lambda w, j=0: pl.BlockSpec((tm, w), lambda i: (i, j))
    full = lambda a: pl.BlockSpec(a.shape, lambda i: (0,) * a.ndim)
    return pl.pallas_call(
        _merge_body,
        grid=(N // tm,),
        in_specs=[rows(D), rows(ATTN_Q_DIM), rows(SSD_INNER), rows(SSD_INNER), rows(SSD_INNER),
                  rows(SSD_INNER, PA_Z // SSD_INNER), rows(S5_WIDTH), rows(SC_WIDTH), rows(4 * D, PA_G // (4 * D)),
                  _mod_spec(D, rc, B, 2),
                  full(wa), full(wb), full(wc), full(wd), full(wo), full(wglu), full(bglu), full(ng), full(dsk)],
        out_specs=rows(D),
        out_shape=_sds((N, D), F32),
        compiler_params=_cp(("parallel",), 48),
        name="merge",
    )(xa, ya, yf, yb, xbc_c, proj, y5, yd, proj, mods3, wa, wb, wc, wd, wo, wglu, bglu, ng, dsk)


def _prefix_excl(mask_f):
    R, T = mask_f.shape
    r = lax.broadcasted_iota(I32, (128, 128), 0)
    c = lax.broadcasted_iota(I32, (128, 128), 1)
    upper = (r <= c).astype(BF16)
    outs = []
    off = jnp.zeros((R, 1), F32)
    for k in range(T // 128):
        blk = mask_f[:, k * 128:(k + 1) * 128]
        inc = jnp.dot(blk.astype(BF16), upper, preferred_element_type=F32)
        outs.append(inc - blk + off)
        off = off + inc[:, 127:128]
    return jnp.concatenate(outs, axis=1)


def _topk_slots(affs, caps):
    E = affs[0].shape[0]
    bits = [pltpu.bitcast(a, I32) for a in affs]

    def step(i, ths):
        out = []
        for b, cap, th in zip(bits, caps, ths):
            cand = th | (jnp.int32(1) << (30 - i))
            cnt = jnp.sum((b >= cand).astype(I32), axis=1, keepdims=True)
            out.append(jnp.where(cnt >= cap, cand, th))
        return tuple(out)

    ths = lax.fori_loop(0, 31, step, tuple(jnp.zeros((E, 1), I32) for _ in affs))
    slots = []
    for b, cap, th in zip(bits, caps, ths):
        gt = b > th
        eq = b == th
        n_gt = jnp.sum(gt.astype(F32), axis=1, keepdims=True)
        sel = gt | (eq & (n_gt + _prefix_excl(eq.astype(F32)) < cap))
        slots.append(jnp.where(sel, _prefix_excl(sel.astype(F32)), -1.0))
    return slots


def _router_body(x_ref, g_ref, shl_ref, scl_ref, shc_ref, scc_ref, wr_ref, hm_ref, slot_ref, gate_ref, rng_ref,
                 *, Sc, cap_l, cap_c, ctx_out, tb):
    x = x_ref[...]
    S = x.shape[0]
    ms = jnp.mean(x * x, axis=-1, keepdims=True)
    y = x * lax.rsqrt(ms + RMS_EPS) * g_ref[...]
    t = lax.broadcasted_iota(I32, (S, 1), 0)
    isc = t < Sc
    hm = y * (1.0 + jnp.where(isc, scc_ref[...], scl_ref[...])) + jnp.where(isc, shc_ref[...], shl_ref[...])
    hm_ref[...] = hm.astype(hm_ref.dtype)
    wr = wr_ref[...]
    h_hi = hm.astype(BF16)
    h_lo = (hm - h_hi.astype(F32)).astype(BF16)
    w_hi = wr.astype(BF16)
    w_lo = (wr - w_hi.astype(F32)).astype(BF16)
    logits = (jnp.dot(h_hi, w_hi, preferred_element_type=F32) + jnp.dot(h_lo, w_hi, preferred_element_type=F32)
              + jnp.dot(h_hi, w_lo, preferred_element_type=F32))
    lt = logits.T[0:N_EXPERTS, :]
    mx = jnp.max(lt, axis=0, keepdims=True)
    ex = jnp.exp(lt - mx)
    aff = ex / jnp.sum(ex, axis=0, keepdims=True)
    gate_ref[...] = aff
    if ctx_out:
        slot_l, slot_c = _topk_slots([aff[:, Sc:], aff[:, 0:Sc]], [cap_l, cap_c])
        slot_c = jnp.where(slot_c >= 0.0, slot_c + cap_l, -1.0)
    else:
        slot_l, = _topk_slots([aff[:, Sc:]], [cap_l])
        slot_c = jnp.full((N_EXPERTS, Sc), -1.0, F32)
    slot_ref[:, 0:Sc] = slot_c
    slot_ref[:, Sc:] = slot_l
    lane = lax.broadcasted_iota(I32, (1, 128), 1)
    lo_a = jnp.zeros((N_EXPERTS, 128), F32)
    nwin = jnp.zeros((N_EXPERTS, 1), F32)
    for seg0, seg1, base, slots in ((0, Sc, float(cap_l), slot_c), (Sc, S, 0.0, slot_l)):
        lo = jnp.full((N_EXPERTS, 1), base, F32)
        for k in range(seg0 // tb, seg1 // tb):
            blk = slots[:, k * tb - seg0:(k + 1) * tb - seg0]
            hi = lo + jnp.sum((blk >= 0.0).astype(F32), axis=1, keepdims=True)
            lo_al = jnp.floor(lo * (1.0 / SLOT_ALIGN)) * SLOT_ALIGN
            need = jnp.where(hi > lo, jnp.floor((hi - lo_al + (SLOT_WINDOW - 1)) * (1.0 / SLOT_WINDOW)), 0.0)
            nwin = jnp.maximum(nwin, need)
            lo_a = jnp.where(lane == k, lo, lo_a)
            lo = hi
    rng_ref[...] = jnp.where(lane == 127, jnp.max(nwin, axis=0, keepdims=True), lo_a).astype(I32)


def _token_block(S, Sc):
    return math.gcd(math.gcd(Sc, S - Sc), 256)


def _router(xa, g, mods3, wr_pad, B, S, Sc, ctx_out):
    N, D = xa.shape
    T = S - Sc
    cap_l = EC_CAPACITY * T // N_EXPERTS
    cap_c = EC_CAPACITY * Sc // N_EXPERTS
    tb = _token_block(S, Sc)
    lat = lambda ch: pl.BlockSpec((None, 1, D), lambda b_: (b_, 0, ch))
    ctx = lambda ch: pl.BlockSpec((None, 1, D), lambda b_: (B, 0, ch))
    es = pl.BlockSpec((None, N_EXPERTS, S), lambda b_: (b_, 0, 0))
    return pl.pallas_call(
        functools.partial(_router_body, Sc=Sc, cap_l=cap_l, cap_c=cap_c, ctx_out=ctx_out, tb=tb),
        grid=(B,),
        in_specs=[pl.BlockSpec((S, D), lambda b_: (b_, 0)),
                  pl.BlockSpec((1, D), lambda b_: (0, 0)),
                  lat(3), lat(4), ctx(3), ctx(4),
                  pl.BlockSpec((D, 128), lambda b_: (0, 0))],
        out_specs=[pl.BlockSpec((S, D), lambda b_: (b_, 0)), es, es,
                   pl.BlockSpec((None, N_EXPERTS, 128), lambda b_: (b_, 0, 0))],
        out_shape=[_sds((N, D), BF16), _sds((B, N_EXPERTS, S), F32), _sds((B, N_EXPERTS, S), F32),
                   _sds((B, N_EXPERTS, 128), I32)],
        compiler_params=_cp(("parallel",), 56),
        name="router",
    )(xa, g, mods3, mods3, mods3, mods3, wr_pad)


def _window_rows(rng_ref, k, j, ncap):
    rows = []
    for e in range(N_EXPERTS):
        lo = rng_ref[e, k]
        lo_al = lax.shift_left(lax.shift_right_logical(lo, SLOT_ALIGN.bit_length() - 1), SLOT_ALIGN.bit_length() - 1)
        rows.append(pl.multiple_of(jnp.minimum(lo_al + j * SLOT_WINDOW, ncap), SLOT_ALIGN))
    return rows


def _window_hits(rows, slot_ref):
    wrow = lax.broadcasted_iota(I32, (SLOT_WINDOW, 1), 0)
    return [(wrow + rows[e]).astype(F32) == slot_ref[e:e + 1, :] for e in range(N_EXPERTS)]


def _moe_gather_body(rng_ref, hm_ref, slot_ref, gate_ref, xg_ref, gv_ref, *, ncap):
    k = pl.program_id(1)
    W = SLOT_WINDOW

    @pl.when(k == 0)
    def _():
        xg_ref[...] = jnp.zeros_like(xg_ref)
        gv_ref[...] = jnp.zeros_like(gv_ref)

    def window(j, carry):
        rows = _window_rows(rng_ref, k, j, ncap)
        hits = _window_hits(rows, slot_ref)
        hit_all = jnp.concatenate([h.astype(BF16) for h in hits], axis=0)
        xgw = jnp.dot(hit_all, hm_ref[...], preferred_element_type=F32).astype(BF16)
        for e in range(N_EXPERTS):
            r = pl.ds(rows[e], W)
            xg_ref[e, r, :] += xgw[e * W:(e + 1) * W, :]
            gv_ref[e, r, :] += jnp.sum(jnp.where(hits[e], gate_ref[e:e + 1, :], 0.0), axis=1, keepdims=True)
        return carry

    lax.fori_loop(0, rng_ref[0, 127], window, 0)


def _moe_ffn_body(xg_ref, gv_ref, wg_ref, wu_ref, wd_ref, yw_ref, wgs_ref, wus_ref, wds_ref, *, ncap):
    @pl.when(pl.program_id(1) == 0)
    def _():
        wgs_ref[...] = wg_ref[...].astype(BF16)
        wus_ref[...] = wu_ref[...].astype(BF16)
        wds_ref[...] = wd_ref[...].astype(BF16)

    xg = xg_ref[0:ncap, :]
    hid = _silu(jnp.dot(xg, wgs_ref[...], preferred_element_type=F32)) * jnp.dot(
        xg, wus_ref[...], preferred_element_type=F32)
    ye = jnp.dot(hid.astype(BF16), wds_ref[...], preferred_element_type=F32) * gv_ref[0:ncap, :]
    yw_ref[0:ncap, :] = ye.astype(yw_ref.dtype)
    yw_ref[ncap:, :] = jnp.zeros((yw_ref.shape[0] - ncap, yw_ref.shape[1]), yw_ref.dtype)


def _moe_scatter_body(rng_ref, slot_ref, yw_ref, x_ref, gl_ref, gc_ref, fg_ref, o_ref, *, ncap, nbc, final):
    k = pl.program_id(1)
    W = SLOT_WINDOW

    def window(j, acc):
        rows = _window_rows(rng_ref, k, j, ncap)
        hit_all = jnp.concatenate([h.astype(BF16) for h in _window_hits(rows, slot_ref)], axis=0)
        yw = jnp.concatenate([yw_ref[e, pl.ds(rows[e], W), :] for e in range(N_EXPERTS)], axis=0)
        return acc + lax.dot_general(hit_all, yw, (((0,), (0,)), ((), ())), preferred_element_type=F32)

    def emit():
        acc = lax.fori_loop(0, rng_ref[0, 127], window, jnp.zeros(o_ref.shape, F32))
        x = x_ref[...] + jnp.where(k < nbc, gc_ref[...], gl_ref[...]) * acc
        if final:
            x = x * lax.rsqrt(jnp.mean(x * x, axis=-1, keepdims=True) + RMS_EPS) * fg_ref[...]
        o_ref[...] = x

    if final:
        pl.when(k >= nbc)(emit)
    else:
        emit()


def _experts(hm, slot, gate, rng, wg, wu, wd, li, xres, mods3, B, S, Sc, ctx_out, final_g):
    N, D = hm.shape
    _, E, _, Fd = wg.shape
    T = S - Sc
    tb = _token_block(S, Sc)
    nblk, nbc = S // tb, Sc // tb
    ncap = EC_CAPACITY * T // N_EXPERTS + (EC_CAPACITY * Sc // N_EXPERTS if ctx_out else 0)
    assert ncap % SLOT_ALIGN == 0
    rows = ncap + SLOT_WINDOW
    smem = pl.BlockSpec((None, E, 128), lambda b_, k: (b_, 0, 0), memory_space=pltpu.SMEM)
    es = pl.BlockSpec((None, E, tb), lambda b_, k: (b_, 0, k))
    xg, gv = pl.pallas_call(
        functools.partial(_moe_gather_body, ncap=ncap),
        grid=(B, nblk),
        in_specs=[smem, pl.BlockSpec((tb, D), lambda b_, k: (b_ * nblk + k, 0)), es, es],
        out_specs=[pl.BlockSpec((None, E, rows, D), lambda b_, k: (b_, 0, 0, 0)),
                   pl.BlockSpec((None, E, rows, 1), lambda b_, k: (b_, 0, 0, 0))],
        out_shape=[_sds((B, E, rows, D), BF16), _sds((B, E, rows, 1), F32)],
        compiler_params=_cp(("parallel", "arbitrary"), 48),
        name="moe_gather",
    )(rng, hm, slot, gate)
    wspec = lambda a, b: pl.BlockSpec((None, None, a, b), lambda e, b_: (li, e, 0, 0))
    yw = pl.pallas_call(
        functools.partial(_moe_ffn_body, ncap=ncap),
        grid=(E, B),
        in_specs=[pl.BlockSpec((None, None, rows, D), lambda e, b_: (b_, e, 0, 0)),
                  pl.BlockSpec((None, None, rows, 1), lambda e, b_: (b_, e, 0, 0)),
                  wspec(D, Fd), wspec(D, Fd), wspec(Fd, D)],
        out_specs=pl.BlockSpec((None, None, rows, D), lambda e, b_: (b_, e, 0, 0)),
        out_shape=_sds((B, E, rows, D), BF16),
        scratch_shapes=[pltpu.VMEM((D, Fd), BF16), pltpu.VMEM((D, Fd), BF16), pltpu.VMEM((Fd, D), BF16)],
        compiler_params=_cp(("parallel", "arbitrary"), 48),
        name="moe_ffn",
    )(xg, gv, wg, wu, wd)
    final = final_g is not None
    nl = nblk - nbc
    if final:
        out_spec = pl.BlockSpec((tb, D), lambda b_, k: (b_ * nl + jnp.maximum(k - nbc, 0), 0))
        out_shape = _sds((B * nl * tb, D), F32)
    else:
        out_spec = pl.BlockSpec((tb, D), lambda b_, k: (b_ * nblk + k, 0))
        out_shape = _sds((N, D), F32)
        final_g = jnp.ones((1, D), F32)
    return pl.pallas_call(
        functools.partial(_moe_scatter_body, ncap=ncap, nbc=nbc, final=final),
        grid=(B, nblk),
        in_specs=[smem, es, pl.BlockSpec((None, E, rows, D), lambda b_, k: (b_, 0, 0, 0)),
                  pl.BlockSpec((tb, D), lambda b_, k: (b_ * nblk + k, 0)),
                  pl.BlockSpec((None, 1, D), lambda b_, k: (b_, 0, 5)),
                  pl.BlockSpec((None, 1, D), lambda b_, k: (B, 0, 5)),
                  pl.BlockSpec((1, D), lambda b_, k: (0, 0))],
        out_specs=out_spec,
        out_shape=out_shape,
        compiler_params=_cp(("parallel", "arbitrary"), 48),
        name="moe_scatter",
    )(rng, slot, yw, xres, mods3, mods3, final_g)


def _layer(xa, mods3, lp, ew, li, tabs, B, S, Sc, ctx_out, final_g):
    N, D = xa.shape
    rc = _row_cfg(S, Sc)
    w_in = lp["w_in"]
    seg = lambda a, n: w_in[:, a:a + n]
    o_q, o_k, o_v, o_z = 0, ATTN_Q_DIM, ATTN_Q_DIM + ATTN_KV_DIM, ATTN_Q_DIM + 2 * ATTN_KV_DIM
    o_xbc = o_z + SSD_INNER
    o_dt = o_xbc + SSD_CONV_DIM
    o_u = o_dt + SSD_HEADS
    o_sb, o_sg, o_sh = o_u + S5_WIDTH, o_u + S5_WIDTH + SC_WIDTH, o_u + S5_WIDTH + 2 * SC_WIDTH
    o_g = o_sh + SC_WIDTH
    w_p = jnp.concatenate([seg(o_g, 4 * D), seg(o_q, ATTN_Q_DIM), seg(o_z, SSD_INNER), seg(o_k, ATTN_KV_DIM),
                           seg(o_v, ATTN_KV_DIM), seg(o_u, S5_WIDTH), seg(o_sb, SC_WIDTH), seg(o_sg, SC_WIDTH),
                           seg(o_sh, SC_WIDTH), seg(o_xbc, SSD_CONV_DIM)], axis=1).astype(BF16)
    w_dt = jnp.pad(seg(o_dt, SSD_HEADS), ((0, 0), (0, 128 - SSD_HEADS))).astype(BF16)
    proj_a, dt = _in_proj(xa, lp["norm_mix_g"][None, :], mods3, w_p, w_dt, B, S, Sc)

    ya = _attention(proj_a, tabs, lp["q_norm_g"], lp["k_norm_g"], B, S, Sc)

    xbc_c = _ssdconv(proj_a, lp["ssd_conv_w"], lp["ssd_conv_b"][None, :], B, S, Sc)
    yf, yb = _ssd(xbc_c, dt, lp["ssd_dt_bias"], lp["ssd_a_log"], B, S, Sc)

    Lc, Gn, G = S5_CHUNK, S5_GROUPS, S5_GROUP_DIM
    nc5, ncc5 = S // Lc, Sc // Lc
    w5, so5, ar5 = _s5gen(lp["s5_lambda_re"], lp["s5_lambda_im"], lp["s5_log_step"], lp["s5_b_re"], lp["s5_b_im"],
                          lp["s5_c_re"], lp["s5_c_im"], lp["s5_d"])
    u = proj_a[:, PA_U:PA_U + S5_WIDTH].reshape(B, nc5, Lc, S5_WIDTH)
    up = jnp.transpose(u, (2, 1, 0, 3)).reshape(N, S5_WIDTH)
    tr5 = nc5 * B
    y5t = _s5(_transpose2d(up, tr5, "s5_in_t"), w5, so5, ar5, B, nc5, ncc5)
    y5p = _untranspose2d(y5t, tr5, "s5_out_t").reshape(Lc, nc5, B, S5_WIDTH)
    y5 = jnp.transpose(y5p, (2, 1, 0, 3)).reshape(N, S5_WIDTH)

    yd = _shortconv(proj_a, lp["sc_conv_w"], B, S, Sc)

    dsk = jnp.repeat(lp["ssd_d"], SSD_HEAD_DIM)[None, :]
    x1 = _merge(xa, ya, yf, yb, xbc_c, proj_a, y5, yd, mods3,
                lp["w_br_attn"].astype(BF16), lp["w_br_ssd"].astype(BF16), lp["w_br_s5"].astype(BF16),
                lp["w_br_sc"].astype(BF16), lp["w_out"].astype(BF16), lp["s5_w_glu"].astype(BF16),
                lp["s5_b_glu"][None, :], lp["ssd_norm_g"][None, :], dsk, B, rc)

    wr_pad = jnp.pad(lp["w_router"], ((0, 0), (0, 128 - N_EXPERTS)))
    hm, slot, gate, rng = _router(x1, lp["norm_ffn_g"][None, :], mods3, wr_pad, B, S, Sc, ctx_out)
    return _experts(hm, slot, gate, rng, ew[0], ew[1], ew[2], li, x1, mods3, B, S, Sc, ctx_out, final_g)


def kernel(x, c, ctx, c_ctx, w_mod, b_mod, norm_mix_g, norm_ffn_g, w_in, q_norm_g, k_norm_g, ssd_conv_w, ssd_conv_b, ssd_dt_bias, ssd_a_log, ssd_d, ssd_norm_g, s5_lambda_re, s5_lambda_im, s5_log_step, s5_b_re, s5_b_im, s5_c_re, s5_c_im, s5_d, s5_w_glu, s5_b_glu, sc_conv_w, w_br_attn, w_br_ssd, w_br_s5, w_br_sc, w_out, w_router, w_exp_gate, w_exp_up, w_exp_down, final_norm_g):
    B, T, D = x.shape
    Sc = ctx.shape[1]
    S = Sc + T
    depth = w_in.shape[0]
    xa = jnp.concatenate([ctx, x], axis=1).reshape(B * S, D)
    cc = jnp.zeros((16, D), F32).at[0:B].set(c).at[B].set(c_ctx)
    cos, sin = _rope_tables(T, Sc)
    tabs = (jnp.tile(cos, (1, ATTN_HEADS)), jnp.tile(sin, (1, ATTN_HEADS)),
            jnp.tile(cos, (1, ATTN_KV_HEADS)), jnp.tile(sin, (1, ATTN_KV_HEADS)),
            _block_diag_ones(ATTN_Q_DIM), _block_diag_ones(ATTN_KV_DIM))
    stacked = dict(
        w_in=w_in, norm_mix_g=norm_mix_g, norm_ffn_g=norm_ffn_g, q_norm_g=q_norm_g, k_norm_g=k_norm_g,
        ssd_conv_w=ssd_conv_w, ssd_conv_b=ssd_conv_b, ssd_dt_bias=ssd_dt_bias, ssd_a_log=ssd_a_log, ssd_d=ssd_d,
        ssd_norm_g=ssd_norm_g, s5_lambda_re=s5_lambda_re, s5_lambda_im=s5_lambda_im, s5_log_step=s5_log_step,
        s5_b_re=s5_b_re, s5_b_im=s5_b_im, s5_c_re=s5_c_re, s5_c_im=s5_c_im, s5_d=s5_d, s5_w_glu=s5_w_glu,
        s5_b_glu=s5_b_glu, sc_conv_w=sc_conv_w, w_br_attn=w_br_attn, w_br_ssd=w_br_ssd, w_br_s5=w_br_s5,
        w_br_sc=w_br_sc, w_out=w_out, w_router=w_router)
    ew = (w_exp_gate, w_exp_up, w_exp_down)
    for i in range(depth):
        lp = {k: v[i] for k, v in stacked.items()}
        mods3 = _mods(cc, w_mod, b_mod[:, None, :], i).reshape(16, 1, 6 * D)
        last = i == depth - 1
        xa = _layer(xa, mods3, lp, ew, i, tabs, B, S, Sc, ctx_out=not last,
                    final_g=final_norm_g[None, :] if last else None)
    return xa.reshape(B, T, D)
```

```python
import functools
import math

import jax
import jax.numpy as jnp
import numpy as np
from jax import lax
from jax.experimental import pallas as pl
from jax.experimental.pallas import tpu as pltpu

F32 = jnp.float32
BF16 = jnp.bfloat16
I32 = jnp.int32
HI = lax.Precision.HIGHEST

RMS_EPS = 1e-6
GRID_W = 64
ROPE_THETA = 10000.0
HEAD_DIM = 64
ATTN_HEADS = 8
ATTN_KV_HEADS = 2
SSD_HEADS = 8
SSD_HEAD_DIM = 64
SSD_GROUPS = 2
SSD_STATE = 64
SSD_CHUNK = 128
S5_GROUPS = 24
S5_GROUP_DIM = 16
S5_STATE = 64
S5_CHUNK = 16
SC_WIDTH = 384
N_EXPERTS = 16
EC_CAPACITY = 2
MAX_ROW_BLOCK = 256
SLOT_WINDOW = 64
SLOT_ALIGN = 16

ATTN_Q_DIM = ATTN_HEADS * HEAD_DIM
ATTN_KV_DIM = ATTN_KV_HEADS * HEAD_DIM
SSD_INNER = SSD_HEADS * SSD_HEAD_DIM
SSD_BC_DIM = SSD_GROUPS * SSD_STATE
SSD_CONV_DIM = SSD_INNER + 2 * SSD_BC_DIM
S5_WIDTH = S5_GROUPS * S5_GROUP_DIM

PA_G, PA_Q, PA_Z, PA_K, PA_V = 0, 4096, 4608, 5120, 5248
PA_U, PA_SB, PA_SG, PA_SH, PA_XBC, PA_W = 5376, 5760, 6144, 6528, 6912, 7680


def _sds(shape, dtype):
    return jax.ShapeDtypeStruct(shape, dtype)


def _cp(sem, vmem_mb=None):
    kw = dict(dimension_semantics=sem)
    if vmem_mb is not None:
        kw["vmem_limit_bytes"] = vmem_mb << 20
    return pltpu.CompilerParams(**kw)


def _sigmoid(x):
    return 0.5 * jnp.tanh(0.5 * x) + 0.5


def _silu(x):
    return x * _sigmoid(x)


def _mods_body(c_ref, w_ref, b_ref, o_ref):
    s = _silu(c_ref[...])
    o_ref[...] = jnp.dot(s.astype(BF16), w_ref[...].astype(BF16), preferred_element_type=F32) + b_ref[...]


def _mods(cc, w, b, li):
    R, D = cc.shape
    N = w.shape[2]
    tn = 1536
    return pl.pallas_call(
        _mods_body,
        grid=(N // tn,),
        in_specs=[pl.BlockSpec((R, D), lambda j: (0, 0)),
                  pl.BlockSpec((None, D, tn), lambda j: (li, 0, j)),
                  pl.BlockSpec((None, 1, tn), lambda j: (li, 0, j))],
        out_specs=pl.BlockSpec((R, tn), lambda j: (0, j)),
        out_shape=_sds((R, N), F32),
        compiler_params=_cp(("parallel",), 40),
        name="mods",
    )(cc, w, b)


def _row_cfg(S, Sc):
    tm = math.gcd(math.gcd(Sc, S - Sc), MAX_ROW_BLOCK)
    return tm, S // tm, Sc // tm


def _mod_spec(D, rc, B, chunk):
    _, nb, nbc = rc
    return pl.BlockSpec((None, 1, D), lambda i: (jnp.where(i % nb < nbc, B, i // nb), 0, chunk))


def _in_proj_body(x_ref, g_ref, shl_ref, scl_ref, shc_ref, scc_ref, w_ref, wdt_ref, o_ref, dt_ref, hn_ref,
                  *, tm, nb, Sc):
    @pl.when(pl.program_id(1) == 0)
    def _():
        x = x_ref[...]
        ms = jnp.mean(x * x, axis=-1, keepdims=True)
        y = x * lax.rsqrt(ms + RMS_EPS) * g_ref[...]
        t = (pl.program_id(0) % nb) * tm + lax.broadcasted_iota(I32, (tm, 1), 0)
        isc = t < Sc
        hn = y * (1.0 + jnp.where(isc, scc_ref[...], scl_ref[...])) + jnp.where(isc, shc_ref[...], shl_ref[...])
        hn_ref[...] = hn.astype(BF16)
        dt_ref[...] = jnp.dot(hn_ref[...], wdt_ref[...], preferred_element_type=F32)

    o_ref[...] = jnp.dot(hn_ref[...], w_ref[...], preferred_element_type=F32).astype(o_ref.dtype)


def _in_proj(xa, g, mods3, w_p, w_dt, B, S, Sc):
    N, D = xa.shape
    tm = S // 2 if (S // 2) % 128 == 0 else S
    nb = S // tm
    tn = 1536
    assert PA_W % tn == 0
    lat = lambda ch: pl.BlockSpec((None, 1, D), lambda i, j: (i // nb, 0, ch))
    ctx = lambda ch: pl.BlockSpec((None, 1, D), lambda i, j: (B, 0, ch))
    return pl.pallas_call(
        functools.partial(_in_proj_body, tm=tm, nb=nb, Sc=Sc),
        grid=(N // tm, PA_W // tn),
        in_specs=[pl.BlockSpec((tm, D), lambda i, j: (i, 0)),
                  pl.BlockSpec((1, D), lambda i, j: (0, 0)),
                  lat(0), lat(1), ctx(0), ctx(1),
                  pl.BlockSpec((D, tn), lambda i, j: (0, j)),
                  pl.BlockSpec((D, 128), lambda i, j: (0, 0))],
        out_specs=[pl.BlockSpec((tm, tn), lambda i, j: (i, j)), pl.BlockSpec((tm, 128), lambda i, j: (i, 0))],
        out_shape=[_sds((N, PA_W), BF16), _sds((N, 128), F32)],
        scratch_shapes=[pltpu.VMEM((tm, D), BF16)],
        compiler_params=_cp(("parallel", "arbitrary"), 48),
        name="in_proj",
    )(xa, g, mods3, mods3, mods3, mods3, w_p, w_dt)


def _shifted(x, Sc):
    S = x.shape[0]
    t = lax.broadcasted_iota(I32, (S, 1), 0)
    prev = jnp.where((t == 0) | (t == Sc), 0.0, pltpu.roll(x, 1, 0))
    nxt = jnp.where((t == Sc - 1) | (t == S - 1), 0.0, pltpu.roll(x, S - 1, 0))
    return prev, nxt


def _ssdconv_body(x_ref, w_ref, b_ref, o_ref, *, Sc):
    x = x_ref[...].astype(F32)
    prev, nxt = _shifted(x, Sc)
    w = w_ref[...]
    y = w[0:1, :] * prev + w[1:2, :] * x + w[2:3, :] * nxt + b_ref[...]
    o_ref[...] = _silu(y).astype(o_ref.dtype)


def _ssdconv(proj_a, w, b, B, S, Sc):
    C = 384
    j0 = PA_XBC // C
    return pl.pallas_call(
        functools.partial(_ssdconv_body, Sc=Sc),
        grid=(B, SSD_CONV_DIM // C),
        in_specs=[pl.BlockSpec((S, C), lambda b_, j: (b_, j0 + j)),
                  pl.BlockSpec((3, C), lambda b_, j: (0, j)),
                  pl.BlockSpec((1, C), lambda b_, j: (0, j))],
        out_specs=pl.BlockSpec((S, C), lambda b_, j: (b_, j)),
        out_shape=_sds((B * S, SSD_CONV_DIM), BF16),
        compiler_params=_cp(("parallel", "parallel"), 40),
        name="ssdconv",
    )(proj_a, w, b)


def _shortconv_body(sb_ref, sg_ref, sh_ref, w_ref, o_ref, *, Sc):
    x = sg_ref[...].astype(F32) * sh_ref[...].astype(F32)
    prev, nxt = _shifted(x, Sc)
    w = w_ref[...]
    y = w[0:1, :] * prev + w[1:2, :] * x + w[2:3, :] * nxt
    o_ref[...] = (sb_ref[...].astype(F32) * y).astype(o_ref.dtype)


def _shortconv(proj_a, w, B, S, Sc):
    C = SC_WIDTH
    return pl.pallas_call(
        functools.partial(_shortconv_body, Sc=Sc),
        grid=(B,),
        in_specs=[pl.BlockSpec((S, C), lambda b_: (b_, PA_SB // C)),
                  pl.BlockSpec((S, C), lambda b_: (b_, PA_SG // C)),
                  pl.BlockSpec((S, C), lambda b_: (b_, PA_SH // C)),
                  pl.BlockSpec((3, C), lambda b_: (0, 0))],
        out_specs=pl.BlockSpec((S, C), lambda b_: (b_, 0)),
        out_shape=_sds((B * S, C), BF16),
        compiler_params=_cp(("parallel",), 40),
        name="shortconv",
    )(proj_a, proj_a, proj_a, w)


def _norm_rope(x, g, cos, sin, bd, scale):
    W = x.shape[1]
    sq = x * x
    hi = sq.astype(BF16)
    lo = (sq - hi.astype(F32)).astype(BF16)
    ssq = jnp.dot(hi, bd, preferred_element_type=F32) + jnp.dot(lo, bd, preferred_element_type=F32)
    y = x * lax.rsqrt(ssq * (1.0 / HEAD_DIM) + RMS_EPS) * g
    lane = lax.broadcasted_iota(I32, (1, W), 1)
    first = (lane % 32) < 16
    partner = jnp.where(first, pltpu.roll(y, W - 16, 1), pltpu.roll(y, 16, 1))
    return (y * cos + partner * sin) * scale


def _attn_body(q_ref, k_ref, v_ref, cq_ref, sq_ref, ck_ref, sk_ref, gq_ref, gk_ref, bdq_ref, bdk_ref,
               o_ref, kh_ref, ve_ref, *, Sc, S, tq):
    qb = pl.program_id(1)
    hd = HEAD_DIM

    @pl.when(qb == 0)
    def _():
        k = k_ref[...].astype(F32)
        kh_ref[...] = _norm_rope(k, gk_ref[...], ck_ref[...], sk_ref[...], bdk_ref[...], 1.0).astype(BF16)
        v = v_ref[...].astype(F32)
        lane = lax.broadcasted_iota(I32, (1, 2 * hd), 1)
        ve_ref[0] = jnp.where(lane < hd, v, 1.0).astype(BF16)
        ve_ref[1] = jnp.where(lane < hd, pltpu.roll(v, hd, 1), 1.0).astype(BF16)

    q = q_ref[...].astype(F32)
    qh = _norm_rope(q, gq_ref[...], cq_ref[...], sq_ref[...], bdq_ref[...],
                    HEAD_DIM ** -0.5 * math.log2(math.e)).astype(BF16)
    rep = ATTN_HEADS // ATTN_KV_HEADS
    nt = (((1,), (1,)), ((), ()))

    def attend(splits):
        scores = []
        for g in range(ATTN_KV_HEADS):
            qg = jnp.concatenate([qh[:, (g * rep + r) * hd:(g * rep + r + 1) * hd] for r in range(rep)], axis=0)
            scores.append([lax.dot_general(qg, kh_ref[a:b, g * hd:(g + 1) * hd], nt, preferred_element_type=F32)
                           for a, b in splits])
        for g in range(ATTN_KV_HEADS):
            m = functools.reduce(jnp.maximum, [jnp.max(s, axis=-1, keepdims=True) for s in scores[g]])
            acc = None
            for s, (a, b) in zip(scores[g], splits):
                p = jnp.exp2(s - m).astype(BF16)
                part = jnp.dot(p, ve_ref[g, a:b, :], preferred_element_type=F32)
                acc = part if acc is None else acc + part
            o = acc[:, 0:hd] / acc[:, hd:hd + 1]
            for r in range(rep):
                h = g * rep + r
                o_ref[:, h * hd:(h + 1) * hd] = o[r * tq:(r + 1) * tq].astype(o_ref.dtype)

    @pl.when(qb < Sc // tq)
    def _():
        attend([(0, Sc)])

    @pl.when(qb >= Sc // tq)
    def _():
        half = (S // 2 + 255) // 256 * 256 if S >= 512 else S
        attend([(0, half), (half, S)] if half < S else [(0, S)])


def _rope_tables(T, Sc):
    rows = T // GRID_W
    row = np.repeat(np.arange(rows, dtype=np.float32), GRID_W)
    col = np.tile(np.arange(GRID_W, dtype=np.float32), rows)
    half = HEAD_DIM // 2
    inv = jnp.asarray(ROPE_THETA, F32) ** (-jnp.arange(0, half, 2, dtype=F32) / half)
    ra = jnp.asarray(row)[:, None] * inv
    ca = jnp.asarray(col)[:, None] * inv
    cos = jnp.concatenate([jnp.cos(ra), jnp.cos(ra), jnp.cos(ca), jnp.cos(ca)], axis=1)
    sin = jnp.concatenate([-jnp.sin(ra), jnp.sin(ra), -jnp.sin(ca), jnp.sin(ca)], axis=1)
    cos = jnp.concatenate([jnp.ones((Sc, HEAD_DIM), F32), cos], axis=0)
    sin = jnp.concatenate([jnp.zeros((Sc, HEAD_DIM), F32), sin], axis=0)
    return cos, sin


def _block_diag_ones(W):
    i = np.arange(W) // HEAD_DIM
    return jnp.asarray((i[:, None] == i[None, :]).astype(np.float32), BF16)


def _attention(proj_a, tabs, q_norm_g, k_norm_g, B, S, Sc):
    cos_q, sin_q, cos_k, sin_k, bdq, bdk = tabs
    tq = math.gcd(Sc, 256)
    nq = S // tq
    gq = jnp.tile(q_norm_g, ATTN_HEADS)[None, :]
    gk = jnp.tile(k_norm_g, ATTN_KV_HEADS)[None, :]
    const = lambda b_, i: (0, 0)
    return pl.pallas_call(
        functools.partial(_attn_body, Sc=Sc, S=S, tq=tq),
        grid=(B, nq),
        in_specs=[pl.BlockSpec((tq, ATTN_Q_DIM), lambda b_, i: (b_ * nq + i, PA_Q // ATTN_Q_DIM)),
                  pl.BlockSpec((S, ATTN_KV_DIM), lambda b_, i: (b_, PA_K // ATTN_KV_DIM)),
                  pl.BlockSpec((S, ATTN_KV_DIM), lambda b_, i: (b_, PA_V // ATTN_KV_DIM)),
                  pl.BlockSpec((tq, ATTN_Q_DIM), lambda b_, i: (i, 0)),
                  pl.BlockSpec((tq, ATTN_Q_DIM), lambda b_, i: (i, 0)),
                  pl.BlockSpec((S, ATTN_KV_DIM), const),
                  pl.BlockSpec((S, ATTN_KV_DIM), const),
                  pl.BlockSpec((1, ATTN_Q_DIM), const),
                  pl.BlockSpec((1, ATTN_KV_DIM), const),
                  pl.BlockSpec((ATTN_Q_DIM, ATTN_Q_DIM), const),
                  pl.BlockSpec((ATTN_KV_DIM, ATTN_KV_DIM), const)],
        out_specs=pl.BlockSpec((tq, ATTN_Q_DIM), lambda b_, i: (b_ * nq + i, 0)),
        out_shape=_sds((B * S, ATTN_Q_DIM), BF16),
        scratch_shapes=[pltpu.VMEM((S, ATTN_KV_DIM), BF16),
                        pltpu.VMEM((ATTN_KV_HEADS, S, 2 * HEAD_DIM), BF16)],
        compiler_params=_cp(("parallel", "arbitrary"), 56),
        name="attention",
    )(proj_a, proj_a, proj_a, cos_q, sin_q, cos_k, sin_k, gq, gk, bdq, bdk)


def _ssd_chains(chains):
    L = SSD_CHUNK
    P, Nst = SSD_HEAD_DIM, SSD_STATE
    rep = SSD_HEADS // SSD_GROUPS
    row = lax.broadcasted_iota(I32, (L, L), 0)
    col = lax.broadcasted_iota(I32, (L, L), 1)
    tri = (row >= col).astype(F32)
    lane = lax.broadcasted_iota(I32, (1, 2 * P), 1)
    lo_half = lane < P
    blockdiag = (row // Nst) == (col // P)
    upper = lax.broadcasted_iota(I32, (2 * Nst, 1), 0) < Nst
    st = []
    for xbc_ref, dt_ref, bias, a_neg, h_ref, y_ref, rev in chains:
        xbc = xbc_ref[...]
        bm = xbc[:, SSD_INNER:SSD_INNER + SSD_BC_DIM]
        cm = xbc[:, SSD_INNER + SSD_BC_DIM:SSD_CONV_DIM]
        dt = jax.nn.softplus(dt_ref[...] + bias)
        a = dt * a_neg
        cum = jnp.dot(tri, a, precision=HI, preferred_element_type=F32)
        e = cum - a if rev else cum
        tot_row = cum[L - 1:L, :]
        win_c = jnp.exp(tot_row - e) if rev else jnp.exp(e)
        win_hi = win_c.astype(BF16)
        cm32 = cm.astype(F32)
        st.append(dict(
            rev=rev, h_ref=h_ref, y_ref=y_ref, xbc=xbc, cm=cm, cm32=cm32, cm_rolled=pltpu.roll(cm32, P, 1),
            cum=cum, e=e, eT=e.T, dtT=dt.T, bT=bm.astype(F32).T.astype(BF16),
            mask=(col >= row) if rev else (row >= col),
            win_hi=win_hi, win_lo=(win_c - win_hi.astype(F32)).astype(BF16)))
    for g in range(SSD_GROUPS):
        own = (lane // P) == g
        for c in st:
            cg0 = jnp.where(own, c["cm"], jnp.zeros_like(c["cm"]))
            c["sg"] = jnp.dot(cg0, c["bT"], preferred_element_type=F32)
            c["cdup"] = jnp.where(own, c["cm32"], c["cm_rolled"])
            c["bTg"] = c["bT"][g * Nst:(g + 1) * Nst, :].astype(F32)
        for q in range(rep // 2):
            pq = (g * rep) // 2 + q
            hs = (2 * pq, 2 * pq + 1)
            sel = (lax.broadcasted_iota(I32, (2 * P, 2 * P), 0) == hs[0] + col // P).astype(BF16)
            for c in st:
                rev, e, eT, cum, dtT = c["rev"], c["e"], c["eT"], c["cum"], c["dtT"]
                ms, bws, tots = [], [], []
                for h in hs:
                    ecol = e[:, h:h + 1]
                    erow = eT[h:h + 1, :]
                    tot = cum[L - 1:L, h:h + 1]
                    diff = (erow - ecol) if rev else (ecol - erow)
                    dec = jnp.exp(jnp.where(c["mask"], diff, -1e30))
                    ms.append((c["sg"] * dec * dtT[h:h + 1, :]).astype(BF16))
                    wrow = (jnp.exp(erow) if rev else jnp.exp(tot - erow)) * dtT[h:h + 1, :]
                    bws.append(c["bTg"] * wrow)
                    tots.append(jnp.exp(tot))
                xp = c["xbc"][:, 2 * P * pq:2 * P * (pq + 1)]
                zero = jnp.zeros_like(xp)
                xbd = jnp.concatenate([jnp.where(lo_half, xp, zero), jnp.where(lo_half, zero, xp)], axis=0)
                y = jnp.dot(jnp.concatenate(ms, axis=1), xbd, preferred_element_type=F32)
                wexp = (jnp.dot(c["win_hi"], sel, preferred_element_type=F32)
                        + jnp.dot(c["win_lo"], sel, preferred_element_type=F32))
                hp = c["h_ref"][pq]
                y = y + jnp.dot((wexp * c["cdup"]).astype(BF16), hp.astype(BF16), preferred_element_type=F32)
                upd = jnp.dot(jnp.concatenate(bws, axis=0).astype(BF16), xp, preferred_element_type=F32)
                keep = jnp.where(upper, tots[0], tots[1])
                c["h_ref"][pq] = jnp.where(blockdiag, keep * hp + upd, 0.0)
                c["y_ref"][:, 2 * P * pq:2 * P * (pq + 1)] = y.astype(c["y_ref"].dtype)


def _ssd_body(xf_ref, dtf_ref, xb_ref, dtb_ref, bias_ref, alog_ref, yf_ref, yb_ref, *h_refs, nper):
    @pl.when(pl.program_id(1) == 0)
    def _():
        for h_ref in h_refs:
            h_ref[...] = jnp.zeros_like(h_ref)

    lane = lax.broadcasted_iota(I32, (1, 128), 1)
    a_neg = jnp.where(lane < SSD_HEADS, -jnp.exp(alog_ref[...]), 0.0)
    bias = bias_ref[...]
    chains = []
    for p in range(nper):
        chains.append((xf_ref.at[p], dtf_ref.at[p], bias[0:1, :], a_neg[0:1, :], h_refs[2 * p], yf_ref.at[p], False))
        chains.append((xb_ref.at[p], dtb_ref.at[p], bias[1:2, :], a_neg[1:2, :], h_refs[2 * p + 1], yb_ref.at[p], True))
    _ssd_chains(chains)


def _ssd(xbc_c, dt, dt_bias, a_log, B, S, Sc):
    L = SSD_CHUNK
    nc, ncc = S // L, Sc // L
    nper = math.gcd(B, 4)
    bg = B // nper
    padh = 128 - SSD_HEADS
    bias = jnp.pad(dt_bias, ((0, 0), (0, padh)))
    alog = jnp.pad(a_log, ((0, 0), (0, padh)))
    x3 = xbc_c.reshape(nper, bg * S, SSD_CONV_DIM)
    dt3 = dt.reshape(nper, bg * S, 128)

    def fwd(b_, i):
        return (0, b_ * nc + i, 0)

    def bwd(b_, i):
        return (0, b_ * nc + jnp.where(i < ncc, ncc - 1 - i, nc + ncc - 1 - i), 0)

    const = lambda b_, i: (0, 0)
    out = _sds((nper, bg * S, SSD_INNER), BF16)
    yf, yb = pl.pallas_call(
        functools.partial(_ssd_body, nper=nper),
        grid=(bg, nc),
        in_specs=[pl.BlockSpec((nper, L, SSD_CONV_DIM), fwd), pl.BlockSpec((nper, L, 128), fwd),
                  pl.BlockSpec((nper, L, SSD_CONV_DIM), bwd), pl.BlockSpec((nper, L, 128), bwd),
                  pl.BlockSpec((2, 128), const), pl.BlockSpec((2, 128), const)],
        out_specs=[pl.BlockSpec((nper, L, SSD_INNER), fwd), pl.BlockSpec((nper, L, SSD_INNER), bwd)],
        out_shape=[out, out],
        scratch_shapes=[pltpu.VMEM((SSD_HEADS // 2, 2 * SSD_STATE, 2 * SSD_HEAD_DIM), F32)] * (2 * nper),
        compiler_params=_cp(("parallel", "arbitrary")),
        name="ssd",
    )(x3, dt3, x3, dt3, bias, alog)
    return yf.reshape(B * S, SSD_INNER), yb.reshape(B * S, SSD_INNER)


def _cpow(n, lr, li, st):
    mag = jnp.exp(n * lr * st)
    ang = n * li * st
    return mag * jnp.cos(ang), mag * jnp.sin(ang)


def _zoh_coef(lr, li, st):
    ar, ai = _cpow(1.0, lr, li, st)
    nr, ni = ar - 1.0, ai
    den = lr * lr + li * li
    return (nr * lr + ni * li) / den, (ni * lr - nr * li) / den


def _s5gen_body(colp_ref, rowp_ref, ctr_ref, cti_ref, btr_ref, bti_ref, blr_ref, bli_ref, d_ref,
                wt_ref, so_ref, ar_ref, ws_ref, sos_ref):
    Lc, G, P = S5_CHUNK, S5_GROUP_DIM, S5_STATE
    NS = 2 * Lc + 1
    colp = colp_ref[...]
    rowp = rowp_ref[...]
    lrf, lif, stf = colp[:, 0:1], colp[:, 1:2], jnp.exp(colp[:, 2:3])
    lrb, lib, stb = colp[:, 3:4], colp[:, 4:5], jnp.exp(colp[:, 5:6])
    lane = lax.broadcasted_iota(I32, (1, NS * G), 1)
    slot = lane // G
    isb = slot < Lc
    lag = jnp.abs(slot - Lc).astype(F32)
    pr, pi = _cpow(lag, jnp.where(isb, lrb, lrf), jnp.where(isb, lib, lif), jnp.where(isb, stb, stf))
    ctr, cti = ctr_ref[...], cti_ref[...]
    er = ctr * pr - cti * pi
    ei = ctr * pi + cti * pr
    rlrf, rlif, rstf = rowp[0:1, :], rowp[1:2, :], jnp.exp(rowp[2:3, :])
    rlrb, rlib, rstb = rowp[3:4, :], rowp[4:5, :], jnp.exp(rowp[5:6, :])
    btr, bti = btr_ref[...], bti_ref[...]
    cfr, cfi = _zoh_coef(rlrf, rlif, rstf)
    cbr, cbi = _zoh_coef(rlrb, rlib, rstb)
    bbf_r, bbf_i = cfr * btr - cfi * bti, cfr * bti + cfi * btr
    bbb_r, bbb_i = cbr * btr - cbi * bti, cbr * bti + cbi * btr

    def kt(br, bi):
        return (jnp.dot(br, er, precision=HI, preferred_element_type=F32)
                - jnp.dot(bi, ei, precision=HI, preferred_element_type=F32))

    ktf, ktb = kt(bbf_r, bbf_i), kt(bbb_r, bbb_i)
    ii = lax.broadcasted_iota(I32, (G, NS * G), 0)
    dmat = jnp.where((slot == Lc) & (ii == lane - Lc * G), d_ref[...], 0.0)
    strip = jnp.where(slot == Lc, ktf + ktb, jnp.where(isb, ktb, ktf)) + dmat
    for s in range(Lc):
        off = (Lc - s) * G
        ws_ref[s * G:(s + 1) * G, :] = strip[:, off:off + Lc * G]
    fo = (Lc + 1) * G
    ws_ref[Lc * G:Lc * G + P, :] = er[:, fo:fo + Lc * G]
    ws_ref[Lc * G + P:Lc * G + 2 * P, :] = er[:, 0:Lc * G]
    ws_ref[Lc * G + 2 * P:Lc * G + 3 * P, :] = -ei[:, fo:fo + Lc * G]
    ws_ref[Lc * G + 3 * P:Lc * G + 4 * P, :] = -ei[:, 0:Lc * G]
    wt_ref[:, 0:Lc * G] = ws_ref[0:Lc * G, :].T.astype(wt_ref.dtype)
    wt_ref[:, Lc * G:Lc * G + 4 * P] = ws_ref[Lc * G:Lc * G + 4 * P, :].T.astype(wt_ref.dtype)
    lane2 = lax.broadcasted_iota(I32, (1, Lc * G), 1)
    s_idx = (lane2 // G).astype(F32)
    qfr, qfi = _cpow((Lc - 1.0) - s_idx, lrf, lif, stf)
    qbr, qbi = _cpow(s_idx, lrb, lib, stb)
    ccfr, ccfi = _zoh_coef(lrf, lif, stf)
    ccbr, ccbi = _zoh_coef(lrb, lib, stb)
    blr, bli = blr_ref[...], bli_ref[...]
    bfr, bfi = ccfr * blr - ccfi * bli, ccfr * bli + ccfi * blr
    bbr, bbi = ccbr * blr - ccbi * bli, ccbr * bli + ccbi * blr
    sos_ref[0:P, :] = qfr * bfr - qfi * bfi
    sos_ref[P:2 * P, :] = qbr * bbr - qbi * bbi
    sos_ref[2 * P:3 * P, :] = qfr * bfi + qfi * bfr
    sos_ref[3 * P:4 * P, :] = qbr * bbi + qbi * bbr
    so_ref[...] = sos_ref[...].T.astype(so_ref.dtype)
    afr, afi = _cpow(float(Lc), rlrf, rlif, rstf)
    abr, abi = _cpow(float(Lc), rlrb, rlib, rstb)
    ar_ref[...] = jnp.zeros_like(ar_ref)
    ar_ref[0:1, 0:P] = afr
    ar_ref[0:1, P:2 * P] = abr
    ar_ref[1:2, 0:P] = afi
    ar_ref[1:2, P:2 * P] = abi


def _s5gen(lam_re, lam_im, log_step, b_re, b_im, c_re, c_im, d_skip):
    Gn, P, G, Lc = S5_GROUPS, S5_STATE, S5_GROUP_DIM, S5_CHUNK
    NS = 2 * Lc + 1
    ls = jnp.broadcast_to(log_step[:, :, None], (2, Gn, P))
    z = jnp.zeros((Gn, P), F32)
    rowp = jnp.stack([lam_re[0], lam_im[0], ls[0], lam_re[1], lam_im[1], ls[1], z, z], axis=1)
    colp = jnp.swapaxes(rowp, 1, 2)
    ctr = jnp.tile(jnp.swapaxes(c_re, 1, 2), (1, 1, NS))
    cti = jnp.tile(jnp.swapaxes(c_im, 1, 2), (1, 1, NS))
    btr = jnp.swapaxes(b_re, 1, 2)
    bti = jnp.swapaxes(b_im, 1, 2)
    blr = jnp.tile(b_re, (1, 1, Lc))
    bli = jnp.tile(b_im, (1, 1, Lc))
    dt = jnp.tile(d_skip.reshape(Gn, 1, G), (1, 1, NS))
    g3 = lambda a, b: pl.BlockSpec((None, a, b), lambda g: (g, 0, 0))
    return pl.pallas_call(
        _s5gen_body,
        grid=(Gn,),
        in_specs=[g3(P, 8), g3(8, P), g3(P, NS * G), g3(P, NS * G), g3(G, P), g3(G, P),
                  g3(P, Lc * G), g3(P, Lc * G), g3(1, NS * G)],
        out_specs=[g3(Lc * G, Lc * G + 4 * P), g3(Lc * G, 4 * P), g3(8, 2 * P)],
        out_shape=[_sds((Gn, Lc * G, Lc * G + 4 * P), BF16), _sds((Gn, Lc * G, 4 * P), BF16),
                   _sds((Gn, 8, 2 * P), F32)],
        scratch_shapes=[pltpu.VMEM((Lc * G + 4 * P, Lc * G), F32), pltpu.VMEM((4 * P, Lc * G), F32)],
        compiler_params=_cp(("parallel",)),
        name="s5gen",
    )(colp, rowp, ctr, cti, btr, bti, blr, bli, dt)


def _s5_body(xt_ref, wt_ref, so_ref, ar_ref, y_ref, v_ref, h_ref, *, B, nc, ncc):
    P, Lc, G = S5_STATE, S5_CHUNK, S5_GROUP_DIM
    LG = Lc * G
    R = nc * B
    ucol = jnp.concatenate([xt_ref[:, l * R:(l + 1) * R] for l in range(Lc)], axis=0)
    v_ref[...] = lax.dot_general(ucol, so_ref[...], (((0,), (0,)), ((), ())), preferred_element_type=F32)
    a_re = ar_ref[0:1, :]
    a_im = ar_ref[1:2, :]
    lane = lax.broadcasted_iota(I32, (1, 2 * P), 1)
    isf = lane < P

    def step(i, carry):
        hr, hi = carry
        cf = pl.multiple_of(i * B, B)
        cb = pl.multiple_of(jnp.where(i < ncc, ncc - 1 - i, nc + ncc - 1 - i) * B, B)
        h_ref[pl.ds(cf, B), 0:P] = hr[:, 0:P]
        h_ref[pl.ds(cf, B), 2 * P:3 * P] = hi[:, 0:P]
        h_ref[pl.ds(cb, B), P:2 * P] = hr[:, P:2 * P]
        h_ref[pl.ds(cb, B), 3 * P:4 * P] = hi[:, P:2 * P]
        vr = jnp.where(isf, v_ref[pl.ds(cf, B), 0:2 * P], v_ref[pl.ds(cb, B), 0:2 * P])
        vi = jnp.where(isf, v_ref[pl.ds(cf, B), 2 * P:4 * P], v_ref[pl.ds(cb, B), 2 * P:4 * P])
        return hr * a_re - hi * a_im + vr, hr * a_im + hi * a_re + vi

    z = jnp.zeros((B, 2 * P), F32)
    lax.fori_loop(0, nc, step, (z, z))
    yt = jnp.dot(wt_ref[:, 0:LG], ucol, preferred_element_type=F32)
    yt = yt + lax.dot_general(wt_ref[:, LG:LG + 4 * P], h_ref[...].astype(BF16), (((1,), (1,)), ((), ())),
                              preferred_element_type=F32)
    for t in range(Lc):
        y_ref[:, t * R:(t + 1) * R] = yt[t * G:(t + 1) * G, :]


def _s5(xt, wt, so, ar, B, nc, ncc):
    P, Lc, G, Gn = S5_STATE, S5_CHUNK, S5_GROUP_DIM, S5_GROUPS
    N = xt.shape[1]
    R = nc * B
    g3 = lambda a, b: pl.BlockSpec((None, a, b), lambda g: (g, 0, 0))
    return pl.pallas_call(
        functools.partial(_s5_body, B=B, nc=nc, ncc=ncc),
        grid=(Gn,),
        in_specs=[pl.BlockSpec((G, N), lambda g: (g, 0)), g3(Lc * G, Lc * G + 4 * P), g3(Lc * G, 4 * P),
                  g3(8, 2 * P)],
        out_specs=pl.BlockSpec((G, N), lambda g: (g, 0)),
        out_shape=_sds((Gn * G, N), F32),
        scratch_shapes=[pltpu.VMEM((R, 4 * P), F32), pltpu.VMEM((R, 4 * P), F32)],
        compiler_params=_cp(("parallel",)),
        name="s5",
    )(xt, wt, so, ar)


def _t2d_body(x_ref, o_ref):
    o_ref[...] = x_ref[...].astype(F32).T.astype(o_ref.dtype)


def _transpose2d(x, tr, name):
    M, C = x.shape
    return pl.pallas_call(
        _t2d_body,
        grid=(M // tr,),
        in_specs=[pl.BlockSpec((tr, C), lambda i: (i, 0))],
        out_specs=pl.BlockSpec((C, tr), lambda i: (0, i)),
        out_shape=_sds((C, M), x.dtype),
        compiler_params=_cp(("parallel",)),
        name=name,
    )(x)


def _untranspose2d(xt, tr, name):
    C, M = xt.shape
    return pl.pallas_call(
        _t2d_body,
        grid=(M // tr,),
        in_specs=[pl.BlockSpec((C, tr), lambda i: (0, i))],
        out_specs=pl.BlockSpec((tr, C), lambda i: (i, 0)),
        out_shape=_sds((M, C), xt.dtype),
        compiler_params=_cp(("parallel",)),
        name=name,
    )(xt)


def _merge_body(x_ref, ya_ref, yf_ref, yb_ref, xc_ref, z_ref, y5_ref, yd_ref, g_ref, gate_ref,
                wa_ref, wb_ref, wc_ref, wd_ref, wo_ref, wglu_ref, bglu_ref, ng_ref, dsk_ref, o_ref):
    tm, D = o_ref.shape
    nparts = 2 if tm % 32 == 0 else 1
    parts = [slice(p * tm // nparts, (p + 1) * tm // nparts) for p in range(nparts)]
    ys = [yf_ref[r, :].astype(F32) + yb_ref[r, :].astype(F32) + dsk_ref[...] * xc_ref[r, :].astype(F32) for r in parts]
    ys = [y * _silu(z_ref[r, :].astype(F32)) for y, r in zip(ys, parts)]
    ys = [(y * lax.rsqrt(jnp.mean(y * y, axis=-1, keepdims=True) + RMS_EPS) * ng_ref[...]).astype(BF16) for y in ys]
    cs = [jax.nn.gelu(y5_ref[r, :]).astype(BF16) for r in parts]
    glus = [_sigmoid(jnp.dot(c, wglu_ref[...], preferred_element_type=F32) + bglu_ref[...]) for c in cs]
    cs = [(c.astype(F32) * glu).astype(BF16) for c, glu in zip(cs, glus)]

    def gate(r, k):
        return _sigmoid(g_ref[r, k * D:(k + 1) * D].astype(F32))

    ms = [gate(r, 0) * jnp.dot(ya_ref[r, :], wa_ref[...], preferred_element_type=F32) for r in parts]
    ms = [m + gate(r, 1) * jnp.dot(y, wb_ref[...], preferred_element_type=F32) for m, y, r in zip(ms, ys, parts)]
    ms = [m + gate(r, 2) * jnp.dot(c, wc_ref[...], preferred_element_type=F32) for m, c, r in zip(ms, cs, parts)]
    ms = [m + gate(r, 3) * jnp.dot(yd_ref[r, :], wd_ref[...], preferred_element_type=F32) for m, r in zip(ms, parts)]
    outs = [jnp.dot(m.astype(BF16), wo_ref[...], preferred_element_type=F32) for m in ms]
    for out, r in zip(outs, parts):
        o_ref[r, :] = x_ref[r, :] + gate_ref[...] * out


def _merge(xa, ya, yf, yb, xbc_c, proj, y5, yd, mods3, wa, wb, wc, wd, wo, wglu, bglu, ng, dsk, B, rc):
    N, D = xa.shape
    tm = rc[0]
    rows = lambda w, j=0: pl.BlockSpec((tm, w), lambda i: (i, j))
    full = lambda a: pl.BlockSpec(a.shape, lambda i: (0,) * a.ndim)
    return pl.pallas_call(
        _merge_body,
        grid=(N // tm,),
        in_specs=[rows(D), rows(ATTN_Q_DIM), rows(SSD_INNER), rows(SSD_INNER), rows(SSD_INNER),
                  rows(SSD_INNER, PA_Z // SSD_INNER), rows(S5_WIDTH), rows(SC_WIDTH), rows(4 * D, PA_G // (4 * D)),
                  _mod_spec(D, rc, B, 2),
                  full(wa), full(wb), full(wc), full(wd), full(wo), full(wglu), full(bglu), full(ng), full(dsk)],
        out_specs=rows(D),
        out_shape=_sds((N, D), F32),
        compiler_params=_cp(("parallel",), 48),
        name="merge",
    )(xa, ya, yf, yb, xbc_c, proj, y5, yd, proj, mods3, wa, wb, wc, wd, wo, wglu, bglu, ng, dsk)


def _prefix_excl(mask_f):
    R, T = mask_f.shape
    r = lax.broadcasted_iota(I32, (128, 128), 0)
    c = lax.broadcasted_iota(I32, (128, 128), 1)
    upper = (r <= c).astype(BF16)
    outs = []
    off = jnp.zeros((R, 1), F32)
    for k in range(T // 128):
        blk = mask_f[:, k * 128:(k + 1) * 128]
        inc = jnp.dot(blk.astype(BF16), upper, preferred_element_type=F32)
        outs.append(inc - blk + off)
        off = off + inc[:, 127:128]
    return jnp.concatenate(outs, axis=1)


def _topk_slots(affs, caps):
    E = affs[0].shape[0]
    bits = [pltpu.bitcast(a, I32) for a in affs]

    def step(i, ths):
        out = []
        for b, cap, th in zip(bits, caps, ths):
            cand = th | (jnp.int32(1) << (30 - i))
            cnt = jnp.sum((b >= cand).astype(I32), axis=1, keepdims=True)
            out.append(jnp.where(cnt >= cap, cand, th))
        return tuple(out)

    ths = lax.fori_loop(0, 31, step, tuple(jnp.zeros((E, 1), I32) for _ in affs))
    slots = []
    for b, cap, th in zip(bits, caps, ths):
        gt = b > th
        eq = b == th
        n_gt = jnp.sum(gt.astype(F32), axis=1, keepdims=True)
        sel = gt | (eq & (n_gt + _prefix_excl(eq.astype(F32)) < cap))
        slots.append(jnp.where(sel, _prefix_excl(sel.astype(F32)), -1.0))
    return slots


def _router_body(x_ref, g_ref, shl_ref, scl_ref, shc_ref, scc_ref, wr_ref, hm_ref, slot_ref, gate_ref, rng_ref,
                 *, Sc, cap_l, cap_c, ctx_out, tb):
    x = x_ref[...]
    S = x.shape[0]
    ms = jnp.mean(x * x, axis=-1, keepdims=True)
    y = x * lax.rsqrt(ms + RMS_EPS) * g_ref[...]
    t = lax.broadcasted_iota(I32, (S, 1), 0)
    isc = t < Sc
    hm = y * (1.0 + jnp.where(isc, scc_ref[...], scl_ref[...])) + jnp.where(isc, shc_ref[...], shl_ref[...])
    hm_ref[...] = hm.astype(hm_ref.dtype)
    wr = wr_ref[...]
    h_hi = hm.astype(BF16)
    h_lo = (hm - h_hi.astype(F32)).astype(BF16)
    w_hi = wr.astype(BF16)
    w_lo = (wr - w_hi.astype(F32)).astype(BF16)
    logits = (jnp.dot(h_hi, w_hi, preferred_element_type=F32) + jnp.dot(h_lo, w_hi, preferred_element_type=F32)
              + jnp.dot(h_hi, w_lo, preferred_element_type=F32))
    lt = logits.T[0:N_EXPERTS, :]
    mx = jnp.max(lt, axis=0, keepdims=True)
    ex = jnp.exp(lt - mx)
    aff = ex / jnp.sum(ex, axis=0, keepdims=True)
    gate_ref[...] = aff
    if ctx_out:
        slot_l, slot_c = _topk_slots([aff[:, Sc:], aff[:, 0:Sc]], [cap_l, cap_c])
        slot_c = jnp.where(slot_c >= 0.0, slot_c + cap_l, -1.0)
    else:
        slot_l, = _topk_slots([aff[:, Sc:]], [cap_l])
        slot_c = jnp.full((N_EXPERTS, Sc), -1.0, F32)
    slot_ref[:, 0:Sc] = slot_c
    slot_ref[:, Sc:] = slot_l
    lane = lax.broadcasted_iota(I32, (1, 128), 1)
    lo_a = jnp.zeros((N_EXPERTS, 128), F32)
    nwin = jnp.zeros((N_EXPERTS, 1), F32)
    for seg0, seg1, base, slots in ((0, Sc, float(cap_l), slot_c), (Sc, S, 0.0, slot_l)):
        lo = jnp.full((N_EXPERTS, 1), base, F32)
        for k in range(seg0 // tb, seg1 // tb):
            blk = slots[:, k * tb - seg0:(k + 1) * tb - seg0]
            hi = lo + jnp.sum((blk >= 0.0).astype(F32), axis=1, keepdims=True)
            lo_al = jnp.floor(lo * (1.0 / SLOT_ALIGN)) * SLOT_ALIGN
            need = jnp.where(hi > lo, jnp.floor((hi - lo_al + (SLOT_WINDOW - 1)) * (1.0 / SLOT_WINDOW)), 0.0)
            nwin = jnp.maximum(nwin, need)
            lo_a = jnp.where(lane == k, lo, lo_a)
            lo = hi
    rng_ref[...] = jnp.where(lane == 127, jnp.max(nwin, axis=0, keepdims=True), lo_a).astype(I32)


def _token_block(S, Sc):
    return math.gcd(math.gcd(Sc, S - Sc), 256)


def _router(xa, g, mods3, wr_pad, B, S, Sc, ctx_out):
    N, D = xa.shape
    T = S - Sc
    cap_l = EC_CAPACITY * T // N_EXPERTS
    cap_c = EC_CAPACITY * Sc // N_EXPERTS
    tb = _token_block(S, Sc)
    lat = lambda ch: pl.BlockSpec((None, 1, D), lambda b_: (b_, 0, ch))
    ctx = lambda ch: pl.BlockSpec((None, 1, D), lambda b_: (B, 0, ch))
    es = pl.BlockSpec((None, N_EXPERTS, S), lambda b_: (b_, 0, 0))
    return pl.pallas_call(
        functools.partial(_router_body, Sc=Sc, cap_l=cap_l, cap_c=cap_c, ctx_out=ctx_out, tb=tb),
        grid=(B,),
        in_specs=[pl.BlockSpec((S, D), lambda b_: (b_, 0)),
                  pl.BlockSpec((1, D), lambda b_: (0, 0)),
                  lat(3), lat(4), ctx(3), ctx(4),
                  pl.BlockSpec((D, 128), lambda b_: (0, 0))],
        out_specs=[pl.BlockSpec((S, D), lambda b_: (b_, 0)), es, es,
                   pl.BlockSpec((None, N_EXPERTS, 128), lambda b_: (b_, 0, 0))],
        out_shape=[_sds((N, D), BF16), _sds((B, N_EXPERTS, S), F32), _sds((B, N_EXPERTS, S), F32),
                   _sds((B, N_EXPERTS, 128), I32)],
        compiler_params=_cp(("parallel",), 56),
        name="router",
    )(xa, g, mods3, mods3, mods3, mods3, wr_pad)


def _window_rows(rng_ref, k, j, ncap):
    rows = []
    for e in range(N_EXPERTS):
        lo = rng_ref[e, k]
        lo_al = lax.shift_left(lax.shift_right_logical(lo, SLOT_ALIGN.bit_length() - 1), SLOT_ALIGN.bit_length() - 1)
        rows.append(pl.multiple_of(jnp.minimum(lo_al + j * SLOT_WINDOW, ncap), SLOT_ALIGN))
    return rows


def _window_hits(rows, slot_ref):
    wrow = lax.broadcasted_iota(I32, (SLOT_WINDOW, 1), 0)
    return [(wrow + rows[e]).astype(F32) == slot_ref[e:e + 1, :] for e in range(N_EXPERTS)]


def _moe_gather_body(rng_ref, hm_ref, slot_ref, gate_ref, xg_ref, gv_ref, *, ncap):
    k = pl.program_id(1)
    W = SLOT_WINDOW

    @pl.when(k == 0)
    def _():
        xg_ref[...] = jnp.zeros_like(xg_ref)
        gv_ref[...] = jnp.zeros_like(gv_ref)

    def window(j, carry):
        rows = _window_rows(rng_ref, k, j, ncap)
        hits = _window_hits(rows, slot_ref)
        hit_all = jnp.concatenate([h.astype(BF16) for h in hits], axis=0)
        xgw = jnp.dot(hit_all, hm_ref[...], preferred_element_type=F32).astype(BF16)
        for e in range(N_EXPERTS):
            r = pl.ds(rows[e], W)
            xg_ref[e, r, :] += xgw[e * W:(e + 1) * W, :]
            gv_ref[e, r, :] += jnp.sum(jnp.where(hits[e], gate_ref[e:e + 1, :], 0.0), axis=1, keepdims=True)
        return carry

    lax.fori_loop(0, rng_ref[0, 127], window, 0)


def _moe_ffn_body(xg_ref, gv_ref, wg_ref, wu_ref, wd_ref, yw_ref, wgs_ref, wus_ref, wds_ref, *, ncap):
    @pl.when(pl.program_id(1) == 0)
    def _():
        wgs_ref[...] = wg_ref[...].astype(BF16)
        wus_ref[...] = wu_ref[...].astype(BF16)
        wds_ref[...] = wd_ref[...].astype(BF16)

    nbp = xg_ref.shape[0]
    xg = jnp.concatenate([xg_ref[p, 0:ncap, :] for p in range(nbp)], axis=0)
    gv = jnp.concatenate([gv_ref[p, 0:ncap, :] for p in range(nbp)], axis=0)
    hid = _silu(jnp.dot(xg, wgs_ref[...], preferred_element_type=F32)) * jnp.dot(
        xg, wus_ref[...], preferred_element_type=F32)
    ye = (jnp.dot(hid.astype(BF16), wds_ref[...], preferred_element_type=F32) * gv).astype(yw_ref.dtype)
    for p in range(nbp):
        yw_ref[p, 0:ncap, :] = ye[p * ncap:(p + 1) * ncap, :]
        yw_ref[p, ncap:, :] = jnp.zeros((yw_ref.shape[1] - ncap, yw_ref.shape[2]), yw_ref.dtype)


def _moe_scatter_body(rng_ref, slot_ref, yw_ref, x_ref, gl_ref, gc_ref, fg_ref, o_ref, *, ncap, nbc, final):
    k = pl.program_id(1)
    W = SLOT_WINDOW

    def window(j, acc):
        rows = _window_rows(rng_ref, k, j, ncap)
        hit_all = jnp.concatenate([h.astype(BF16) for h in _window_hits(rows, slot_ref)], axis=0)
        yw = jnp.concatenate([yw_ref[e, pl.ds(rows[e], W), :] for e in range(N_EXPERTS)], axis=0)
        return acc + lax.dot_general(hit_all, yw, (((0,), (0,)), ((), ())), preferred_element_type=F32)

    def emit():
        acc = lax.fori_loop(0, rng_ref[0, 127], window, jnp.zeros(o_ref.shape, F32))
        x = x_ref[...] + jnp.where(k < nbc, gc_ref[...], gl_ref[...]) * acc
        if final:
            x = x * lax.rsqrt(jnp.mean(x * x, axis=-1, keepdims=True) + RMS_EPS) * fg_ref[...]
        o_ref[...] = x

    if final:
        pl.when(k >= nbc)(emit)
    else:
        emit()


def _experts(hm, slot, gate, rng, wg, wu, wd, li, xres, mods3, B, S, Sc, ctx_out, final_g):
    N, D = hm.shape
    _, E, _, Fd = wg.shape
    T = S - Sc
    tb = _token_block(S, Sc)
    nblk, nbc = S // tb, Sc // tb
    ncap = EC_CAPACITY * T // N_EXPERTS + (EC_CAPACITY * Sc // N_EXPERTS if ctx_out else 0)
    assert ncap % SLOT_ALIGN == 0
    rows = ncap + SLOT_WINDOW
    smem = pl.BlockSpec((None, E, 128), lambda b_, k: (b_, 0, 0), memory_space=pltpu.SMEM)
    es = pl.BlockSpec((None, E, tb), lambda b_, k: (b_, 0, k))
    xg, gv = pl.pallas_call(
        functools.partial(_moe_gather_body, ncap=ncap),
        grid=(B, nblk),
        in_specs=[smem, pl.BlockSpec((tb, D), lambda b_, k: (b_ * nblk + k, 0)), es, es],
        out_specs=[pl.BlockSpec((None, E, rows, D), lambda b_, k: (b_, 0, 0, 0)),
                   pl.BlockSpec((None, E, rows, 1), lambda b_, k: (b_, 0, 0, 0))],
        out_shape=[_sds((B, E, rows, D), BF16), _sds((B, E, rows, 1), F32)],
        compiler_params=_cp(("parallel", "arbitrary"), 48),
        name="moe_gather",
    )(rng, hm, slot, gate)
    wspec = lambda a, b: pl.BlockSpec((None, None, a, b), lambda e, b_: (li, e, 0, 0))
    nbp = math.gcd(B, 2)
    yw = pl.pallas_call(
        functools.partial(_moe_ffn_body, ncap=ncap),
        grid=(E, B // nbp),
        in_specs=[pl.BlockSpec((nbp, None, rows, D), lambda e, b_: (b_, e, 0, 0)),
                  pl.BlockSpec((nbp, None, rows, 1), lambda e, b_: (b_, e, 0, 0)),
                  wspec(D, Fd), wspec(D, Fd), wspec(Fd, D)],
        out_specs=pl.BlockSpec((nbp, None, rows, D), lambda e, b_: (b_, e, 0, 0)),
        out_shape=_sds((B, E, rows, D), BF16),
        scratch_shapes=[pltpu.VMEM((D, Fd), BF16), pltpu.VMEM((D, Fd), BF16), pltpu.VMEM((Fd, D), BF16)],
        compiler_params=_cp(("parallel", "arbitrary"), 48),
        name="moe_ffn",
    )(xg, gv, wg, wu, wd)
    final = final_g is not None
    nl = nblk - nbc
    if final:
        out_spec = pl.BlockSpec((tb, D), lambda b_, k: (b_ * nl + jnp.maximum(k - nbc, 0), 0))
        out_shape = _sds((B * nl * tb, D), F32)
    else:
        out_spec = pl.BlockSpec((tb, D), lambda b_, k: (b_ * nblk + k, 0))
        out_shape = _sds((N, D), F32)
        final_g = jnp.ones((1, D), F32)
    return pl.pallas_call(
        functools.partial(_moe_scatter_body, ncap=ncap, nbc=nbc, final=final),
        grid=(B, nblk),
        in_specs=[smem, es, pl.BlockSpec((None, E, rows, D), lambda b_, k: (b_, 0, 0, 0)),
                  pl.BlockSpec((tb, D), lambda b_, k: (b_ * nblk + k, 0)),
                  pl.BlockSpec((None, 1, D), lambda b_, k: (b_, 0, 5)),
                  pl.BlockSpec((None, 1, D), lambda b_, k: (B, 0, 5)),
                  pl.BlockSpec((1, D), lambda b_, k: (0, 0))],
        out_specs=out_spec,
        out_shape=out_shape,
        compiler_params=_cp(("parallel", "arbitrary"), 48),
        name="moe_scatter",
    )(rng, slot, yw, xres, mods3, mods3, final_g)


def _layer(xa, mods3, lp, ew, li, tabs, B, S, Sc, ctx_out, final_g):
    N, D = xa.shape
    rc = _row_cfg(S, Sc)
    w_in = lp["w_in"]
    seg = lambda a, n: w_in[:, a:a + n]
    o_q, o_k, o_v, o_z = 0, ATTN_Q_DIM, ATTN_Q_DIM + ATTN_KV_DIM, ATTN_Q_DIM + 2 * ATTN_KV_DIM
    o_xbc = o_z + SSD_INNER
    o_dt = o_xbc + SSD_CONV_DIM
    o_u = o_dt + SSD_HEADS
    o_sb, o_sg, o_sh = o_u + S5_WIDTH, o_u + S5_WIDTH + SC_WIDTH, o_u + S5_WIDTH + 2 * SC_WIDTH
    o_g = o_sh + SC_WIDTH
    w_p = jnp.concatenate([seg(o_g, 4 * D), seg(o_q, ATTN_Q_DIM), seg(o_z, SSD_INNER), seg(o_k, ATTN_KV_DIM),
                           seg(o_v, ATTN_KV_DIM), seg(o_u, S5_WIDTH), seg(o_sb, SC_WIDTH), seg(o_sg, SC_WIDTH),
                           seg(o_sh, SC_WIDTH), seg(o_xbc, SSD_CONV_DIM)], axis=1).astype(BF16)
    w_dt = jnp.pad(seg(o_dt, SSD_HEADS), ((0, 0), (0, 128 - SSD_HEADS))).astype(BF16)
    proj_a, dt = _in_proj(xa, lp["norm_mix_g"][None, :], mods3, w_p, w_dt, B, S, Sc)

    ya = _attention(proj_a, tabs, lp["q_norm_g"], lp["k_norm_g"], B, S, Sc)

    xbc_c = _ssdconv(proj_a, lp["ssd_conv_w"], lp["ssd_conv_b"][None, :], B, S, Sc)
    yf, yb = _ssd(xbc_c, dt, lp["ssd_dt_bias"], lp["ssd_a_log"], B, S, Sc)

    Lc, Gn, G = S5_CHUNK, S5_GROUPS, S5_GROUP_DIM
    nc5, ncc5 = S // Lc, Sc // Lc
    w5, so5, ar5 = _s5gen(lp["s5_lambda_re"], lp["s5_lambda_im"], lp["s5_log_step"], lp["s5_b_re"], lp["s5_b_im"],
                          lp["s5_c_re"], lp["s5_c_im"], lp["s5_d"])
    u = proj_a[:, PA_U:PA_U + S5_WIDTH].reshape(B, nc5, Lc, S5_WIDTH)
    up = jnp.transpose(u, (2, 1, 0, 3)).reshape(N, S5_WIDTH)
    tr5 = nc5 * B
    y5t = _s5(_transpose2d(up, tr5, "s5_in_t"), w5, so5, ar5, B, nc5, ncc5)
    y5p = _untranspose2d(y5t, tr5, "s5_out_t").reshape(Lc, nc5, B, S5_WIDTH)
    y5 = jnp.transpose(y5p, (2, 1, 0, 3)).reshape(N, S5_WIDTH)

    yd = _shortconv(proj_a, lp["sc_conv_w"], B, S, Sc)

    dsk = jnp.repeat(lp["ssd_d"], SSD_HEAD_DIM)[None, :]
    x1 = _merge(xa, ya, yf, yb, xbc_c, proj_a, y5, yd, mods3,
                lp["w_br_attn"].astype(BF16), lp["w_br_ssd"].astype(BF16), lp["w_br_s5"].astype(BF16),
                lp["w_br_sc"].astype(BF16), lp["w_out"].astype(BF16), lp["s5_w_glu"].astype(BF16),
                lp["s5_b_glu"][None, :], lp["ssd_norm_g"][None, :], dsk, B, rc)

    wr_pad = jnp.pad(lp["w_router"], ((0, 0), (0, 128 - N_EXPERTS)))
    hm, slot, gate, rng = _router(x1, lp["norm_ffn_g"][None, :], mods3, wr_pad, B, S, Sc, ctx_out)
    return _experts(hm, slot, gate, rng, ew[0], ew[1], ew[2], li, x1, mods3, B, S, Sc, ctx_out, final_g)


def kernel(x, c, ctx, c_ctx, w_mod, b_mod, norm_mix_g, norm_ffn_g, w_in, q_norm_g, k_norm_g, ssd_conv_w, ssd_conv_b, ssd_dt_bias, ssd_a_log, ssd_d, ssd_norm_g, s5_lambda_re, s5_lambda_im, s5_log_step, s5_b_re, s5_b_im, s5_c_re, s5_c_im, s5_d, s5_w_glu, s5_b_glu, sc_conv_w, w_br_attn, w_br_ssd, w_br_s5, w_br_sc, w_out, w_router, w_exp_gate, w_exp_up, w_exp_down, final_norm_g):
    B, T, D = x.shape
    Sc = ctx.shape[1]
    S = Sc + T
    depth = w_in.shape[0]
    xa = jnp.concatenate([ctx, x], axis=1).reshape(B * S, D)
    cc = jnp.zeros((16, D), F32).at[0:B].set(c).at[B].set(c_ctx)
    cos, sin = _rope_tables(T, Sc)
    tabs = (jnp.tile(cos, (1, ATTN_HEADS)), jnp.tile(sin, (1, ATTN_HEADS)),
            jnp.tile(cos, (1, ATTN_KV_HEADS)), jnp.tile(sin, (1, ATTN_KV_HEADS)),
            _block_diag_ones(ATTN_Q_DIM), _block_diag_ones(ATTN_KV_DIM))
    stacked = dict(
        w_in=w_in, norm_mix_g=norm_mix_g, norm_ffn_g=norm_ffn_g, q_norm_g=q_norm_g, k_norm_g=k_norm_g,
        ssd_conv_w=ssd_conv_w, ssd_conv_b=ssd_conv_b, ssd_dt_bias=ssd_dt_bias, ssd_a_log=ssd_a_log, ssd_d=ssd_d,
        ssd_norm_g=ssd_norm_g, s5_lambda_re=s5_lambda_re, s5_lambda_im=s5_lambda_im, s5_log_step=s5_log_step,
        s5_b_re=s5_b_re, s5_b_im=s5_b_im, s5_c_re=s5_c_re, s5_c_im=s5_c_im, s5_d=s5_d, s5_w_glu=s5_w_glu,
        s5_b_glu=s5_b_glu, sc_conv_w=sc_conv_w, w_br_attn=w_br_attn, w_br_ssd=w_br_ssd, w_br_s5=w_br_s5,
        w_br_sc=w_br_sc, w_out=w_out, w_router=w_router)
    ew = (w_exp_gate, w_exp_up, w_exp_down)
    for i in range(depth):
        lp = {k: v[i] for k, v in stacked.items()}
        mods3 = _mods(cc, w_mod, b_mod[:, None, :], i).reshape(16, 1, 6 * D)
        last = i == depth - 1
        xa = _layer(xa, mods3, lp, ew, i, tabs, B, S, Sc, ctx_out=not last,
                    final_g=final_norm_g[None, :] if last else None)
    return xa.reshape(B, T, D)
```

```python
import functools
import math

import jax
import jax.numpy as jnp
import numpy as np
from jax import lax
from jax.experimental import pallas as pl
from jax.experimental.pallas import tpu as pltpu

F32 = jnp.float32
BF16 = jnp.bfloat16
I32 = jnp.int32
HI = lax.Precision.HIGHEST

RMS_EPS = 1e-6
GRID_W = 64
ROPE_THETA = 10000.0
HEAD_DIM = 64
ATTN_HEADS = 8
ATTN_KV_HEADS = 2
SSD_HEADS = 8
SSD_HEAD_DIM = 64
SSD_GROUPS = 2
SSD_STATE = 64
SSD_CHUNK = 128
S5_GROUPS = 24
S5_GROUP_DIM = 16
S5_STATE = 64
S5_CHUNK = 16
SC_WIDTH = 384
N_EXPERTS = 16
EC_CAPACITY = 2
MAX_ROW_BLOCK = 256
SLOT_WINDOW = 64
SLOT_ALIGN = 16

ATTN_Q_DIM = ATTN_HEADS * HEAD_DIM
ATTN_KV_DIM = ATTN_KV_HEADS * HEAD_DIM
SSD_INNER = SSD_HEADS * SSD_HEAD_DIM
SSD_BC_DIM = SSD_GROUPS * SSD_STATE
SSD_CONV_DIM = SSD_INNER + 2 * SSD_BC_DIM
S5_WIDTH = S5_GROUPS * S5_GROUP_DIM

PA_G, PA_Q, PA_Z, PA_K, PA_V = 0, 4096, 4608, 5120, 5248
PA_U, PA_SB, PA_SG, PA_SH, PA_XBC, PA_W = 5376, 5760, 6144, 6528, 6912, 7680


def _sds(shape, dtype):
    return jax.ShapeDtypeStruct(shape, dtype)


def _cp(sem, vmem_mb=None):
    kw = dict(dimension_semantics=sem)
    if vmem_mb is not None:
        kw["vmem_limit_bytes"] = vmem_mb << 20
    return pltpu.CompilerParams(**kw)


def _sigmoid(x):
    return 0.5 * jnp.tanh(0.5 * x) + 0.5


def _silu(x):
    return x * _sigmoid(x)


def _mods_body(c_ref, w_ref, b_ref, o_ref):
    s = _silu(c_ref[...])
    o_ref[...] = jnp.dot(s.astype(BF16), w_ref[...].astype(BF16), preferred_element_type=F32) + b_ref[...]


def _mods(cc, w, b, li):
    R, D = cc.shape
    N = w.shape[2]
    tn = 1536
    return pl.pallas_call(
        _mods_body,
        grid=(N // tn,),
        in_specs=[pl.BlockSpec((R, D), lambda j: (0, 0)),
                  pl.BlockSpec((None, D, tn), lambda j: (li, 0, j)),
                  pl.BlockSpec((None, 1, tn), lambda j: (li, 0, j))],
        out_specs=pl.BlockSpec((R, tn), lambda j: (0, j)),
        out_shape=_sds((R, N), F32),
        compiler_params=_cp(("parallel",), 40),
        name="mods",
    )(cc, w, b)


def _row_cfg(S, Sc):
    tm = math.gcd(math.gcd(Sc, S - Sc), MAX_ROW_BLOCK)
    return tm, S // tm, Sc // tm


def _mod_spec(D, rc, B, chunk):
    _, nb, nbc = rc
    return pl.BlockSpec((None, 1, D), lambda i: (jnp.where(i % nb < nbc, B, i // nb), 0, chunk))


def _in_proj_body(x_ref, g_ref, shl_ref, scl_ref, shc_ref, scc_ref, w_ref, wdt_ref, o_ref, dt_ref, hn_ref,
                  *, tm, nb, Sc):
    @pl.when(pl.program_id(1) == 0)
    def _():
        x = x_ref[...]
        ms = jnp.mean(x * x, axis=-1, keepdims=True)
        y = x * lax.rsqrt(ms + RMS_EPS) * g_ref[...]
        t = (pl.program_id(0) % nb) * tm + lax.broadcasted_iota(I32, (tm, 1), 0)
        isc = t < Sc
        hn = y * (1.0 + jnp.where(isc, scc_ref[...], scl_ref[...])) + jnp.where(isc, shc_ref[...], shl_ref[...])
        hn_ref[...] = hn.astype(BF16)
        dt_ref[...] = jnp.dot(hn_ref[...], wdt_ref[...], preferred_element_type=F32)

    o_ref[...] = jnp.dot(hn_ref[...], w_ref[...], preferred_element_type=F32).astype(o_ref.dtype)


def _in_proj(xa, g, mods3, w_p, w_dt, B, S, Sc):
    N, D = xa.shape
    tm = S // 2 if (S // 2) % 128 == 0 else S
    nb = S // tm
    tn = PA_W // 3
    assert tn % 256 == 0
    lat = lambda ch: pl.BlockSpec((None, 1, D), lambda i, j: (i // nb, 0, ch))
    ctx = lambda ch: pl.BlockSpec((None, 1, D), lambda i, j: (B, 0, ch))
    return pl.pallas_call(
        functools.partial(_in_proj_body, tm=tm, nb=nb, Sc=Sc),
        grid=(N // tm, PA_W // tn),
        in_specs=[pl.BlockSpec((tm, D), lambda i, j: (i, 0)),
                  pl.BlockSpec((1, D), lambda i, j: (0, 0)),
                  lat(0), lat(1), ctx(0), ctx(1),
                  pl.BlockSpec((D, tn), lambda i, j: (0, j)),
                  pl.BlockSpec((D, 128), lambda i, j: (0, 0))],
        out_specs=[pl.BlockSpec((tm, tn), lambda i, j: (i, j)), pl.BlockSpec((tm, 128), lambda i, j: (i, 0))],
        out_shape=[_sds((N, PA_W), BF16), _sds((N, 128), F32)],
        scratch_shapes=[pltpu.VMEM((tm, D), BF16)],
        compiler_params=_cp(("parallel", "arbitrary"), 56),
        name="in_proj",
    )(xa, g, mods3, mods3, mods3, mods3, w_p, w_dt)


def _shifted(x, Sc):
    S = x.shape[0]
    t = lax.broadcasted_iota(I32, (S, 1), 0)
    prev = jnp.where((t == 0) | (t == Sc), 0.0, pltpu.roll(x, 1, 0))
    nxt = jnp.where((t == Sc - 1) | (t == S - 1), 0.0, pltpu.roll(x, S - 1, 0))
    return prev, nxt


def _ssdconv_body(x_ref, w_ref, b_ref, o_ref, *, Sc):
    x = x_ref[...].astype(F32)
    prev, nxt = _shifted(x, Sc)
    w = w_ref[...]
    y = w[0:1, :] * prev + w[1:2, :] * x + w[2:3, :] * nxt + b_ref[...]
    o_ref[...] = _silu(y).astype(o_ref.dtype)


def _ssdconv(proj_a, w, b, B, S, Sc):
    C = 384
    j0 = PA_XBC // C
    return pl.pallas_call(
        functools.partial(_ssdconv_body, Sc=Sc),
        grid=(B, SSD_CONV_DIM // C),
        in_specs=[pl.BlockSpec((S, C), lambda b_, j: (b_, j0 + j)),
                  pl.BlockSpec((3, C), lambda b_, j: (0, j)),
                  pl.BlockSpec((1, C), lambda b_, j: (0, j))],
        out_specs=pl.BlockSpec((S, C), lambda b_, j: (b_, j)),
        out_shape=_sds((B * S, SSD_CONV_DIM), BF16),
        compiler_params=_cp(("parallel", "parallel"), 40),
        name="ssdconv",
    )(proj_a, w, b)


def _shortconv_body(sb_ref, sg_ref, sh_ref, w_ref, o_ref, *, Sc):
    x = sg_ref[...].astype(F32) * sh_ref[...].astype(F32)
    prev, nxt = _shifted(x, Sc)
    w = w_ref[...]
    y = w[0:1, :] * prev + w[1:2, :] * x + w[2:3, :] * nxt
    o_ref[...] = (sb_ref[...].astype(F32) * y).astype(o_ref.dtype)


def _shortconv(proj_a, w, B, S, Sc):
    C = SC_WIDTH
    return pl.pallas_call(
        functools.partial(_shortconv_body, Sc=Sc),
        grid=(B,),
        in_specs=[pl.BlockSpec((S, C), lambda b_: (b_, PA_SB // C)),
                  pl.BlockSpec((S, C), lambda b_: (b_, PA_SG // C)),
                  pl.BlockSpec((S, C), lambda b_: (b_, PA_SH // C)),
                  pl.BlockSpec((3, C), lambda b_: (0, 0))],
        out_specs=pl.BlockSpec((S, C), lambda b_: (b_, 0)),
        out_shape=_sds((B * S, C), BF16),
        compiler_params=_cp(("parallel",), 40),
        name="shortconv",
    )(proj_a, proj_a, proj_a, w)


def _norm_rope(x, g, cos, sin, bd, scale):
    W = x.shape[1]
    sq = x * x
    hi = sq.astype(BF16)
    lo = (sq - hi.astype(F32)).astype(BF16)
    ssq = jnp.dot(hi, bd, preferred_element_type=F32) + jnp.dot(lo, bd, preferred_element_type=F32)
    y = x * lax.rsqrt(ssq * (1.0 / HEAD_DIM) + RMS_EPS) * g
    lane = lax.broadcasted_iota(I32, (1, W), 1)
    first = (lane % 32) < 16
    partner = jnp.where(first, pltpu.roll(y, W - 16, 1), pltpu.roll(y, 16, 1))
    return (y * cos + partner * sin) * scale


def _attn_body(q_ref, k_ref, v_ref, cq_ref, sq_ref, ck_ref, sk_ref, gq_ref, gk_ref, bdq_ref, bdk_ref,
               o_ref, kh_ref, ve_ref, *, Sc, S, tq):
    qb = pl.program_id(1)
    hd = HEAD_DIM

    @pl.when(qb == 0)
    def _():
        k = k_ref[...].astype(F32)
        kh_ref[...] = _norm_rope(k, gk_ref[...], ck_ref[...], sk_ref[...], bdk_ref[...], 1.0).astype(BF16)
        v = v_ref[...].astype(F32)
        lane = lax.broadcasted_iota(I32, (1, 2 * hd), 1)
        ve_ref[0] = jnp.where(lane < hd, v, 1.0).astype(BF16)
        ve_ref[1] = jnp.where(lane < hd, pltpu.roll(v, hd, 1), 1.0).astype(BF16)

    q = q_ref[...].astype(F32)
    qh = _norm_rope(q, gq_ref[...], cq_ref[...], sq_ref[...], bdq_ref[...],
                    HEAD_DIM ** -0.5 * math.log2(math.e)).astype(BF16)
    rep = ATTN_HEADS // ATTN_KV_HEADS
    nt = (((1,), (1,)), ((), ()))

    def attend(splits):
        scores = []
        for g in range(ATTN_KV_HEADS):
            qg = jnp.concatenate([qh[:, (g * rep + r) * hd:(g * rep + r + 1) * hd] for r in range(rep)], axis=0)
            scores.append([lax.dot_general(qg, kh_ref[a:b, g * hd:(g + 1) * hd], nt, preferred_element_type=F32)
                           for a, b in splits])
        for g in range(ATTN_KV_HEADS):
            m = functools.reduce(jnp.maximum, [jnp.max(s, axis=-1, keepdims=True) for s in scores[g]])
            acc = None
            for s, (a, b) in zip(scores[g], splits):
                p = jnp.exp2(s - m).astype(BF16)
                part = jnp.dot(p, ve_ref[g, a:b, :], preferred_element_type=F32)
                acc = part if acc is None else acc + part
            o = acc[:, 0:hd] / acc[:, hd:hd + 1]
            for r in range(rep):
                h = g * rep + r
                o_ref[:, h * hd:(h + 1) * hd] = o[r * tq:(r + 1) * tq].astype(o_ref.dtype)

    @pl.when(qb < Sc // tq)
    def _():
        attend([(0, Sc)])

    @pl.when(qb >= Sc // tq)
    def _():
        half = (S // 2 + 255) // 256 * 256 if S >= 512 else S
        attend([(0, half), (half, S)] if half < S else [(0, S)])


def _rope_tables(T, Sc):
    rows = T // GRID_W
    row = np.repeat(np.arange(rows, dtype=np.float32), GRID_W)
    col = np.tile(np.arange(GRID_W, dtype=np.float32), rows)
    half = HEAD_DIM // 2
    inv = jnp.asarray(ROPE_THETA, F32) ** (-jnp.arange(0, half, 2, dtype=F32) / half)
    ra = jnp.asarray(row)[:, None] * inv
    ca = jnp.asarray(col)[:, None] * inv
    cos = jnp.concatenate([jnp.cos(ra), jnp.cos(ra), jnp.cos(ca), jnp.cos(ca)], axis=1)
    sin = jnp.concatenate([-jnp.sin(ra), jnp.sin(ra), -jnp.sin(ca), jnp.sin(ca)], axis=1)
    cos = jnp.concatenate([jnp.ones((Sc, HEAD_DIM), F32), cos], axis=0)
    sin = jnp.concatenate([jnp.zeros((Sc, HEAD_DIM), F32), sin], axis=0)
    return cos, sin


def _block_diag_ones(W):
    i = np.arange(W) // HEAD_DIM
    return jnp.asarray((i[:, None] == i[None, :]).astype(np.float32), BF16)


def _attention(proj_a, tabs, q_norm_g, k_norm_g, B, S, Sc):
    cos_q, sin_q, cos_k, sin_k, bdq, bdk = tabs
    tq = math.gcd(Sc, 256)
    nq = S // tq
    gq = jnp.tile(q_norm_g, ATTN_HEADS)[None, :]
    gk = jnp.tile(k_norm_g, ATTN_KV_HEADS)[None, :]
    const = lambda b_, i: (0, 0)
    return pl.pallas_call(
        functools.partial(_attn_body, Sc=Sc, S=S, tq=tq),
        grid=(B, nq),
        in_specs=[pl.BlockSpec((tq, ATTN_Q_DIM), lambda b_, i: (b_ * nq + i, PA_Q // ATTN_Q_DIM)),
                  pl.BlockSpec((S, ATTN_KV_DIM), lambda b_, i: (b_, PA_K // ATTN_KV_DIM)),
                  pl.BlockSpec((S, ATTN_KV_DIM), lambda b_, i: (b_, PA_V // ATTN_KV_DIM)),
                  pl.BlockSpec((tq, ATTN_Q_DIM), lambda b_, i: (i, 0)),
                  pl.BlockSpec((tq, ATTN_Q_DIM), lambda b_, i: (i, 0)),
                  pl.BlockSpec((S, ATTN_KV_DIM), const),
                  pl.BlockSpec((S, ATTN_KV_DIM), const),
                  pl.BlockSpec((1, ATTN_Q_DIM), const),
                  pl.BlockSpec((1, ATTN_KV_DIM), const),
                  pl.BlockSpec((ATTN_Q_DIM, ATTN_Q_DIM), const),
                  pl.BlockSpec((ATTN_KV_DIM, ATTN_KV_DIM), const)],
        out_specs=pl.BlockSpec((tq, ATTN_Q_DIM), lambda b_, i: (b_ * nq + i, 0)),
        out_shape=_sds((B * S, ATTN_Q_DIM), BF16),
        scratch_shapes=[pltpu.VMEM((S, ATTN_KV_DIM), BF16),
                        pltpu.VMEM((ATTN_KV_HEADS, S, 2 * HEAD_DIM), BF16)],
        compiler_params=_cp(("parallel", "arbitrary"), 56),
        name="attention",
    )(proj_a, proj_a, proj_a, cos_q, sin_q, cos_k, sin_k, gq, gk, bdq, bdk)


def _ssd_chains(chains):
    L = SSD_CHUNK
    P, Nst = SSD_HEAD_DIM, SSD_STATE
    rep = SSD_HEADS // SSD_GROUPS
    row = lax.broadcasted_iota(I32, (L, L), 0)
    col = lax.broadcasted_iota(I32, (L, L), 1)
    tri = (row >= col).astype(F32)
    lane = lax.broadcasted_iota(I32, (1, 2 * P), 1)
    lo_half = lane < P
    blockdiag = (row // Nst) == (col // P)
    upper = lax.broadcasted_iota(I32, (2 * Nst, 1), 0) < Nst
    st = []
    for xbc_ref, dt_ref, bias, a_neg, h_ref, y_ref, rev in chains:
        xbc = xbc_ref[...]
        bm = xbc[:, SSD_INNER:SSD_INNER + SSD_BC_DIM]
        cm = xbc[:, SSD_INNER + SSD_BC_DIM:SSD_CONV_DIM]
        dt = jax.nn.softplus(dt_ref[...] + bias)
        a = dt * a_neg
        cum = jnp.dot(tri, a, precision=HI, preferred_element_type=F32)
        e = cum - a if rev else cum
        tot_row = cum[L - 1:L, :]
        win_c = jnp.exp(tot_row - e) if rev else jnp.exp(e)
        win_hi = win_c.astype(BF16)
        cm32 = cm.astype(F32)
        st.append(dict(
            rev=rev, h_ref=h_ref, y_ref=y_ref, xbc=xbc, cm=cm, cm32=cm32, cm_rolled=pltpu.roll(cm32, P, 1),
            cum=cum, e=e, eT=e.T, dtT=dt.T, bT=bm.astype(F32).T.astype(BF16),
            mask=(col >= row) if rev else (row >= col),
            win_hi=win_hi, win_lo=(win_c - win_hi.astype(F32)).astype(BF16)))
    for g in range(SSD_GROUPS):
        own = (lane // P) == g
        for c in st:
            cg0 = jnp.where(own, c["cm"], jnp.zeros_like(c["cm"]))
            c["sg"] = jnp.dot(cg0, c["bT"], preferred_element_type=F32)
            c["cdup"] = jnp.where(own, c["cm32"], c["cm_rolled"])
            c["bTg"] = c["bT"][g * Nst:(g + 1) * Nst, :].astype(F32)
        for q in range(rep // 2):
            pq = (g * rep) // 2 + q
            hs = (2 * pq, 2 * pq + 1)
            sel = (lax.broadcasted_iota(I32, (2 * P, 2 * P), 0) == hs[0] + col // P).astype(BF16)
            for c in st:
                rev, e, eT, cum, dtT = c["rev"], c["e"], c["eT"], c["cum"], c["dtT"]
                ms, bws, tots = [], [], []
                for h in hs:
                    ecol = e[:, h:h + 1]
                    erow = eT[h:h + 1, :]
                    tot = cum[L - 1:L, h:h + 1]
                    diff = (erow - ecol) if rev else (ecol - erow)
                    dec = jnp.exp(jnp.where(c["mask"], diff, -1e30))
                    ms.append((c["sg"] * dec * dtT[h:h + 1, :]).astype(BF16))
                    wrow = (jnp.exp(erow) if rev else jnp.exp(tot - erow)) * dtT[h:h + 1, :]
                    bws.append(c["bTg"] * wrow)
                    tots.append(jnp.exp(tot))
                xp = c["xbc"][:, 2 * P * pq:2 * P * (pq + 1)]
                zero = jnp.zeros_like(xp)
                xbd = jnp.concatenate([jnp.where(lo_half, xp, zero), jnp.where(lo_half, zero, xp)], axis=0)
                y = jnp.dot(jnp.concatenate(ms, axis=1), xbd, preferred_element_type=F32)
                wexp = (jnp.dot(c["win_hi"], sel, preferred_element_type=F32)
                        + jnp.dot(c["win_lo"], sel, preferred_element_type=F32))
                hp = c["h_ref"][pq]
                y = y + jnp.dot((wexp * c["cdup"]).astype(BF16), hp.astype(BF16), preferred_element_type=F32)
                upd = jnp.dot(jnp.concatenate(bws, axis=0).astype(BF16), xp, preferred_element_type=F32)
                keep = jnp.where(upper, tots[0], tots[1])
                c["h_ref"][pq] = jnp.where(blockdiag, keep * hp + upd, 0.0)
                c["y_ref"][:, 2 * P * pq:2 * P * (pq + 1)] = y.astype(c["y_ref"].dtype)


def _ssd_body(xf_ref, dtf_ref, xb_ref, dtb_ref, bias_ref, alog_ref, yf_ref, yb_ref, *h_refs, nper):
    @pl.when(pl.program_id(1) == 0)
    def _():
        for h_ref in h_refs:
            h_ref[...] = jnp.zeros_like(h_ref)

    lane = lax.broadcasted_iota(I32, (1, 128), 1)
    a_neg = jnp.where(lane < SSD_HEADS, -jnp.exp(alog_ref[...]), 0.0)
    bias = bias_ref[...]
    chains = []
    for p in range(nper):
        chains.append((xf_ref.at[p], dtf_ref.at[p], bias[0:1, :], a_neg[0:1, :], h_refs[2 * p], yf_ref.at[p], False))
        chains.append((xb_ref.at[p], dtb_ref.at[p], bias[1:2, :], a_neg[1:2, :], h_refs[2 * p + 1], yb_ref.at[p], True))
    _ssd_chains(chains)


def _ssd(xbc_c, dt, dt_bias, a_log, B, S, Sc):
    L = SSD_CHUNK
    nc, ncc = S // L, Sc // L
    nper = math.gcd(B, 4)
    bg = B // nper
    padh = 128 - SSD_HEADS
    bias = jnp.pad(dt_bias, ((0, 0), (0, padh)))
    alog = jnp.pad(a_log, ((0, 0), (0, padh)))
    x3 = xbc_c.reshape(nper, bg * S, SSD_CONV_DIM)
    dt3 = dt.reshape(nper, bg * S, 128)

    def fwd(b_, i):
        return (0, b_ * nc + i, 0)

    def bwd(b_, i):
        return (0, b_ * nc + jnp.where(i < ncc, ncc - 1 - i, nc + ncc - 1 - i), 0)

    const = lambda b_, i: (0, 0)
    out = _sds((nper, bg * S, SSD_INNER), BF16)
    yf, yb = pl.pallas_call(
        functools.partial(_ssd_body, nper=nper),
        grid=(bg, nc),
        in_specs=[pl.BlockSpec((nper, L, SSD_CONV_DIM), fwd), pl.BlockSpec((nper, L, 128), fwd),
                  pl.BlockSpec((nper, L, SSD_CONV_DIM), bwd), pl.BlockSpec((nper, L, 128), bwd),
                  pl.BlockSpec((2, 128), const), pl.BlockSpec((2, 128), const)],
        out_specs=[pl.BlockSpec((nper, L, SSD_INNER), fwd), pl.BlockSpec((nper, L, SSD_INNER), bwd)],
        out_shape=[out, out],
        scratch_shapes=[pltpu.VMEM((SSD_HEADS // 2, 2 * SSD_STATE, 2 * SSD_HEAD_DIM), F32)] * (2 * nper),
        compiler_params=_cp(("parallel", "arbitrary")),
        name="ssd",
    )(x3, dt3, x3, dt3, bias, alog)
    return yf.reshape(B * S, SSD_INNER), yb.reshape(B * S, SSD_INNER)


def _cpow(n, lr, li, st):
    mag = jnp.exp(n * lr * st)
    ang = n * li * st
    return mag * jnp.cos(ang), mag * jnp.sin(ang)


def _zoh_coef(lr, li, st):
    ar, ai = _cpow(1.0, lr, li, st)
    nr, ni = ar - 1.0, ai
    den = lr * lr + li * li
    return (nr * lr + ni * li) / den, (ni * lr - nr * li) / den


def _spread_lanes(x, sel):
    hi = x.astype(BF16)
    r1 = x - hi.astype(F32)
    mid = r1.astype(BF16)
    lo = (r1 - mid.astype(F32)).astype(BF16)
    return (jnp.dot(hi, sel, preferred_element_type=F32) + jnp.dot(mid, sel, preferred_element_type=F32)
            + jnp.dot(lo, sel, preferred_element_type=F32))


def _s5gen_body(colp_ref, rowp_ref, ctr_ref, cti_ref, btr_ref, bti_ref, blr_ref, bli_ref, d_ref,
                wt_ref, so_ref, ar_ref, ws_ref, sos_ref):
    Lc, G, P = S5_CHUNK, S5_GROUP_DIM, S5_STATE
    NS = 2 * Lc + 1
    colp = colp_ref[...]
    rowp = rowp_ref[...]
    lrf, lif, stf = colp[:, 0:1], colp[:, 1:2], jnp.exp(colp[:, 2:3])
    lrb, lib, stb = colp[:, 3:4], colp[:, 4:5], jnp.exp(colp[:, 5:6])
    lane = lax.broadcasted_iota(I32, (1, NS * G), 1)
    slot = lane // G
    isb = slot < Lc
    cslot = lax.broadcasted_iota(I32, (1, 128), 1)
    cisb = cslot < Lc
    clag = jnp.where(cslot < NS, jnp.abs(cslot - Lc), 0).astype(F32)
    pr_c, pi_c = _cpow(clag, jnp.where(cisb, lrb, lrf), jnp.where(cisb, lib, lif), jnp.where(cisb, stb, stf))
    srow = lax.broadcasted_iota(I32, (128, 1), 0)
    spread = (srow == slot).astype(BF16)
    pr, pi = _spread_lanes(pr_c, spread), _spread_lanes(pi_c, spread)
    ctr, cti = ctr_ref[...], cti_ref[...]
    er = ctr * pr - cti * pi
    ei = ctr * pi + cti * pr
    rlrf, rlif, rstf = rowp[0:1, :], rowp[1:2, :], jnp.exp(rowp[2:3, :])
    rlrb, rlib, rstb = rowp[3:4, :], rowp[4:5, :], jnp.exp(rowp[5:6, :])
    btr, bti = btr_ref[...], bti_ref[...]
    cfr, cfi = _zoh_coef(rlrf, rlif, rstf)
    cbr, cbi = _zoh_coef(rlrb, rlib, rstb)
    bbf_r, bbf_i = cfr * btr - cfi * bti, cfr * bti + cfi * btr
    bbb_r, bbb_i = cbr * btr - cbi * bti, cbr * bti + cbi * btr

    def kt(br, bi):
        return (jnp.dot(br, er, precision=HI, preferred_element_type=F32)
                - jnp.dot(bi, ei, precision=HI, preferred_element_type=F32))

    ktf, ktb = kt(bbf_r, bbf_i), kt(bbb_r, bbb_i)
    ii = lax.broadcasted_iota(I32, (G, NS * G), 0)
    dmat = jnp.where((slot == Lc) & (ii == lane - Lc * G), d_ref[...], 0.0)
    strip = jnp.where(slot == Lc, ktf + ktb, jnp.where(isb, ktb, ktf)) + dmat
    for s in range(Lc):
        off = (Lc - s) * G
        ws_ref[s * G:(s + 1) * G, :] = strip[:, off:off + Lc * G]
    fo = (Lc + 1) * G
    ws_ref[Lc * G:Lc * G + P, :] = er[:, fo:fo + Lc * G]
    ws_ref[Lc * G + P:Lc * G + 2 * P, :] = er[:, 0:Lc * G]
    ws_ref[Lc * G + 2 * P:Lc * G + 3 * P, :] = -ei[:, fo:fo + Lc * G]
    ws_ref[Lc * G + 3 * P:Lc * G + 4 * P, :] = -ei[:, 0:Lc * G]
    wt_ref[:, 0:Lc * G] = ws_ref[0:Lc * G, :].T.astype(wt_ref.dtype)
    wt_ref[:, Lc * G:Lc * G + 4 * P] = ws_ref[Lc * G:Lc * G + 4 * P, :].T.astype(wt_ref.dtype)
    s_idx = lax.broadcasted_iota(I32, (1, Lc * G), 1) // G
    sel_f = (srow == 2 * Lc - 1 - s_idx).astype(BF16)
    sel_b = (srow == Lc - s_idx).astype(BF16)
    qfr, qfi = _spread_lanes(pr_c, sel_f), _spread_lanes(pi_c, sel_f)
    qbr, qbi = _spread_lanes(pr_c, sel_b), _spread_lanes(pi_c, sel_b)
    ccfr, ccfi = _zoh_coef(lrf, lif, stf)
    ccbr, ccbi = _zoh_coef(lrb, lib, stb)
    blr, bli = blr_ref[...], bli_ref[...]
    bfr, bfi = ccfr * blr - ccfi * bli, ccfr * bli + ccfi * blr
    bbr, bbi = ccbr * blr - ccbi * bli, ccbr * bli + ccbi * blr
    sos_ref[0:P, :] = qfr * bfr - qfi * bfi
    sos_ref[P:2 * P, :] = qbr * bbr - qbi * bbi
    sos_ref[2 * P:3 * P, :] = qfr * bfi + qfi * bfr
    sos_ref[3 * P:4 * P, :] = qbr * bbi + qbi * bbr
    so_ref[...] = sos_ref[...].T.astype(so_ref.dtype)
    afr, afi = _cpow(float(Lc), rlrf, rlif, rstf)
    abr, abi = _cpow(float(Lc), rlrb, rlib, rstb)
    ar_ref[...] = jnp.zeros_like(ar_ref)
    ar_ref[0:1, 0:P] = afr
    ar_ref[0:1, P:2 * P] = abr
    ar_ref[1:2, 0:P] = afi
    ar_ref[1:2, P:2 * P] = abi


def _s5gen(lam_re, lam_im, log_step, b_re, b_im, c_re, c_im, d_skip):
    Gn, P, G, Lc = S5_GROUPS, S5_STATE, S5_GROUP_DIM, S5_CHUNK
    NS = 2 * Lc + 1
    ls = jnp.broadcast_to(log_step[:, :, None], (2, Gn, P))
    z = jnp.zeros((Gn, P), F32)
    rowp = jnp.stack([lam_re[0], lam_im[0], ls[0], lam_re[1], lam_im[1], ls[1], z, z], axis=1)
    colp = jnp.swapaxes(rowp, 1, 2)
    ctr = jnp.tile(jnp.swapaxes(c_re, 1, 2), (1, 1, NS))
    cti = jnp.tile(jnp.swapaxes(c_im, 1, 2), (1, 1, NS))
    btr = jnp.swapaxes(b_re, 1, 2)
    bti = jnp.swapaxes(b_im, 1, 2)
    blr = jnp.tile(b_re, (1, 1, Lc))
    bli = jnp.tile(b_im, (1, 1, Lc))
    dt = jnp.tile(d_skip.reshape(Gn, 1, G), (1, 1, NS))
    g3 = lambda a, b: pl.BlockSpec((None, a, b), lambda g: (g, 0, 0))
    return pl.pallas_call(
        _s5gen_body,
        grid=(Gn,),
        in_specs=[g3(P, 8), g3(8, P), g3(P, NS * G), g3(P, NS * G), g3(G, P), g3(G, P),
                  g3(P, Lc * G), g3(P, Lc * G), g3(1, NS * G)],
        out_specs=[g3(Lc * G, Lc * G + 4 * P), g3(Lc * G, 4 * P), g3(8, 2 * P)],
        out_shape=[_sds((Gn, Lc * G, Lc * G + 4 * P), BF16), _sds((Gn, Lc * G, 4 * P), BF16),
                   _sds((Gn, 8, 2 * P), F32)],
        scratch_shapes=[pltpu.VMEM((Lc * G + 4 * P, Lc * G), F32), pltpu.VMEM((4 * P, Lc * G), F32)],
        compiler_params=_cp(("parallel",)),
        name="s5gen",
    )(colp, rowp, ctr, cti, btr, bti, blr, bli, dt)


def _s5_body(xt_ref, wt_ref, so_ref, ar_ref, y_ref, *scratch, B, nc, ncc, ng):
    P, Lc, G = S5_STATE, S5_CHUNK, S5_GROUP_DIM
    LG = Lc * G
    R = nc * B
    v_refs, h_refs = scratch[:ng], scratch[ng:]
    ucols = []
    for k in range(ng):
        ucol = jnp.concatenate([xt_ref[k * G:(k + 1) * G, l * R:(l + 1) * R] for l in range(Lc)], axis=0)
        v_refs[k][...] = lax.dot_general(ucol, so_ref[k], (((0,), (0,)), ((), ())), preferred_element_type=F32)
        ucols.append(ucol)
    lane = lax.broadcasted_iota(I32, (1, 2 * P), 1)
    isf = lane < P

    def step(i, carry):
        cf = pl.multiple_of(i * B, B)
        cb = pl.multiple_of(jnp.where(i < ncc, ncc - 1 - i, nc + ncc - 1 - i) * B, B)
        out = []
        for k in range(ng):
            hr, hi = carry[k]
            v_ref, h_ref = v_refs[k], h_refs[k]
            a_re, a_im = ar_ref[k, 0:1, :], ar_ref[k, 1:2, :]
            h_ref[pl.ds(cf, B), 0:P] = hr[:, 0:P]
            h_ref[pl.ds(cf, B), 2 * P:3 * P] = hi[:, 0:P]
            h_ref[pl.ds(cb, B), P:2 * P] = hr[:, P:2 * P]
            h_ref[pl.ds(cb, B), 3 * P:4 * P] = hi[:, P:2 * P]
            vr = jnp.where(isf, v_ref[pl.ds(cf, B), 0:2 * P], v_ref[pl.ds(cb, B), 0:2 * P])
            vi = jnp.where(isf, v_ref[pl.ds(cf, B), 2 * P:4 * P], v_ref[pl.ds(cb, B), 2 * P:4 * P])
            out.append((hr * a_re - hi * a_im + vr, hr * a_im + hi * a_re + vi))
        return tuple(out)

    z = jnp.zeros((B, 2 * P), F32)
    lax.fori_loop(0, nc, step, tuple((z, z) for _ in range(ng)))
    for k in range(ng):
        yt = jnp.dot(wt_ref[k, :, 0:LG], ucols[k], preferred_element_type=F32)
        yt = yt + lax.dot_general(wt_ref[k, :, LG:LG + 4 * P], h_refs[k][...].astype(BF16),
                                  (((1,), (1,)), ((), ())), preferred_element_type=F32)
        for t in range(Lc):
            y_ref[k * G:(k + 1) * G, t * R:(t + 1) * R] = yt[t * G:(t + 1) * G, :].astype(y_ref.dtype)


def _s5(xt, wt, so, ar, B, nc, ncc):
    P, Lc, G, Gn = S5_STATE, S5_CHUNK, S5_GROUP_DIM, S5_GROUPS
    N = xt.shape[1]
    R = nc * B
    ng = 2
    g3 = lambda a, b: pl.BlockSpec((ng, a, b), lambda g: (g, 0, 0))
    return pl.pallas_call(
        functools.partial(_s5_body, B=B, nc=nc, ncc=ncc, ng=ng),
        grid=(Gn // ng,),
        in_specs=[pl.BlockSpec((ng * G, N), lambda g: (g, 0)), g3(Lc * G, Lc * G + 4 * P), g3(Lc * G, 4 * P),
                  g3(8, 2 * P)],
        out_specs=pl.BlockSpec((ng * G, N), lambda g: (g, 0)),
        out_shape=_sds((Gn * G, N), BF16),
        scratch_shapes=[pltpu.VMEM((R, 4 * P), F32)] * (2 * ng),
        compiler_params=_cp(("parallel",)),
        name="s5",
    )(xt, wt, so, ar)


def _t2d_body(x_ref, o_ref):
    o_ref[...] = x_ref[...].astype(F32).T.astype(o_ref.dtype)


def _transpose2d(x, tr, name):
    M, C = x.shape
    return pl.pallas_call(
        _t2d_body,
        grid=(M // tr,),
        in_specs=[pl.BlockSpec((tr, C), lambda i: (i, 0))],
        out_specs=pl.BlockSpec((C, tr), lambda i: (0, i)),
        out_shape=_sds((C, M), x.dtype),
        compiler_params=_cp(("parallel",)),
        name=name,
    )(x)


def _untranspose2d(xt, tr, name):
    C, M = xt.shape
    return pl.pallas_call(
        _t2d_body,
        grid=(M // tr,),
        in_specs=[pl.BlockSpec((C, tr), lambda i: (0, i))],
        out_specs=pl.BlockSpec((tr, C), lambda i: (i, 0)),
        out_shape=_sds((M, C), xt.dtype),
        compiler_params=_cp(("parallel",)),
        name=name,
    )(xt)


def _merge_body(x_ref, ya_ref, yf_ref, yb_ref, xc_ref, z_ref, y5_ref, yd_ref, g_ref, gate_ref,
                wa_ref, wb_ref, wc_ref, wd_ref, wo_ref, wglu_ref, bglu_ref, ng_ref, dsk_ref, o_ref):
    tm, D = o_ref.shape
    nparts = 2 if tm % 32 == 0 else 1
    parts = [slice(p * tm // nparts, (p + 1) * tm // nparts) for p in range(nparts)]
    ys = [yf_ref[r, :].astype(F32) + yb_ref[r, :].astype(F32) + dsk_ref[...] * xc_ref[r, :].astype(F32) for r in parts]
    ys = [y * _silu(z_ref[r, :].astype(F32)) for y, r in zip(ys, parts)]
    ys = [(y * lax.rsqrt(jnp.mean(y * y, axis=-1, keepdims=True) + RMS_EPS) * ng_ref[...]).astype(BF16) for y in ys]
    cs = [jax.nn.gelu(y5_ref[r, :].astype(F32)).astype(BF16) for r in parts]
    glus = [_sigmoid(jnp.dot(c, wglu_ref[...], preferred_element_type=F32) + bglu_ref[...]) for c in cs]
    cs = [(c.astype(F32) * glu).astype(BF16) for c, glu in zip(cs, glus)]

    def gate(r, k):
        return _sigmoid(g_ref[r, k * D:(k + 1) * D].astype(F32))

    ms = [gate(r, 0) * jnp.dot(ya_ref[r, :], wa_ref[...], preferred_element_type=F32) for r in parts]
    ms = [m + gate(r, 1) * jnp.dot(y, wb_ref[...], preferred_element_type=F32) for m, y, r in zip(ms, ys, parts)]
    ms = [m + gate(r, 2) * jnp.dot(c, wc_ref[...], preferred_element_type=F32) for m, c, r in zip(ms, cs, parts)]
    ms = [m + gate(r, 3) * jnp.dot(yd_ref[r, :], wd_ref[...], preferred_element_type=F32) for m, r in zip(ms, parts)]
    outs = [jnp.dot(m.astype(BF16), wo_ref[...], preferred_element_type=F32) for m in ms]
    for out, r in zip(outs, parts):
        o_ref[r, :] = x_ref[r, :] + gate_ref[...] * out


def _merge(xa, ya, yf, yb, xbc_c, proj, y5, yd, mods3, wa, wb, wc, wd, wo, wglu, bglu, ng, dsk, B, rc):
    N, D = xa.shape
    tm = rc[0]
    rows = lambda w, j=0: pl.BlockSpec((tm, w), lambda i: (i, j))
    full = lambda a: pl.BlockSpec(a.shape, lambda i: (0,) * a.ndim)
    return pl.pallas_call(
        _merge_body,
        grid=(N // tm,),
        in_specs=[rows(D), rows(ATTN_Q_DIM), rows(SSD_INNER), rows(SSD_INNER), rows(SSD_INNER),
                  rows(SSD_INNER, PA_Z // SSD_INNER), rows(S5_WIDTH), rows(SC_WIDTH), rows(4 * D, PA_G // (4 * D)),
                  _mod_spec(D, rc, B, 2),
                  full(wa), full(wb), full(wc), full(wd), full(wo), full(wglu), full(bglu), full(ng), full(dsk)],
        out_specs=rows(D),
        out_shape=_sds((N, D), F32),
        compiler_params=_cp(("parallel",), 48),
        name="merge",
    )(xa, ya, yf, yb, xbc_c, proj, y5, yd, proj, mods3, wa, wb, wc, wd, wo, wglu, bglu, ng, dsk)


def _prefix_excl(mask_f):
    R, T = mask_f.shape
    r = lax.broadcasted_iota(I32, (128, 128), 0)
    c = lax.broadcasted_iota(I32, (128, 128), 1)
    upper = (r <= c).astype(BF16)
    outs = []
    off = jnp.zeros((R, 1), F32)
    for k in range(T // 128):
        blk = mask_f[:, k * 128:(k + 1) * 128]
        inc = jnp.dot(blk.astype(BF16), upper, preferred_element_type=F32)
        outs.append(inc - blk + off)
        off = off + inc[:, 127:128]
    return jnp.concatenate(outs, axis=1)


def _topk_slots(affs, caps):
    E = affs[0].shape[0]

    def step(i, ths):
        out = []
        for a, cap, th in zip(affs, caps, ths):
            cand = th | (jnp.int32(1) << (30 - i))
            cnt = jnp.sum((a >= pltpu.bitcast(cand, F32)).astype(I32), axis=1, keepdims=True)
            out.append(jnp.where(cnt >= cap, cand, th))
        return tuple(out)

    ths = lax.fori_loop(0, 31, step, tuple(jnp.zeros((E, 1), I32) for _ in affs))
    slots = []
    for a, cap, th in zip(affs, caps, ths):
        above = a >= pltpu.bitcast(jnp.maximum(th + 1, 0x00800000), F32)
        tie = (a >= pltpu.bitcast(th, F32)) & jnp.logical_not(above)
        n_above = jnp.sum(above.astype(F32), axis=1, keepdims=True)
        sel = above | (tie & (n_above + _prefix_excl(tie.astype(F32)) < cap))
        slots.append(jnp.where(sel, _prefix_excl(sel.astype(F32)), -1.0))
    return slots


def _router_body(x_ref, g_ref, shl_ref, scl_ref, shc_ref, scc_ref, wr_ref, hm_ref, slot_ref, gate_ref, rng_ref,
                 *, Sc, cap_l, cap_c, ctx_out, tb):
    x = x_ref[...]
    S = x.shape[0]
    ms = jnp.mean(x * x, axis=-1, keepdims=True)
    y = x * lax.rsqrt(ms + RMS_EPS) * g_ref[...]
    t = lax.broadcasted_iota(I32, (S, 1), 0)
    isc = t < Sc
    hm = y * (1.0 + jnp.where(isc, scc_ref[...], scl_ref[...])) + jnp.where(isc, shc_ref[...], shl_ref[...])
    hm_ref[...] = hm.astype(hm_ref.dtype)
    wr = wr_ref[...]
    h_hi = hm.astype(BF16)
    h_lo = (hm - h_hi.astype(F32)).astype(BF16)
    w_hi = wr.astype(BF16)
    w_lo = (wr - w_hi.astype(F32)).astype(BF16)
    logits = (jnp.dot(h_hi, w_hi, preferred_element_type=F32) + jnp.dot(h_lo, w_hi, preferred_element_type=F32)
              + jnp.dot(h_hi, w_lo, preferred_element_type=F32))
    lt = logits.T[0:N_EXPERTS, :]
    mx = jnp.max(lt, axis=0, keepdims=True)
    ex = jnp.exp(lt - mx)
    aff = ex / jnp.sum(ex, axis=0, keepdims=True)
    gate_ref[...] = aff
    if ctx_out:
        slot_l, slot_c = _topk_slots([aff[:, Sc:], aff[:, 0:Sc]], [cap_l, cap_c])
        slot_c = jnp.where(slot_c >= 0.0, slot_c + cap_l, -1.0)
    else:
        slot_l, = _topk_slots([aff[:, Sc:]], [cap_l])
        slot_c = jnp.full((N_EXPERTS, Sc), -1.0, F32)
    slot_ref[:, 0:Sc] = slot_c
    slot_ref[:, Sc:] = slot_l
    lane = lax.broadcasted_iota(I32, (1, 128), 1)
    lo_a = jnp.zeros((N_EXPERTS, 128), F32)
    nwin = jnp.zeros((N_EXPERTS, 1), F32)
    for seg0, seg1, base, slots in ((0, Sc, float(cap_l), slot_c), (Sc, S, 0.0, slot_l)):
        lo = jnp.full((N_EXPERTS, 1), base, F32)
        for k in range(seg0 // tb, seg1 // tb):
            blk = slots[:, k * tb - seg0:(k + 1) * tb - seg0]
            hi = lo + jnp.sum((blk >= 0.0).astype(F32), axis=1, keepdims=True)
            lo_al = jnp.floor(lo * (1.0 / SLOT_ALIGN)) * SLOT_ALIGN
            need = jnp.where(hi > lo, jnp.floor((hi - lo_al + (SLOT_WINDOW - 1)) * (1.0 / SLOT_WINDOW)), 0.0)
            nwin = jnp.maximum(nwin, need)
            lo_a = jnp.where(lane == k, lo, lo_a)
            lo = hi
    rng_ref[...] = jnp.where(lane == 127, jnp.max(nwin, axis=0, keepdims=True), lo_a).astype(I32)


def _token_block(S, Sc):
    return math.gcd(math.gcd(Sc, S - Sc), 256)


def _router(xa, g, mods3, wr_pad, B, S, Sc, ctx_out):
    N, D = xa.shape
    T = S - Sc
    cap_l = EC_CAPACITY * T // N_EXPERTS
    cap_c = EC_CAPACITY * Sc // N_EXPERTS
    tb = _token_block(S, Sc)
    lat = lambda ch: pl.BlockSpec((None, 1, D), lambda b_: (b_, 0, ch))
    ctx = lambda ch: pl.BlockSpec((None, 1, D), lambda b_: (B, 0, ch))
    es = pl.BlockSpec((None, N_EXPERTS, S), lambda b_: (b_, 0, 0))
    return pl.pallas_call(
        functools.partial(_router_body, Sc=Sc, cap_l=cap_l, cap_c=cap_c, ctx_out=ctx_out, tb=tb),
        grid=(B,),
        in_specs=[pl.BlockSpec((S, D), lambda b_: (b_, 0)),
                  pl.BlockSpec((1, D), lambda b_: (0, 0)),
                  lat(3), lat(4), ctx(3), ctx(4),
                  pl.BlockSpec((D, 128), lambda b_: (0, 0))],
        out_specs=[pl.BlockSpec((S, D), lambda b_: (b_, 0)), es, es,
                   pl.BlockSpec((None, N_EXPERTS, 128), lambda b_: (b_, 0, 0))],
        out_shape=[_sds((N, D), BF16), _sds((B, N_EXPERTS, S), F32), _sds((B, N_EXPERTS, S), F32),
                   _sds((B, N_EXPERTS, 128), I32)],
        compiler_params=_cp(("parallel",), 56),
        name="router",
    )(xa, g, mods3, mods3, mods3, mods3, wr_pad)


def _window_rows(rng_ref, k, j, ncap):
    rows = []
    for e in range(N_EXPERTS):
        lo = rng_ref[e, k]
        lo_al = lax.shift_left(lax.shift_right_logical(lo, SLOT_ALIGN.bit_length() - 1), SLOT_ALIGN.bit_length() - 1)
        rows.append(pl.multiple_of(jnp.minimum(lo_al + j * SLOT_WINDOW, ncap), SLOT_ALIGN))
    return rows


def _window_hits(rows, slot_ref, cols):
    wrow = lax.broadcasted_iota(I32, (SLOT_WINDOW, 1), 0)
    return [(wrow + rows[e]).astype(F32) == slot_ref[e:e + 1, cols] for e in range(N_EXPERTS)]


def _moe_gather_body(rng_ref, hm_ref, slot_ref, gate_ref, xg_ref, gv_ref, *, ncap, tb, nsub):
    kk = pl.program_id(1)
    W = SLOT_WINDOW

    @pl.when(kk == 0)
    def _():
        xg_ref[...] = jnp.zeros_like(xg_ref)
        gv_ref[...] = jnp.zeros_like(gv_ref)

    for sub in range(nsub):
        k = kk * nsub + sub
        cols = slice(sub * tb, (sub + 1) * tb)

        def window(j, carry, k=k, cols=cols):
            rows = _window_rows(rng_ref, k, j, ncap)
            hits = _window_hits(rows, slot_ref, cols)
            hit_all = jnp.concatenate([h.astype(BF16) for h in hits], axis=0)
            xgw = jnp.dot(hit_all, hm_ref[cols, :], preferred_element_type=F32).astype(BF16)
            for e in range(N_EXPERTS):
                r = pl.ds(rows[e], W)
                xg_ref[e, r, :] += xgw[e * W:(e + 1) * W, :]
                gv_ref[e, r, :] += jnp.sum(jnp.where(hits[e], gate_ref[e:e + 1, cols], 0.0), axis=1, keepdims=True)
            return carry

        lax.fori_loop(0, rng_ref[0, 127], window, 0)


def _moe_ffn_body(xg_ref, gv_ref, wg_ref, wu_ref, wd_ref, yw_ref, wgs_ref, wus_ref, wds_ref, *, ncap):
    @pl.when(pl.program_id(1) == 0)
    def _():
        wgs_ref[...] = wg_ref[...].astype(BF16)
        wus_ref[...] = wu_ref[...].astype(BF16)
        wds_ref[...] = wd_ref[...].astype(BF16)

    nbp = xg_ref.shape[0]
    xg = jnp.concatenate([xg_ref[p, 0:ncap, :] for p in range(nbp)], axis=0)
    gv = jnp.concatenate([gv_ref[p, 0:ncap, :] for p in range(nbp)], axis=0)
    hid = _silu(jnp.dot(xg, wgs_ref[...], preferred_element_type=F32)) * jnp.dot(
        xg, wus_ref[...], preferred_element_type=F32)
    ye = (jnp.dot(hid.astype(BF16), wds_ref[...], preferred_element_type=F32) * gv).astype(yw_ref.dtype)
    for p in range(nbp):
        yw_ref[p, 0:ncap, :] = ye[p * ncap:(p + 1) * ncap, :]
        yw_ref[p, ncap:, :] = jnp.zeros((yw_ref.shape[1] - ncap, yw_ref.shape[2]), yw_ref.dtype)


def _moe_scatter_body(rng_ref, slot_ref, yw_ref, x_ref, gl_ref, gc_ref, fg_ref, o_ref,
                      *, ncap, nbc, final, tb, nsub):
    kk = pl.program_id(1)
    W = SLOT_WINDOW
    D = o_ref.shape[1]

    for sub in range(nsub):
        k = kk * nsub + sub
        cols = slice(sub * tb, (sub + 1) * tb)

        def window(j, acc, k=k, cols=cols):
            rows = _window_rows(rng_ref, k, j, ncap)
            hit_all = jnp.concatenate([h.astype(BF16) for h in _window_hits(rows, slot_ref, cols)], axis=0)
            yw = jnp.concatenate([yw_ref[e, pl.ds(rows[e], W), :] for e in range(N_EXPERTS)], axis=0)
            return acc + lax.dot_general(hit_all, yw, (((0,), (0,)), ((), ())), preferred_element_type=F32)

        def emit(k=k, cols=cols, window=window):
            acc = lax.fori_loop(0, rng_ref[0, 127], window, jnp.zeros((tb, D), F32))
            x = x_ref[cols, :] + jnp.where(k < nbc, gc_ref[...], gl_ref[...]) * acc
            if final:
                x = x * lax.rsqrt(jnp.mean(x * x, axis=-1, keepdims=True) + RMS_EPS) * fg_ref[...]
            o_ref[cols, :] = x

        if final:
            pl.when(k >= nbc)(emit)
        else:
            emit()


def _experts(hm, slot, gate, rng, wg, wu, wd, li, xres, mods3, B, S, Sc, ctx_out, final_g):
    N, D = hm.shape
    _, E, _, Fd = wg.shape
    T = S - Sc
    tb = _token_block(S, Sc)
    nblk, nbc = S // tb, Sc // tb
    ncap = EC_CAPACITY * T // N_EXPERTS + (EC_CAPACITY * Sc // N_EXPERTS if ctx_out else 0)
    assert ncap % SLOT_ALIGN == 0
    rows = ncap + SLOT_WINDOW
    smem = pl.BlockSpec((None, E, 128), lambda b_, k: (b_, 0, 0), memory_space=pltpu.SMEM)
    nsub = 3 if nblk % 3 == 0 else 1
    es = pl.BlockSpec((None, E, nsub * tb), lambda b_, k: (b_, 0, k))
    xg, gv = pl.pallas_call(
        functools.partial(_moe_gather_body, ncap=ncap, tb=tb, nsub=nsub),
        grid=(B, nblk // nsub),
        in_specs=[smem, pl.BlockSpec((nsub * tb, D), lambda b_, k: (b_ * (nblk // nsub) + k, 0)), es, es],
        out_specs=[pl.BlockSpec((None, E, rows, D), lambda b_, k: (b_, 0, 0, 0)),
                   pl.BlockSpec((None, E, rows, 1), lambda b_, k: (b_, 0, 0, 0))],
        out_shape=[_sds((B, E, rows, D), BF16), _sds((B, E, rows, 1), F32)],
        compiler_params=_cp(("parallel", "arbitrary"), 48),
        name="moe_gather",
    )(rng, hm, slot, gate)
    wspec = lambda a, b: pl.BlockSpec((None, None, a, b), lambda e, b_: (li, e, 0, 0))
    nbp = math.gcd(B, 2)
    yw = pl.pallas_call(
        functools.partial(_moe_ffn_body, ncap=ncap),
        grid=(E, B // nbp),
        in_specs=[pl.BlockSpec((nbp, None, rows, D), lambda e, b_: (b_, e, 0, 0)),
                  pl.BlockSpec((nbp, None, rows, 1), lambda e, b_: (b_, e, 0, 0)),
                  wspec(D, Fd), wspec(D, Fd), wspec(Fd, D)],
        out_specs=pl.BlockSpec((nbp, None, rows, D), lambda e, b_: (b_, e, 0, 0)),
        out_shape=_sds((B, E, rows, D), BF16),
        scratch_shapes=[pltpu.VMEM((D, Fd), BF16), pltpu.VMEM((D, Fd), BF16), pltpu.VMEM((Fd, D), BF16)],
        compiler_params=_cp(("parallel", "arbitrary"), 48),
        name="moe_ffn",
    )(xg, gv, wg, wu, wd)
    final = final_g is not None
    nl = nblk - nbc
    if final:
        nsub = 1
        out_spec = pl.BlockSpec((tb, D), lambda b_, k: (b_ * nl + jnp.maximum(k - nbc, 0), 0))
        out_shape = _sds((B * nl * tb, D), F32)
    else:
        out_spec = pl.BlockSpec((nsub * tb, D), lambda b_, k: (b_ * (nblk // nsub) + k, 0))
        out_shape = _sds((N, D), F32)
        final_g = jnp.ones((1, D), F32)
    nstep = nblk // nsub
    es = pl.BlockSpec((None, E, nsub * tb), lambda b_, k: (b_, 0, k))
    return pl.pallas_call(
        functools.partial(_moe_scatter_body, ncap=ncap, nbc=nbc, final=final, tb=tb, nsub=nsub),
        grid=(B, nstep),
        in_specs=[smem, es, pl.BlockSpec((None, E, rows, D), lambda b_, k: (b_, 0, 0, 0)),
                  pl.BlockSpec((nsub * tb, D), lambda b_, k: (b_ * nstep + k, 0)),
                  pl.BlockSpec((None, 1, D), lambda b_, k: (b_, 0, 5)),
                  pl.BlockSpec((None, 1, D), lambda b_, k: (B, 0, 5)),
                  pl.BlockSpec((1, D), lambda b_, k: (0, 0))],
        out_specs=out_spec,
        out_shape=out_shape,
        compiler_params=_cp(("parallel", "arbitrary"), 48),
        name="moe_scatter",
    )(rng, slot, yw, xres, mods3, mods3, final_g)


def _layer(xa, mods3, lp, ew, li, tabs, B, S, Sc, ctx_out, final_g):
    N, D = xa.shape
    rc = _row_cfg(S, Sc)
    w_in = lp["w_in"]
    seg = lambda a, n: w_in[:, a:a + n]
    o_q, o_k, o_v, o_z = 0, ATTN_Q_DIM, ATTN_Q_DIM + ATTN_KV_DIM, ATTN_Q_DIM + 2 * ATTN_KV_DIM
    o_xbc = o_z + SSD_INNER
    o_dt = o_xbc + SSD_CONV_DIM
    o_u = o_dt + SSD_HEADS
    o_sb, o_sg, o_sh = o_u + S5_WIDTH, o_u + S5_WIDTH + SC_WIDTH, o_u + S5_WIDTH + 2 * SC_WIDTH
    o_g = o_sh + SC_WIDTH
    w_p = jnp.concatenate([seg(o_g, 4 * D), seg(o_q, ATTN_Q_DIM), seg(o_z, SSD_INNER), seg(o_k, ATTN_KV_DIM),
                           seg(o_v, ATTN_KV_DIM), seg(o_u, S5_WIDTH), seg(o_sb, SC_WIDTH), seg(o_sg, SC_WIDTH),
                           seg(o_sh, SC_WIDTH), seg(o_xbc, SSD_CONV_DIM)], axis=1).astype(BF16)
    w_dt = jnp.pad(seg(o_dt, SSD_HEADS), ((0, 0), (0, 128 - SSD_HEADS))).astype(BF16)
    proj_a, dt = _in_proj(xa, lp["norm_mix_g"][None, :], mods3, w_p, w_dt, B, S, Sc)

    ya = _attention(proj_a, tabs, lp["q_norm_g"], lp["k_norm_g"], B, S, Sc)

    xbc_c = _ssdconv(proj_a, lp["ssd_conv_w"], lp["ssd_conv_b"][None, :], B, S, Sc)
    yf, yb = _ssd(xbc_c, dt, lp["ssd_dt_bias"], lp["ssd_a_log"], B, S, Sc)

    Lc, Gn, G = S5_CHUNK, S5_GROUPS, S5_GROUP_DIM
    nc5, ncc5 = S // Lc, Sc // Lc
    w5, so5, ar5 = _s5gen(lp["s5_lambda_re"], lp["s5_lambda_im"], lp["s5_log_step"], lp["s5_b_re"], lp["s5_b_im"],
                          lp["s5_c_re"], lp["s5_c_im"], lp["s5_d"])
    u = proj_a[:, PA_U:PA_U + S5_WIDTH].reshape(B, nc5, Lc, S5_WIDTH)
    up = jnp.transpose(u, (2, 1, 0, 3)).reshape(N, S5_WIDTH)
    tr5 = nc5 * B
    y5t = _s5(_transpose2d(up, tr5, "s5_in_t"), w5, so5, ar5, B, nc5, ncc5)
    y5p = _untranspose2d(y5t, tr5, "s5_out_t").reshape(Lc, nc5, B, S5_WIDTH)
    y5 = jnp.transpose(y5p, (2, 1, 0, 3)).reshape(N, S5_WIDTH)

    yd = _shortconv(proj_a, lp["sc_conv_w"], B, S, Sc)

    dsk = jnp.repeat(lp["ssd_d"], SSD_HEAD_DIM)[None, :]
    x1 = _merge(xa, ya, yf, yb, xbc_c, proj_a, y5, yd, mods3,
                lp["w_br_attn"].astype(BF16), lp["w_br_ssd"].astype(BF16), lp["w_br_s5"].astype(BF16),
                lp["w_br_sc"].astype(BF16), lp["w_out"].astype(BF16), lp["s5_w_glu"].astype(BF16),
                lp["s5_b_glu"][None, :], lp["ssd_norm_g"][None, :], dsk, B, rc)

    wr_pad = jnp.pad(lp["w_router"], ((0, 0), (0, 128 - N_EXPERTS)))
    hm, slot, gate, rng = _router(x1, lp["norm_ffn_g"][None, :], mods3, wr_pad, B, S, Sc, ctx_out)
    return _experts(hm, slot, gate, rng, ew[0], ew[1], ew[2], li, x1, mods3, B, S, Sc, ctx_out, final_g)


def kernel(x, c, ctx, c_ctx, w_mod, b_mod, norm_mix_g, norm_ffn_g, w_in, q_norm_g, k_norm_g, ssd_conv_w, ssd_conv_b, ssd_dt_bias, ssd_a_log, ssd_d, ssd_norm_g, s5_lambda_re, s5_lambda_im, s5_log_step, s5_b_re, s5_b_im, s5_c_re, s5_c_im, s5_d, s5_w_glu, s5_b_glu, sc_conv_w, w_br_attn, w_br_ssd, w_br_s5, w_br_sc, w_out, w_router, w_exp_gate, w_exp_up, w_exp_down, final_norm_g):
    B, T, D = x.shape
    Sc = ctx.shape[1]
    S = Sc + T
    depth = w_in.shape[0]
    xa = jnp.concatenate([ctx, x], axis=1).reshape(B * S, D)
    cc = jnp.zeros((16, D), F32).at[0:B].set(c).at[B].set(c_ctx)
    cos, sin = _rope_tables(T, Sc)
    tabs = (jnp.tile(cos, (1, ATTN_HEADS)), jnp.tile(sin, (1, ATTN_HEADS)),
            jnp.tile(cos, (1, ATTN_KV_HEADS)), jnp.tile(sin, (1, ATTN_KV_HEADS)),
            _block_diag_ones(ATTN_Q_DIM), _block_diag_ones(ATTN_KV_DIM))
    stacked = dict(
        w_in=w_in, norm_mix_g=norm_mix_g, norm_ffn_g=norm_ffn_g, q_norm_g=q_norm_g, k_norm_g=k_norm_g,
        ssd_conv_w=ssd_conv_w, ssd_conv_b=ssd_conv_b, ssd_dt_bias=ssd_dt_bias, ssd_a_log=ssd_a_log, ssd_d=ssd_d,
        ssd_norm_g=ssd_norm_g, s5_lambda_re=s5_lambda_re, s5_lambda_im=s5_lambda_im, s5_log_step=s5_log_step,
        s5_b_re=s5_b_re, s5_b_im=s5_b_im, s5_c_re=s5_c_re, s5_c_im=s5_c_im, s5_d=s5_d, s5_w_glu=s5_w_glu,
        s5_b_glu=s5_b_glu, sc_conv_w=sc_conv_w, w_br_attn=w_br_attn, w_br_ssd=w_br_ssd, w_br_s5=w_br_s5,
        w_br_sc=w_br_sc, w_out=w_out, w_router=w_router)
    ew = (w_exp_gate, w_exp_up, w_exp_down)
    for i in range(depth):
        lp = {k: v[i] for k, v in stacked.items()}
        mods3 = _mods(cc, w_mod, b_mod[:, None, :], i).reshape(16, 1, 6 * D)
        last = i == depth - 1
        xa = _layer(xa, mods3, lp, ew, i, tabs, B, S, Sc, ctx_out=not last,
                    final_g=final_norm_g[None, :] if last else None)
    return xa.reshape(B, T, D)
```

```python
import functools
import math

import jax
import jax.numpy as jnp
import numpy as np
from jax import lax
from jax.experimental import pallas as pl
from jax.experimental.pallas import tpu as pltpu

F32 = jnp.float32
BF16 = jnp.bfloat16
I32 = jnp.int32
HI = lax.Precision.HIGHEST

RMS_EPS = 1e-6
GRID_W = 64
ROPE_THETA = 10000.0
HEAD_DIM = 64
ATTN_HEADS = 8
ATTN_KV_HEADS = 2
SSD_HEADS = 8
SSD_HEAD_DIM = 64
SSD_GROUPS = 2
SSD_STATE = 64
SSD_CHUNK = 128
S5_GROUPS = 24
S5_GROUP_DIM = 16
S5_STATE = 64
S5_CHUNK = 16
SC_WIDTH = 384
N_EXPERTS = 16
EC_CAPACITY = 2
MAX_ROW_BLOCK = 256
SLOT_WINDOW = 64
SLOT_ALIGN = 16

ATTN_Q_DIM = ATTN_HEADS * HEAD_DIM
ATTN_KV_DIM = ATTN_KV_HEADS * HEAD_DIM
SSD_INNER = SSD_HEADS * SSD_HEAD_DIM
SSD_BC_DIM = SSD_GROUPS * SSD_STATE
SSD_CONV_DIM = SSD_INNER + 2 * SSD_BC_DIM
S5_WIDTH = S5_GROUPS * S5_GROUP_DIM

PA_G, PA_Q, PA_Z, PA_K, PA_V = 0, 4096, 4608, 5120, 5248
PA_U, PA_SB, PA_SG, PA_SH, PA_XBC, PA_W = 5376, 5760, 6144, 6528, 6912, 7680


def _sds(shape, dtype):
    return jax.ShapeDtypeStruct(shape, dtype)


def _cp(sem, vmem_mb=None):
    kw = dict(dimension_semantics=sem)
    if vmem_mb is not None:
        kw["vmem_limit_bytes"] = vmem_mb << 20
    return pltpu.CompilerParams(**kw)


def _sigmoid(x):
    return 0.5 * jnp.tanh(0.5 * x) + 0.5


def _silu(x):
    return x * _sigmoid(x)


def _mods_body(c_ref, w_ref, b_ref, o_ref):
    s = _silu(c_ref[...])
    o_ref[...] = jnp.dot(s.astype(BF16), w_ref[...].astype(BF16), preferred_element_type=F32) + b_ref[...]


def _mods(cc, w, b, li):
    R, D = cc.shape
    N = w.shape[2]
    tn = 1536
    return pl.pallas_call(
        _mods_body,
        grid=(N // tn,),
        in_specs=[pl.BlockSpec((R, D), lambda j: (0, 0)),
                  pl.BlockSpec((None, D, tn), lambda j: (li, 0, j)),
                  pl.BlockSpec((None, 1, tn), lambda j: (li, 0, j))],
        out_specs=pl.BlockSpec((R, tn), lambda j: (0, j)),
        out_shape=_sds((R, N), F32),
        compiler_params=_cp(("parallel",), 40),
        name="mods",
    )(cc, w, b)


def _row_cfg(S, Sc):
    tm = math.gcd(math.gcd(Sc, S - Sc), MAX_ROW_BLOCK)
    return tm, S // tm, Sc // tm


def _mod_spec(D, rc, B, chunk):
    _, nb, nbc = rc
    return pl.BlockSpec((None, 1, D), lambda i: (jnp.where(i % nb < nbc, B, i // nb), 0, chunk))


def _in_proj_body(x_ref, g_ref, shl_ref, scl_ref, shc_ref, scc_ref, w_ref, wdt_ref, o_ref, dt_ref, hn_ref,
                  *, tm, nb, Sc):
    @pl.when(pl.program_id(1) == 0)
    def _():
        x = x_ref[...]
        ms = jnp.mean(x * x, axis=-1, keepdims=True)
        y = x * lax.rsqrt(ms + RMS_EPS) * g_ref[...]
        t = (pl.program_id(0) % nb) * tm + lax.broadcasted_iota(I32, (tm, 1), 0)
        isc = t < Sc
        hn = y * (1.0 + jnp.where(isc, scc_ref[...], scl_ref[...])) + jnp.where(isc, shc_ref[...], shl_ref[...])
        hn_ref[...] = hn.astype(BF16)
        dt_ref[...] = jnp.dot(hn_ref[...], wdt_ref[...], preferred_element_type=F32)

    o_ref[...] = jnp.dot(hn_ref[...], w_ref[...], preferred_element_type=F32).astype(o_ref.dtype)


def _in_proj(xa, g, mods3, w_p, w_dt, li, B, S, Sc):
    N, D = xa.shape
    tm = S // 2 if (S // 2) % 128 == 0 else S
    nb = S // tm
    tn = PA_W // 3
    assert tn % 256 == 0
    lat = lambda ch: pl.BlockSpec((None, 1, D), lambda i, j: (i // nb, 0, ch))
    ctx = lambda ch: pl.BlockSpec((None, 1, D), lambda i, j: (B, 0, ch))
    return pl.pallas_call(
        functools.partial(_in_proj_body, tm=tm, nb=nb, Sc=Sc),
        grid=(N // tm, PA_W // tn),
        in_specs=[pl.BlockSpec((tm, D), lambda i, j: (i, 0)),
                  pl.BlockSpec((1, D), lambda i, j: (0, 0)),
                  lat(0), lat(1), ctx(0), ctx(1),
                  pl.BlockSpec((None, D, tn), lambda i, j: (li, 0, j)),
                  pl.BlockSpec((None, D, 128), lambda i, j: (li, 0, 0))],
        out_specs=[pl.BlockSpec((tm, tn), lambda i, j: (i, j)), pl.BlockSpec((tm, 128), lambda i, j: (i, 0))],
        out_shape=[_sds((N, PA_W), BF16), _sds((N, 128), F32)],
        scratch_shapes=[pltpu.VMEM((tm, D), BF16)],
        compiler_params=_cp(("parallel", "arbitrary"), 56),
        name="in_proj",
    )(xa, g, mods3, mods3, mods3, mods3, w_p, w_dt)


def _shifted(x, Sc):
    S = x.shape[0]
    t = lax.broadcasted_iota(I32, (S, 1), 0)
    prev = jnp.where((t == 0) | (t == Sc), 0.0, pltpu.roll(x, 1, 0))
    nxt = jnp.where((t == Sc - 1) | (t == S - 1), 0.0, pltpu.roll(x, S - 1, 0))
    return prev, nxt


def _ssdconv_body(x_ref, w_ref, b_ref, o_ref, *, Sc):
    x = x_ref[...].astype(F32)
    prev, nxt = _shifted(x, Sc)
    w = w_ref[...]
    y = w[0:1, :] * prev + w[1:2, :] * x + w[2:3, :] * nxt + b_ref[...]
    o_ref[...] = _silu(y).astype(o_ref.dtype)


def _ssdconv(proj_a, w, b, B, S, Sc):
    C = 384
    j0 = PA_XBC // C
    return pl.pallas_call(
        functools.partial(_ssdconv_body, Sc=Sc),
        grid=(B, SSD_CONV_DIM // C),
        in_specs=[pl.BlockSpec((S, C), lambda b_, j: (b_, j0 + j)),
                  pl.BlockSpec((3, C), lambda b_, j: (0, j)),
                  pl.BlockSpec((1, C), lambda b_, j: (0, j))],
        out_specs=pl.BlockSpec((S, C), lambda b_, j: (b_, j)),
        out_shape=_sds((B * S, SSD_CONV_DIM), BF16),
        compiler_params=_cp(("parallel", "parallel"), 40),
        name="ssdconv",
    )(proj_a, w, b)


def _shortconv_body(sb_ref, sg_ref, sh_ref, w_ref, o_ref, *, Sc):
    x = sg_ref[...].astype(F32) * sh_ref[...].astype(F32)
    prev, nxt = _shifted(x, Sc)
    w = w_ref[...]
    y = w[0:1, :] * prev + w[1:2, :] * x + w[2:3, :] * nxt
    o_ref[...] = (sb_ref[...].astype(F32) * y).astype(o_ref.dtype)


def _shortconv(proj_a, w, B, S, Sc):
    C = SC_WIDTH
    return pl.pallas_call(
        functools.partial(_shortconv_body, Sc=Sc),
        grid=(B,),
        in_specs=[pl.BlockSpec((S, C), lambda b_: (b_, PA_SB // C)),
                  pl.BlockSpec((S, C), lambda b_: (b_, PA_SG // C)),
                  pl.BlockSpec((S, C), lambda b_: (b_, PA_SH // C)),
                  pl.BlockSpec((3, C), lambda b_: (0, 0))],
        out_specs=pl.BlockSpec((S, C), lambda b_: (b_, 0)),
        out_shape=_sds((B * S, C), BF16),
        compiler_params=_cp(("parallel",), 40),
        name="shortconv",
    )(proj_a, proj_a, proj_a, w)


def _norm_rope(x, g, cos, sin, bd, scale):
    W = x.shape[1]
    sq = x * x
    hi = sq.astype(BF16)
    lo = (sq - hi.astype(F32)).astype(BF16)
    ssq = jnp.dot(hi, bd, preferred_element_type=F32) + jnp.dot(lo, bd, preferred_element_type=F32)
    y = x * lax.rsqrt(ssq * (1.0 / HEAD_DIM) + RMS_EPS) * g
    lane = lax.broadcasted_iota(I32, (1, W), 1)
    first = (lane % 32) < 16
    partner = jnp.where(first, pltpu.roll(y, W - 16, 1), pltpu.roll(y, 16, 1))
    return (y * cos + partner * sin) * scale


def _attn_body(q_ref, k_ref, v_ref, cq_ref, sq_ref, ck_ref, sk_ref, gq_ref, gk_ref, bdq_ref, bdk_ref,
               o_ref, kh_ref, ve_ref, *, Sc, S, tq):
    qb = pl.program_id(1)
    hd = HEAD_DIM

    @pl.when(qb == 0)
    def _():
        k = k_ref[...].astype(F32)
        kh_ref[...] = _norm_rope(k, gk_ref[...], ck_ref[...], sk_ref[...], bdk_ref[...], 1.0).astype(BF16)
        v = v_ref[...].astype(F32)
        lane = lax.broadcasted_iota(I32, (1, 2 * hd), 1)
        ve_ref[0] = jnp.where(lane < hd, v, 1.0).astype(BF16)
        ve_ref[1] = jnp.where(lane < hd, pltpu.roll(v, hd, 1), 1.0).astype(BF16)

    q = q_ref[...].astype(F32)
    qh = _norm_rope(q, gq_ref[...], cq_ref[...], sq_ref[...], bdq_ref[...],
                    HEAD_DIM ** -0.5 * math.log2(math.e)).astype(BF16)
    rep = ATTN_HEADS // ATTN_KV_HEADS
    nt = (((1,), (1,)), ((), ()))

    def attend(splits):
        scores = []
        for g in range(ATTN_KV_HEADS):
            qg = jnp.concatenate([qh[:, (g * rep + r) * hd:(g * rep + r + 1) * hd] for r in range(rep)], axis=0)
            scores.append([lax.dot_general(qg, kh_ref[a:b, g * hd:(g + 1) * hd], nt, preferred_element_type=F32)
                           for a, b in splits])
        for g in range(ATTN_KV_HEADS):
            m = functools.reduce(jnp.maximum, [jnp.max(s, axis=-1, keepdims=True) for s in scores[g]])
            acc = None
            for s, (a, b) in zip(scores[g], splits):
                p = jnp.exp2(s - m).astype(BF16)
                part = jnp.dot(p, ve_ref[g, a:b, :], preferred_element_type=F32)
                acc = part if acc is None else acc + part
            o = acc[:, 0:hd] / acc[:, hd:hd + 1]
            for r in range(rep):
                h = g * rep + r
                o_ref[:, h * hd:(h + 1) * hd] = o[r * tq:(r + 1) * tq].astype(o_ref.dtype)

    @pl.when(qb < Sc // tq)
    def _():
        attend([(0, Sc)])

    @pl.when(qb >= Sc // tq)
    def _():
        half = (S // 2 + 255) // 256 * 256 if S >= 512 else S
        attend([(0, half), (half, S)] if half < S else [(0, S)])


def _rope_tables(T, Sc):
    rows = T // GRID_W
    row = np.repeat(np.arange(rows, dtype=np.float32), GRID_W)
    col = np.tile(np.arange(GRID_W, dtype=np.float32), rows)
    half = HEAD_DIM // 2
    inv = jnp.asarray(ROPE_THETA, F32) ** (-jnp.arange(0, half, 2, dtype=F32) / half)
    ra = jnp.asarray(row)[:, None] * inv
    ca = jnp.asarray(col)[:, None] * inv
    cos = jnp.concatenate([jnp.cos(ra), jnp.cos(ra), jnp.cos(ca), jnp.cos(ca)], axis=1)
    sin = jnp.concatenate([-jnp.sin(ra), jnp.sin(ra), -jnp.sin(ca), jnp.sin(ca)], axis=1)
    cos = jnp.concatenate([jnp.ones((Sc, HEAD_DIM), F32), cos], axis=0)
    sin = jnp.concatenate([jnp.zeros((Sc, HEAD_DIM), F32), sin], axis=0)
    return cos, sin


def _block_diag_ones(W):
    i = np.arange(W) // HEAD_DIM
    return jnp.asarray((i[:, None] == i[None, :]).astype(np.float32), BF16)


def _attention(proj_a, tabs, q_norm_g, k_norm_g, B, S, Sc):
    cos_q, sin_q, cos_k, sin_k, bdq, bdk = tabs
    tq = math.gcd(Sc, 256)
    nq = S // tq
    gq = jnp.tile(q_norm_g, ATTN_HEADS)[None, :]
    gk = jnp.tile(k_norm_g, ATTN_KV_HEADS)[None, :]
    const = lambda b_, i: (0, 0)
    return pl.pallas_call(
        functools.partial(_attn_body, Sc=Sc, S=S, tq=tq),
        grid=(B, nq),
        in_specs=[pl.BlockSpec((tq, ATTN_Q_DIM), lambda b_, i: (b_ * nq + i, PA_Q // ATTN_Q_DIM)),
                  pl.BlockSpec((S, ATTN_KV_DIM), lambda b_, i: (b_, PA_K // ATTN_KV_DIM)),
                  pl.BlockSpec((S, ATTN_KV_DIM), lambda b_, i: (b_, PA_V // ATTN_KV_DIM)),
                  pl.BlockSpec((tq, ATTN_Q_DIM), lambda b_, i: (i, 0)),
                  pl.BlockSpec((tq, ATTN_Q_DIM), lambda b_, i: (i, 0)),
                  pl.BlockSpec((S, ATTN_KV_DIM), const),
                  pl.BlockSpec((S, ATTN_KV_DIM), const),
                  pl.BlockSpec((1, ATTN_Q_DIM), const),
                  pl.BlockSpec((1, ATTN_KV_DIM), const),
                  pl.BlockSpec((ATTN_Q_DIM, ATTN_Q_DIM), const),
                  pl.BlockSpec((ATTN_KV_DIM, ATTN_KV_DIM), const)],
        out_specs=pl.BlockSpec((tq, ATTN_Q_DIM), lambda b_, i: (b_ * nq + i, 0)),
        out_shape=_sds((B * S, ATTN_Q_DIM), BF16),
        scratch_shapes=[pltpu.VMEM((S, ATTN_KV_DIM), BF16),
                        pltpu.VMEM((ATTN_KV_HEADS, S, 2 * HEAD_DIM), BF16)],
        compiler_params=_cp(("parallel", "arbitrary"), 56),
        name="attention",
    )(proj_a, proj_a, proj_a, cos_q, sin_q, cos_k, sin_k, gq, gk, bdq, bdk)


def _ssd_chains(chains):
    L = SSD_CHUNK
    P, Nst = SSD_HEAD_DIM, SSD_STATE
    rep = SSD_HEADS // SSD_GROUPS
    row = lax.broadcasted_iota(I32, (L, L), 0)
    col = lax.broadcasted_iota(I32, (L, L), 1)
    tri = (row >= col).astype(F32)
    lane = lax.broadcasted_iota(I32, (1, 2 * P), 1)
    lo_half = lane < P
    blockdiag = (row // Nst) == (col // P)
    upper = lax.broadcasted_iota(I32, (2 * Nst, 1), 0) < Nst
    st = []
    for xbc_ref, dt_ref, bias, a_neg, h_ref, y_ref, rev in chains:
        xbc = xbc_ref[...]
        bm = xbc[:, SSD_INNER:SSD_INNER + SSD_BC_DIM]
        cm = xbc[:, SSD_INNER + SSD_BC_DIM:SSD_CONV_DIM]
        dt = jax.nn.softplus(dt_ref[...] + bias)
        a = dt * a_neg
        cum = jnp.dot(tri, a, precision=HI, preferred_element_type=F32)
        e = cum - a if rev else cum
        tot_row = cum[L - 1:L, :]
        win_c = jnp.exp(tot_row - e) if rev else jnp.exp(e)
        win_hi = win_c.astype(BF16)
        cm32 = cm.astype(F32)
        st.append(dict(
            rev=rev, h_ref=h_ref, y_ref=y_ref, xbc=xbc, cm=cm, cm32=cm32, cm_rolled=pltpu.roll(cm32, P, 1),
            cum=cum, e=e, eT=e.T, dtT=dt.T, bT=bm.astype(F32).T.astype(BF16),
            mask=(col >= row) if rev else (row >= col),
            win_hi=win_hi, win_lo=(win_c - win_hi.astype(F32)).astype(BF16)))
    for g in range(SSD_GROUPS):
        own = (lane // P) == g
        for c in st:
            cg0 = jnp.where(own, c["cm"], jnp.zeros_like(c["cm"]))
            c["sg"] = jnp.dot(cg0, c["bT"], preferred_element_type=F32)
            c["cdup"] = jnp.where(own, c["cm32"], c["cm_rolled"])
            c["bTg"] = c["bT"][g * Nst:(g + 1) * Nst, :].astype(F32)
        for q in range(rep // 2):
            pq = (g * rep) // 2 + q
            hs = (2 * pq, 2 * pq + 1)
            sel = (lax.broadcasted_iota(I32, (2 * P, 2 * P), 0) == hs[0] + col // P).astype(BF16)
            for c in st:
                rev, e, eT, cum, dtT = c["rev"], c["e"], c["eT"], c["cum"], c["dtT"]
                ms, bws, tots = [], [], []
                for h in hs:
                    ecol = e[:, h:h + 1]
                    erow = eT[h:h + 1, :]
                    tot = cum[L - 1:L, h:h + 1]
                    diff = (erow - ecol) if rev else (ecol - erow)
                    dec = jnp.exp(jnp.where(c["mask"], diff, -1e30))
                    ms.append((c["sg"] * dec * dtT[h:h + 1, :]).astype(BF16))
                    wrow = (jnp.exp(erow) if rev else jnp.exp(tot - erow)) * dtT[h:h + 1, :]
                    bws.append(c["bTg"] * wrow)
                    tots.append(jnp.exp(tot))
                xp = c["xbc"][:, 2 * P * pq:2 * P * (pq + 1)]
                zero = jnp.zeros_like(xp)
                xbd = jnp.concatenate([jnp.where(lo_half, xp, zero), jnp.where(lo_half, zero, xp)], axis=0)
                y = jnp.dot(jnp.concatenate(ms, axis=1), xbd, preferred_element_type=F32)
                wexp = (jnp.dot(c["win_hi"], sel, preferred_element_type=F32)
                        + jnp.dot(c["win_lo"], sel, preferred_element_type=F32))
                hp = c["h_ref"][pq]
                y = y + jnp.dot((wexp * c["cdup"]).astype(BF16), hp.astype(BF16), preferred_element_type=F32)
                upd = jnp.dot(jnp.concatenate(bws, axis=0).astype(BF16), xp, preferred_element_type=F32)
                keep = jnp.where(upper, tots[0], tots[1])
                c["h_ref"][pq] = jnp.where(blockdiag, keep * hp + upd, 0.0)
                c["y_ref"][:, 2 * P * pq:2 * P * (pq + 1)] = y.astype(c["y_ref"].dtype)


def _ssd_body(xf_ref, dtf_ref, xb_ref, dtb_ref, bias_ref, alog_ref, yf_ref, yb_ref, *h_refs, nper):
    @pl.when(pl.program_id(1) == 0)
    def _():
        for h_ref in h_refs:
            h_ref[...] = jnp.zeros_like(h_ref)

    lane = lax.broadcasted_iota(I32, (1, 128), 1)
    a_neg = jnp.where(lane < SSD_HEADS, -jnp.exp(alog_ref[...]), 0.0)
    bias = bias_ref[...]
    chains = []
    for p in range(nper):
        chains.append((xf_ref.at[p], dtf_ref.at[p], bias[0:1, :], a_neg[0:1, :], h_refs[2 * p], yf_ref.at[p], False))
        chains.append((xb_ref.at[p], dtb_ref.at[p], bias[1:2, :], a_neg[1:2, :], h_refs[2 * p + 1], yb_ref.at[p], True))
    _ssd_chains(chains)


def _ssd(xbc_c, dt, dt_bias, a_log, B, S, Sc):
    L = SSD_CHUNK
    nc, ncc = S // L, Sc // L
    nper = math.gcd(B, 4)
    bg = B // nper
    padh = 128 - SSD_HEADS
    bias = jnp.pad(dt_bias, ((0, 0), (0, padh)))
    alog = jnp.pad(a_log, ((0, 0), (0, padh)))
    x3 = xbc_c.reshape(nper, bg * S, SSD_CONV_DIM)
    dt3 = dt.reshape(nper, bg * S, 128)

    def fwd(b_, i):
        return (0, b_ * nc + i, 0)

    def bwd(b_, i):
        return (0, b_ * nc + jnp.where(i < ncc, ncc - 1 - i, nc + ncc - 1 - i), 0)

    const = lambda b_, i: (0, 0)
    out = _sds((nper, bg * S, SSD_INNER), BF16)
    yf, yb = pl.pallas_call(
        functools.partial(_ssd_body, nper=nper),
        grid=(bg, nc),
        in_specs=[pl.BlockSpec((nper, L, SSD_CONV_DIM), fwd), pl.BlockSpec((nper, L, 128), fwd),
                  pl.BlockSpec((nper, L, SSD_CONV_DIM), bwd), pl.BlockSpec((nper, L, 128), bwd),
                  pl.BlockSpec((2, 128), const), pl.BlockSpec((2, 128), const)],
        out_specs=[pl.BlockSpec((nper, L, SSD_INNER), fwd), pl.BlockSpec((nper, L, SSD_INNER), bwd)],
        out_shape=[out, out],
        scratch_shapes=[pltpu.VMEM((SSD_HEADS // 2, 2 * SSD_STATE, 2 * SSD_HEAD_DIM), F32)] * (2 * nper),
        compiler_params=_cp(("parallel", "arbitrary")),
        name="ssd",
    )(x3, dt3, x3, dt3, bias, alog)
    return yf.reshape(B * S, SSD_INNER), yb.reshape(B * S, SSD_INNER)


def _cpow(n, lr, li, st):
    mag = jnp.exp(n * lr * st)
    ang = n * li * st
    return mag * jnp.cos(ang), mag * jnp.sin(ang)


def _zoh_coef(lr, li, st):
    ar, ai = _cpow(1.0, lr, li, st)
    nr, ni = ar - 1.0, ai
    den = lr * lr + li * li
    return (nr * lr + ni * li) / den, (ni * lr - nr * li) / den


def _spread_lanes(x, sel):
    hi = x.astype(BF16)
    r1 = x - hi.astype(F32)
    mid = r1.astype(BF16)
    lo = (r1 - mid.astype(F32)).astype(BF16)
    return (jnp.dot(hi, sel, preferred_element_type=F32) + jnp.dot(mid, sel, preferred_element_type=F32)
            + jnp.dot(lo, sel, preferred_element_type=F32))


def _s5gen_body(colp_ref, rowp_ref, ctr_ref, cti_ref, btr_ref, bti_ref, blr_ref, bli_ref, d_ref,
                wt_ref, so_ref, ar_ref, ws_ref, sos_ref):
    Lc, G, P = S5_CHUNK, S5_GROUP_DIM, S5_STATE
    NS = 2 * Lc + 1
    colp = colp_ref[...]
    rowp = rowp_ref[...]
    lrf, lif, stf = colp[:, 0:1], colp[:, 1:2], jnp.exp(colp[:, 2:3])
    lrb, lib, stb = colp[:, 3:4], colp[:, 4:5], jnp.exp(colp[:, 5:6])
    lane = lax.broadcasted_iota(I32, (1, NS * G), 1)
    slot = lane // G
    isb = slot < Lc
    cslot = lax.broadcasted_iota(I32, (1, 128), 1)
    cisb = cslot < Lc
    clag = jnp.where(cslot < NS, jnp.abs(cslot - Lc), 0).astype(F32)
    pr_c, pi_c = _cpow(clag, jnp.where(cisb, lrb, lrf), jnp.where(cisb, lib, lif), jnp.where(cisb, stb, stf))
    srow = lax.broadcasted_iota(I32, (128, 1), 0)
    spread = (srow == slot).astype(BF16)
    pr, pi = _spread_lanes(pr_c, spread), _spread_lanes(pi_c, spread)
    tile = (lax.broadcasted_iota(I32, (G, 1), 0) == lane % G).astype(BF16)
    ctr, cti = _spread_lanes(ctr_ref[...], tile), _spread_lanes(cti_ref[...], tile)
    er = ctr * pr - cti * pi
    ei = ctr * pi + cti * pr
    rlrf, rlif, rstf = rowp[0:1, :], rowp[1:2, :], jnp.exp(rowp[2:3, :])
    rlrb, rlib, rstb = rowp[3:4, :], rowp[4:5, :], jnp.exp(rowp[5:6, :])
    btr, bti = btr_ref[...], bti_ref[...]
    cfr, cfi = _zoh_coef(rlrf, rlif, rstf)
    cbr, cbi = _zoh_coef(rlrb, rlib, rstb)
    bbf_r, bbf_i = cfr * btr - cfi * bti, cfr * bti + cfi * btr
    bbb_r, bbb_i = cbr * btr - cbi * bti, cbr * bti + cbi * btr

    def kt(br, bi):
        return (jnp.dot(br, er, precision=HI, preferred_element_type=F32)
                - jnp.dot(bi, ei, precision=HI, preferred_element_type=F32))

    ktf, ktb = kt(bbf_r, bbf_i), kt(bbb_r, bbb_i)
    ii = lax.broadcasted_iota(I32, (G, NS * G), 0)
    dmat = jnp.where((slot == Lc) & (ii == lane - Lc * G), _spread_lanes(d_ref[...], tile), 0.0)
    strip = jnp.where(slot == Lc, ktf + ktb, jnp.where(isb, ktb, ktf)) + dmat
    for s in range(Lc):
        off = (Lc - s) * G
        ws_ref[s * G:(s + 1) * G, :] = strip[:, off:off + Lc * G]
    fo = (Lc + 1) * G
    ws_ref[Lc * G:Lc * G + P, :] = er[:, fo:fo + Lc * G]
    ws_ref[Lc * G + P:Lc * G + 2 * P, :] = er[:, 0:Lc * G]
    ws_ref[Lc * G + 2 * P:Lc * G + 3 * P, :] = -ei[:, fo:fo + Lc * G]
    ws_ref[Lc * G + 3 * P:Lc * G + 4 * P, :] = -ei[:, 0:Lc * G]
    wt_ref[:, 0:Lc * G] = ws_ref[0:Lc * G, :].T.astype(wt_ref.dtype)
    wt_ref[:, Lc * G:Lc * G + 4 * P] = ws_ref[Lc * G:Lc * G + 4 * P, :].T.astype(wt_ref.dtype)
    s_idx = lax.broadcasted_iota(I32, (1, Lc * G), 1) // G
    sel_f = (srow == 2 * Lc - 1 - s_idx).astype(BF16)
    sel_b = (srow == Lc - s_idx).astype(BF16)
    qfr, qfi = _spread_lanes(pr_c, sel_f), _spread_lanes(pi_c, sel_f)
    qbr, qbi = _spread_lanes(pr_c, sel_b), _spread_lanes(pi_c, sel_b)
    ccfr, ccfi = _zoh_coef(lrf, lif, stf)
    ccbr, ccbi = _zoh_coef(lrb, lib, stb)
    blr, bli = _spread_lanes(blr_ref[...], tile[:, 0:Lc * G]), _spread_lanes(bli_ref[...], tile[:, 0:Lc * G])
    bfr, bfi = ccfr * blr - ccfi * bli, ccfr * bli + ccfi * blr
    bbr, bbi = ccbr * blr - ccbi * bli, ccbr * bli + ccbi * blr
    sos_ref[0:P, :] = qfr * bfr - qfi * bfi
    sos_ref[P:2 * P, :] = qbr * bbr - qbi * bbi
    sos_ref[2 * P:3 * P, :] = qfr * bfi + qfi * bfr
    sos_ref[3 * P:4 * P, :] = qbr * bbi + qbi * bbr
    so_ref[...] = sos_ref[...].T.astype(so_ref.dtype)
    afr, afi = _cpow(float(Lc), rlrf, rlif, rstf)
    abr, abi = _cpow(float(Lc), rlrb, rlib, rstb)
    ar_ref[...] = jnp.zeros_like(ar_ref)
    ar_ref[0:1, 0:P] = afr
    ar_ref[0:1, P:2 * P] = abr
    ar_ref[1:2, 0:P] = afi
    ar_ref[1:2, P:2 * P] = abi


def _s5gen(lam_re, lam_im, log_step, b_re, b_im, c_re, c_im, d_skip):
    Gn, P, G, Lc = S5_GROUPS, S5_STATE, S5_GROUP_DIM, S5_CHUNK
    NS = 2 * Lc + 1
    ls = jnp.broadcast_to(log_step[:, :, None], (2, Gn, P))
    z = jnp.zeros((Gn, P), F32)
    rowp = jnp.stack([lam_re[0], lam_im[0], ls[0], lam_re[1], lam_im[1], ls[1], z, z], axis=1)
    colp = jnp.swapaxes(rowp, 1, 2)
    ctr = jnp.swapaxes(c_re, 1, 2)
    cti = jnp.swapaxes(c_im, 1, 2)
    btr = jnp.swapaxes(b_re, 1, 2)
    bti = jnp.swapaxes(b_im, 1, 2)
    blr, bli = b_re, b_im
    dt = d_skip.reshape(Gn, 1, G)
    g3 = lambda a, b: pl.BlockSpec((None, a, b), lambda g: (g, 0, 0))
    return pl.pallas_call(
        _s5gen_body,
        grid=(Gn,),
        in_specs=[g3(P, 8), g3(8, P), g3(P, G), g3(P, G), g3(G, P), g3(G, P), g3(P, G), g3(P, G), g3(1, G)],
        out_specs=[g3(Lc * G, Lc * G + 4 * P), g3(Lc * G, 4 * P), g3(8, 2 * P)],
        out_shape=[_sds((Gn, Lc * G, Lc * G + 4 * P), BF16), _sds((Gn, Lc * G, 4 * P), BF16),
                   _sds((Gn, 8, 2 * P), F32)],
        scratch_shapes=[pltpu.VMEM((Lc * G + 4 * P, Lc * G), F32), pltpu.VMEM((4 * P, Lc * G), F32)],
        compiler_params=_cp(("parallel",)),
        name="s5gen",
    )(colp, rowp, ctr, cti, btr, bti, blr, bli, dt)


def _s5_body(xt_ref, wt_ref, so_ref, ar_ref, y_ref, *scratch, B, nc, ncc, ng):
    P, Lc, G = S5_STATE, S5_CHUNK, S5_GROUP_DIM
    LG = Lc * G
    R = nc * B
    v_refs, h_refs = scratch[:ng], scratch[ng:]
    ucols = []
    for k in range(ng):
        ucol = jnp.concatenate([xt_ref[k * G:(k + 1) * G, l * R:(l + 1) * R] for l in range(Lc)], axis=0)
        v_refs[k][...] = lax.dot_general(ucol, so_ref[k], (((0,), (0,)), ((), ())), preferred_element_type=F32)
        ucols.append(ucol)
    lane = lax.broadcasted_iota(I32, (1, 2 * P), 1)
    isf = lane < P

    def step(i, carry):
        cf = pl.multiple_of(i * B, B)
        cb = pl.multiple_of(jnp.where(i < ncc, ncc - 1 - i, nc + ncc - 1 - i) * B, B)
        out = []
        for k in range(ng):
            hr, hi = carry[k]
            v_ref, h_ref = v_refs[k], h_refs[k]
            a_re, a_im = ar_ref[k, 0:1, :], ar_ref[k, 1:2, :]
            h_ref[pl.ds(cf, B), 0:P] = hr[:, 0:P]
            h_ref[pl.ds(cf, B), 2 * P:3 * P] = hi[:, 0:P]
            h_ref[pl.ds(cb, B), P:2 * P] = hr[:, P:2 * P]
            h_ref[pl.ds(cb, B), 3 * P:4 * P] = hi[:, P:2 * P]
            vr = jnp.where(isf, v_ref[pl.ds(cf, B), 0:2 * P], v_ref[pl.ds(cb, B), 0:2 * P])
            vi = jnp.where(isf, v_ref[pl.ds(cf, B), 2 * P:4 * P], v_ref[pl.ds(cb, B), 2 * P:4 * P])
            out.append((hr * a_re - hi * a_im + vr, hr * a_im + hi * a_re + vi))
        return tuple(out)

    z = jnp.zeros((B, 2 * P), F32)
    lax.fori_loop(0, nc, step, tuple((z, z) for _ in range(ng)))
    for k in range(ng):
        yt = jnp.dot(wt_ref[k, :, 0:LG], ucols[k], preferred_element_type=F32)
        yt = yt + lax.dot_general(wt_ref[k, :, LG:LG + 4 * P], h_refs[k][...].astype(BF16),
                                  (((1,), (1,)), ((), ())), preferred_element_type=F32)
        for t in range(Lc):
            y_ref[k * G:(k + 1) * G, t * R:(t + 1) * R] = yt[t * G:(t + 1) * G, :].astype(y_ref.dtype)


def _s5(xt, wt, so, ar, B, nc, ncc):
    P, Lc, G, Gn = S5_STATE, S5_CHUNK, S5_GROUP_DIM, S5_GROUPS
    N = xt.shape[1]
    R = nc * B
    ng = 2
    g3 = lambda a, b: pl.BlockSpec((ng, a, b), lambda g: (g, 0, 0))
    return pl.pallas_call(
        functools.partial(_s5_body, B=B, nc=nc, ncc=ncc, ng=ng),
        grid=(Gn // ng,),
        in_specs=[pl.BlockSpec((ng * G, N), lambda g: (g, 0)), g3(Lc * G, Lc * G + 4 * P), g3(Lc * G, 4 * P),
                  g3(8, 2 * P)],
        out_specs=pl.BlockSpec((ng * G, N), lambda g: (g, 0)),
        out_shape=_sds((Gn * G, N), BF16),
        scratch_shapes=[pltpu.VMEM((R, 4 * P), F32)] * (2 * ng),
        compiler_params=_cp(("parallel",)),
        name="s5",
    )(xt, wt, so, ar)


def _t2d_body(x_ref, o_ref):
    o_ref[...] = x_ref[...].astype(F32).T.astype(o_ref.dtype)


def _transpose2d(x, tr, name):
    M, C = x.shape
    return pl.pallas_call(
        _t2d_body,
        grid=(M // tr,),
        in_specs=[pl.BlockSpec((tr, C), lambda i: (i, 0))],
        out_specs=pl.BlockSpec((C, tr), lambda i: (0, i)),
        out_shape=_sds((C, M), x.dtype),
        compiler_params=_cp(("parallel",)),
        name=name,
    )(x)


def _untranspose2d(xt, tr, name):
    C, M = xt.shape
    return pl.pallas_call(
        _t2d_body,
        grid=(M // tr,),
        in_specs=[pl.BlockSpec((C, tr), lambda i: (0, i))],
        out_specs=pl.BlockSpec((tr, C), lambda i: (i, 0)),
        out_shape=_sds((M, C), xt.dtype),
        compiler_params=_cp(("parallel",)),
        name=name,
    )(xt)


def _merge_body(x_ref, ya_ref, yf_ref, yb_ref, xc_ref, z_ref, y5_ref, yd_ref, g_ref, gate_ref, sh_ref, sc_ref,
                wa_ref, wb_ref, wc_ref, wd_ref, wo_ref, wglu_ref, bglu_ref, ng_ref, dsk_ref, fg_ref, wrh_ref, wrl_ref,
                o_ref, hm_ref, lg_ref):
    tm, D = o_ref.shape
    nparts = 2 if tm % 32 == 0 else 1
    parts = [slice(p * tm // nparts, (p + 1) * tm // nparts) for p in range(nparts)]
    ys = [yf_ref[r, :].astype(F32) + yb_ref[r, :].astype(F32) + dsk_ref[...] * xc_ref[r, :].astype(F32) for r in parts]
    ys = [y * _silu(z_ref[r, :].astype(F32)) for y, r in zip(ys, parts)]
    ys = [(y * lax.rsqrt(jnp.mean(y * y, axis=-1, keepdims=True) + RMS_EPS) * ng_ref[...]).astype(BF16) for y in ys]
    cs = [jax.nn.gelu(y5_ref[r, :].astype(F32)).astype(BF16) for r in parts]
    glus = [_sigmoid(jnp.dot(c, wglu_ref[...], preferred_element_type=F32) + bglu_ref[...]) for c in cs]
    cs = [(c.astype(F32) * glu).astype(BF16) for c, glu in zip(cs, glus)]

    def gate(r, k):
        return _sigmoid(g_ref[r, k * D:(k + 1) * D].astype(F32))

    ms = [gate(r, 0) * jnp.dot(ya_ref[r, :], wa_ref[...], preferred_element_type=F32) for r in parts]
    ms = [m + gate(r, 1) * jnp.dot(y, wb_ref[...], preferred_element_type=F32) for m, y, r in zip(ms, ys, parts)]
    ms = [m + gate(r, 2) * jnp.dot(c, wc_ref[...], preferred_element_type=F32) for m, c, r in zip(ms, cs, parts)]
    ms = [m + gate(r, 3) * jnp.dot(yd_ref[r, :], wd_ref[...], preferred_element_type=F32) for m, r in zip(ms, parts)]
    outs = [jnp.dot(m.astype(BF16), wo_ref[...], preferred_element_type=F32) for m in ms]
    xs = [x_ref[r, :] + gate_ref[...] * out for out, r in zip(outs, parts)]
    hs = [x * lax.rsqrt(jnp.mean(x * x, axis=-1, keepdims=True) + RMS_EPS) * fg_ref[...] for x in xs]
    hs = [h * (1.0 + sc_ref[...]) + sh_ref[...] for h in hs]
    his = [h.astype(BF16) for h in hs]
    los = [(h - hi.astype(F32)).astype(BF16) for h, hi in zip(hs, his)]
    lgs = [jnp.dot(hi, wrh_ref[...], preferred_element_type=F32) + jnp.dot(lo, wrh_ref[...], preferred_element_type=F32)
           + jnp.dot(hi, wrl_ref[...], preferred_element_type=F32) for hi, lo in zip(his, los)]
    for x, hi, lg, r in zip(xs, his, lgs, parts):
        o_ref[r, :] = x
        hm_ref[r, :] = hi
        lg_ref[r, :] = lg


def _merge(xa, ya, yf, yb, xbc_c, proj, y5, yd, mods3, wa, wb, wc, wd, wo, wglu, bglu, ng, dsk, fg, wr, B, rc):
    N, D = xa.shape
    tm = rc[0]
    rows = lambda w, j=0: pl.BlockSpec((tm, w), lambda i: (i, j))
    full = lambda a: pl.BlockSpec(a.shape, lambda i: (0,) * a.ndim)
    wr_hi = wr.astype(BF16)
    wr_lo = (wr - wr_hi.astype(F32)).astype(BF16)
    return pl.pallas_call(
        _merge_body,
        grid=(N // tm,),
        in_specs=[rows(D), rows(ATTN_Q_DIM), rows(SSD_INNER), rows(SSD_INNER), rows(SSD_INNER),
                  rows(SSD_INNER, PA_Z // SSD_INNER), rows(S5_WIDTH), rows(SC_WIDTH), rows(4 * D, PA_G // (4 * D)),
                  _mod_spec(D, rc, B, 2), _mod_spec(D, rc, B, 3), _mod_spec(D, rc, B, 4),
                  full(wa), full(wb), full(wc), full(wd), full(wo), full(wglu), full(bglu), full(ng), full(dsk),
                  full(fg), full(wr_hi), full(wr_lo)],
        out_specs=[rows(D), rows(D), rows(128)],
        out_shape=[_sds((N, D), F32), _sds((N, D), BF16), _sds((N, 128), F32)],
        compiler_params=_cp(("parallel",), 48),
        name="merge",
    )(xa, ya, yf, yb, xbc_c, proj, y5, yd, proj, mods3, mods3, mods3, wa, wb, wc, wd, wo, wglu, bglu, ng, dsk,
      fg, wr_hi, wr_lo)


def _prefix_excl(mask_f):
    R, T = mask_f.shape
    r = lax.broadcasted_iota(I32, (128, 128), 0)
    c = lax.broadcasted_iota(I32, (128, 128), 1)
    upper = (r <= c).astype(BF16)
    outs = []
    off = jnp.zeros((R, 1), F32)
    for k in range(T // 128):
        blk = mask_f[:, k * 128:(k + 1) * 128]
        inc = jnp.dot(blk.astype(BF16), upper, preferred_element_type=F32)
        outs.append(inc - blk + off)
        off = off + inc[:, 127:128]
    return jnp.concatenate(outs, axis=1)


def _topk_slots(affs, caps):
    E = affs[0].shape[0]

    def step(i, ths):
        out = []
        for a, cap, th in zip(affs, caps, ths):
            cand = th | (jnp.int32(1) << (30 - i))
            cnt = jnp.sum((a >= pltpu.bitcast(cand, F32)).astype(I32), axis=1, keepdims=True)
            out.append(jnp.where(cnt >= cap, cand, th))
        return tuple(out)

    ths = lax.fori_loop(0, 31, step, tuple(jnp.zeros((E, 1), I32) for _ in affs))
    slots = []
    for a, cap, th in zip(affs, caps, ths):
        above = a >= pltpu.bitcast(jnp.maximum(th + 1, 0x00800000), F32)
        tie = (a >= pltpu.bitcast(th, F32)) & jnp.logical_not(above)
        n_above = jnp.sum(above.astype(F32), axis=1, keepdims=True)
        sel = above | (tie & (n_above + _prefix_excl(tie.astype(F32)) < cap))
        slots.append(jnp.where(sel, _prefix_excl(sel.astype(F32)), -1.0))
    return slots


def _router_body(lg_ref, slot_ref, gate_ref, rng_ref, *, Sc, cap_l, cap_c, ctx_out, tb):
    S = lg_ref.shape[0]
    lt = lg_ref[...].T[0:N_EXPERTS, :]
    mx = jnp.max(lt, axis=0, keepdims=True)
    ex = jnp.exp(lt - mx)
    aff = ex / jnp.sum(ex, axis=0, keepdims=True)
    gate_ref[...] = aff
    if ctx_out:
        slot_l, slot_c = _topk_slots([aff[:, Sc:], aff[:, 0:Sc]], [cap_l, cap_c])
        slot_c = jnp.where(slot_c >= 0.0, slot_c + cap_l, -1.0)
    else:
        slot_l, = _topk_slots([aff[:, Sc:]], [cap_l])
        slot_c = jnp.full((N_EXPERTS, Sc), -1.0, F32)
    slot_ref[:, 0:Sc] = slot_c
    slot_ref[:, Sc:] = slot_l
    lane = lax.broadcasted_iota(I32, (1, 128), 1)
    lo_a = jnp.zeros((N_EXPERTS, 128), F32)
    nwin = jnp.zeros((N_EXPERTS, 1), F32)
    for seg0, seg1, base, slots in ((0, Sc, float(cap_l), slot_c), (Sc, S, 0.0, slot_l)):
        lo = jnp.full((N_EXPERTS, 1), base, F32)
        for k in range(seg0 // tb, seg1 // tb):
            blk = slots[:, k * tb - seg0:(k + 1) * tb - seg0]
            hi = lo + jnp.sum((blk >= 0.0).astype(F32), axis=1, keepdims=True)
            lo_al = jnp.floor(lo * (1.0 / SLOT_ALIGN)) * SLOT_ALIGN
            need = jnp.where(hi > lo, jnp.floor((hi - lo_al + (SLOT_WINDOW - 1)) * (1.0 / SLOT_WINDOW)), 0.0)
            nwin = jnp.maximum(nwin, need)
            lo_a = jnp.where(lane == k, lo, lo_a)
            lo = hi
    rng_ref[...] = jnp.where(lane == 127, jnp.max(nwin, axis=0, keepdims=True), lo_a).astype(I32)


def _token_block(S, Sc):
    return math.gcd(math.gcd(Sc, S - Sc), 256)


def _router(logits, B, S, Sc, ctx_out):
    T = S - Sc
    cap_l = EC_CAPACITY * T // N_EXPERTS
    cap_c = EC_CAPACITY * Sc // N_EXPERTS
    tb = _token_block(S, Sc)
    es = pl.BlockSpec((None, N_EXPERTS, S), lambda b_: (b_, 0, 0))
    return pl.pallas_call(
        functools.partial(_router_body, Sc=Sc, cap_l=cap_l, cap_c=cap_c, ctx_out=ctx_out, tb=tb),
        grid=(B,),
        in_specs=[pl.BlockSpec((S, 128), lambda b_: (b_, 0))],
        out_specs=[es, es, pl.BlockSpec((None, N_EXPERTS, 128), lambda b_: (b_, 0, 0))],
        out_shape=[_sds((B, N_EXPERTS, S), F32), _sds((B, N_EXPERTS, S), F32), _sds((B, N_EXPERTS, 128), I32)],
        compiler_params=_cp(("parallel",)),
        name="router",
    )(logits)


def _window_rows(rng_ref, k, j, ncap):
    rows = []
    for e in range(N_EXPERTS):
        lo = rng_ref[e, k]
        lo_al = lax.shift_left(lax.shift_right_logical(lo, SLOT_ALIGN.bit_length() - 1), SLOT_ALIGN.bit_length() - 1)
        rows.append(pl.multiple_of(jnp.minimum(lo_al + j * SLOT_WINDOW, ncap), SLOT_ALIGN))
    return rows


def _window_hits(rows, slot_ref, cols):
    wrow = lax.broadcasted_iota(I32, (SLOT_WINDOW, 1), 0)
    return [(wrow + rows[e]).astype(F32) == slot_ref[e:e + 1, cols] for e in range(N_EXPERTS)]


def _moe_gather_body(rng_ref, hm_ref, slot_ref, gate_ref, xg_ref, gv_ref, *, ncap, tb, nsub):
    kk = pl.program_id(1)
    W = SLOT_WINDOW

    @pl.when(kk == 0)
    def _():
        xg_ref[...] = jnp.zeros_like(xg_ref)
        gv_ref[...] = jnp.zeros_like(gv_ref)

    for sub in range(nsub):
        k = kk * nsub + sub
        cols = slice(sub * tb, (sub + 1) * tb)

        def window(j, carry, k=k, cols=cols):
            rows = _window_rows(rng_ref, k, j, ncap)
            hits = _window_hits(rows, slot_ref, cols)
            hit_all = jnp.concatenate([h.astype(BF16) for h in hits], axis=0)
            xgw = jnp.dot(hit_all, hm_ref[cols, :], preferred_element_type=F32).astype(BF16)
            for e in range(N_EXPERTS):
                r = pl.ds(rows[e], W)
                xg_ref[e, r, :] += xgw[e * W:(e + 1) * W, :]
                gv_ref[e, r, :] += jnp.sum(jnp.where(hits[e], gate_ref[e:e + 1, cols], 0.0), axis=1, keepdims=True)
            return carry

        lax.fori_loop(0, rng_ref[0, 127], window, 0)


def _moe_ffn_body(xg_ref, gv_ref, wg_ref, wu_ref, wd_ref, yw_ref, wgs_ref, wus_ref, wds_ref, *, ncap):
    @pl.when(pl.program_id(1) == 0)
    def _():
        wgs_ref[...] = wg_ref[...].astype(BF16)
        wus_ref[...] = wu_ref[...].astype(BF16)
        wds_ref[...] = wd_ref[...].astype(BF16)

    nbp = xg_ref.shape[0]
    xg = jnp.concatenate([xg_ref[p, 0:ncap, :] for p in range(nbp)], axis=0)
    gv = jnp.concatenate([gv_ref[p, 0:ncap, :] for p in range(nbp)], axis=0)
    hid = _silu(jnp.dot(xg, wgs_ref[...], preferred_element_type=F32)) * jnp.dot(
        xg, wus_ref[...], preferred_element_type=F32)
    ye = (jnp.dot(hid.astype(BF16), wds_ref[...], preferred_element_type=F32) * gv).astype(yw_ref.dtype)
    for p in range(nbp):
        yw_ref[p, 0:ncap, :] = ye[p * ncap:(p + 1) * ncap, :]
        yw_ref[p, ncap:, :] = jnp.zeros((yw_ref.shape[1] - ncap, yw_ref.shape[2]), yw_ref.dtype)


def _moe_scatter_body(rng_ref, slot_ref, yw_ref, x_ref, gl_ref, gc_ref, fg_ref, o_ref,
                      *, ncap, nbc, final, tb, nsub):
    kk = pl.program_id(1)
    W = SLOT_WINDOW
    D = o_ref.shape[1]

    for sub in range(nsub):
        k = kk * nsub + sub
        cols = slice(sub * tb, (sub + 1) * tb)

        def window(j, acc, k=k, cols=cols):
            rows = _window_rows(rng_ref, k, j, ncap)
            hit_all = jnp.concatenate([h.astype(BF16) for h in _window_hits(rows, slot_ref, cols)], axis=0)
            yw = jnp.concatenate([yw_ref[e, pl.ds(rows[e], W), :] for e in range(N_EXPERTS)], axis=0)
            return acc + lax.dot_general(hit_all, yw, (((0,), (0,)), ((), ())), preferred_element_type=F32)

        def emit(k=k, cols=cols, window=window):
            acc = lax.fori_loop(0, rng_ref[0, 127], window, jnp.zeros((tb, D), F32))
            x = x_ref[cols, :] + jnp.where(k < nbc, gc_ref[...], gl_ref[...]) * acc
            if final:
                x = x * lax.rsqrt(jnp.mean(x * x, axis=-1, keepdims=True) + RMS_EPS) * fg_ref[...]
            o_ref[cols, :] = x

        if final:
            pl.when(k >= nbc)(emit)
        else:
            emit()


def _experts(hm, slot, gate, rng, wg, wu, wd, li, xres, mods3, B, S, Sc, ctx_out, final_g):
    N, D = hm.shape
    _, E, _, Fd = wg.shape
    T = S - Sc
    tb = _token_block(S, Sc)
    nblk, nbc = S // tb, Sc // tb
    ncap = EC_CAPACITY * T // N_EXPERTS + (EC_CAPACITY * Sc // N_EXPERTS if ctx_out else 0)
    assert ncap % SLOT_ALIGN == 0
    rows = ncap + SLOT_WINDOW
    smem = pl.BlockSpec((None, E, 128), lambda b_, k: (b_, 0, 0), memory_space=pltpu.SMEM)
    nsub = 3 if nblk % 3 == 0 else 1
    es = pl.BlockSpec((None, E, nsub * tb), lambda b_, k: (b_, 0, k))
    xg, gv = pl.pallas_call(
        functools.partial(_moe_gather_body, ncap=ncap, tb=tb, nsub=nsub),
        grid=(B, nblk // nsub),
        in_specs=[smem, pl.BlockSpec((nsub * tb, D), lambda b_, k: (b_ * (nblk // nsub) + k, 0)), es, es],
        out_specs=[pl.BlockSpec((None, E, rows, D), lambda b_, k: (b_, 0, 0, 0)),
                   pl.BlockSpec((None, E, rows, 1), lambda b_, k: (b_, 0, 0, 0))],
        out_shape=[_sds((B, E, rows, D), BF16), _sds((B, E, rows, 1), F32)],
        compiler_params=_cp(("parallel", "arbitrary"), 48),
        name="moe_gather",
    )(rng, hm, slot, gate)
    wspec = lambda a, b: pl.BlockSpec((None, None, a, b), lambda e, b_: (li, e, 0, 0))
    nbp = math.gcd(B, 2)
    yw = pl.pallas_call(
        functools.partial(_moe_ffn_body, ncap=ncap),
        grid=(E, B // nbp),
        in_specs=[pl.BlockSpec((nbp, None, rows, D), lambda e, b_: (b_, e, 0, 0)),
                  pl.BlockSpec((nbp, None, rows, 1), lambda e, b_: (b_, e, 0, 0)),
                  wspec(D, Fd), wspec(D, Fd), wspec(Fd, D)],
        out_specs=pl.BlockSpec((nbp, None, rows, D), lambda e, b_: (b_, e, 0, 0)),
        out_shape=_sds((B, E, rows, D), BF16),
        scratch_shapes=[pltpu.VMEM((D, Fd), BF16), pltpu.VMEM((D, Fd), BF16), pltpu.VMEM((Fd, D), BF16)],
        compiler_params=_cp(("parallel", "arbitrary"), 48),
        name="moe_ffn",
    )(xg, gv, wg, wu, wd)
    final = final_g is not None
    nl = nblk - nbc
    if final:
        nsub = 1
        out_spec = pl.BlockSpec((tb, D), lambda b_, k: (b_ * nl + jnp.maximum(k - nbc, 0), 0))
        out_shape = _sds((B * nl * tb, D), F32)
    else:
        out_spec = pl.BlockSpec((nsub * tb, D), lambda b_, k: (b_ * (nblk // nsub) + k, 0))
        out_shape = _sds((N, D), F32)
        final_g = jnp.ones((1, D), F32)
    nstep = nblk // nsub
    es = pl.BlockSpec((None, E, nsub * tb), lambda b_, k: (b_, 0, k))
    return pl.pallas_call(
        functools.partial(_moe_scatter_body, ncap=ncap, nbc=nbc, final=final, tb=tb, nsub=nsub),
        grid=(B, nstep),
        in_specs=[smem, es, pl.BlockSpec((None, E, rows, D), lambda b_, k: (b_, 0, 0, 0)),
                  pl.BlockSpec((nsub * tb, D), lambda b_, k: (b_ * nstep + k, 0)),
                  pl.BlockSpec((None, 1, D), lambda b_, k: (b_, 0, 5)),
                  pl.BlockSpec((None, 1, D), lambda b_, k: (B, 0, 5)),
                  pl.BlockSpec((1, D), lambda b_, k: (0, 0))],
        out_specs=out_spec,
        out_shape=out_shape,
        compiler_params=_cp(("parallel", "arbitrary"), 48),
        name="moe_scatter",
    )(rng, slot, yw, xres, mods3, mods3, final_g)


def _pack_w_in(w_in):
    D = w_in.shape[1]
    seg = lambda a, n: w_in[:, :, a:a + n]
    o_q, o_k, o_v, o_z = 0, ATTN_Q_DIM, ATTN_Q_DIM + ATTN_KV_DIM, ATTN_Q_DIM + 2 * ATTN_KV_DIM
    o_xbc = o_z + SSD_INNER
    o_dt = o_xbc + SSD_CONV_DIM
    o_u = o_dt + SSD_HEADS
    o_sb, o_sg, o_sh = o_u + S5_WIDTH, o_u + S5_WIDTH + SC_WIDTH, o_u + S5_WIDTH + 2 * SC_WIDTH
    o_g = o_sh + SC_WIDTH
    w_p = jnp.concatenate([seg(o_g, 4 * D), seg(o_q, ATTN_Q_DIM), seg(o_z, SSD_INNER), seg(o_k, ATTN_KV_DIM),
                           seg(o_v, ATTN_KV_DIM), seg(o_u, S5_WIDTH), seg(o_sb, SC_WIDTH), seg(o_sg, SC_WIDTH),
                           seg(o_sh, SC_WIDTH), seg(o_xbc, SSD_CONV_DIM)], axis=2).astype(BF16)
    w_dt = jnp.pad(seg(o_dt, SSD_HEADS), ((0, 0), (0, 0), (0, 128 - SSD_HEADS))).astype(BF16)
    return w_p, w_dt


def _layer(xa, mods3, lp, wpk, ew, li, tabs, B, S, Sc, ctx_out, final_g):
    N, D = xa.shape
    rc = _row_cfg(S, Sc)
    proj_a, dt = _in_proj(xa, lp["norm_mix_g"][None, :], mods3, wpk[0], wpk[1], li, B, S, Sc)

    ya = _attention(proj_a, tabs, lp["q_norm_g"], lp["k_norm_g"], B, S, Sc)

    xbc_c = _ssdconv(proj_a, lp["ssd_conv_w"], lp["ssd_conv_b"][None, :], B, S, Sc)
    yf, yb = _ssd(xbc_c, dt, lp["ssd_dt_bias"], lp["ssd_a_log"], B, S, Sc)

    Lc, Gn, G = S5_CHUNK, S5_GROUPS, S5_GROUP_DIM
    nc5, ncc5 = S // Lc, Sc // Lc
    w5, so5, ar5 = _s5gen(lp["s5_lambda_re"], lp["s5_lambda_im"], lp["s5_log_step"], lp["s5_b_re"], lp["s5_b_im"],
                          lp["s5_c_re"], lp["s5_c_im"], lp["s5_d"])
    u = proj_a[:, PA_U:PA_U + S5_WIDTH].reshape(B, nc5, Lc, S5_WIDTH)
    up = jnp.transpose(u, (2, 1, 0, 3)).reshape(N, S5_WIDTH)
    tr5 = nc5 * B
    y5t = _s5(_transpose2d(up, tr5, "s5_in_t"), w5, so5, ar5, B, nc5, ncc5)
    y5p = _untranspose2d(y5t, tr5, "s5_out_t").reshape(Lc, nc5, B, S5_WIDTH)
    y5 = jnp.transpose(y5p, (2, 1, 0, 3)).reshape(N, S5_WIDTH)

    yd = _shortconv(proj_a, lp["sc_conv_w"], B, S, Sc)

    dsk = jnp.repeat(lp["ssd_d"], SSD_HEAD_DIM)[None, :]
    wr_pad = jnp.pad(lp["w_router"], ((0, 0), (0, 128 - N_EXPERTS)))
    x1, hm, logits = _merge(xa, ya, yf, yb, xbc_c, proj_a, y5, yd, mods3,
                            lp["w_br_attn"].astype(BF16), lp["w_br_ssd"].astype(BF16), lp["w_br_s5"].astype(BF16),
                            lp["w_br_sc"].astype(BF16), lp["w_out"].astype(BF16), lp["s5_w_glu"].astype(BF16),
                            lp["s5_b_glu"][None, :], lp["ssd_norm_g"][None, :], dsk,
                            lp["norm_ffn_g"][None, :], wr_pad, B, rc)
    slot, gate, rng = _router(logits, B, S, Sc, ctx_out)
    return _experts(hm, slot, gate, rng, ew[0], ew[1], ew[2], li, x1, mods3, B, S, Sc, ctx_out, final_g)


def kernel(x, c, ctx, c_ctx, w_mod, b_mod, norm_mix_g, norm_ffn_g, w_in, q_norm_g, k_norm_g, ssd_conv_w, ssd_conv_b, ssd_dt_bias, ssd_a_log, ssd_d, ssd_norm_g, s5_lambda_re, s5_lambda_im, s5_log_step, s5_b_re, s5_b_im, s5_c_re, s5_c_im, s5_d, s5_w_glu, s5_b_glu, sc_conv_w, w_br_attn, w_br_ssd, w_br_s5, w_br_sc, w_out, w_router, w_exp_gate, w_exp_up, w_exp_down, final_norm_g):
    B, T, D = x.shape
    Sc = ctx.shape[1]
    S = Sc + T
    depth = w_in.shape[0]
    xa = jnp.concatenate([ctx, x], axis=1).reshape(B * S, D)
    cc = jnp.zeros((16, D), F32).at[0:B].set(c).at[B].set(c_ctx)
    cos, sin = _rope_tables(T, Sc)
    tabs = (jnp.tile(cos, (1, ATTN_HEADS)), jnp.tile(sin, (1, ATTN_HEADS)),
            jnp.tile(cos, (1, ATTN_KV_HEADS)), jnp.tile(sin, (1, ATTN_KV_HEADS)),
            _block_diag_ones(ATTN_Q_DIM), _block_diag_ones(ATTN_KV_DIM))
    stacked = dict(
        norm_mix_g=norm_mix_g, norm_ffn_g=norm_ffn_g, q_norm_g=q_norm_g, k_norm_g=k_norm_g,
        ssd_conv_w=ssd_conv_w, ssd_conv_b=ssd_conv_b, ssd_dt_bias=ssd_dt_bias, ssd_a_log=ssd_a_log, ssd_d=ssd_d,
        ssd_norm_g=ssd_norm_g, s5_lambda_re=s5_lambda_re, s5_lambda_im=s5_lambda_im, s5_log_step=s5_log_step,
        s5_b_re=s5_b_re, s5_b_im=s5_b_im, s5_c_re=s5_c_re, s5_c_im=s5_c_im, s5_d=s5_d, s5_w_glu=s5_w_glu,
        s5_b_glu=s5_b_glu, sc_conv_w=sc_conv_w, w_br_attn=w_br_attn, w_br_ssd=w_br_ssd, w_br_s5=w_br_s5,
        w_br_sc=w_br_sc, w_out=w_out, w_router=w_router)
    ew = (w_exp_gate, w_exp_up, w_exp_down)
    wpk = _pack_w_in(w_in)
    for i in range(depth):
        lp = {k: v[i] for k, v in stacked.items()}
        mods3 = _mods(cc, w_mod, b_mod[:, None, :], i).reshape(16, 1, 6 * D)
        last = i == depth - 1
        xa = _layer(xa, mods3, lp, wpk, ew, i, tabs, B, S, Sc, ctx_out=not last,
                    final_g=final_norm_g[None, :] if last else None)
    return xa.reshape(B, T, D)
```

```python
import functools
import math

import jax
import jax.numpy as jnp
import numpy as np
from jax import lax
from jax.experimental import pallas as pl
from jax.experimental.pallas import tpu as pltpu

F32 = jnp.float32
BF16 = jnp.bfloat16
I32 = jnp.int32
HI = lax.Precision.HIGHEST

RMS_EPS = 1e-6
GRID_W = 64
ROPE_THETA = 10000.0
HEAD_DIM = 64
ATTN_HEADS = 8
ATTN_KV_HEADS = 2
SSD_HEADS = 8
SSD_HEAD_DIM = 64
SSD_GROUPS = 2
SSD_STATE = 64
SSD_CHUNK = 128
S5_GROUPS = 24
S5_GROUP_DIM = 16
S5_STATE = 64
S5_CHUNK = 16
SC_WIDTH = 384
N_EXPERTS = 16
EC_CAPACITY = 2
MAX_ROW_BLOCK = 256
SLOT_WINDOW = 64
SLOT_ALIGN = 16

ATTN_Q_DIM = ATTN_HEADS * HEAD_DIM
ATTN_KV_DIM = ATTN_KV_HEADS * HEAD_DIM
SSD_INNER = SSD_HEADS * SSD_HEAD_DIM
SSD_BC_DIM = SSD_GROUPS * SSD_STATE
SSD_CONV_DIM = SSD_INNER + 2 * SSD_BC_DIM
S5_WIDTH = S5_GROUPS * S5_GROUP_DIM

PA_G, PA_Q, PA_Z, PA_K, PA_V = 0, 4096, 4608, 5120, 5248
PA_U, PA_SB, PA_SG, PA_SH, PA_XBC, PA_W = 5376, 5760, 6144, 6528, 6912, 7680


def _sds(shape, dtype):
    return jax.ShapeDtypeStruct(shape, dtype)


def _cp(sem, vmem_mb=None):
    kw = dict(dimension_semantics=sem)
    if vmem_mb is not None:
        kw["vmem_limit_bytes"] = vmem_mb << 20
    return pltpu.CompilerParams(**kw)


def _sigmoid(x):
    return 0.5 * jnp.tanh(0.5 * x) + 0.5


def _silu(x):
    return x * _sigmoid(x)


def _mods_body(c_ref, w_ref, b_ref, o_ref):
    s = _silu(c_ref[...])
    o_ref[...] = jnp.dot(s.astype(BF16), w_ref[...].astype(BF16), preferred_element_type=F32) + b_ref[...]


def _mods(cc, w, b, li):
    R, D = cc.shape
    N = w.shape[2]
    tn = 1536
    return pl.pallas_call(
        _mods_body,
        grid=(N // tn,),
        in_specs=[pl.BlockSpec((R, D), lambda j: (0, 0)),
                  pl.BlockSpec((None, D, tn), lambda j: (li, 0, j)),
                  pl.BlockSpec((None, 1, tn), lambda j: (li, 0, j))],
        out_specs=pl.BlockSpec((R, tn), lambda j: (0, j)),
        out_shape=_sds((R, N), F32),
        compiler_params=_cp(("parallel",), 40),
        name="mods",
    )(cc, w, b)


def _row_cfg(S, Sc):
    tm = math.gcd(math.gcd(Sc, S - Sc), MAX_ROW_BLOCK)
    return tm, S // tm, Sc // tm


def _mod_spec(D, rc, B, chunk):
    _, nb, nbc = rc
    return pl.BlockSpec((None, 1, D), lambda i: (jnp.where(i % nb < nbc, B, i // nb), 0, chunk))


def _in_proj_body(x_ref, g_ref, shl_ref, scl_ref, shc_ref, scc_ref, w_ref, wdt_ref, o_ref, dt_ref, hn_ref,
                  *, tm, nb, Sc):
    j = pl.program_id(1)

    @pl.when(j == 0)
    def _():
        nh = 2 if tm % 32 == 0 else 1
        for p in range(nh):
            r = slice(p * tm // nh, (p + 1) * tm // nh)
            x = x_ref[r, :]
            ms = jnp.mean(x * x, axis=-1, keepdims=True)
            y = x * lax.rsqrt(ms + RMS_EPS) * g_ref[...]
            t = (pl.program_id(0) % nb) * tm + p * tm // nh + lax.broadcasted_iota(I32, (tm // nh, 1), 0)
            isc = t < Sc
            hn = y * (1.0 + jnp.where(isc, scc_ref[...], scl_ref[...])) + jnp.where(isc, shc_ref[...], shl_ref[...])
            hn_ref[r, :] = hn.astype(BF16)
            o_ref[r, :] = jnp.dot(hn_ref[r, :], w_ref[...], preferred_element_type=F32).astype(o_ref.dtype)
        dt_ref[...] = jnp.dot(hn_ref[...], wdt_ref[...], preferred_element_type=F32)

    @pl.when(j > 0)
    def _():
        o_ref[...] = jnp.dot(hn_ref[...], w_ref[...], preferred_element_type=F32).astype(o_ref.dtype)


def _in_proj(xa, g, mods3, w_p, w_dt, B, S, Sc):
    N, D = xa.shape
    tm = S // 2 if (S // 2) % 128 == 0 else S
    nb = S // tm
    tn = PA_W // 3
    assert tn % 256 == 0
    lat = lambda ch: pl.BlockSpec((None, 1, D), lambda i, j: (i // nb, 0, ch))
    ctx = lambda ch: pl.BlockSpec((None, 1, D), lambda i, j: (B, 0, ch))
    return pl.pallas_call(
        functools.partial(_in_proj_body, tm=tm, nb=nb, Sc=Sc),
        grid=(N // tm, PA_W // tn),
        in_specs=[pl.BlockSpec((tm, D), lambda i, j: (i, 0)),
                  pl.BlockSpec((1, D), lambda i, j: (0, 0)),
                  lat(0), lat(1), ctx(0), ctx(1),
                  pl.BlockSpec((D, tn), lambda i, j: (0, j)),
                  pl.BlockSpec((D, 128), lambda i, j: (0, 0))],
        out_specs=[pl.BlockSpec((tm, tn), lambda i, j: (i, j)), pl.BlockSpec((tm, 128), lambda i, j: (i, 0))],
        out_shape=[_sds((N, PA_W), BF16), _sds((N, 128), F32)],
        scratch_shapes=[pltpu.VMEM((tm, D), BF16)],
        compiler_params=_cp(("parallel", "arbitrary"), 56),
        name="in_proj",
    )(xa, g, mods3, mods3, mods3, mods3, w_p, w_dt)


def _shifted(x, Sc):
    S = x.shape[0]
    t = lax.broadcasted_iota(I32, (S, 1), 0)
    prev = jnp.where((t == 0) | (t == Sc), 0.0, pltpu.roll(x, 1, 0))
    nxt = jnp.where((t == Sc - 1) | (t == S - 1), 0.0, pltpu.roll(x, S - 1, 0))
    return prev, nxt


def _ssdconv_body(x_ref, w_ref, b_ref, o_ref, *, Sc):
    x = x_ref[...].astype(F32)
    prev, nxt = _shifted(x, Sc)
    w = w_ref[...]
    y = w[0:1, :] * prev + w[1:2, :] * x + w[2:3, :] * nxt + b_ref[...]
    o_ref[...] = _silu(y).astype(o_ref.dtype)


def _ssdconv(proj_a, w, b, B, S, Sc):
    C = 384
    j0 = PA_XBC // C
    return pl.pallas_call(
        functools.partial(_ssdconv_body, Sc=Sc),
        grid=(B, SSD_CONV_DIM // C),
        in_specs=[pl.BlockSpec((S, C), lambda b_, j: (b_, j0 + j)),
                  pl.BlockSpec((3, C), lambda b_, j: (0, j)),
                  pl.BlockSpec((1, C), lambda b_, j: (0, j))],
        out_specs=pl.BlockSpec((S, C), lambda b_, j: (b_, j)),
        out_shape=_sds((B * S, SSD_CONV_DIM), BF16),
        compiler_params=_cp(("parallel", "parallel"), 40),
        name="ssdconv",
    )(proj_a, w, b)


def _shortconv_body(sb_ref, sg_ref, sh_ref, w_ref, o_ref, *, Sc):
    x = sg_ref[...].astype(F32) * sh_ref[...].astype(F32)
    prev, nxt = _shifted(x, Sc)
    w = w_ref[...]
    y = w[0:1, :] * prev + w[1:2, :] * x + w[2:3, :] * nxt
    o_ref[...] = (sb_ref[...].astype(F32) * y).astype(o_ref.dtype)


def _shortconv(proj_a, w, B, S, Sc):
    C = SC_WIDTH
    return pl.pallas_call(
        functools.partial(_shortconv_body, Sc=Sc),
        grid=(B,),
        in_specs=[pl.BlockSpec((S, C), lambda b_: (b_, PA_SB // C)),
                  pl.BlockSpec((S, C), lambda b_: (b_, PA_SG // C)),
                  pl.BlockSpec((S, C), lambda b_: (b_, PA_SH // C)),
                  pl.BlockSpec((3, C), lambda b_: (0, 0))],
        out_specs=pl.BlockSpec((S, C), lambda b_: (b_, 0)),
        out_shape=_sds((B * S, C), BF16),
        compiler_params=_cp(("parallel",), 40),
        name="shortconv",
    )(proj_a, proj_a, proj_a, w)


def _norm_rope(x, g, cos, sin, bd, scale):
    W = x.shape[1]
    sq = x * x
    hi = sq.astype(BF16)
    lo = (sq - hi.astype(F32)).astype(BF16)
    ssq = jnp.dot(hi, bd, preferred_element_type=F32) + jnp.dot(lo, bd, preferred_element_type=F32)
    y = x * lax.rsqrt(ssq * (1.0 / HEAD_DIM) + RMS_EPS) * g
    lane = lax.broadcasted_iota(I32, (1, W), 1)
    first = (lane % 32) < 16
    partner = jnp.where(first, pltpu.roll(y, W - 16, 1), pltpu.roll(y, 16, 1))
    return (y * cos + partner * sin) * scale


def _attn_body(q_ref, k_ref, v_ref, cq_ref, sq_ref, ck_ref, sk_ref, gq_ref, gk_ref, bdq_ref, bdk_ref,
               o_ref, kh_ref, ve_ref, *, Sc, S, tq):
    qb = pl.program_id(1)
    hd = HEAD_DIM

    @pl.when(qb == 0)
    def _():
        k = k_ref[...].astype(F32)
        kh_ref[...] = _norm_rope(k, gk_ref[...], ck_ref[...], sk_ref[...], bdk_ref[...], 1.0).astype(BF16)
        v = v_ref[...].astype(F32)
        lane = lax.broadcasted_iota(I32, (1, 2 * hd), 1)
        ve_ref[0] = jnp.where(lane < hd, v, 1.0).astype(BF16)
        ve_ref[1] = jnp.where(lane < hd, pltpu.roll(v, hd, 1), 1.0).astype(BF16)

    q = q_ref[...].astype(F32)
    qh = _norm_rope(q, gq_ref[...], cq_ref[...], sq_ref[...], bdq_ref[...],
                    HEAD_DIM ** -0.5 * math.log2(math.e)).astype(BF16)
    rep = ATTN_HEADS // ATTN_KV_HEADS
    nt = (((1,), (1,)), ((), ()))

    def attend(splits):
        scores = []
        for g in range(ATTN_KV_HEADS):
            qg = jnp.concatenate([qh[:, (g * rep + r) * hd:(g * rep + r + 1) * hd] for r in range(rep)], axis=0)
            scores.append([lax.dot_general(qg, kh_ref[a:b, g * hd:(g + 1) * hd], nt, preferred_element_type=F32)
                           for a, b in splits])
        for g in range(ATTN_KV_HEADS):
            m = functools.reduce(jnp.maximum, [jnp.max(s, axis=-1, keepdims=True) for s in scores[g]])
            acc = None
            for s, (a, b) in zip(scores[g], splits):
                p = jnp.exp2(s - m).astype(BF16)
                part = jnp.dot(p, ve_ref[g, a:b, :], preferred_element_type=F32)
                acc = part if acc is None else acc + part
            o = acc[:, 0:hd] / acc[:, hd:hd + 1]
            for r in range(rep):
                h = g * rep + r
                o_ref[:, h * hd:(h + 1) * hd] = o[r * tq:(r + 1) * tq].astype(o_ref.dtype)

    @pl.when(qb < Sc // tq)
    def _():
        attend([(0, Sc)])

    @pl.when(qb >= Sc // tq)
    def _():
        half = (S // 2 + 255) // 256 * 256 if S >= 512 else S
        attend([(0, half), (half, S)] if half < S else [(0, S)])


def _rope_tables(T, Sc):
    rows = T // GRID_W
    row = np.repeat(np.arange(rows, dtype=np.float32), GRID_W)
    col = np.tile(np.arange(GRID_W, dtype=np.float32), rows)
    half = HEAD_DIM // 2
    inv = jnp.asarray(ROPE_THETA, F32) ** (-jnp.arange(0, half, 2, dtype=F32) / half)
    ra = jnp.asarray(row)[:, None] * inv
    ca = jnp.asarray(col)[:, None] * inv
    cos = jnp.concatenate([jnp.cos(ra), jnp.cos(ra), jnp.cos(ca), jnp.cos(ca)], axis=1)
    sin = jnp.concatenate([-jnp.sin(ra), jnp.sin(ra), -jnp.sin(ca), jnp.sin(ca)], axis=1)
    cos = jnp.concatenate([jnp.ones((Sc, HEAD_DIM), F32), cos], axis=0)
    sin = jnp.concatenate([jnp.zeros((Sc, HEAD_DIM), F32), sin], axis=0)
    return cos, sin


def _block_diag_ones(W):
    i = np.arange(W) // HEAD_DIM
    return jnp.asarray((i[:, None] == i[None, :]).astype(np.float32), BF16)


def _attention(proj_a, tabs, q_norm_g, k_norm_g, B, S, Sc):
    cos_q, sin_q, cos_k, sin_k, bdq, bdk = tabs
    tq = math.gcd(Sc, 256)
    nq = S // tq
    gq = jnp.tile(q_norm_g, ATTN_HEADS)[None, :]
    gk = jnp.tile(k_norm_g, ATTN_KV_HEADS)[None, :]
    const = lambda b_, i: (0, 0)
    return pl.pallas_call(
        functools.partial(_attn_body, Sc=Sc, S=S, tq=tq),
        grid=(B, nq),
        in_specs=[pl.BlockSpec((tq, ATTN_Q_DIM), lambda b_, i: (b_ * nq + i, PA_Q // ATTN_Q_DIM)),
                  pl.BlockSpec((S, ATTN_KV_DIM), lambda b_, i: (b_, PA_K // ATTN_KV_DIM)),
                  pl.BlockSpec((S, ATTN_KV_DIM), lambda b_, i: (b_, PA_V // ATTN_KV_DIM)),
                  pl.BlockSpec((tq, ATTN_Q_DIM), lambda b_, i: (i, 0)),
                  pl.BlockSpec((tq, ATTN_Q_DIM), lambda b_, i: (i, 0)),
                  pl.BlockSpec((S, ATTN_KV_DIM), const),
                  pl.BlockSpec((S, ATTN_KV_DIM), const),
                  pl.BlockSpec((1, ATTN_Q_DIM), const),
                  pl.BlockSpec((1, ATTN_KV_DIM), const),
                  pl.BlockSpec((ATTN_Q_DIM, ATTN_Q_DIM), const),
                  pl.BlockSpec((ATTN_KV_DIM, ATTN_KV_DIM), const)],
        out_specs=pl.BlockSpec((tq, ATTN_Q_DIM), lambda b_, i: (b_ * nq + i, 0)),
        out_shape=_sds((B * S, ATTN_Q_DIM), BF16),
        scratch_shapes=[pltpu.VMEM((S, ATTN_KV_DIM), BF16),
                        pltpu.VMEM((ATTN_KV_HEADS, S, 2 * HEAD_DIM), BF16)],
        compiler_params=_cp(("parallel", "arbitrary"), 56),
        name="attention",
    )(proj_a, proj_a, proj_a, cos_q, sin_q, cos_k, sin_k, gq, gk, bdq, bdk)


def _ssd_chains(chains):
    L = SSD_CHUNK
    P, Nst = SSD_HEAD_DIM, SSD_STATE
    rep = SSD_HEADS // SSD_GROUPS
    row = lax.broadcasted_iota(I32, (L, L), 0)
    col = lax.broadcasted_iota(I32, (L, L), 1)
    tri = (row >= col).astype(F32)
    lane = lax.broadcasted_iota(I32, (1, 2 * P), 1)
    lo_half = lane < P
    blockdiag = (row // Nst) == (col // P)
    upper = lax.broadcasted_iota(I32, (2 * Nst, 1), 0) < Nst
    st = []
    for xbc_ref, dt_ref, bias, a_neg, h_ref, y_ref, rev in chains:
        xbc = xbc_ref[...]
        bm = xbc[:, SSD_INNER:SSD_INNER + SSD_BC_DIM]
        cm = xbc[:, SSD_INNER + SSD_BC_DIM:SSD_CONV_DIM]
        dt = jax.nn.softplus(dt_ref[...] + bias)
        a = dt * a_neg
        cum = jnp.dot(tri, a, precision=HI, preferred_element_type=F32)
        e = cum - a if rev else cum
        tot_row = cum[L - 1:L, :]
        win_c = jnp.exp(tot_row - e) if rev else jnp.exp(e)
        win_hi = win_c.astype(BF16)
        cm32 = cm.astype(F32)
        st.append(dict(
            rev=rev, h_ref=h_ref, y_ref=y_ref, xbc=xbc, cm=cm, cm32=cm32, cm_rolled=pltpu.roll(cm32, P, 1),
            cum=cum, e=e, eT=e.T, dtT=dt.T, bT=bm.astype(F32).T.astype(BF16),
            mask=(col >= row) if rev else (row >= col),
            win_hi=win_hi, win_lo=(win_c - win_hi.astype(F32)).astype(BF16)))
    for g in range(SSD_GROUPS):
        own = (lane // P) == g
        for c in st:
            cg0 = jnp.where(own, c["cm"], jnp.zeros_like(c["cm"]))
            c["sg"] = jnp.dot(cg0, c["bT"], preferred_element_type=F32)
            c["cdup"] = jnp.where(own, c["cm32"], c["cm_rolled"])
            c["bTg"] = c["bT"][g * Nst:(g + 1) * Nst, :].astype(F32)
        for q in range(rep // 2):
            pq = (g * rep) // 2 + q
            hs = (2 * pq, 2 * pq + 1)
            sel = (lax.broadcasted_iota(I32, (2 * P, 2 * P), 0) == hs[0] + col // P).astype(BF16)
            for c in st:
                rev, e, eT, cum, dtT = c["rev"], c["e"], c["eT"], c["cum"], c["dtT"]
                ms, bws, tots = [], [], []
                for h in hs:
                    ecol = e[:, h:h + 1]
                    erow = eT[h:h + 1, :]
                    tot = cum[L - 1:L, h:h + 1]
                    diff = (erow - ecol) if rev else (ecol - erow)
                    dec = jnp.exp(jnp.where(c["mask"], diff, -1e30))
                    ms.append((c["sg"] * dec * dtT[h:h + 1, :]).astype(BF16))
                    wrow = (jnp.exp(erow) if rev else jnp.exp(tot - erow)) * dtT[h:h + 1, :]
                    bws.append(c["bTg"] * wrow)
                    tots.append(jnp.exp(tot))
                xp = c["xbc"][:, 2 * P * pq:2 * P * (pq + 1)]
                zero = jnp.zeros_like(xp)
                xbd = jnp.concatenate([jnp.where(lo_half, xp, zero), jnp.where(lo_half, zero, xp)], axis=0)
                y = jnp.dot(jnp.concatenate(ms, axis=1), xbd, preferred_element_type=F32)
                wexp = (jnp.dot(c["win_hi"], sel, preferred_element_type=F32)
                        + jnp.dot(c["win_lo"], sel, preferred_element_type=F32))
                hp = c["h_ref"][pq]
                y = y + jnp.dot((wexp * c["cdup"]).astype(BF16), hp.astype(BF16), preferred_element_type=F32)
                upd = jnp.dot(jnp.concatenate(bws, axis=0).astype(BF16), xp, preferred_element_type=F32)
                keep = jnp.where(upper, tots[0], tots[1])
                c["h_ref"][pq] = jnp.where(blockdiag, keep * hp + upd, 0.0)
                c["y_ref"][:, 2 * P * pq:2 * P * (pq + 1)] = y.astype(c["y_ref"].dtype)


def _ssd_body(xf_ref, dtf_ref, xb_ref, dtb_ref, bias_ref, alog_ref, yf_ref, yb_ref, *h_refs, nper):
    @pl.when(pl.program_id(1) == 0)
    def _():
        for h_ref in h_refs:
            h_ref[...] = jnp.zeros_like(h_ref)

    lane = lax.broadcasted_iota(I32, (1, 128), 1)
    a_neg = jnp.where(lane < SSD_HEADS, -jnp.exp(alog_ref[...]), 0.0)
    bias = bias_ref[...]
    chains = []
    for p in range(nper):
        chains.append((xf_ref.at[p], dtf_ref.at[p], bias[0:1, :], a_neg[0:1, :], h_refs[2 * p], yf_ref.at[p], False))
        chains.append((xb_ref.at[p], dtb_ref.at[p], bias[1:2, :], a_neg[1:2, :], h_refs[2 * p + 1], yb_ref.at[p], True))
    _ssd_chains(chains)


def _ssd(xbc_c, dt, dt_bias, a_log, B, S, Sc):
    L = SSD_CHUNK
    nc, ncc = S // L, Sc // L
    nper = math.gcd(B, 4)
    bg = B // nper
    padh = 128 - SSD_HEADS
    bias = jnp.pad(dt_bias, ((0, 0), (0, padh)))
    alog = jnp.pad(a_log, ((0, 0), (0, padh)))
    x3 = xbc_c.reshape(nper, bg * S, SSD_CONV_DIM)
    dt3 = dt.reshape(nper, bg * S, 128)

    def fwd(b_, i):
        return (0, b_ * nc + i, 0)

    def bwd(b_, i):
        return (0, b_ * nc + jnp.where(i < ncc, ncc - 1 - i, nc + ncc - 1 - i), 0)

    const = lambda b_, i: (0, 0)
    out = _sds((nper, bg * S, SSD_INNER), BF16)
    yf, yb = pl.pallas_call(
        functools.partial(_ssd_body, nper=nper),
        grid=(bg, nc),
        in_specs=[pl.BlockSpec((nper, L, SSD_CONV_DIM), fwd), pl.BlockSpec((nper, L, 128), fwd),
                  pl.BlockSpec((nper, L, SSD_CONV_DIM), bwd), pl.BlockSpec((nper, L, 128), bwd),
                  pl.BlockSpec((2, 128), const), pl.BlockSpec((2, 128), const)],
        out_specs=[pl.BlockSpec((nper, L, SSD_INNER), fwd), pl.BlockSpec((nper, L, SSD_INNER), bwd)],
        out_shape=[out, out],
        scratch_shapes=[pltpu.VMEM((SSD_HEADS // 2, 2 * SSD_STATE, 2 * SSD_HEAD_DIM), F32)] * (2 * nper),
        compiler_params=_cp(("parallel", "arbitrary")),
        name="ssd",
    )(x3, dt3, x3, dt3, bias, alog)
    return yf.reshape(B * S, SSD_INNER), yb.reshape(B * S, SSD_INNER)


def _cpow(n, lr, li, st):
    mag = jnp.exp(n * lr * st)
    ang = n * li * st
    return mag * jnp.cos(ang), mag * jnp.sin(ang)


def _zoh_coef(lr, li, st):
    ar, ai = _cpow(1.0, lr, li, st)
    nr, ni = ar - 1.0, ai
    den = lr * lr + li * li
    return (nr * lr + ni * li) / den, (ni * lr - nr * li) / den


def _spread_lanes(x, sel):
    hi = x.astype(BF16)
    r1 = x - hi.astype(F32)
    mid = r1.astype(BF16)
    lo = (r1 - mid.astype(F32)).astype(BF16)
    return (jnp.dot(hi, sel, preferred_element_type=F32) + jnp.dot(mid, sel, preferred_element_type=F32)
            + jnp.dot(lo, sel, preferred_element_type=F32))


def _s5gen_body(colp_ref, rowp_ref, ctr_ref, cti_ref, btr_ref, bti_ref, blr_ref, bli_ref, d_ref,
                wt_ref, so_ref, ar_ref, ws_ref, sos_ref):
    Lc, G, P = S5_CHUNK, S5_GROUP_DIM, S5_STATE
    NS = 2 * Lc + 1
    colp = colp_ref[...]
    rowp = rowp_ref[...]
    lrf, lif, stf = colp[:, 0:1], colp[:, 1:2], jnp.exp(colp[:, 2:3])
    lrb, lib, stb = colp[:, 3:4], colp[:, 4:5], jnp.exp(colp[:, 5:6])
    lane = lax.broadcasted_iota(I32, (1, NS * G), 1)
    slot = lane // G
    isb = slot < Lc
    cslot = lax.broadcasted_iota(I32, (1, 128), 1)
    cisb = cslot < Lc
    clag = jnp.where(cslot < NS, jnp.abs(cslot - Lc), 0).astype(F32)
    pr_c, pi_c = _cpow(clag, jnp.where(cisb, lrb, lrf), jnp.where(cisb, lib, lif), jnp.where(cisb, stb, stf))
    srow = lax.broadcasted_iota(I32, (128, 1), 0)
    spread = (srow == slot).astype(BF16)
    pr, pi = _spread_lanes(pr_c, spread), _spread_lanes(pi_c, spread)
    tile = (lax.broadcasted_iota(I32, (G, 1), 0) == lane % G).astype(BF16)
    ctr, cti = _spread_lanes(ctr_ref[...], tile), _spread_lanes(cti_ref[...], tile)
    er = ctr * pr - cti * pi
    ei = ctr * pi + cti * pr
    rlrf, rlif, rstf = rowp[0:1, :], rowp[1:2, :], jnp.exp(rowp[2:3, :])
    rlrb, rlib, rstb = rowp[3:4, :], rowp[4:5, :], jnp.exp(rowp[5:6, :])
    btr, bti = btr_ref[...], bti_ref[...]
    cfr, cfi = _zoh_coef(rlrf, rlif, rstf)
    cbr, cbi = _zoh_coef(rlrb, rlib, rstb)
    bbf_r, bbf_i = cfr * btr - cfi * bti, cfr * bti + cfi * btr
    bbb_r, bbb_i = cbr * btr - cbi * bti, cbr * bti + cbi * btr

    def kt(br, bi):
        return (jnp.dot(br, er, precision=HI, preferred_element_type=F32)
                - jnp.dot(bi, ei, precision=HI, preferred_element_type=F32))

    ktf, ktb = kt(bbf_r, bbf_i), kt(bbb_r, bbb_i)
    ii = lax.broadcasted_iota(I32, (G, NS * G), 0)
    dmat = jnp.where((slot == Lc) & (ii == lane - Lc * G), _spread_lanes(d_ref[...], tile), 0.0)
    strip = jnp.where(slot == Lc, ktf + ktb, jnp.where(isb, ktb, ktf)) + dmat
    for s in range(Lc):
        off = (Lc - s) * G
        ws_ref[s * G:(s + 1) * G, :] = strip[:, off:off + Lc * G]
    fo = (Lc + 1) * G
    ws_ref[Lc * G:Lc * G + P, :] = er[:, fo:fo + Lc * G]
    ws_ref[Lc * G + P:Lc * G + 2 * P, :] = er[:, 0:Lc * G]
    ws_ref[Lc * G + 2 * P:Lc * G + 3 * P, :] = -ei[:, fo:fo + Lc * G]
    ws_ref[Lc * G + 3 * P:Lc * G + 4 * P, :] = -ei[:, 0:Lc * G]
    wt_ref[:, 0:Lc * G] = ws_ref[0:Lc * G, :].T.astype(wt_ref.dtype)
    wt_ref[:, Lc * G:Lc * G + 4 * P] = ws_ref[Lc * G:Lc * G + 4 * P, :].T.astype(wt_ref.dtype)
    s_idx = lax.broadcasted_iota(I32, (1, Lc * G), 1) // G
    sel_f = (srow == 2 * Lc - 1 - s_idx).astype(BF16)
    sel_b = (srow == Lc - s_idx).astype(BF16)
    qfr, qfi = _spread_lanes(pr_c, sel_f), _spread_lanes(pi_c, sel_f)
    qbr, qbi = _spread_lanes(pr_c, sel_b), _spread_lanes(pi_c, sel_b)
    ccfr, ccfi = _zoh_coef(lrf, lif, stf)
    ccbr, ccbi = _zoh_coef(lrb, lib, stb)
    blr, bli = _spread_lanes(blr_ref[...], tile[:, 0:Lc * G]), _spread_lanes(bli_ref[...], tile[:, 0:Lc * G])
    bfr, bfi = ccfr * blr - ccfi * bli, ccfr * bli + ccfi * blr
    bbr, bbi = ccbr * blr - ccbi * bli, ccbr * bli + ccbi * blr
    sos_ref[0:P, :] = qfr * bfr - qfi * bfi
    sos_ref[P:2 * P, :] = qbr * bbr - qbi * bbi
    sos_ref[2 * P:3 * P, :] = qfr * bfi + qfi * bfr
    sos_ref[3 * P:4 * P, :] = qbr * bbi + qbi * bbr
    so_ref[...] = sos_ref[...].T.astype(so_ref.dtype)
    afr, afi = _cpow(float(Lc), rlrf, rlif, rstf)
    abr, abi = _cpow(float(Lc), rlrb, rlib, rstb)
    ar_ref[...] = jnp.zeros_like(ar_ref)
    ar_ref[0:1, 0:P] = afr
    ar_ref[0:1, P:2 * P] = abr
    ar_ref[1:2, 0:P] = afi
    ar_ref[1:2, P:2 * P] = abi


def _s5gen(lam_re, lam_im, log_step, b_re, b_im, c_re, c_im, d_skip):
    Gn, P, G, Lc = S5_GROUPS, S5_STATE, S5_GROUP_DIM, S5_CHUNK
    NS = 2 * Lc + 1
    ls = jnp.broadcast_to(log_step[:, :, None], (2, Gn, P))
    z = jnp.zeros((Gn, P), F32)
    rowp = jnp.stack([lam_re[0], lam_im[0], ls[0], lam_re[1], lam_im[1], ls[1], z, z], axis=1)
    colp = jnp.swapaxes(rowp, 1, 2)
    ctr = jnp.swapaxes(c_re, 1, 2)
    cti = jnp.swapaxes(c_im, 1, 2)
    btr = jnp.swapaxes(b_re, 1, 2)
    bti = jnp.swapaxes(b_im, 1, 2)
    blr, bli = b_re, b_im
    dt = d_skip.reshape(Gn, 1, G)
    g3 = lambda a, b: pl.BlockSpec((None, a, b), lambda g: (g, 0, 0))
    return pl.pallas_call(
        _s5gen_body,
        grid=(Gn,),
        in_specs=[g3(P, 8), g3(8, P), g3(P, G), g3(P, G), g3(G, P), g3(G, P), g3(P, G), g3(P, G), g3(1, G)],
        out_specs=[g3(Lc * G, Lc * G + 4 * P), g3(Lc * G, 4 * P), g3(8, 2 * P)],
        out_shape=[_sds((Gn, Lc * G, Lc * G + 4 * P), BF16), _sds((Gn, Lc * G, 4 * P), BF16),
                   _sds((Gn, 8, 2 * P), F32)],
        scratch_shapes=[pltpu.VMEM((Lc * G + 4 * P, Lc * G), F32), pltpu.VMEM((4 * P, Lc * G), F32)],
        compiler_params=_cp(("parallel",)),
        name="s5gen",
    )(colp, rowp, ctr, cti, btr, bti, blr, bli, dt)


def _s5_body(xt_ref, wt_ref, so_ref, ar_ref, y_ref, *scratch, B, nc, ncc, ng):
    P, Lc, G = S5_STATE, S5_CHUNK, S5_GROUP_DIM
    LG = Lc * G
    R = nc * B
    v_refs, h_refs = scratch[:ng], scratch[ng:]
    ucols = []
    for k in range(ng):
        ucol = jnp.concatenate([xt_ref[k * G:(k + 1) * G, l * R:(l + 1) * R] for l in range(Lc)], axis=0)
        v_refs[k][...] = lax.dot_general(ucol, so_ref[k], (((0,), (0,)), ((), ())), preferred_element_type=F32)
        ucols.append(ucol)
    lane = lax.broadcasted_iota(I32, (1, 2 * P), 1)
    isf = lane < P

    def step(i, carry):
        cf = pl.multiple_of(i * B, B)
        cb = pl.multiple_of(jnp.where(i < ncc, ncc - 1 - i, nc + ncc - 1 - i) * B, B)
        out = []
        for k in range(ng):
            hr, hi = carry[k]
            v_ref, h_ref = v_refs[k], h_refs[k]
            a_re, a_im = ar_ref[k, 0:1, :], ar_ref[k, 1:2, :]
            h_ref[pl.ds(cf, B), 0:P] = hr[:, 0:P]
            h_ref[pl.ds(cf, B), 2 * P:3 * P] = hi[:, 0:P]
            h_ref[pl.ds(cb, B), P:2 * P] = hr[:, P:2 * P]
            h_ref[pl.ds(cb, B), 3 * P:4 * P] = hi[:, P:2 * P]
            vr = jnp.where(isf, v_ref[pl.ds(cf, B), 0:2 * P], v_ref[pl.ds(cb, B), 0:2 * P])
            vi = jnp.where(isf, v_ref[pl.ds(cf, B), 2 * P:4 * P], v_ref[pl.ds(cb, B), 2 * P:4 * P])
            out.append((hr * a_re - hi * a_im + vr, hr * a_im + hi * a_re + vi))
        return tuple(out)

    z = jnp.zeros((B, 2 * P), F32)
    lax.fori_loop(0, nc, step, tuple((z, z) for _ in range(ng)))
    for k in range(ng):
        yt = jnp.dot(wt_ref[k, :, 0:LG], ucols[k], preferred_element_type=F32)
        yt = yt + lax.dot_general(wt_ref[k, :, LG:LG + 4 * P], h_refs[k][...].astype(BF16),
                                  (((1,), (1,)), ((), ())), preferred_element_type=F32)
        for t in range(Lc):
            y_ref[k * G:(k + 1) * G, t * R:(t + 1) * R] = yt[t * G:(t + 1) * G, :].astype(y_ref.dtype)


def _s5(xt, wt, so, ar, B, nc, ncc):
    P, Lc, G, Gn = S5_STATE, S5_CHUNK, S5_GROUP_DIM, S5_GROUPS
    N = xt.shape[1]
    R = nc * B
    ng = 2
    g3 = lambda a, b: pl.BlockSpec((ng, a, b), lambda g: (g, 0, 0))
    return pl.pallas_call(
        functools.partial(_s5_body, B=B, nc=nc, ncc=ncc, ng=ng),
        grid=(Gn // ng,),
        in_specs=[pl.BlockSpec((ng * G, N), lambda g: (g, 0)), g3(Lc * G, Lc * G + 4 * P), g3(Lc * G, 4 * P),
                  g3(8, 2 * P)],
        out_specs=pl.BlockSpec((ng * G, N), lambda g: (g, 0)),
        out_shape=_sds((Gn * G, N), BF16),
        scratch_shapes=[pltpu.VMEM((R, 4 * P), F32)] * (2 * ng),
        compiler_params=_cp(("parallel",)),
        name="s5",
    )(xt, wt, so, ar)


def _t2d_body(x_ref, o_ref):
    o_ref[...] = x_ref[...].astype(F32).T.astype(o_ref.dtype)


def _transpose2d(x, tr, name):
    M, C = x.shape
    return pl.pallas_call(
        _t2d_body,
        grid=(M // tr,),
        in_specs=[pl.BlockSpec((tr, C), lambda i: (i, 0))],
        out_specs=pl.BlockSpec((C, tr), lambda i: (0, i)),
        out_shape=_sds((C, M), x.dtype),
        compiler_params=_cp(("parallel",)),
        name=name,
    )(x)


def _untranspose2d(xt, tr, name):
    C, M = xt.shape
    return pl.pallas_call(
        _t2d_body,
        grid=(M // tr,),
        in_specs=[pl.BlockSpec((C, tr), lambda i: (0, i))],
        out_specs=pl.BlockSpec((tr, C), lambda i: (i, 0)),
        out_shape=_sds((M, C), xt.dtype),
        compiler_params=_cp(("parallel",)),
        name=name,
    )(xt)


def _merge_body(x_ref, ya_ref, yf_ref, yb_ref, xc_ref, z_ref, y5_ref, yd_ref, g_ref, gate_ref, sh_ref, sc_ref,
                wa_ref, wb_ref, wc_ref, wd_ref, wo_ref, wglu_ref, bglu_ref, ng_ref, dsk_ref, fg_ref, wrh_ref, wrl_ref,
                o_ref, hm_ref, lg_ref):
    tm, D = o_ref.shape
    nparts = 2 if tm % 32 == 0 else 1
    parts = [slice(p * tm // nparts, (p + 1) * tm // nparts) for p in range(nparts)]
    ys = [yf_ref[r, :].astype(F32) + yb_ref[r, :].astype(F32) + dsk_ref[...] * xc_ref[r, :].astype(F32) for r in parts]
    ys = [y * _silu(z_ref[r, :].astype(F32)) for y, r in zip(ys, parts)]
    ys = [(y * lax.rsqrt(jnp.mean(y * y, axis=-1, keepdims=True) + RMS_EPS) * ng_ref[...]).astype(BF16) for y in ys]
    cs = [jax.nn.gelu(y5_ref[r, :].astype(F32)).astype(BF16) for r in parts]
    glus = [_sigmoid(jnp.dot(c, wglu_ref[...], preferred_element_type=F32) + bglu_ref[...]) for c in cs]
    cs = [(c.astype(F32) * glu).astype(BF16) for c, glu in zip(cs, glus)]

    def gate(r, k):
        return 0.5 * jnp.tanh(g_ref[r, k * D:(k + 1) * D].astype(F32)) + 0.5

    ms = [gate(r, 0) * jnp.dot(ya_ref[r, :], wa_ref[...], preferred_element_type=F32) for r in parts]
    ms = [m + gate(r, 1) * jnp.dot(y, wb_ref[...], preferred_element_type=F32) for m, y, r in zip(ms, ys, parts)]
    ms = [m + gate(r, 2) * jnp.dot(c, wc_ref[...], preferred_element_type=F32) for m, c, r in zip(ms, cs, parts)]
    ms = [m + gate(r, 3) * jnp.dot(yd_ref[r, :], wd_ref[...], preferred_element_type=F32) for m, r in zip(ms, parts)]
    outs = [jnp.dot(m.astype(BF16), wo_ref[...], preferred_element_type=F32) for m in ms]
    xs = [x_ref[r, :] + gate_ref[...] * out for out, r in zip(outs, parts)]
    scale = fg_ref[...] * (1.0 + sc_ref[...])
    hs = [x * lax.rsqrt(jnp.mean(x * x, axis=-1, keepdims=True) + RMS_EPS) * scale + sh_ref[...] for x in xs]
    his = [h.astype(BF16) for h in hs]
    los = [(h - hi.astype(F32)).astype(BF16) for h, hi in zip(hs, his)]
    lgs = [jnp.dot(hi, wrh_ref[...], preferred_element_type=F32) + jnp.dot(lo, wrh_ref[...], preferred_element_type=F32)
           + jnp.dot(hi, wrl_ref[...], preferred_element_type=F32) for hi, lo in zip(his, los)]
    for x, hi, lg, r in zip(xs, his, lgs, parts):
        o_ref[r, :] = x
        hm_ref[r, :] = hi
        lg_ref[r, :] = lg


def _merge(xa, ya, yf, yb, xbc_c, proj, y5, yd, mods3, wa, wb, wc, wd, wo, wglu, bglu, ng, dsk, fg, wr, B, rc):
    N, D = xa.shape
    tm = rc[0]
    rows = lambda w, j=0: pl.BlockSpec((tm, w), lambda i: (i, j))
    full = lambda a: pl.BlockSpec(a.shape, lambda i: (0,) * a.ndim)
    wr_hi = wr.astype(BF16)
    wr_lo = (wr - wr_hi.astype(F32)).astype(BF16)
    return pl.pallas_call(
        _merge_body,
        grid=(N // tm,),
        in_specs=[rows(D), rows(ATTN_Q_DIM), rows(SSD_INNER), rows(SSD_INNER), rows(SSD_INNER),
                  rows(SSD_INNER, PA_Z // SSD_INNER), rows(S5_WIDTH), rows(SC_WIDTH), rows(4 * D, PA_G // (4 * D)),
                  _mod_spec(D, rc, B, 2), _mod_spec(D, rc, B, 3), _mod_spec(D, rc, B, 4),
                  full(wa), full(wb), full(wc), full(wd), full(wo), full(wglu), full(bglu), full(ng), full(dsk),
                  full(fg), full(wr_hi), full(wr_lo)],
        out_specs=[rows(D), rows(D), rows(128)],
        out_shape=[_sds((N, D), F32), _sds((N, D), BF16), _sds((N, 128), F32)],
        compiler_params=_cp(("parallel",), 48),
        name="merge",
    )(xa, ya, yf, yb, xbc_c, proj, y5, yd, proj, mods3, mods3, mods3, wa, wb, wc, wd, wo, wglu, bglu, ng, dsk,
      fg, wr_hi, wr_lo)


def _prefix_excl(mask_f):
    R, T = mask_f.shape
    r = lax.broadcasted_iota(I32, (128, 128), 0)
    c = lax.broadcasted_iota(I32, (128, 128), 1)
    upper = (r <= c).astype(BF16)
    outs = []
    off = jnp.zeros((R, 1), F32)
    for k in range(T // 128):
        blk = mask_f[:, k * 128:(k + 1) * 128]
        inc = jnp.dot(blk.astype(BF16), upper, preferred_element_type=F32)
        outs.append(inc - blk + off)
        off = off + inc[:, 127:128]
    return jnp.concatenate(outs, axis=1)


def _topk_slots(affs, caps):
    E = affs[0].shape[0]

    def step(i, ths):
        out = []
        for a, cap, th in zip(affs, caps, ths):
            cand = th | (jnp.int32(1) << (30 - i))
            cnt = jnp.sum((a >= pltpu.bitcast(cand, F32)).astype(I32), axis=1, keepdims=True)
            out.append(jnp.where(cnt >= cap, cand, th))
        return tuple(out)

    ths = lax.fori_loop(0, 31, step, tuple(jnp.zeros((E, 1), I32) for _ in affs))
    slots = []
    for a, cap, th in zip(affs, caps, ths):
        above = a >= pltpu.bitcast(jnp.maximum(th + 1, 0x00800000), F32)
        tie = (a >= pltpu.bitcast(th, F32)) & jnp.logical_not(above)
        n_above = jnp.sum(above.astype(F32), axis=1, keepdims=True)
        sel = above | (tie & (n_above + _prefix_excl(tie.astype(F32)) < cap))
        slots.append(jnp.where(sel, _prefix_excl(sel.astype(F32)), -1.0))
    return slots


def _router_body(lg_ref, slot_ref, gate_ref, rng_ref, *, Sc, cap_l, cap_c, ctx_out, tb):
    S = lg_ref.shape[0]
    lt = lg_ref[...].T[0:N_EXPERTS, :]
    mx = jnp.max(lt, axis=0, keepdims=True)
    ex = jnp.exp(lt - mx)
    aff = ex / jnp.sum(ex, axis=0, keepdims=True)
    gate_ref[...] = aff
    if ctx_out:
        slot_l, slot_c = _topk_slots([aff[:, Sc:], aff[:, 0:Sc]], [cap_l, cap_c])
        slot_c = jnp.where(slot_c >= 0.0, slot_c + cap_l, -1.0)
    else:
        slot_l, = _topk_slots([aff[:, Sc:]], [cap_l])
        slot_c = jnp.full((N_EXPERTS, Sc), -1.0, F32)
    slot_ref[:, 0:Sc] = slot_c
    slot_ref[:, Sc:] = slot_l
    lane = lax.broadcasted_iota(I32, (1, 128), 1)
    lo_a = jnp.zeros((N_EXPERTS, 128), F32)
    nwin = jnp.zeros((N_EXPERTS, 1), F32)
    for seg0, seg1, base, slots in ((0, Sc, float(cap_l), slot_c), (Sc, S, 0.0, slot_l)):
        lo = jnp.full((N_EXPERTS, 1), base, F32)
        for k in range(seg0 // tb, seg1 // tb):
            blk = slots[:, k * tb - seg0:(k + 1) * tb - seg0]
            hi = lo + jnp.sum((blk >= 0.0).astype(F32), axis=1, keepdims=True)
            lo_al = jnp.floor(lo * (1.0 / SLOT_ALIGN)) * SLOT_ALIGN
            need = jnp.where(hi > lo, jnp.floor((hi - lo_al + (SLOT_WINDOW - 1)) * (1.0 / SLOT_WINDOW)), 0.0)
            nwin = jnp.maximum(nwin, need)
            lo_a = jnp.where(lane == k, lo, lo_a)
            lo = hi
    rng_ref[...] = jnp.where(lane == 127, jnp.max(nwin, axis=0, keepdims=True), lo_a).astype(I32)


def _token_block(S, Sc):
    return math.gcd(math.gcd(Sc, S - Sc), 256)


def _router(logits, B, S, Sc, ctx_out):
    T = S - Sc
    cap_l = EC_CAPACITY * T // N_EXPERTS
    cap_c = EC_CAPACITY * Sc // N_EXPERTS
    tb = _token_block(S, Sc)
    es = pl.BlockSpec((None, N_EXPERTS, S), lambda b_: (b_, 0, 0))
    return pl.pallas_call(
        functools.partial(_router_body, Sc=Sc, cap_l=cap_l, cap_c=cap_c, ctx_out=ctx_out, tb=tb),
        grid=(B,),
        in_specs=[pl.BlockSpec((S, 128), lambda b_: (b_, 0))],
        out_specs=[es, es, pl.BlockSpec((None, N_EXPERTS, 128), lambda b_: (b_, 0, 0))],
        out_shape=[_sds((B, N_EXPERTS, S), F32), _sds((B, N_EXPERTS, S), F32), _sds((B, N_EXPERTS, 128), I32)],
        compiler_params=_cp(("parallel",)),
        name="router",
    )(logits)


def _window_rows(rng_ref, k, j, ncap):
    rows = []
    for e in range(N_EXPERTS):
        lo = rng_ref[e, k]
        lo_al = lax.shift_left(lax.shift_right_logical(lo, SLOT_ALIGN.bit_length() - 1), SLOT_ALIGN.bit_length() - 1)
        rows.append(pl.multiple_of(jnp.minimum(lo_al + j * SLOT_WINDOW, ncap), SLOT_ALIGN))
    return rows


def _window_hits(rows, slot_ref, cols):
    wrow = lax.broadcasted_iota(I32, (SLOT_WINDOW, 1), 0)
    return [(wrow + rows[e]).astype(F32) == slot_ref[e:e + 1, cols] for e in range(N_EXPERTS)]


def _moe_gather_body(rng_ref, hm_ref, slot_ref, gate_ref, xg_ref, gv_ref, *, ncap, tb, nsub):
    kk = pl.program_id(1)
    W = SLOT_WINDOW

    @pl.when(kk == 0)
    def _():
        xg_ref[...] = jnp.zeros_like(xg_ref)
        gv_ref[...] = jnp.zeros_like(gv_ref)

    for sub in range(nsub):
        k = kk * nsub + sub
        cols = slice(sub * tb, (sub + 1) * tb)

        def window(j, carry, k=k, cols=cols):
            rows = _window_rows(rng_ref, k, j, ncap)
            hits = _window_hits(rows, slot_ref, cols)
            hit_all = jnp.concatenate([h.astype(BF16) for h in hits], axis=0)
            xgw = jnp.dot(hit_all, hm_ref[cols, :], preferred_element_type=F32).astype(BF16)
            for e in range(N_EXPERTS):
                r = pl.ds(rows[e], W)
                xg_ref[e, r, :] += xgw[e * W:(e + 1) * W, :]
                gv_ref[e, r, :] += jnp.sum(jnp.where(hits[e], gate_ref[e:e + 1, cols], 0.0), axis=1, keepdims=True)
            return carry

        lax.fori_loop(0, rng_ref[0, 127], window, 0)


def _moe_ffn_body(xg_ref, gv_ref, wg_ref, wu_ref, wd_ref, yw_ref, wgs_ref, wus_ref, wds_ref, *, ncap):
    @pl.when(pl.program_id(1) == 0)
    def _():
        wgs_ref[...] = wg_ref[...].astype(BF16)
        wus_ref[...] = wu_ref[...].astype(BF16)
        wds_ref[...] = wd_ref[...].astype(BF16)

    nbp = xg_ref.shape[0]
    xg = jnp.concatenate([xg_ref[p, 0:ncap, :] for p in range(nbp)], axis=0)
    gv = jnp.concatenate([gv_ref[p, 0:ncap, :] for p in range(nbp)], axis=0)
    hid = _silu(jnp.dot(xg, wgs_ref[...], preferred_element_type=F32)) * jnp.dot(
        xg, wus_ref[...], preferred_element_type=F32)
    ye = (jnp.dot(hid.astype(BF16), wds_ref[...], preferred_element_type=F32) * gv).astype(yw_ref.dtype)
    for p in range(nbp):
        yw_ref[p, 0:ncap, :] = ye[p * ncap:(p + 1) * ncap, :]
        yw_ref[p, ncap:, :] = jnp.zeros((yw_ref.shape[1] - ncap, yw_ref.shape[2]), yw_ref.dtype)


def _moe_scatter_body(rng_ref, slot_ref, yw_ref, x_ref, gl_ref, gc_ref, fg_ref, o_ref,
                      *, ncap, nbc, final, tb, nsub):
    kk = pl.program_id(1)
    W = SLOT_WINDOW
    D = o_ref.shape[1]

    for sub in range(nsub):
        k = kk * nsub + sub
        cols = slice(sub * tb, (sub + 1) * tb)

        def window(j, acc, k=k, cols=cols):
            rows = _window_rows(rng_ref, k, j, ncap)
            hit_all = jnp.concatenate([h.astype(BF16) for h in _window_hits(rows, slot_ref, cols)], axis=0)
            yw = jnp.concatenate([yw_ref[e, pl.ds(rows[e], W), :] for e in range(N_EXPERTS)], axis=0)
            return acc + lax.dot_general(hit_all, yw, (((0,), (0,)), ((), ())), preferred_element_type=F32)

        def emit(k=k, cols=cols, window=window):
            acc = lax.fori_loop(0, rng_ref[0, 127], window, jnp.zeros((tb, D), F32))
            x = x_ref[cols, :] + jnp.where(k < nbc, gc_ref[...], gl_ref[...]) * acc
            if final:
                x = x * lax.rsqrt(jnp.mean(x * x, axis=-1, keepdims=True) + RMS_EPS) * fg_ref[...]
            o_ref[cols, :] = x

        if final:
            pl.when(k >= nbc)(emit)
        else:
            emit()


def _experts(hm, slot, gate, rng, wg, wu, wd, li, xres, mods3, B, S, Sc, ctx_out, final_g):
    N, D = hm.shape
    _, E, _, Fd = wg.shape
    T = S - Sc
    tb = _token_block(S, Sc)
    nblk, nbc = S // tb, Sc // tb
    ncap = EC_CAPACITY * T // N_EXPERTS + (EC_CAPACITY * Sc // N_EXPERTS if ctx_out else 0)
    assert ncap % SLOT_ALIGN == 0
    rows = ncap + SLOT_WINDOW
    smem = pl.BlockSpec((None, E, 128), lambda b_, k: (b_, 0, 0), memory_space=pltpu.SMEM)
    nsub = 3 if nblk % 3 == 0 else 1
    es = pl.BlockSpec((None, E, nsub * tb), lambda b_, k: (b_, 0, k))
    xg, gv = pl.pallas_call(
        functools.partial(_moe_gather_body, ncap=ncap, tb=tb, nsub=nsub),
        grid=(B, nblk // nsub),
        in_specs=[smem, pl.BlockSpec((nsub * tb, D), lambda b_, k: (b_ * (nblk // nsub) + k, 0)), es, es],
        out_specs=[pl.BlockSpec((None, E, rows, D), lambda b_, k: (b_, 0, 0, 0)),
                   pl.BlockSpec((None, E, rows, 1), lambda b_, k: (b_, 0, 0, 0))],
        out_shape=[_sds((B, E, rows, D), BF16), _sds((B, E, rows, 1), F32)],
        compiler_params=_cp(("parallel", "arbitrary"), 48),
        name="moe_gather",
    )(rng, hm, slot, gate)
    wspec = lambda a, b: pl.BlockSpec((None, None, a, b), lambda e, b_: (li, e, 0, 0))
    nbp = math.gcd(B, 2)
    yw = pl.pallas_call(
        functools.partial(_moe_ffn_body, ncap=ncap),
        grid=(E, B // nbp),
        in_specs=[pl.BlockSpec((nbp, None, rows, D), lambda e, b_: (b_, e, 0, 0)),
                  pl.BlockSpec((nbp, None, rows, 1), lambda e, b_: (b_, e, 0, 0)),
                  wspec(D, Fd), wspec(D, Fd), wspec(Fd, D)],
        out_specs=pl.BlockSpec((nbp, None, rows, D), lambda e, b_: (b_, e, 0, 0)),
        out_shape=_sds((B, E, rows, D), BF16),
        scratch_shapes=[pltpu.VMEM((D, Fd), BF16), pltpu.VMEM((D, Fd), BF16), pltpu.VMEM((Fd, D), BF16)],
        compiler_params=_cp(("parallel", "arbitrary"), 48),
        name="moe_ffn",
    )(xg, gv, wg, wu, wd)
    final = final_g is not None
    nl = nblk - nbc
    if final:
        nsub = 1
        out_spec = pl.BlockSpec((tb, D), lambda b_, k: (b_ * nl + jnp.maximum(k - nbc, 0), 0))
        out_shape = _sds((B * nl * tb, D), F32)
    else:
        out_spec = pl.BlockSpec((nsub * tb, D), lambda b_, k: (b_ * (nblk // nsub) + k, 0))
        out_shape = _sds((N, D), F32)
        final_g = jnp.ones((1, D), F32)
    nstep = nblk // nsub
    es = pl.BlockSpec((None, E, nsub * tb), lambda b_, k: (b_, 0, k))
    return pl.pallas_call(
        functools.partial(_moe_scatter_body, ncap=ncap, nbc=nbc, final=final, tb=tb, nsub=nsub),
        grid=(B, nstep),
        in_specs=[smem, es, pl.BlockSpec((None, E, rows, D), lambda b_, k: (b_, 0, 0, 0)),
                  pl.BlockSpec((nsub * tb, D), lambda b_, k: (b_ * nstep + k, 0)),
                  pl.BlockSpec((None, 1, D), lambda b_, k: (b_, 0, 5)),
                  pl.BlockSpec((None, 1, D), lambda b_, k: (B, 0, 5)),
                  pl.BlockSpec((1, D), lambda b_, k: (0, 0))],
        out_specs=out_spec,
        out_shape=out_shape,
        compiler_params=_cp(("parallel", "arbitrary"), 48),
        name="moe_scatter",
    )(rng, slot, yw, xres, mods3, mods3, final_g)


def _pack_w_in(w_in):
    D = w_in.shape[0]
    seg = lambda a, n: w_in[:, a:a + n]
    o_q, o_k, o_v, o_z = 0, ATTN_Q_DIM, ATTN_Q_DIM + ATTN_KV_DIM, ATTN_Q_DIM + 2 * ATTN_KV_DIM
    o_xbc = o_z + SSD_INNER
    o_dt = o_xbc + SSD_CONV_DIM
    o_u = o_dt + SSD_HEADS
    o_sb, o_sg, o_sh = o_u + S5_WIDTH, o_u + S5_WIDTH + SC_WIDTH, o_u + S5_WIDTH + 2 * SC_WIDTH
    o_g = o_sh + SC_WIDTH
    w_p = jnp.concatenate([0.5 * seg(o_g, 4 * D), seg(o_q, ATTN_Q_DIM), seg(o_z, SSD_INNER), seg(o_k, ATTN_KV_DIM),
                           seg(o_v, ATTN_KV_DIM), seg(o_u, S5_WIDTH), seg(o_sb, SC_WIDTH), seg(o_sg, SC_WIDTH),
                           seg(o_sh, SC_WIDTH), seg(o_xbc, SSD_CONV_DIM)], axis=1).astype(BF16)
    w_dt = jnp.pad(seg(o_dt, SSD_HEADS), ((0, 0), (0, 128 - SSD_HEADS))).astype(BF16)
    return w_p, w_dt


def _layer(xa, mods3, lp, ew, li, tabs, B, S, Sc, ctx_out, final_g):
    N, D = xa.shape
    rc = _row_cfg(S, Sc)
    w_p, w_dt = _pack_w_in(lp["w_in"])
    proj_a, dt = _in_proj(xa, lp["norm_mix_g"][None, :], mods3, w_p, w_dt, B, S, Sc)

    ya = _attention(proj_a, tabs, lp["q_norm_g"], lp["k_norm_g"], B, S, Sc)

    xbc_c = _ssdconv(proj_a, lp["ssd_conv_w"], lp["ssd_conv_b"][None, :], B, S, Sc)
    yf, yb = _ssd(xbc_c, dt, lp["ssd_dt_bias"], lp["ssd_a_log"], B, S, Sc)

    Lc, Gn, G = S5_CHUNK, S5_GROUPS, S5_GROUP_DIM
    nc5, ncc5 = S // Lc, Sc // Lc
    w5, so5, ar5 = _s5gen(lp["s5_lambda_re"], lp["s5_lambda_im"], lp["s5_log_step"], lp["s5_b_re"], lp["s5_b_im"],
                          lp["s5_c_re"], lp["s5_c_im"], lp["s5_d"])
    u = proj_a[:, PA_U:PA_U + S5_WIDTH].reshape(B, nc5, Lc, S5_WIDTH)
    up = jnp.transpose(u, (2, 1, 0, 3)).reshape(N, S5_WIDTH)
    tr5 = nc5 * B
    y5t = _s5(_transpose2d(up, tr5, "s5_in_t"), w5, so5, ar5, B, nc5, ncc5)
    y5p = _untranspose2d(y5t, tr5, "s5_out_t").reshape(Lc, nc5, B, S5_WIDTH)
    y5 = jnp.transpose(y5p, (2, 1, 0, 3)).reshape(N, S5_WIDTH)

    yd = _shortconv(proj_a, lp["sc_conv_w"], B, S, Sc)

    dsk = jnp.repeat(lp["ssd_d"], SSD_HEAD_DIM)[None, :]
    wr_pad = jnp.pad(lp["w_router"], ((0, 0), (0, 128 - N_EXPERTS)))
    x1, hm, logits = _merge(xa, ya, yf, yb, xbc_c, proj_a, y5, yd, mods3,
                            lp["w_br_attn"].astype(BF16), lp["w_br_ssd"].astype(BF16), lp["w_br_s5"].astype(BF16),
                            lp["w_br_sc"].astype(BF16), lp["w_out"].astype(BF16), lp["s5_w_glu"].astype(BF16),
                            lp["s5_b_glu"][None, :], lp["ssd_norm_g"][None, :], dsk,
                            lp["norm_ffn_g"][None, :], wr_pad, B, rc)
    slot, gate, rng = _router(logits, B, S, Sc, ctx_out)
    return _experts(hm, slot, gate, rng, ew[0], ew[1], ew[2], li, x1, mods3, B, S, Sc, ctx_out, final_g)


def kernel(x, c, ctx, c_ctx, w_mod, b_mod, norm_mix_g, norm_ffn_g, w_in, q_norm_g, k_norm_g, ssd_conv_w, ssd_conv_b, ssd_dt_bias, ssd_a_log, ssd_d, ssd_norm_g, s5_lambda_re, s5_lambda_im, s5_log_step, s5_b_re, s5_b_im, s5_c_re, s5_c_im, s5_d, s5_w_glu, s5_b_glu, sc_conv_w, w_br_attn, w_br_ssd, w_br_s5, w_br_sc, w_out, w_router, w_exp_gate, w_exp_up, w_exp_down, final_norm_g):
    B, T, D = x.shape
    Sc = ctx.shape[1]
    S = Sc + T
    depth = w_in.shape[0]
    xa = jnp.concatenate([ctx, x], axis=1).reshape(B * S, D)
    cc = jnp.zeros((16, D), F32).at[0:B].set(c).at[B].set(c_ctx)
    cos, sin = _rope_tables(T, Sc)
    tabs = (jnp.tile(cos, (1, ATTN_HEADS)), jnp.tile(sin, (1, ATTN_HEADS)),
            jnp.tile(cos, (1, ATTN_KV_HEADS)), jnp.tile(sin, (1, ATTN_KV_HEADS)),
            _block_diag_ones(ATTN_Q_DIM), _block_diag_ones(ATTN_KV_DIM))
    stacked = dict(
        w_in=w_in, norm_mix_g=norm_mix_g, norm_ffn_g=norm_ffn_g, q_norm_g=q_norm_g, k_norm_g=k_norm_g,
        ssd_conv_w=ssd_conv_w, ssd_conv_b=ssd_conv_b, ssd_dt_bias=ssd_dt_bias, ssd_a_log=ssd_a_log, ssd_d=ssd_d,
        ssd_norm_g=ssd_norm_g, s5_lambda_re=s5_lambda_re, s5_lambda_im=s5_lambda_im, s5_log_step=s5_log_step,
        s5_b_re=s5_b_re, s5_b_im=s5_b_im, s5_c_re=s5_c_re, s5_c_im=s5_c_im, s5_d=s5_d, s5_w_glu=s5_w_glu,
        s5_b_glu=s5_b_glu, sc_conv_w=sc_conv_w, w_br_attn=w_br_attn, w_br_ssd=w_br_ssd, w_br_s5=w_br_s5,
        w_br_sc=w_br_sc, w_out=w_out, w_router=w_router)
    ew = (w_exp_gate, w_exp_up, w_exp_down)
    for i in range(depth):
        lp = {k: v[i] for k, v in stacked.items()}
        mods3 = _mods(cc, w_mod, b_mod[:, None, :], i).reshape(16, 1, 6 * D)
        last = i == depth - 1
        xa = _layer(xa, mods3, lp, ew, i, tabs, B, S, Sc, ctx_out=not last,
                    final_g=final_norm_g[None, :] if last else None)
    return xa.reshape(B, T, D)
```

```python
import functools
import math

import jax
import jax.numpy as jnp
import numpy as np
from jax import lax
from jax.experimental import pallas as pl
from jax.experimental.pallas import tpu as pltpu

F32 = jnp.float32
BF16 = jnp.bfloat16
I32 = jnp.int32
HI = lax.Precision.HIGHEST

RMS_EPS = 1e-6
GRID_W = 64
ROPE_THETA = 10000.0
HEAD_DIM = 64
ATTN_HEADS = 8
ATTN_KV_HEADS = 2
SSD_HEADS = 8
SSD_HEAD_DIM = 64
SSD_GROUPS = 2
SSD_STATE = 64
SSD_CHUNK = 128
S5_GROUPS = 24
S5_GROUP_DIM = 16
S5_STATE = 64
S5_CHUNK = 16
SC_WIDTH = 384
N_EXPERTS = 16
EC_CAPACITY = 2
MAX_ROW_BLOCK = 256
SLOT_WINDOW = 64
SLOT_ALIGN = 16

ATTN_Q_DIM = ATTN_HEADS * HEAD_DIM
ATTN_KV_DIM = ATTN_KV_HEADS * HEAD_DIM
SSD_INNER = SSD_HEADS * SSD_HEAD_DIM
SSD_BC_DIM = SSD_GROUPS * SSD_STATE
SSD_CONV_DIM = SSD_INNER + 2 * SSD_BC_DIM
S5_WIDTH = S5_GROUPS * S5_GROUP_DIM

PA_G, PA_Q, PA_Z, PA_K, PA_V = 0, 4096, 4608, 5120, 5248
PA_U, PA_SB, PA_SG, PA_SH, PA_XBC, PA_W = 5376, 5760, 6144, 6528, 6912, 7680


def _sds(shape, dtype):
    return jax.ShapeDtypeStruct(shape, dtype)


def _cp(sem, vmem_mb=None):
    kw = dict(dimension_semantics=sem)
    if vmem_mb is not None:
        kw["vmem_limit_bytes"] = vmem_mb << 20
    return pltpu.CompilerParams(**kw)


def _sigmoid(x):
    return 0.5 * jnp.tanh(0.5 * x) + 0.5


def _silu(x):
    return x * _sigmoid(x)


def _mods_body(c_ref, w_ref, b_ref, o_ref):
    s = _silu(c_ref[...])
    o_ref[...] = jnp.dot(s.astype(BF16), w_ref[...].astype(BF16), preferred_element_type=F32) + b_ref[...]


def _mods(cc, w, b):
    R, D = cc.shape
    L, _, N = w.shape
    tn = 1536
    return pl.pallas_call(
        _mods_body,
        grid=(L, N // tn),
        in_specs=[pl.BlockSpec((R, D), lambda l, j: (0, 0)),
                  pl.BlockSpec((None, D, tn), lambda l, j: (l, 0, j)),
                  pl.BlockSpec((None, 1, tn), lambda l, j: (l, 0, j))],
        out_specs=pl.BlockSpec((None, R, tn), lambda l, j: (l, 0, j)),
        out_shape=_sds((L, R, N), F32),
        compiler_params=_cp(("parallel", "parallel"), 40),
        name="mods",
    )(cc, w, b)


def _row_cfg(S, Sc):
    tm = math.gcd(math.gcd(Sc, S - Sc), MAX_ROW_BLOCK)
    return tm, S // tm, Sc // tm


def _mod_spec(D, rc, B, chunk):
    _, nb, nbc = rc
    return pl.BlockSpec((None, 1, D), lambda i: (jnp.where(i % nb < nbc, B, i // nb), 0, chunk))


def _in_proj_body(x_ref, g_ref, shl_ref, scl_ref, shc_ref, scc_ref, w_ref, wdt_ref, cs_ref, o_ref, dt_ref, hn_ref,
                  *, tm, nb, Sc):
    j = pl.program_id(1)

    @pl.when(j == 0)
    def _():
        nh = 2 if tm % 32 == 0 else 1
        for p in range(nh):
            r = slice(p * tm // nh, (p + 1) * tm // nh)
            x = x_ref[r, :]
            ms = jnp.mean(x * x, axis=-1, keepdims=True)
            y = x * lax.rsqrt(ms + RMS_EPS) * g_ref[...]
            t = (pl.program_id(0) % nb) * tm + p * tm // nh + lax.broadcasted_iota(I32, (tm // nh, 1), 0)
            isc = t < Sc
            hn = y * (1.0 + jnp.where(isc, scc_ref[...], scl_ref[...])) + jnp.where(isc, shc_ref[...], shl_ref[...])
            hn_ref[r, :] = hn.astype(BF16)
            o_ref[r, :] = (jnp.dot(hn_ref[r, :], w_ref[...], preferred_element_type=F32)
                           * cs_ref[...]).astype(o_ref.dtype)
        dt_ref[...] = jnp.dot(hn_ref[...], wdt_ref[...], preferred_element_type=F32)

    @pl.when(j > 0)
    def _():
        o_ref[...] = (jnp.dot(hn_ref[...], w_ref[...], preferred_element_type=F32) * cs_ref[...]).astype(o_ref.dtype)


def _in_proj(xa, g, mods3, w_p, w_dt, B, S, Sc):
    N, D = xa.shape
    tm = S // 2 if (S // 2) % 128 == 0 else S
    nb = S // tm
    tn = PA_W // 3
    assert tn % 256 == 0
    col_scale = jnp.asarray(np.where(np.arange(PA_W) < PA_Q, 0.5, 1.0)[None, :], F32)
    lat = lambda ch: pl.BlockSpec((None, 1, D), lambda i, j: (i // nb, 0, ch))
    ctx = lambda ch: pl.BlockSpec((None, 1, D), lambda i, j: (B, 0, ch))
    return pl.pallas_call(
        functools.partial(_in_proj_body, tm=tm, nb=nb, Sc=Sc),
        grid=(N // tm, PA_W // tn),
        in_specs=[pl.BlockSpec((tm, D), lambda i, j: (i, 0)),
                  pl.BlockSpec((1, D), lambda i, j: (0, 0)),
                  lat(0), lat(1), ctx(0), ctx(1),
                  pl.BlockSpec((D, tn), lambda i, j: (0, j)),
                  pl.BlockSpec((D, 128), lambda i, j: (0, 0)),
                  pl.BlockSpec((1, tn), lambda i, j: (0, j))],
        out_specs=[pl.BlockSpec((tm, tn), lambda i, j: (i, j)), pl.BlockSpec((tm, 128), lambda i, j: (i, 0))],
        out_shape=[_sds((N, PA_W), BF16), _sds((N, 128), F32)],
        scratch_shapes=[pltpu.VMEM((tm, D), BF16)],
        compiler_params=_cp(("parallel", "arbitrary"), 56),
        name="in_proj",
    )(xa, g, mods3, mods3, mods3, mods3, w_p, w_dt, col_scale)


def _shifted(x, Sc):
    S = x.shape[0]
    t = lax.broadcasted_iota(I32, (S, 1), 0)
    prev = jnp.where((t == 0) | (t == Sc), 0.0, pltpu.roll(x, 1, 0))
    nxt = jnp.where((t == Sc - 1) | (t == S - 1), 0.0, pltpu.roll(x, S - 1, 0))
    return prev, nxt


def _ssdconv_body(x_ref, w_ref, b_ref, o_ref, *, Sc):
    x = x_ref[...].astype(F32)
    prev, nxt = _shifted(x, Sc)
    w = w_ref[...]
    y = w[0:1, :] * prev + w[1:2, :] * x + w[2:3, :] * nxt + b_ref[...]
    o_ref[...] = _silu(y).astype(o_ref.dtype)


def _ssdconv(proj_a, w, b, B, S, Sc):
    C = 384
    j0 = PA_XBC // C
    return pl.pallas_call(
        functools.partial(_ssdconv_body, Sc=Sc),
        grid=(B, SSD_CONV_DIM // C),
        in_specs=[pl.BlockSpec((S, C), lambda b_, j: (b_, j0 + j)),
                  pl.BlockSpec((3, C), lambda b_, j: (0, j)),
                  pl.BlockSpec((1, C), lambda b_, j: (0, j))],
        out_specs=pl.BlockSpec((S, C), lambda b_, j: (b_, j)),
        out_shape=_sds((B * S, SSD_CONV_DIM), BF16),
        compiler_params=_cp(("parallel", "parallel"), 40),
        name="ssdconv",
    )(proj_a, w, b)


def _shortconv_body(sb_ref, sg_ref, sh_ref, w_ref, o_ref, *, Sc):
    x = sg_ref[...].astype(F32) * sh_ref[...].astype(F32)
    prev, nxt = _shifted(x, Sc)
    w = w_ref[...]
    y = w[0:1, :] * prev + w[1:2, :] * x + w[2:3, :] * nxt
    o_ref[...] = (sb_ref[...].astype(F32) * y).astype(o_ref.dtype)


def _shortconv(proj_a, w, B, S, Sc):
    C = SC_WIDTH
    return pl.pallas_call(
        functools.partial(_shortconv_body, Sc=Sc),
        grid=(B,),
        in_specs=[pl.BlockSpec((S, C), lambda b_: (b_, PA_SB // C)),
                  pl.BlockSpec((S, C), lambda b_: (b_, PA_SG // C)),
                  pl.BlockSpec((S, C), lambda b_: (b_, PA_SH // C)),
                  pl.BlockSpec((3, C), lambda b_: (0, 0))],
        out_specs=pl.BlockSpec((S, C), lambda b_: (b_, 0)),
        out_shape=_sds((B * S, C), BF16),
        compiler_params=_cp(("parallel",), 40),
        name="shortconv",
    )(proj_a, proj_a, proj_a, w)


def _norm_rope(x, g, cos, sin, bd, scale):
    W = x.shape[1]
    sq = x * x
    hi = sq.astype(BF16)
    lo = (sq - hi.astype(F32)).astype(BF16)
    ssq = jnp.dot(hi, bd, preferred_element_type=F32) + jnp.dot(lo, bd, preferred_element_type=F32)
    y = x * lax.rsqrt(ssq * (1.0 / HEAD_DIM) + RMS_EPS) * g
    lane = lax.broadcasted_iota(I32, (1, W), 1)
    first = (lane % 32) < 16
    partner = jnp.where(first, pltpu.roll(y, W - 16, 1), pltpu.roll(y, 16, 1))
    return (y * cos + partner * sin) * scale


def _attn_body(q_ref, k_ref, v_ref, cq_ref, sq_ref, ck_ref, sk_ref, gq_ref, gk_ref, bdq_ref, bdk_ref,
               o_ref, kh_ref, ve_ref, *, Sc, S, tq):
    qb = pl.program_id(1)
    hd = HEAD_DIM

    @pl.when(qb == 0)
    def _():
        k = k_ref[...].astype(F32)
        kh_ref[...] = _norm_rope(k, gk_ref[...], ck_ref[...], sk_ref[...], bdk_ref[...], 1.0).astype(BF16)
        v = v_ref[...].astype(F32)
        lane = lax.broadcasted_iota(I32, (1, 2 * hd), 1)
        ve_ref[0] = jnp.where(lane < hd, v, 1.0).astype(BF16)
        ve_ref[1] = jnp.where(lane < hd, pltpu.roll(v, hd, 1), 1.0).astype(BF16)

    q = q_ref[...].astype(F32)
    qh = _norm_rope(q, gq_ref[...], cq_ref[...], sq_ref[...], bdq_ref[...],
                    HEAD_DIM ** -0.5 * math.log2(math.e)).astype(BF16)
    rep = ATTN_HEADS // ATTN_KV_HEADS
    nt = (((1,), (1,)), ((), ()))

    def attend(splits):
        scores = []
        for g in range(ATTN_KV_HEADS):
            qg = jnp.concatenate([qh[:, (g * rep + r) * hd:(g * rep + r + 1) * hd] for r in range(rep)], axis=0)
            scores.append([lax.dot_general(qg, kh_ref[a:b, g * hd:(g + 1) * hd], nt, preferred_element_type=F32)
                           for a, b in splits])
        for g in range(ATTN_KV_HEADS):
            m = functools.reduce(jnp.maximum, [jnp.max(s, axis=-1, keepdims=True) for s in scores[g]])
            acc = None
            for s, (a, b) in zip(scores[g], splits):
                p = jnp.exp2(s - m).astype(BF16)
                part = jnp.dot(p, ve_ref[g, a:b, :], preferred_element_type=F32)
                acc = part if acc is None else acc + part
            o = acc[:, 0:hd] / acc[:, hd:hd + 1]
            for r in range(rep):
                h = g * rep + r
                o_ref[:, h * hd:(h + 1) * hd] = o[r * tq:(r + 1) * tq].astype(o_ref.dtype)

    @pl.when(qb < Sc // tq)
    def _():
        attend([(0, Sc)])

    @pl.when(qb >= Sc // tq)
    def _():
        half = (S // 2 + 255) // 256 * 256 if S >= 512 else S
        attend([(0, half), (half, S)] if half < S else [(0, S)])


def _rope_tables(T, Sc):
    rows = T // GRID_W
    row = np.repeat(np.arange(rows, dtype=np.float32), GRID_W)
    col = np.tile(np.arange(GRID_W, dtype=np.float32), rows)
    half = HEAD_DIM // 2
    inv = jnp.asarray(ROPE_THETA, F32) ** (-jnp.arange(0, half, 2, dtype=F32) / half)
    ra = jnp.asarray(row)[:, None] * inv
    ca = jnp.asarray(col)[:, None] * inv
    cos = jnp.concatenate([jnp.cos(ra), jnp.cos(ra), jnp.cos(ca), jnp.cos(ca)], axis=1)
    sin = jnp.concatenate([-jnp.sin(ra), jnp.sin(ra), -jnp.sin(ca), jnp.sin(ca)], axis=1)
    cos = jnp.concatenate([jnp.ones((Sc, HEAD_DIM), F32), cos], axis=0)
    sin = jnp.concatenate([jnp.zeros((Sc, HEAD_DIM), F32), sin], axis=0)
    return cos, sin


def _block_diag_ones(W):
    i = np.arange(W) // HEAD_DIM
    return jnp.asarray((i[:, None] == i[None, :]).astype(np.float32), BF16)


def _attention(proj_a, tabs, q_norm_g, k_norm_g, B, S, Sc):
    cos_q, sin_q, cos_k, sin_k, bdq, bdk = tabs
    tq = math.gcd(Sc, 256)
    nq = S // tq
    gq = jnp.tile(q_norm_g, ATTN_HEADS)[None, :]
    gk = jnp.tile(k_norm_g, ATTN_KV_HEADS)[None, :]
    const = lambda b_, i: (0, 0)
    return pl.pallas_call(
        functools.partial(_attn_body, Sc=Sc, S=S, tq=tq),
        grid=(B, nq),
        in_specs=[pl.BlockSpec((tq, ATTN_Q_DIM), lambda b_, i: (b_ * nq + i, PA_Q // ATTN_Q_DIM)),
                  pl.BlockSpec((S, ATTN_KV_DIM), lambda b_, i: (b_, PA_K // ATTN_KV_DIM)),
                  pl.BlockSpec((S, ATTN_KV_DIM), lambda b_, i: (b_, PA_V // ATTN_KV_DIM)),
                  pl.BlockSpec((tq, ATTN_Q_DIM), lambda b_, i: (i, 0)),
                  pl.BlockSpec((tq, ATTN_Q_DIM), lambda b_, i: (i, 0)),
                  pl.BlockSpec((S, ATTN_KV_DIM), const),
                  pl.BlockSpec((S, ATTN_KV_DIM), const),
                  pl.BlockSpec((1, ATTN_Q_DIM), const),
                  pl.BlockSpec((1, ATTN_KV_DIM), const),
                  pl.BlockSpec((ATTN_Q_DIM, ATTN_Q_DIM), const),
                  pl.BlockSpec((ATTN_KV_DIM, ATTN_KV_DIM), const)],
        out_specs=pl.BlockSpec((tq, ATTN_Q_DIM), lambda b_, i: (b_ * nq + i, 0)),
        out_shape=_sds((B * S, ATTN_Q_DIM), BF16),
        scratch_shapes=[pltpu.VMEM((S, ATTN_KV_DIM), BF16),
                        pltpu.VMEM((ATTN_KV_HEADS, S, 2 * HEAD_DIM), BF16)],
        compiler_params=_cp(("parallel", "arbitrary"), 56),
        name="attention",
    )(proj_a, proj_a, proj_a, cos_q, sin_q, cos_k, sin_k, gq, gk, bdq, bdk)


def _ssd_chains(chains):
    L = SSD_CHUNK
    P, Nst = SSD_HEAD_DIM, SSD_STATE
    rep = SSD_HEADS // SSD_GROUPS
    row = lax.broadcasted_iota(I32, (L, L), 0)
    col = lax.broadcasted_iota(I32, (L, L), 1)
    tri = (row >= col).astype(F32)
    lane = lax.broadcasted_iota(I32, (1, 2 * P), 1)
    lo_half = lane < P
    blockdiag = (row // Nst) == (col // P)
    upper = lax.broadcasted_iota(I32, (2 * Nst, 1), 0) < Nst
    st = []
    for xbc_ref, dt_ref, bias, a_neg, h_ref, y_ref, rev in chains:
        xbc = xbc_ref[...]
        bm = xbc[:, SSD_INNER:SSD_INNER + SSD_BC_DIM]
        cm = xbc[:, SSD_INNER + SSD_BC_DIM:SSD_CONV_DIM]
        dt = jax.nn.softplus(dt_ref[...] + bias)
        a = dt * a_neg
        cum = jnp.dot(tri, a, precision=HI, preferred_element_type=F32)
        e = cum - a if rev else cum
        tot_row = cum[L - 1:L, :]
        win_c = jnp.exp(tot_row - e) if rev else jnp.exp(e)
        win_hi = win_c.astype(BF16)
        cm32 = cm.astype(F32)
        st.append(dict(
            rev=rev, h_ref=h_ref, y_ref=y_ref, xbc=xbc, cm=cm, cm32=cm32, cm_rolled=pltpu.roll(cm32, P, 1),
            cum=cum, e=e, eT=e.T, dtT=dt.T, bT=bm.astype(F32).T.astype(BF16),
            mask=(col >= row) if rev else (row >= col),
            win_hi=win_hi, win_lo=(win_c - win_hi.astype(F32)).astype(BF16)))
    for g in range(SSD_GROUPS):
        own = (lane // P) == g
        for c in st:
            cg0 = jnp.where(own, c["cm"], jnp.zeros_like(c["cm"]))
            c["sg"] = jnp.dot(cg0, c["bT"], preferred_element_type=F32)
            c["cdup"] = jnp.where(own, c["cm32"], c["cm_rolled"])
            c["bTg"] = c["bT"][g * Nst:(g + 1) * Nst, :].astype(F32)
        for q in range(rep // 2):
            pq = (g * rep) // 2 + q
            hs = (2 * pq, 2 * pq + 1)
            sel = (lax.broadcasted_iota(I32, (2 * P, 2 * P), 0) == hs[0] + col // P).astype(BF16)
            for c in st:
                rev, e, eT, cum, dtT = c["rev"], c["e"], c["eT"], c["cum"], c["dtT"]
                ms, bws, tots = [], [], []
                for h in hs:
                    ecol = e[:, h:h + 1]
                    erow = eT[h:h + 1, :]
                    tot = cum[L - 1:L, h:h + 1]
                    diff = (erow - ecol) if rev else (ecol - erow)
                    dec = jnp.exp(jnp.where(c["mask"], diff, -1e30))
                    ms.append((c["sg"] * dec * dtT[h:h + 1, :]).astype(BF16))
                    wrow = (jnp.exp(erow) if rev else jnp.exp(tot - erow)) * dtT[h:h + 1, :]
                    bws.append(c["bTg"] * wrow)
                    tots.append(jnp.exp(tot))
                xp = c["xbc"][:, 2 * P * pq:2 * P * (pq + 1)]
                zero = jnp.zeros_like(xp)
                xbd = jnp.concatenate([jnp.where(lo_half, xp, zero), jnp.where(lo_half, zero, xp)], axis=0)
                y = jnp.dot(jnp.concatenate(ms, axis=1), xbd, preferred_element_type=F32)
                wexp = (jnp.dot(c["win_hi"], sel, preferred_element_type=F32)
                        + jnp.dot(c["win_lo"], sel, preferred_element_type=F32))
                hp = c["h_ref"][pq]
                y = y + jnp.dot((wexp * c["cdup"]).astype(BF16), hp.astype(BF16), preferred_element_type=F32)
                upd = jnp.dot(jnp.concatenate(bws, axis=0).astype(BF16), xp, preferred_element_type=F32)
                keep = jnp.where(upper, tots[0], tots[1])
                c["h_ref"][pq] = jnp.where(blockdiag, keep * hp + upd, 0.0)
                c["y_ref"][:, 2 * P * pq:2 * P * (pq + 1)] = y.astype(c["y_ref"].dtype)


def _ssd_body(xf_ref, dtf_ref, xb_ref, dtb_ref, bias_ref, alog_ref, yf_ref, yb_ref, *h_refs, nper):
    @pl.when(pl.program_id(1) == 0)
    def _():
        for h_ref in h_refs:
            h_ref[...] = jnp.zeros_like(h_ref)

    lane = lax.broadcasted_iota(I32, (1, 128), 1)
    a_neg = jnp.where(lane < SSD_HEADS, -jnp.exp(alog_ref[...]), 0.0)
    bias = bias_ref[...]
    chains = []
    for p in range(nper):
        chains.append((xf_ref.at[p], dtf_ref.at[p], bias[0:1, :], a_neg[0:1, :], h_refs[2 * p], yf_ref.at[p], False))
        chains.append((xb_ref.at[p], dtb_ref.at[p], bias[1:2, :], a_neg[1:2, :], h_refs[2 * p + 1], yb_ref.at[p], True))
    _ssd_chains(chains)


def _ssd(xbc_c, dt, dt_bias, a_log, B, S, Sc):
    L = SSD_CHUNK
    nc, ncc = S // L, Sc // L
    nper = math.gcd(B, 4)
    bg = B // nper
    padh = 128 - SSD_HEADS
    bias = jnp.pad(dt_bias, ((0, 0), (0, padh)))
    alog = jnp.pad(a_log, ((0, 0), (0, padh)))
    x3 = xbc_c.reshape(nper, bg * S, SSD_CONV_DIM)
    dt3 = dt.reshape(nper, bg * S, 128)

    def fwd(b_, i):
        return (0, b_ * nc + i, 0)

    def bwd(b_, i):
        return (0, b_ * nc + jnp.where(i < ncc, ncc - 1 - i, nc + ncc - 1 - i), 0)

    const = lambda b_, i: (0, 0)
    out = _sds((nper, bg * S, SSD_INNER), BF16)
    yf, yb = pl.pallas_call(
        functools.partial(_ssd_body, nper=nper),
        grid=(bg, nc),
        in_specs=[pl.BlockSpec((nper, L, SSD_CONV_DIM), fwd), pl.BlockSpec((nper, L, 128), fwd),
                  pl.BlockSpec((nper, L, SSD_CONV_DIM), bwd), pl.BlockSpec((nper, L, 128), bwd),
                  pl.BlockSpec((2, 128), const), pl.BlockSpec((2, 128), const)],
        out_specs=[pl.BlockSpec((nper, L, SSD_INNER), fwd), pl.BlockSpec((nper, L, SSD_INNER), bwd)],
        out_shape=[out, out],
        scratch_shapes=[pltpu.VMEM((SSD_HEADS // 2, 2 * SSD_STATE, 2 * SSD_HEAD_DIM), F32)] * (2 * nper),
        compiler_params=_cp(("parallel", "arbitrary")),
        name="ssd",
    )(x3, dt3, x3, dt3, bias, alog)
    return yf.reshape(B * S, SSD_INNER), yb.reshape(B * S, SSD_INNER)


def _cpow(n, lr, li, st):
    mag = jnp.exp(n * lr * st)
    ang = n * li * st
    return mag * jnp.cos(ang), mag * jnp.sin(ang)


def _zoh_coef(lr, li, st):
    ar, ai = _cpow(1.0, lr, li, st)
    nr, ni = ar - 1.0, ai
    den = lr * lr + li * li
    return (nr * lr + ni * li) / den, (ni * lr - nr * li) / den


def _spread_lanes(x, sel):
    hi = x.astype(BF16)
    r1 = x - hi.astype(F32)
    mid = r1.astype(BF16)
    lo = (r1 - mid.astype(F32)).astype(BF16)
    return (jnp.dot(hi, sel, preferred_element_type=F32) + jnp.dot(mid, sel, preferred_element_type=F32)
            + jnp.dot(lo, sel, preferred_element_type=F32))


def _s5gen_body(colp_ref, rowp_ref, ctr_ref, cti_ref, btr_ref, bti_ref, blr_ref, bli_ref, d_ref,
                wt_ref, so_ref, ar_ref, ws_ref, sos_ref):
    Lc, G, P = S5_CHUNK, S5_GROUP_DIM, S5_STATE
    NS = 2 * Lc + 1
    colp = colp_ref[...]
    rowp = rowp_ref[...]
    lrf, lif, stf = colp[:, 0:1], colp[:, 1:2], jnp.exp(colp[:, 2:3])
    lrb, lib, stb = colp[:, 3:4], colp[:, 4:5], jnp.exp(colp[:, 5:6])
    lane = lax.broadcasted_iota(I32, (1, NS * G), 1)
    slot = lane // G
    isb = slot < Lc
    cslot = lax.broadcasted_iota(I32, (1, 128), 1)
    cisb = cslot < Lc
    clag = jnp.where(cslot < NS, jnp.abs(cslot - Lc), 0).astype(F32)
    pr_c, pi_c = _cpow(clag, jnp.where(cisb, lrb, lrf), jnp.where(cisb, lib, lif), jnp.where(cisb, stb, stf))
    srow = lax.broadcasted_iota(I32, (128, 1), 0)
    spread = (srow == slot).astype(BF16)
    pr, pi = _spread_lanes(pr_c, spread), _spread_lanes(pi_c, spread)
    tile = (lax.broadcasted_iota(I32, (G, 1), 0) == lane % G).astype(BF16)
    ctr, cti = _spread_lanes(ctr_ref[...], tile), _spread_lanes(cti_ref[...], tile)
    er = ctr * pr - cti * pi
    ei = ctr * pi + cti * pr
    rlrf, rlif, rstf = rowp[0:1, :], rowp[1:2, :], jnp.exp(rowp[2:3, :])
    rlrb, rlib, rstb = rowp[3:4, :], rowp[4:5, :], jnp.exp(rowp[5:6, :])
    btr, bti = btr_ref[...], bti_ref[...]
    cfr, cfi = _zoh_coef(rlrf, rlif, rstf)
    cbr, cbi = _zoh_coef(rlrb, rlib, rstb)
    bbf_r, bbf_i = cfr * btr - cfi * bti, cfr * bti + cfi * btr
    bbb_r, bbb_i = cbr * btr - cbi * bti, cbr * bti + cbi * btr

    def kt(br, bi):
        return (jnp.dot(br, er, precision=HI, preferred_element_type=F32)
                - jnp.dot(bi, ei, precision=HI, preferred_element_type=F32))

    ktf, ktb = kt(bbf_r, bbf_i), kt(bbb_r, bbb_i)
    ii = lax.broadcasted_iota(I32, (G, NS * G), 0)
    dmat = jnp.where((slot == Lc) & (ii == lane - Lc * G), _spread_lanes(d_ref[...], tile), 0.0)
    strip = jnp.where(slot == Lc, ktf + ktb, jnp.where(isb, ktb, ktf)) + dmat
    for s in range(Lc):
        off = (Lc - s) * G
        ws_ref[s * G:(s + 1) * G, :] = strip[:, off:off + Lc * G]
    fo = (Lc + 1) * G
    ws_ref[Lc * G:Lc * G + P, :] = er[:, fo:fo + Lc * G]
    ws_ref[Lc * G + P:Lc * G + 2 * P, :] = er[:, 0:Lc * G]
    ws_ref[Lc * G + 2 * P:Lc * G + 3 * P, :] = -ei[:, fo:fo + Lc * G]
    ws_ref[Lc * G + 3 * P:Lc * G + 4 * P, :] = -ei[:, 0:Lc * G]
    wt_ref[:, 0:Lc * G] = ws_ref[0:Lc * G, :].T.astype(wt_ref.dtype)
    wt_ref[:, Lc * G:Lc * G + 4 * P] = ws_ref[Lc * G:Lc * G + 4 * P, :].T.astype(wt_ref.dtype)
    s_idx = lax.broadcasted_iota(I32, (1, Lc * G), 1) // G
    sel_f = (srow == 2 * Lc - 1 - s_idx).astype(BF16)
    sel_b = (srow == Lc - s_idx).astype(BF16)
    qfr, qfi = _spread_lanes(pr_c, sel_f), _spread_lanes(pi_c, sel_f)
    qbr, qbi = _spread_lanes(pr_c, sel_b), _spread_lanes(pi_c, sel_b)
    ccfr, ccfi = _zoh_coef(lrf, lif, stf)
    ccbr, ccbi = _zoh_coef(lrb, lib, stb)
    blr, bli = _spread_lanes(blr_ref[...], tile[:, 0:Lc * G]), _spread_lanes(bli_ref[...], tile[:, 0:Lc * G])
    bfr, bfi = ccfr * blr - ccfi * bli, ccfr * bli + ccfi * blr
    bbr, bbi = ccbr * blr - ccbi * bli, ccbr * bli + ccbi * blr
    sos_ref[0:P, :] = qfr * bfr - qfi * bfi
    sos_ref[P:2 * P, :] = qbr * bbr - qbi * bbi
    sos_ref[2 * P:3 * P, :] = qfr * bfi + qfi * bfr
    sos_ref[3 * P:4 * P, :] = qbr * bbi + qbi * bbr
    so_ref[...] = sos_ref[...].T.astype(so_ref.dtype)
    afr, afi = _cpow(float(Lc), rlrf, rlif, rstf)
    abr, abi = _cpow(float(Lc), rlrb, rlib, rstb)
    ar_ref[...] = jnp.zeros_like(ar_ref)
    ar_ref[0:1, 0:P] = afr
    ar_ref[0:1, P:2 * P] = abr
    ar_ref[1:2, 0:P] = afi
    ar_ref[1:2, P:2 * P] = abi


def _s5gen(lam_re, lam_im, log_step, b_re, b_im, c_re, c_im, d_skip):
    Gn, P, G, Lc = S5_GROUPS, S5_STATE, S5_GROUP_DIM, S5_CHUNK
    NS = 2 * Lc + 1
    ls = jnp.broadcast_to(log_step[:, :, None], (2, Gn, P))
    z = jnp.zeros((Gn, P), F32)
    rowp = jnp.stack([lam_re[0], lam_im[0], ls[0], lam_re[1], lam_im[1], ls[1], z, z], axis=1)
    colp = jnp.swapaxes(rowp, 1, 2)
    ctr = jnp.swapaxes(c_re, 1, 2)
    cti = jnp.swapaxes(c_im, 1, 2)
    btr = jnp.swapaxes(b_re, 1, 2)
    bti = jnp.swapaxes(b_im, 1, 2)
    blr, bli = b_re, b_im
    dt = d_skip.reshape(Gn, 1, G)
    g3 = lambda a, b: pl.BlockSpec((None, a, b), lambda g: (g, 0, 0))
    return pl.pallas_call(
        _s5gen_body,
        grid=(Gn,),
        in_specs=[g3(P, 8), g3(8, P), g3(P, G), g3(P, G), g3(G, P), g3(G, P), g3(P, G), g3(P, G), g3(1, G)],
        out_specs=[g3(Lc * G, Lc * G + 4 * P), g3(Lc * G, 4 * P), g3(8, 2 * P)],
        out_shape=[_sds((Gn, Lc * G, Lc * G + 4 * P), BF16), _sds((Gn, Lc * G, 4 * P), BF16),
                   _sds((Gn, 8, 2 * P), F32)],
        scratch_shapes=[pltpu.VMEM((Lc * G + 4 * P, Lc * G), F32), pltpu.VMEM((4 * P, Lc * G), F32)],
        compiler_params=_cp(("parallel",)),
        name="s5gen",
    )(colp, rowp, ctr, cti, btr, bti, blr, bli, dt)


def _s5_body(xt_ref, wt_ref, so_ref, ar_ref, y_ref, *scratch, B, nc, ncc, ng):
    P, Lc, G = S5_STATE, S5_CHUNK, S5_GROUP_DIM
    LG = Lc * G
    R = nc * B
    v_refs, h_refs = scratch[:ng], scratch[ng:]
    ucols = []
    for k in range(ng):
        ucol = jnp.concatenate([xt_ref[k * G:(k + 1) * G, l * R:(l + 1) * R] for l in range(Lc)], axis=0)
        v_refs[k][...] = lax.dot_general(ucol, so_ref[k], (((0,), (0,)), ((), ())), preferred_element_type=F32)
        ucols.append(ucol)
    lane = lax.broadcasted_iota(I32, (1, 2 * P), 1)
    isf = lane < P

    def step(i, carry):
        cf = pl.multiple_of(i * B, B)
        cb = pl.multiple_of(jnp.where(i < ncc, ncc - 1 - i, nc + ncc - 1 - i) * B, B)
        out = []
        for k in range(ng):
            hr, hi = carry[k]
            v_ref, h_ref = v_refs[k], h_refs[k]
            a_re, a_im = ar_ref[k, 0:1, :], ar_ref[k, 1:2, :]
            h_ref[pl.ds(cf, B), 0:P] = hr[:, 0:P]
            h_ref[pl.ds(cf, B), 2 * P:3 * P] = hi[:, 0:P]
            h_ref[pl.ds(cb, B), P:2 * P] = hr[:, P:2 * P]
            h_ref[pl.ds(cb, B), 3 * P:4 * P] = hi[:, P:2 * P]
            vr = jnp.where(isf, v_ref[pl.ds(cf, B), 0:2 * P], v_ref[pl.ds(cb, B), 0:2 * P])
            vi = jnp.where(isf, v_ref[pl.ds(cf, B), 2 * P:4 * P], v_ref[pl.ds(cb, B), 2 * P:4 * P])
            out.append((hr * a_re - hi * a_im + vr, hr * a_im + hi * a_re + vi))
        return tuple(out)

    z = jnp.zeros((B, 2 * P), F32)
    lax.fori_loop(0, nc, step, tuple((z, z) for _ in range(ng)))
    for k in range(ng):
        yt = jnp.dot(wt_ref[k, :, 0:LG], ucols[k], preferred_element_type=F32)
        yt = yt + lax.dot_general(wt_ref[k, :, LG:LG + 4 * P], h_refs[k][...].astype(BF16),
                                  (((1,), (1,)), ((), ())), preferred_element_type=F32)
        for t in range(Lc):
            y_ref[k * G:(k + 1) * G, t * R:(t + 1) * R] = yt[t * G:(t + 1) * G, :].astype(y_ref.dtype)


def _s5(xt, wt, so, ar, B, nc, ncc):
    P, Lc, G, Gn = S5_STATE, S5_CHUNK, S5_GROUP_DIM, S5_GROUPS
    N = xt.shape[1]
    R = nc * B
    ng = 2
    g3 = lambda a, b: pl.BlockSpec((ng, a, b), lambda g: (g, 0, 0))
    return pl.pallas_call(
        functools.partial(_s5_body, B=B, nc=nc, ncc=ncc, ng=ng),
        grid=(Gn // ng,),
        in_specs=[pl.BlockSpec((ng * G, N), lambda g: (g, 0)), g3(Lc * G, Lc * G + 4 * P), g3(Lc * G, 4 * P),
                  g3(8, 2 * P)],
        out_specs=pl.BlockSpec((ng * G, N), lambda g: (g, 0)),
        out_shape=_sds((Gn * G, N), BF16),
        scratch_shapes=[pltpu.VMEM((R, 4 * P), F32)] * (2 * ng),
        compiler_params=_cp(("parallel",)),
        name="s5",
    )(xt, wt, so, ar)


def _t2d_body(x_ref, o_ref):
    o_ref[...] = x_ref[...].astype(F32).T.astype(o_ref.dtype)


def _transpose2d(x, tr, name):
    M, C = x.shape
    return pl.pallas_call(
        _t2d_body,
        grid=(M // tr,),
        in_specs=[pl.BlockSpec((tr, C), lambda i: (i, 0))],
        out_specs=pl.BlockSpec((C, tr), lambda i: (0, i)),
        out_shape=_sds((C, M), x.dtype),
        compiler_params=_cp(("parallel",)),
        name=name,
    )(x)


def _untranspose2d(xt, tr, name):
    C, M = xt.shape
    return pl.pallas_call(
        _t2d_body,
        grid=(M // tr,),
        in_specs=[pl.BlockSpec((C, tr), lambda i: (0, i))],
        out_specs=pl.BlockSpec((tr, C), lambda i: (i, 0)),
        out_shape=_sds((M, C), xt.dtype),
        compiler_params=_cp(("parallel",)),
        name=name,
    )(xt)


def _merge_body(x_ref, ya_ref, yf_ref, yb_ref, xc_ref, z_ref, y5_ref, yd_ref, g_ref, gate_ref, sh_ref, sc_ref,
                wa_ref, wb_ref, wc_ref, wd_ref, wo_ref, wglu_ref, bglu_ref, ng_ref, dsk_ref, fg_ref, wrh_ref, wrl_ref,
                o_ref, hm_ref, lg_ref):
    tm, D = o_ref.shape
    nparts = 2 if tm % 32 == 0 else 1
    parts = [slice(p * tm // nparts, (p + 1) * tm // nparts) for p in range(nparts)]
    ys = [yf_ref[r, :].astype(F32) + yb_ref[r, :].astype(F32) + dsk_ref[...] * xc_ref[r, :].astype(F32) for r in parts]
    ys = [y * _silu(z_ref[r, :].astype(F32)) for y, r in zip(ys, parts)]
    ys = [(y * lax.rsqrt(jnp.mean(y * y, axis=-1, keepdims=True) + RMS_EPS) * ng_ref[...]).astype(BF16) for y in ys]
    cs = [jax.nn.gelu(y5_ref[r, :].astype(F32)).astype(BF16) for r in parts]
    glus = [_sigmoid(jnp.dot(c, wglu_ref[...], preferred_element_type=F32) + bglu_ref[...]) for c in cs]
    cs = [(c.astype(F32) * glu).astype(BF16) for c, glu in zip(cs, glus)]

    def gate(r, k):
        return 0.5 * jnp.tanh(g_ref[r, k * D:(k + 1) * D].astype(F32)) + 0.5

    ms = [gate(r, 0) * jnp.dot(ya_ref[r, :], wa_ref[...], preferred_element_type=F32) for r in parts]
    ms = [m + gate(r, 1) * jnp.dot(y, wb_ref[...], preferred_element_type=F32) for m, y, r in zip(ms, ys, parts)]
    ms = [m + gate(r, 2) * jnp.dot(c, wc_ref[...], preferred_element_type=F32) for m, c, r in zip(ms, cs, parts)]
    ms = [m + gate(r, 3) * jnp.dot(yd_ref[r, :], wd_ref[...], preferred_element_type=F32) for m, r in zip(ms, parts)]
    outs = [jnp.dot(m.astype(BF16), wo_ref[...], preferred_element_type=F32) for m in ms]
    xs = [x_ref[r, :] + gate_ref[...] * out for out, r in zip(outs, parts)]
    scale = fg_ref[...] * (1.0 + sc_ref[...])
    hs = [x * lax.rsqrt(jnp.mean(x * x, axis=-1, keepdims=True) + RMS_EPS) * scale + sh_ref[...] for x in xs]
    his = [h.astype(BF16) for h in hs]
    los = [(h - hi.astype(F32)).astype(BF16) for h, hi in zip(hs, his)]
    lgs = [jnp.dot(hi, wrh_ref[...], preferred_element_type=F32) + jnp.dot(lo, wrh_ref[...], preferred_element_type=F32)
           + jnp.dot(hi, wrl_ref[...], preferred_element_type=F32) for hi, lo in zip(his, los)]
    for x, hi, lg, r in zip(xs, his, lgs, parts):
        o_ref[r, :] = x
        hm_ref[r, :] = hi
        lg_ref[r, :] = lg


def _merge(xa, ya, yf, yb, xbc_c, proj, y5, yd, mods3, wa, wb, wc, wd, wo, wglu, bglu, ng, dsk, fg, wr, B, rc):
    N, D = xa.shape
    tm = rc[0]
    rows = lambda w, j=0: pl.BlockSpec((tm, w), lambda i: (i, j))
    full = lambda a: pl.BlockSpec(a.shape, lambda i: (0,) * a.ndim)
    wr_hi = wr.astype(BF16)
    wr_lo = (wr - wr_hi.astype(F32)).astype(BF16)
    return pl.pallas_call(
        _merge_body,
        grid=(N // tm,),
        in_specs=[rows(D), rows(ATTN_Q_DIM), rows(SSD_INNER), rows(SSD_INNER), rows(SSD_INNER),
                  rows(SSD_INNER, PA_Z // SSD_INNER), rows(S5_WIDTH), rows(SC_WIDTH), rows(4 * D, PA_G // (4 * D)),
                  _mod_spec(D, rc, B, 2), _mod_spec(D, rc, B, 3), _mod_spec(D, rc, B, 4),
                  full(wa), full(wb), full(wc), full(wd), full(wo), full(wglu), full(bglu), full(ng), full(dsk),
                  full(fg), full(wr_hi), full(wr_lo)],
        out_specs=[rows(D), rows(D), rows(128)],
        out_shape=[_sds((N, D), F32), _sds((N, D), BF16), _sds((N, 128), F32)],
        compiler_params=_cp(("parallel",), 48),
        name="merge",
    )(xa, ya, yf, yb, xbc_c, proj, y5, yd, proj, mods3, mods3, mods3, wa, wb, wc, wd, wo, wglu, bglu, ng, dsk,
      fg, wr_hi, wr_lo)


def _prefix_excl(mask_f):
    R, T = mask_f.shape
    r = lax.broadcasted_iota(I32, (128, 128), 0)
    c = lax.broadcasted_iota(I32, (128, 128), 1)
    upper = (r <= c).astype(BF16)
    outs = []
    off = jnp.zeros((R, 1), F32)
    for k in range(T // 128):
        blk = mask_f[:, k * 128:(k + 1) * 128]
        inc = jnp.dot(blk.astype(BF16), upper, preferred_element_type=F32)
        outs.append(inc - blk + off)
        off = off + inc[:, 127:128]
    return jnp.concatenate(outs, axis=1)


def _topk_slots(affs, caps):
    E = affs[0].shape[0]

    def step(i, ths):
        out = []
        for a, cap, th in zip(affs, caps, ths):
            cand = th | (jnp.int32(1) << (30 - i))
            cnt = jnp.sum((a >= pltpu.bitcast(cand, F32)).astype(I32), axis=1, keepdims=True)
            out.append(jnp.where(cnt >= cap, cand, th))
        return tuple(out)

    ths = lax.fori_loop(0, 31, step, tuple(jnp.zeros((E, 1), I32) for _ in affs))
    slots = []
    for a, cap, th in zip(affs, caps, ths):
        above = a >= pltpu.bitcast(jnp.maximum(th + 1, 0x00800000), F32)
        tie = (a >= pltpu.bitcast(th, F32)) & jnp.logical_not(above)
        n_above = jnp.sum(above.astype(F32), axis=1, keepdims=True)
        sel = above | (tie & (n_above + _prefix_excl(tie.astype(F32)) < cap))
        slots.append(jnp.where(sel, _prefix_excl(sel.astype(F32)), -1.0))
    return slots


def _router_body(lg_ref, slot_ref, gate_ref, rng_ref, *, Sc, cap_l, cap_c, ctx_out, tb):
    S = lg_ref.shape[0]
    lt = lg_ref[...].T[0:N_EXPERTS, :]
    mx = jnp.max(lt, axis=0, keepdims=True)
    ex = jnp.exp(lt - mx)
    aff = ex / jnp.sum(ex, axis=0, keepdims=True)
    gate_ref[...] = aff
    if ctx_out:
        slot_l, slot_c = _topk_slots([aff[:, Sc:], aff[:, 0:Sc]], [cap_l, cap_c])
        slot_c = jnp.where(slot_c >= 0.0, slot_c + cap_l, -1.0)
    else:
        slot_l, = _topk_slots([aff[:, Sc:]], [cap_l])
        slot_c = jnp.full((N_EXPERTS, Sc), -1.0, F32)
    slot_ref[:, 0:Sc] = slot_c
    slot_ref[:, Sc:] = slot_l
    lane = lax.broadcasted_iota(I32, (1, 128), 1)
    lo_a = jnp.zeros((N_EXPERTS, 128), F32)
    nwin = jnp.zeros((N_EXPERTS, 1), F32)
    for seg0, seg1, base, slots in ((0, Sc, float(cap_l), slot_c), (Sc, S, 0.0, slot_l)):
        lo = jnp.full((N_EXPERTS, 1), base, F32)
        for k in range(seg0 // tb, seg1 // tb):
            blk = slots[:, k * tb - seg0:(k + 1) * tb - seg0]
            hi = lo + jnp.sum((blk >= 0.0).astype(F32), axis=1, keepdims=True)
            lo_al = jnp.floor(lo * (1.0 / SLOT_ALIGN)) * SLOT_ALIGN
            need = jnp.where(hi > lo, jnp.floor((hi - lo_al + (SLOT_WINDOW - 1)) * (1.0 / SLOT_WINDOW)), 0.0)
            nwin = jnp.maximum(nwin, need)
            lo_a = jnp.where(lane == k, lo, lo_a)
            lo = hi
    rng_ref[...] = jnp.where(lane == 127, jnp.max(nwin, axis=0, keepdims=True), lo_a).astype(I32)


def _token_block(S, Sc):
    return math.gcd(math.gcd(Sc, S - Sc), 256)


def _router(logits, B, S, Sc, ctx_out):
    T = S - Sc
    cap_l = EC_CAPACITY * T // N_EXPERTS
    cap_c = EC_CAPACITY * Sc // N_EXPERTS
    tb = _token_block(S, Sc)
    es = pl.BlockSpec((None, N_EXPERTS, S), lambda b_: (b_, 0, 0))
    return pl.pallas_call(
        functools.partial(_router_body, Sc=Sc, cap_l=cap_l, cap_c=cap_c, ctx_out=ctx_out, tb=tb),
        grid=(B,),
        in_specs=[pl.BlockSpec((S, 128), lambda b_: (b_, 0))],
        out_specs=[es, es, pl.BlockSpec((None, N_EXPERTS, 128), lambda b_: (b_, 0, 0))],
        out_shape=[_sds((B, N_EXPERTS, S), F32), _sds((B, N_EXPERTS, S), F32), _sds((B, N_EXPERTS, 128), I32)],
        compiler_params=_cp(("parallel",)),
        name="router",
    )(logits)


def _window_rows(rng_ref, k, j, ncap):
    rows = []
    for e in range(N_EXPERTS):
        lo = rng_ref[e, k]
        lo_al = lax.shift_left(lax.shift_right_logical(lo, SLOT_ALIGN.bit_length() - 1), SLOT_ALIGN.bit_length() - 1)
        rows.append(pl.multiple_of(jnp.minimum(lo_al + j * SLOT_WINDOW, ncap), SLOT_ALIGN))
    return rows


def _window_hits(rows, slot_ref, cols):
    wrow = lax.broadcasted_iota(I32, (SLOT_WINDOW, 1), 0)
    return [(wrow + rows[e]).astype(F32) == slot_ref[e:e + 1, cols] for e in range(N_EXPERTS)]


def _moe_gather_body(rng_ref, hm_ref, slot_ref, gate_ref, xg_ref, gv_ref, *, ncap, tb, nsub):
    kk = pl.program_id(1)
    W = SLOT_WINDOW

    @pl.when(kk == 0)
    def _():
        xg_ref[...] = jnp.zeros_like(xg_ref)
        gv_ref[...] = jnp.zeros_like(gv_ref)

    for sub in range(nsub):
        k = kk * nsub + sub
        cols = slice(sub * tb, (sub + 1) * tb)

        def window(j, carry, k=k, cols=cols):
            rows = _window_rows(rng_ref, k, j, ncap)
            hits = _window_hits(rows, slot_ref, cols)
            hit_all = jnp.concatenate([h.astype(BF16) for h in hits], axis=0)
            xgw = jnp.dot(hit_all, hm_ref[cols, :], preferred_element_type=F32).astype(BF16)
            for e in range(N_EXPERTS):
                r = pl.ds(rows[e], W)
                xg_ref[e, r, :] += xgw[e * W:(e + 1) * W, :]
                gv_ref[e, r, :] += jnp.sum(jnp.where(hits[e], gate_ref[e:e + 1, cols], 0.0), axis=1, keepdims=True)
            return carry

        lax.fori_loop(0, rng_ref[0, 127], window, 0)


def _moe_ffn_body(xg_ref, gv_ref, wg_ref, wu_ref, wd_ref, yw_ref, wgs_ref, wus_ref, wds_ref, *, ncap):
    @pl.when(pl.program_id(1) == 0)
    def _():
        wgs_ref[...] = wg_ref[...].astype(BF16)
        wus_ref[...] = wu_ref[...].astype(BF16)
        wds_ref[...] = wd_ref[...].astype(BF16)

    nbp = xg_ref.shape[0]
    xg = jnp.concatenate([xg_ref[p, 0:ncap, :] for p in range(nbp)], axis=0)
    gv = jnp.concatenate([gv_ref[p, 0:ncap, :] for p in range(nbp)], axis=0)
    hid = _silu(jnp.dot(xg, wgs_ref[...], preferred_element_type=F32)) * jnp.dot(
        xg, wus_ref[...], preferred_element_type=F32)
    ye = (jnp.dot(hid.astype(BF16), wds_ref[...], preferred_element_type=F32) * gv).astype(yw_ref.dtype)
    for p in range(nbp):
        yw_ref[p, 0:ncap, :] = ye[p * ncap:(p + 1) * ncap, :]
        yw_ref[p, ncap:, :] = jnp.zeros((yw_ref.shape[1] - ncap, yw_ref.shape[2]), yw_ref.dtype)


def _moe_scatter_body(rng_ref, slot_ref, yw_ref, x_ref, gl_ref, gc_ref, fg_ref, o_ref,
                      *, ncap, nbc, final, tb, nsub):
    kk = pl.program_id(1)
    W = SLOT_WINDOW
    D = o_ref.shape[1]

    for sub in range(nsub):
        k = kk * nsub + sub
        cols = slice(sub * tb, (sub + 1) * tb)

        def window(j, acc, k=k, cols=cols):
            rows = _window_rows(rng_ref, k, j, ncap)
            hit_all = jnp.concatenate([h.astype(BF16) for h in _window_hits(rows, slot_ref, cols)], axis=0)
            yw = jnp.concatenate([yw_ref[e, pl.ds(rows[e], W), :] for e in range(N_EXPERTS)], axis=0)
            return acc + lax.dot_general(hit_all, yw, (((0,), (0,)), ((), ())), preferred_element_type=F32)

        def emit(k=k, cols=cols, window=window):
            acc = lax.fori_loop(0, rng_ref[0, 127], window, jnp.zeros((tb, D), F32))
            x = x_ref[cols, :] + jnp.where(k < nbc, gc_ref[...], gl_ref[...]) * acc
            if final:
                x = x * lax.rsqrt(jnp.mean(x * x, axis=-1, keepdims=True) + RMS_EPS) * fg_ref[...]
            o_ref[cols, :] = x

        if final:
            pl.when(k >= nbc)(emit)
        else:
            emit()


def _experts(hm, slot, gate, rng, wg, wu, wd, li, xres, mods3, B, S, Sc, ctx_out, final_g):
    N, D = hm.shape
    _, E, _, Fd = wg.shape
    T = S - Sc
    tb = _token_block(S, Sc)
    nblk, nbc = S // tb, Sc // tb
    ncap = EC_CAPACITY * T // N_EXPERTS + (EC_CAPACITY * Sc // N_EXPERTS if ctx_out else 0)
    assert ncap % SLOT_ALIGN == 0
    rows = ncap + SLOT_WINDOW
    smem = pl.BlockSpec((None, E, 128), lambda b_, k: (b_, 0, 0), memory_space=pltpu.SMEM)
    nsub = 3 if nblk % 3 == 0 else 1
    es = pl.BlockSpec((None, E, nsub * tb), lambda b_, k: (b_, 0, k))
    xg, gv = pl.pallas_call(
        functools.partial(_moe_gather_body, ncap=ncap, tb=tb, nsub=nsub),
        grid=(B, nblk // nsub),
        in_specs=[smem, pl.BlockSpec((nsub * tb, D), lambda b_, k: (b_ * (nblk // nsub) + k, 0)), es, es],
        out_specs=[pl.BlockSpec((None, E, rows, D), lambda b_, k: (b_, 0, 0, 0)),
                   pl.BlockSpec((None, E, rows, 1), lambda b_, k: (b_, 0, 0, 0))],
        out_shape=[_sds((B, E, rows, D), BF16), _sds((B, E, rows, 1), F32)],
        compiler_params=_cp(("parallel", "arbitrary"), 48),
        name="moe_gather",
    )(rng, hm, slot, gate)
    wspec = lambda a, b: pl.BlockSpec((None, None, a, b), lambda e, b_: (li, e, 0, 0))
    nbp = math.gcd(B, 2)
    yw = pl.pallas_call(
        functools.partial(_moe_ffn_body, ncap=ncap),
        grid=(E, B // nbp),
        in_specs=[pl.BlockSpec((nbp, None, rows, D), lambda e, b_: (b_, e, 0, 0)),
                  pl.BlockSpec((nbp, None, rows, 1), lambda e, b_: (b_, e, 0, 0)),
                  wspec(D, Fd), wspec(D, Fd), wspec(Fd, D)],
        out_specs=pl.BlockSpec((nbp, None, rows, D), lambda e, b_: (b_, e, 0, 0)),
        out_shape=_sds((B, E, rows, D), BF16),
        scratch_shapes=[pltpu.VMEM((D, Fd), BF16), pltpu.VMEM((D, Fd), BF16), pltpu.VMEM((Fd, D), BF16)],
        compiler_params=_cp(("parallel", "arbitrary"), 48),
        name="moe_ffn",
    )(xg, gv, wg, wu, wd)
    final = final_g is not None
    nl = nblk - nbc
    if final:
        nsub = 1
        out_spec = pl.BlockSpec((tb, D), lambda b_, k: (b_ * nl + jnp.maximum(k - nbc, 0), 0))
        out_shape = _sds((B * nl * tb, D), F32)
    else:
        out_spec = pl.BlockSpec((nsub * tb, D), lambda b_, k: (b_ * (nblk // nsub) + k, 0))
        out_shape = _sds((N, D), F32)
        final_g = jnp.ones((1, D), F32)
    nstep = nblk // nsub
    es = pl.BlockSpec((None, E, nsub * tb), lambda b_, k: (b_, 0, k))
    return pl.pallas_call(
        functools.partial(_moe_scatter_body, ncap=ncap, nbc=nbc, final=final, tb=tb, nsub=nsub),
        grid=(B, nstep),
        in_specs=[smem, es, pl.BlockSpec((None, E, rows, D), lambda b_, k: (b_, 0, 0, 0)),
                  pl.BlockSpec((nsub * tb, D), lambda b_, k: (b_ * nstep + k, 0)),
                  pl.BlockSpec((None, 1, D), lambda b_, k: (b_, 0, 5)),
                  pl.BlockSpec((None, 1, D), lambda b_, k: (B, 0, 5)),
                  pl.BlockSpec((1, D), lambda b_, k: (0, 0))],
        out_specs=out_spec,
        out_shape=out_shape,
        compiler_params=_cp(("parallel", "arbitrary"), 48),
        name="moe_scatter",
    )(rng, slot, yw, xres, mods3, mods3, final_g)


def _pack_w_in(w_in):
    D = w_in.shape[0]
    seg = lambda a, n: w_in[:, a:a + n]
    o_q, o_k, o_v, o_z = 0, ATTN_Q_DIM, ATTN_Q_DIM + ATTN_KV_DIM, ATTN_Q_DIM + 2 * ATTN_KV_DIM
    o_xbc = o_z + SSD_INNER
    o_dt = o_xbc + SSD_CONV_DIM
    o_u = o_dt + SSD_HEADS
    o_sb, o_sg, o_sh = o_u + S5_WIDTH, o_u + S5_WIDTH + SC_WIDTH, o_u + S5_WIDTH + 2 * SC_WIDTH
    o_g = o_sh + SC_WIDTH
    w_p = jnp.concatenate([seg(o_g, 4 * D), seg(o_q, ATTN_Q_DIM), seg(o_z, SSD_INNER), seg(o_k, ATTN_KV_DIM),
                           seg(o_v, ATTN_KV_DIM), seg(o_u, S5_WIDTH), seg(o_sb, SC_WIDTH), seg(o_sg, SC_WIDTH),
                           seg(o_sh, SC_WIDTH), seg(o_xbc, SSD_CONV_DIM)], axis=1).astype(BF16)
    w_dt = jnp.pad(seg(o_dt, SSD_HEADS), ((0, 0), (0, 128 - SSD_HEADS))).astype(BF16)
    return w_p, w_dt


def _layer(xa, mods3, lp, ew, li, tabs, B, S, Sc, ctx_out, final_g):
    N, D = xa.shape
    rc = _row_cfg(S, Sc)
    w_p, w_dt = _pack_w_in(lp["w_in"])
    proj_a, dt = _in_proj(xa, lp["norm_mix_g"][None, :], mods3, w_p, w_dt, B, S, Sc)

    ya = _attention(proj_a, tabs, lp["q_norm_g"], lp["k_norm_g"], B, S, Sc)

    xbc_c = _ssdconv(proj_a, lp["ssd_conv_w"], lp["ssd_conv_b"][None, :], B, S, Sc)
    yf, yb = _ssd(xbc_c, dt, lp["ssd_dt_bias"], lp["ssd_a_log"], B, S, Sc)

    Lc, Gn, G = S5_CHUNK, S5_GROUPS, S5_GROUP_DIM
    nc5, ncc5 = S // Lc, Sc // Lc
    w5, so5, ar5 = _s5gen(lp["s5_lambda_re"], lp["s5_lambda_im"], lp["s5_log_step"], lp["s5_b_re"], lp["s5_b_im"],
                          lp["s5_c_re"], lp["s5_c_im"], lp["s5_d"])
    u = proj_a.reshape(B, nc5, Lc, PA_W)[:, :, :, PA_U:PA_U + S5_WIDTH]
    up = jnp.transpose(u, (2, 1, 0, 3)).reshape(N, S5_WIDTH)
    tr5 = nc5 * B
    y5t = _s5(_transpose2d(up, tr5, "s5_in_t"), w5, so5, ar5, B, nc5, ncc5)
    y5p = _untranspose2d(y5t, tr5, "s5_out_t").reshape(Lc, nc5, B, S5_WIDTH)
    y5 = jnp.transpose(y5p, (2, 1, 0, 3)).reshape(N, S5_WIDTH)

    yd = _shortconv(proj_a, lp["sc_conv_w"], B, S, Sc)

    dsk = jnp.repeat(lp["ssd_d"], SSD_HEAD_DIM)[None, :]
    wr_pad = jnp.pad(lp["w_router"], ((0, 0), (0, 128 - N_EXPERTS)))
    x1, hm, logits = _merge(xa, ya, yf, yb, xbc_c, proj_a, y5, yd, mods3,
                            lp["w_br_attn"].astype(BF16), lp["w_br_ssd"].astype(BF16), lp["w_br_s5"].astype(BF16),
                            lp["w_br_sc"].astype(BF16), lp["w_out"].astype(BF16), lp["s5_w_glu"].astype(BF16),
                            lp["s5_b_glu"][None, :], lp["ssd_norm_g"][None, :], dsk,
                            lp["norm_ffn_g"][None, :], wr_pad, B, rc)
    slot, gate, rng = _router(logits, B, S, Sc, ctx_out)
    return _experts(hm, slot, gate, rng, ew[0], ew[1], ew[2], li, x1, mods3, B, S, Sc, ctx_out, final_g)


def kernel(x, c, ctx, c_ctx, w_mod, b_mod, norm_mix_g, norm_ffn_g, w_in, q_norm_g, k_norm_g, ssd_conv_w, ssd_conv_b, ssd_dt_bias, ssd_a_log, ssd_d, ssd_norm_g, s5_lambda_re, s5_lambda_im, s5_log_step, s5_b_re, s5_b_im, s5_c_re, s5_c_im, s5_d, s5_w_glu, s5_b_glu, sc_conv_w, w_br_attn, w_br_ssd, w_br_s5, w_br_sc, w_out, w_router, w_exp_gate, w_exp_up, w_exp_down, final_norm_g):
    B, T, D = x.shape
    Sc = ctx.shape[1]
    S = Sc + T
    depth = w_in.shape[0]
    xa = jnp.concatenate([ctx, x], axis=1).reshape(B * S, D)
    cc = jnp.zeros((16, D), F32).at[0:B].set(c).at[B].set(c_ctx)
    cos, sin = _rope_tables(T, Sc)
    tabs = (jnp.tile(cos, (1, ATTN_HEADS)), jnp.tile(sin, (1, ATTN_HEADS)),
            jnp.tile(cos, (1, ATTN_KV_HEADS)), jnp.tile(sin, (1, ATTN_KV_HEADS)),
            _block_diag_ones(ATTN_Q_DIM), _block_diag_ones(ATTN_KV_DIM))
    stacked = dict(
        w_in=w_in, norm_mix_g=norm_mix_g, norm_ffn_g=norm_ffn_g, q_norm_g=q_norm_g, k_norm_g=k_norm_g,
        ssd_conv_w=ssd_conv_w, ssd_conv_b=ssd_conv_b, ssd_dt_bias=ssd_dt_bias, ssd_a_log=ssd_a_log, ssd_d=ssd_d,
        ssd_norm_g=ssd_norm_g, s5_lambda_re=s5_lambda_re, s5_lambda_im=s5_lambda_im, s5_log_step=s5_log_step,
        s5_b_re=s5_b_re, s5_b_im=s5_b_im, s5_c_re=s5_c_re, s5_c_im=s5_c_im, s5_d=s5_d, s5_w_glu=s5_w_glu,
        s5_b_glu=s5_b_glu, sc_conv_w=sc_conv_w, w_br_attn=w_br_attn, w_br_ssd=w_br_ssd, w_br_s5=w_br_s5,
        w_br_sc=w_br_sc, w_out=w_out, w_router=w_router)
    ew = (w_exp_gate, w_exp_up, w_exp_down)
    mods_all = _mods(cc, w_mod, b_mod[:, None, :])
    for i in range(depth):
        lp = {k: v[i] for k, v in stacked.items()}
        mods3 = mods_all[i].reshape(16, 1, 6 * D)
        last = i == depth - 1
        xa = _layer(xa, mods3, lp, ew, i, tabs, B, S, Sc, ctx_out=not last,
                    final_g=final_norm_g[None, :] if last else None)
    return xa.reshape(B, T, D)
```

```python
import functools
import math

import jax
import jax.numpy as jnp
import numpy as np
from jax import lax
from jax.experimental import pallas as pl
from jax.experimental.pallas import tpu as pltpu

F32 = jnp.float32
BF16 = jnp.bfloat16
I32 = jnp.int32
HI = lax.Precision.HIGHEST

RMS_EPS = 1e-6
GRID_W = 64
ROPE_THETA = 10000.0
HEAD_DIM = 64
ATTN_HEADS = 8
ATTN_KV_HEADS = 2
SSD_HEADS = 8
SSD_HEAD_DIM = 64
SSD_GROUPS = 2
SSD_STATE = 64
SSD_CHUNK = 128
S5_GROUPS = 24
S5_GROUP_DIM = 16
S5_STATE = 64
S5_CHUNK = 16
SC_WIDTH = 384
N_EXPERTS = 16
EC_CAPACITY = 2
MAX_ROW_BLOCK = 256
SLOT_WINDOW = 64
SLOT_ALIGN = 16

ATTN_Q_DIM = ATTN_HEADS * HEAD_DIM
ATTN_KV_DIM = ATTN_KV_HEADS * HEAD_DIM
SSD_INNER = SSD_HEADS * SSD_HEAD_DIM
SSD_BC_DIM = SSD_GROUPS * SSD_STATE
SSD_CONV_DIM = SSD_INNER + 2 * SSD_BC_DIM
S5_WIDTH = S5_GROUPS * S5_GROUP_DIM

PA_G, PA_Q, PA_Z, PA_K, PA_V = 0, 4096, 4608, 5120, 5248
PA_U, PA_SB, PA_SG, PA_SH, PA_XBC, PA_W = 5376, 5760, 6144, 6528, 6912, 7680


def _sds(shape, dtype):
    return jax.ShapeDtypeStruct(shape, dtype)


def _cp(sem, vmem_mb=None):
    kw = dict(dimension_semantics=sem)
    if vmem_mb is not None:
        kw["vmem_limit_bytes"] = vmem_mb << 20
    return pltpu.CompilerParams(**kw)


def _sigmoid(x):
    return 0.5 * jnp.tanh(0.5 * x) + 0.5


def _silu(x):
    return x * _sigmoid(x)


def _mods_body(c_ref, w_ref, b_ref, o_ref):
    s = _silu(c_ref[...])
    o_ref[...] = jnp.dot(s.astype(BF16), w_ref[...].astype(BF16), preferred_element_type=F32) + b_ref[...]


def _mods(cc, w, b):
    R, D = cc.shape
    L, _, N = w.shape
    tn = 1536
    return pl.pallas_call(
        _mods_body,
        grid=(L, N // tn),
        in_specs=[pl.BlockSpec((R, D), lambda l, j: (0, 0)),
                  pl.BlockSpec((None, D, tn), lambda l, j: (l, 0, j)),
                  pl.BlockSpec((None, 1, tn), lambda l, j: (l, 0, j))],
        out_specs=pl.BlockSpec((None, R, tn), lambda l, j: (l, 0, j)),
        out_shape=_sds((L, R, N), F32),
        compiler_params=_cp(("parallel", "parallel"), 40),
        name="mods",
    )(cc, w, b)


def _row_cfg(S, Sc):
    tm = math.gcd(math.gcd(Sc, S - Sc), MAX_ROW_BLOCK)
    return tm, S // tm, Sc // tm


def _mod_spec(D, rc, B, chunk):
    _, nb, nbc = rc
    return pl.BlockSpec((None, 1, D), lambda i: (jnp.where(i % nb < nbc, B, i // nb), 0, chunk))


def _in_proj_body(x_ref, g_ref, shl_ref, scl_ref, shc_ref, scc_ref, w_ref, wdt_ref, cs_ref, o_ref, dt_ref, hn_ref,
                  *, tm, nb, Sc):
    j = pl.program_id(1)

    @pl.when(j == 0)
    def _():
        nh = 2 if tm % 32 == 0 else 1
        for p in range(nh):
            r = slice(p * tm // nh, (p + 1) * tm // nh)
            x = x_ref[r, :]
            ms = jnp.mean(x * x, axis=-1, keepdims=True)
            y = x * lax.rsqrt(ms + RMS_EPS) * g_ref[...]
            t = (pl.program_id(0) % nb) * tm + p * tm // nh + lax.broadcasted_iota(I32, (tm // nh, 1), 0)
            isc = t < Sc
            hn = y * (1.0 + jnp.where(isc, scc_ref[...], scl_ref[...])) + jnp.where(isc, shc_ref[...], shl_ref[...])
            hn_ref[r, :] = hn.astype(BF16)
            o_ref[r, :] = (jnp.dot(hn_ref[r, :], w_ref[...], preferred_element_type=F32)
                           * cs_ref[...]).astype(o_ref.dtype)
        dt_ref[...] = jnp.dot(hn_ref[...], wdt_ref[...], preferred_element_type=F32)

    @pl.when(j > 0)
    def _():
        o_ref[...] = (jnp.dot(hn_ref[...], w_ref[...], preferred_element_type=F32) * cs_ref[...]).astype(o_ref.dtype)


def _in_proj(xa, g, mods3, w_p, w_dt, B, S, Sc):
    N, D = xa.shape
    tm = S // 2 if (S // 2) % 128 == 0 else S
    nb = S // tm
    tn = PA_W // 3
    assert tn % 256 == 0
    col_scale = jnp.asarray(np.where(np.arange(PA_W) < PA_Q, 0.5, 1.0)[None, :], F32)
    lat = lambda ch: pl.BlockSpec((None, 1, D), lambda i, j: (i // nb, 0, ch))
    ctx = lambda ch: pl.BlockSpec((None, 1, D), lambda i, j: (B, 0, ch))
    return pl.pallas_call(
        functools.partial(_in_proj_body, tm=tm, nb=nb, Sc=Sc),
        grid=(N // tm, PA_W // tn),
        in_specs=[pl.BlockSpec((tm, D), lambda i, j: (i, 0)),
                  pl.BlockSpec((1, D), lambda i, j: (0, 0)),
                  lat(0), lat(1), ctx(0), ctx(1),
                  pl.BlockSpec((D, tn), lambda i, j: (0, j)),
                  pl.BlockSpec((D, 128), lambda i, j: (0, 0)),
                  pl.BlockSpec((1, tn), lambda i, j: (0, j))],
        out_specs=[pl.BlockSpec((tm, tn), lambda i, j: (i, j)), pl.BlockSpec((tm, 128), lambda i, j: (i, 0))],
        out_shape=[_sds((N, PA_W), BF16), _sds((N, 128), F32)],
        scratch_shapes=[pltpu.VMEM((tm, D), BF16)],
        compiler_params=_cp(("parallel", "arbitrary"), 56),
        name="in_proj",
    )(xa, g, mods3, mods3, mods3, mods3, w_p, w_dt, col_scale)


def _shifted(x, Sc):
    S = x.shape[0]
    t = lax.broadcasted_iota(I32, (S, 1), 0)
    prev = jnp.where((t == 0) | (t == Sc), 0.0, pltpu.roll(x, 1, 0))
    nxt = jnp.where((t == Sc - 1) | (t == S - 1), 0.0, pltpu.roll(x, S - 1, 0))
    return prev, nxt


def _ssdconv_body(x_ref, w_ref, b_ref, o_ref, *, Sc):
    x = x_ref[...].astype(F32)
    prev, nxt = _shifted(x, Sc)
    w = w_ref[...]
    y = w[0:1, :] * prev + w[1:2, :] * x + w[2:3, :] * nxt + b_ref[...]
    o_ref[...] = _silu(y).astype(o_ref.dtype)


def _ssdconv(proj_a, w, b, B, S, Sc):
    C = 384
    j0 = PA_XBC // C
    return pl.pallas_call(
        functools.partial(_ssdconv_body, Sc=Sc),
        grid=(B, SSD_CONV_DIM // C),
        in_specs=[pl.BlockSpec((S, C), lambda b_, j: (b_, j0 + j)),
                  pl.BlockSpec((3, C), lambda b_, j: (0, j)),
                  pl.BlockSpec((1, C), lambda b_, j: (0, j))],
        out_specs=pl.BlockSpec((S, C), lambda b_, j: (b_, j)),
        out_shape=_sds((B * S, SSD_CONV_DIM), BF16),
        compiler_params=_cp(("parallel", "parallel"), 40),
        name="ssdconv",
    )(proj_a, w, b)


def _shortconv_body(sb_ref, sg_ref, sh_ref, w_ref, o_ref, *, Sc):
    x = sg_ref[...].astype(F32) * sh_ref[...].astype(F32)
    prev, nxt = _shifted(x, Sc)
    w = w_ref[...]
    y = w[0:1, :] * prev + w[1:2, :] * x + w[2:3, :] * nxt
    o_ref[...] = (sb_ref[...].astype(F32) * y).astype(o_ref.dtype)


def _shortconv(proj_a, w, B, S, Sc):
    C = SC_WIDTH
    return pl.pallas_call(
        functools.partial(_shortconv_body, Sc=Sc),
        grid=(B,),
        in_specs=[pl.BlockSpec((S, C), lambda b_: (b_, PA_SB // C)),
                  pl.BlockSpec((S, C), lambda b_: (b_, PA_SG // C)),
                  pl.BlockSpec((S, C), lambda b_: (b_, PA_SH // C)),
                  pl.BlockSpec((3, C), lambda b_: (0, 0))],
        out_specs=pl.BlockSpec((S, C), lambda b_: (b_, 0)),
        out_shape=_sds((B * S, C), BF16),
        compiler_params=_cp(("parallel",), 40),
        name="shortconv",
    )(proj_a, proj_a, proj_a, w)


def _norm_rope(x, g, cos, sin, bd, scale):
    W = x.shape[1]
    sq = x * x
    hi = sq.astype(BF16)
    lo = (sq - hi.astype(F32)).astype(BF16)
    ssq = jnp.dot(hi, bd, preferred_element_type=F32) + jnp.dot(lo, bd, preferred_element_type=F32)
    y = x * lax.rsqrt(ssq * (1.0 / HEAD_DIM) + RMS_EPS) * g
    lane = lax.broadcasted_iota(I32, (1, W), 1)
    first = (lane % 32) < 16
    partner = jnp.where(first, pltpu.roll(y, W - 16, 1), pltpu.roll(y, 16, 1))
    return (y * cos + partner * sin) * scale


def _attn_body(q_ref, k_ref, v_ref, cq_ref, sq_ref, ck_ref, sk_ref, gq_ref, gk_ref, bdq_ref, bdk_ref,
               o_ref, kh_ref, ve_ref, *, Sc, S, tq):
    qb = pl.program_id(1)
    hd = HEAD_DIM

    @pl.when(qb == 0)
    def _():
        k = k_ref[...].astype(F32)
        kh_ref[...] = _norm_rope(k, gk_ref[...], ck_ref[...], sk_ref[...], bdk_ref[...], 1.0).astype(BF16)
        v = v_ref[...].astype(F32)
        lane = lax.broadcasted_iota(I32, (1, 2 * hd), 1)
        ve_ref[0] = jnp.where(lane < hd, v, 1.0).astype(BF16)
        ve_ref[1] = jnp.where(lane < hd, pltpu.roll(v, hd, 1), 1.0).astype(BF16)

    q = q_ref[...].astype(F32)
    qh = _norm_rope(q, gq_ref[...], cq_ref[...], sq_ref[...], bdq_ref[...],
                    HEAD_DIM ** -0.5 * math.log2(math.e)).astype(BF16)
    rep = ATTN_HEADS // ATTN_KV_HEADS
    nt = (((1,), (1,)), ((), ()))

    def attend(splits):
        scores = []
        for g in range(ATTN_KV_HEADS):
            qg = jnp.concatenate([qh[:, (g * rep + r) * hd:(g * rep + r + 1) * hd] for r in range(rep)], axis=0)
            scores.append([lax.dot_general(qg, kh_ref[a:b, g * hd:(g + 1) * hd], nt, preferred_element_type=F32)
                           for a, b in splits])
        for g in range(ATTN_KV_HEADS):
            m = functools.reduce(jnp.maximum, [jnp.max(s, axis=-1, keepdims=True) for s in scores[g]])
            acc = None
            for s, (a, b) in zip(scores[g], splits):
                p = jnp.exp2(s - m).astype(BF16)
                part = jnp.dot(p, ve_ref[g, a:b, :], preferred_element_type=F32)
                acc = part if acc is None else acc + part
            o = acc[:, 0:hd] / acc[:, hd:hd + 1]
            for r in range(rep):
                h = g * rep + r
                o_ref[:, h * hd:(h + 1) * hd] = o[r * tq:(r + 1) * tq].astype(o_ref.dtype)

    @pl.when(qb < Sc // tq)
    def _():
        attend([(0, Sc)])

    @pl.when(qb >= Sc // tq)
    def _():
        half = (S // 2 + 255) // 256 * 256 if S >= 512 else S
        attend([(0, half), (half, S)] if half < S else [(0, S)])


def _rope_tables(T, Sc):
    rows = T // GRID_W
    row = np.repeat(np.arange(rows, dtype=np.float32), GRID_W)
    col = np.tile(np.arange(GRID_W, dtype=np.float32), rows)
    half = HEAD_DIM // 2
    inv = jnp.asarray(ROPE_THETA, F32) ** (-jnp.arange(0, half, 2, dtype=F32) / half)
    ra = jnp.asarray(row)[:, None] * inv
    ca = jnp.asarray(col)[:, None] * inv
    cos = jnp.concatenate([jnp.cos(ra), jnp.cos(ra), jnp.cos(ca), jnp.cos(ca)], axis=1)
    sin = jnp.concatenate([-jnp.sin(ra), jnp.sin(ra), -jnp.sin(ca), jnp.sin(ca)], axis=1)
    cos = jnp.concatenate([jnp.ones((Sc, HEAD_DIM), F32), cos], axis=0)
    sin = jnp.concatenate([jnp.zeros((Sc, HEAD_DIM), F32), sin], axis=0)
    return cos, sin


def _block_diag_ones(W):
    i = np.arange(W) // HEAD_DIM
    return jnp.asarray((i[:, None] == i[None, :]).astype(np.float32), BF16)


def _attention(proj_a, tabs, q_norm_g, k_norm_g, B, S, Sc):
    cos_q, sin_q, cos_k, sin_k, bdq, bdk = tabs
    tq = math.gcd(Sc, 256)
    nq = S // tq
    gq = jnp.tile(q_norm_g, ATTN_HEADS)[None, :]
    gk = jnp.tile(k_norm_g, ATTN_KV_HEADS)[None, :]
    const = lambda b_, i: (0, 0)
    return pl.pallas_call(
        functools.partial(_attn_body, Sc=Sc, S=S, tq=tq),
        grid=(B, nq),
        in_specs=[pl.BlockSpec((tq, ATTN_Q_DIM), lambda b_, i: (b_ * nq + i, PA_Q // ATTN_Q_DIM)),
                  pl.BlockSpec((S, ATTN_KV_DIM), lambda b_, i: (b_, PA_K // ATTN_KV_DIM)),
                  pl.BlockSpec((S, ATTN_KV_DIM), lambda b_, i: (b_, PA_V // ATTN_KV_DIM)),
                  pl.BlockSpec((tq, ATTN_Q_DIM), lambda b_, i: (i, 0)),
                  pl.BlockSpec((tq, ATTN_Q_DIM), lambda b_, i: (i, 0)),
                  pl.BlockSpec((S, ATTN_KV_DIM), const),
                  pl.BlockSpec((S, ATTN_KV_DIM), const),
                  pl.BlockSpec((1, ATTN_Q_DIM), const),
                  pl.BlockSpec((1, ATTN_KV_DIM), const),
                  pl.BlockSpec((ATTN_Q_DIM, ATTN_Q_DIM), const),
                  pl.BlockSpec((ATTN_KV_DIM, ATTN_KV_DIM), const)],
        out_specs=pl.BlockSpec((tq, ATTN_Q_DIM), lambda b_, i: (b_ * nq + i, 0)),
        out_shape=_sds((B * S, ATTN_Q_DIM), BF16),
        scratch_shapes=[pltpu.VMEM((S, ATTN_KV_DIM), BF16),
                        pltpu.VMEM((ATTN_KV_HEADS, S, 2 * HEAD_DIM), BF16)],
        compiler_params=_cp(("parallel", "arbitrary"), 56),
        name="attention",
    )(proj_a, proj_a, proj_a, cos_q, sin_q, cos_k, sin_k, gq, gk, bdq, bdk)


def _ssd_chains(chains):
    L = SSD_CHUNK
    P, Nst = SSD_HEAD_DIM, SSD_STATE
    rep = SSD_HEADS // SSD_GROUPS
    row = lax.broadcasted_iota(I32, (L, L), 0)
    col = lax.broadcasted_iota(I32, (L, L), 1)
    tri = (row >= col).astype(F32)
    lane = lax.broadcasted_iota(I32, (1, 2 * P), 1)
    lo_half = lane < P
    blockdiag = (row // Nst) == (col // P)
    upper = lax.broadcasted_iota(I32, (2 * Nst, 1), 0) < Nst
    st = []
    for xbc_ref, dt_ref, bias, a_neg, h_ref, y_ref, rev in chains:
        xbc = xbc_ref[...]
        bm = xbc[:, SSD_INNER:SSD_INNER + SSD_BC_DIM]
        cm = xbc[:, SSD_INNER + SSD_BC_DIM:SSD_CONV_DIM]
        dt = jax.nn.softplus(dt_ref[...] + bias)
        a = dt * a_neg
        cum = jnp.dot(tri, a, precision=HI, preferred_element_type=F32)
        e = cum - a if rev else cum
        tot_row = cum[L - 1:L, :]
        win_c = jnp.exp(tot_row - e) if rev else jnp.exp(e)
        win_hi = win_c.astype(BF16)
        cm32 = cm.astype(F32)
        st.append(dict(
            rev=rev, h_ref=h_ref, y_ref=y_ref, xbc=xbc, cm=cm, cm32=cm32, cm_rolled=pltpu.roll(cm32, P, 1),
            cum=cum, e=e, eT=e.T, dtT=dt.T, bT=bm.astype(F32).T.astype(BF16),
            mask=(col >= row) if rev else (row >= col),
            win_hi=win_hi, win_lo=(win_c - win_hi.astype(F32)).astype(BF16)))
    for g in range(SSD_GROUPS):
        own = (lane // P) == g
        for c in st:
            cg0 = jnp.where(own, c["cm"], jnp.zeros_like(c["cm"]))
            c["sg"] = jnp.dot(cg0, c["bT"], preferred_element_type=F32)
            c["cdup"] = jnp.where(own, c["cm32"], c["cm_rolled"])
            c["bTg"] = c["bT"][g * Nst:(g + 1) * Nst, :].astype(F32)
        for q in range(rep // 2):
            pq = (g * rep) // 2 + q
            hs = (2 * pq, 2 * pq + 1)
            sel = (lax.broadcasted_iota(I32, (2 * P, 2 * P), 0) == hs[0] + col // P).astype(BF16)
            for c in st:
                rev, e, eT, cum, dtT = c["rev"], c["e"], c["eT"], c["cum"], c["dtT"]
                ms, bws, tots = [], [], []
                for h in hs:
                    ecol = e[:, h:h + 1]
                    erow = eT[h:h + 1, :]
                    tot = cum[L - 1:L, h:h + 1]
                    diff = (erow - ecol) if rev else (ecol - erow)
                    dec = jnp.exp(jnp.where(c["mask"], diff, -1e30))
                    ms.append((c["sg"] * dec * dtT[h:h + 1, :]).astype(BF16))
                    wrow = (jnp.exp(erow) if rev else jnp.exp(tot - erow)) * dtT[h:h + 1, :]
                    bws.append(c["bTg"] * wrow)
                    tots.append(jnp.exp(tot))
                xp = c["xbc"][:, 2 * P * pq:2 * P * (pq + 1)]
                zero = jnp.zeros_like(xp)
                xbd = jnp.concatenate([jnp.where(lo_half, xp, zero), jnp.where(lo_half, zero, xp)], axis=0)
                y = jnp.dot(jnp.concatenate(ms, axis=1), xbd, preferred_element_type=F32)
                wexp = (jnp.dot(c["win_hi"], sel, preferred_element_type=F32)
                        + jnp.dot(c["win_lo"], sel, preferred_element_type=F32))
                hp = c["h_ref"][pq]
                y = y + jnp.dot((wexp * c["cdup"]).astype(BF16), hp.astype(BF16), preferred_element_type=F32)
                upd = jnp.dot(jnp.concatenate(bws, axis=0).astype(BF16), xp, preferred_element_type=F32)
                keep = jnp.where(upper, tots[0], tots[1])
                c["h_ref"][pq] = jnp.where(blockdiag, keep * hp + upd, 0.0)
                c["y_ref"][:, 2 * P * pq:2 * P * (pq + 1)] = y.astype(c["y_ref"].dtype)


def _ssd_body(xf_ref, dtf_ref, xb_ref, dtb_ref, bias_ref, alog_ref, yf_ref, yb_ref, *h_refs, nper):
    @pl.when(pl.program_id(1) == 0)
    def _():
        for h_ref in h_refs:
            h_ref[...] = jnp.zeros_like(h_ref)

    lane = lax.broadcasted_iota(I32, (1, 128), 1)
    a_neg = jnp.where(lane < SSD_HEADS, -jnp.exp(alog_ref[...]), 0.0)
    bias = bias_ref[...]
    chains = []
    for p in range(nper):
        chains.append((xf_ref.at[p], dtf_ref.at[p], bias[0:1, :], a_neg[0:1, :], h_refs[2 * p], yf_ref.at[p], False))
        chains.append((xb_ref.at[p], dtb_ref.at[p], bias[1:2, :], a_neg[1:2, :], h_refs[2 * p + 1], yb_ref.at[p], True))
    _ssd_chains(chains)


def _ssd(xbc_c, dt, dt_bias, a_log, B, S, Sc):
    L = SSD_CHUNK
    nc, ncc = S // L, Sc // L
    nper = math.gcd(B, 4)
    bg = B // nper
    padh = 128 - SSD_HEADS
    bias = jnp.pad(dt_bias, ((0, 0), (0, padh)))
    alog = jnp.pad(a_log, ((0, 0), (0, padh)))
    x3 = xbc_c.reshape(nper, bg * S, SSD_CONV_DIM)
    dt3 = dt.reshape(nper, bg * S, 128)

    def fwd(b_, i):
        return (0, b_ * nc + i, 0)

    def bwd(b_, i):
        return (0, b_ * nc + jnp.where(i < ncc, ncc - 1 - i, nc + ncc - 1 - i), 0)

    const = lambda b_, i: (0, 0)
    out = _sds((nper, bg * S, SSD_INNER), BF16)
    yf, yb = pl.pallas_call(
        functools.partial(_ssd_body, nper=nper),
        grid=(bg, nc),
        in_specs=[pl.BlockSpec((nper, L, SSD_CONV_DIM), fwd), pl.BlockSpec((nper, L, 128), fwd),
                  pl.BlockSpec((nper, L, SSD_CONV_DIM), bwd), pl.BlockSpec((nper, L, 128), bwd),
                  pl.BlockSpec((2, 128), const), pl.BlockSpec((2, 128), const)],
        out_specs=[pl.BlockSpec((nper, L, SSD_INNER), fwd), pl.BlockSpec((nper, L, SSD_INNER), bwd)],
        out_shape=[out, out],
        scratch_shapes=[pltpu.VMEM((SSD_HEADS // 2, 2 * SSD_STATE, 2 * SSD_HEAD_DIM), F32)] * (2 * nper),
        compiler_params=_cp(("parallel", "arbitrary")),
        name="ssd",
    )(x3, dt3, x3, dt3, bias, alog)
    return yf.reshape(B * S, SSD_INNER), yb.reshape(B * S, SSD_INNER)


def _cpow(n, lr, li, st):
    mag = jnp.exp(n * lr * st)
    ang = n * li * st
    return mag * jnp.cos(ang), mag * jnp.sin(ang)


def _zoh_coef(lr, li, st):
    ar, ai = _cpow(1.0, lr, li, st)
    nr, ni = ar - 1.0, ai
    den = lr * lr + li * li
    return (nr * lr + ni * li) / den, (ni * lr - nr * li) / den


def _spread_lanes(x, sel):
    hi = x.astype(BF16)
    r1 = x - hi.astype(F32)
    mid = r1.astype(BF16)
    lo = (r1 - mid.astype(F32)).astype(BF16)
    return (jnp.dot(hi, sel, preferred_element_type=F32) + jnp.dot(mid, sel, preferred_element_type=F32)
            + jnp.dot(lo, sel, preferred_element_type=F32))


def _s5gen_body(colp_ref, rowp_ref, ctr_ref, cti_ref, btr_ref, bti_ref, blr_ref, bli_ref, d_ref,
                wt_ref, so_ref, ar_ref, ws_ref, sos_ref):
    Lc, G, P = S5_CHUNK, S5_GROUP_DIM, S5_STATE
    NS = 2 * Lc + 1
    colp = colp_ref[...]
    rowp = rowp_ref[...]
    lrf, lif, stf = colp[:, 0:1], colp[:, 1:2], jnp.exp(colp[:, 2:3])
    lrb, lib, stb = colp[:, 3:4], colp[:, 4:5], jnp.exp(colp[:, 5:6])
    lane = lax.broadcasted_iota(I32, (1, NS * G), 1)
    slot = lane // G
    isb = slot < Lc
    cslot = lax.broadcasted_iota(I32, (1, 128), 1)
    cisb = cslot < Lc
    clag = jnp.where(cslot < NS, jnp.abs(cslot - Lc), 0).astype(F32)
    pr_c, pi_c = _cpow(clag, jnp.where(cisb, lrb, lrf), jnp.where(cisb, lib, lif), jnp.where(cisb, stb, stf))
    srow = lax.broadcasted_iota(I32, (128, 1), 0)
    spread = (srow == slot).astype(BF16)
    pr, pi = _spread_lanes(pr_c, spread), _spread_lanes(pi_c, spread)
    tile = (lax.broadcasted_iota(I32, (G, 1), 0) == lane % G).astype(BF16)
    ctr, cti = _spread_lanes(ctr_ref[...], tile), _spread_lanes(cti_ref[...], tile)
    er = ctr * pr - cti * pi
    ei = ctr * pi + cti * pr
    rlrf, rlif, rstf = rowp[0:1, :], rowp[1:2, :], jnp.exp(rowp[2:3, :])
    rlrb, rlib, rstb = rowp[3:4, :], rowp[4:5, :], jnp.exp(rowp[5:6, :])
    btr, bti = btr_ref[...], bti_ref[...]
    cfr, cfi = _zoh_coef(rlrf, rlif, rstf)
    cbr, cbi = _zoh_coef(rlrb, rlib, rstb)
    bbf_r, bbf_i = cfr * btr - cfi * bti, cfr * bti + cfi * btr
    bbb_r, bbb_i = cbr * btr - cbi * bti, cbr * bti + cbi * btr

    def kt(br, bi):
        return (jnp.dot(br, er, precision=HI, preferred_element_type=F32)
                - jnp.dot(bi, ei, precision=HI, preferred_element_type=F32))

    ktf, ktb = kt(bbf_r, bbf_i), kt(bbb_r, bbb_i)
    ii = lax.broadcasted_iota(I32, (G, NS * G), 0)
    dmat = jnp.where((slot == Lc) & (ii == lane - Lc * G), _spread_lanes(d_ref[...], tile), 0.0)
    strip = jnp.where(slot == Lc, ktf + ktb, jnp.where(isb, ktb, ktf)) + dmat
    for s in range(Lc):
        off = (Lc - s) * G
        ws_ref[s * G:(s + 1) * G, :] = strip[:, off:off + Lc * G]
    fo = (Lc + 1) * G
    ws_ref[Lc * G:Lc * G + P, :] = er[:, fo:fo + Lc * G]
    ws_ref[Lc * G + P:Lc * G + 2 * P, :] = er[:, 0:Lc * G]
    ws_ref[Lc * G + 2 * P:Lc * G + 3 * P, :] = -ei[:, fo:fo + Lc * G]
    ws_ref[Lc * G + 3 * P:Lc * G + 4 * P, :] = -ei[:, 0:Lc * G]
    wt_ref[:, 0:Lc * G] = ws_ref[0:Lc * G, :].T.astype(wt_ref.dtype)
    wt_ref[:, Lc * G:Lc * G + 4 * P] = ws_ref[Lc * G:Lc * G + 4 * P, :].T.astype(wt_ref.dtype)
    s_idx = lax.broadcasted_iota(I32, (1, Lc * G), 1) // G
    sel_f = (srow == 2 * Lc - 1 - s_idx).astype(BF16)
    sel_b = (srow == Lc - s_idx).astype(BF16)
    qfr, qfi = _spread_lanes(pr_c, sel_f), _spread_lanes(pi_c, sel_f)
    qbr, qbi = _spread_lanes(pr_c, sel_b), _spread_lanes(pi_c, sel_b)
    ccfr, ccfi = _zoh_coef(lrf, lif, stf)
    ccbr, ccbi = _zoh_coef(lrb, lib, stb)
    blr, bli = _spread_lanes(blr_ref[...], tile[:, 0:Lc * G]), _spread_lanes(bli_ref[...], tile[:, 0:Lc * G])
    bfr, bfi = ccfr * blr - ccfi * bli, ccfr * bli + ccfi * blr
    bbr, bbi = ccbr * blr - ccbi * bli, ccbr * bli + ccbi * blr
    sos_ref[0:P, :] = qfr * bfr - qfi * bfi
    sos_ref[P:2 * P, :] = qbr * bbr - qbi * bbi
    sos_ref[2 * P:3 * P, :] = qfr * bfi + qfi * bfr
    sos_ref[3 * P:4 * P, :] = qbr * bbi + qbi * bbr
    so_ref[...] = sos_ref[...].T.astype(so_ref.dtype)
    afr, afi = _cpow(float(Lc), rlrf, rlif, rstf)
    abr, abi = _cpow(float(Lc), rlrb, rlib, rstb)
    ar_ref[...] = jnp.zeros_like(ar_ref)
    ar_ref[0:1, 0:P] = afr
    ar_ref[0:1, P:2 * P] = abr
    ar_ref[1:2, 0:P] = afi
    ar_ref[1:2, P:2 * P] = abi


def _s5gen(lam_re, lam_im, log_step, b_re, b_im, c_re, c_im, d_skip):
    Gn, P, G, Lc = S5_GROUPS, S5_STATE, S5_GROUP_DIM, S5_CHUNK
    NS = 2 * Lc + 1
    ls = jnp.broadcast_to(log_step[:, :, None], (2, Gn, P))
    z = jnp.zeros((Gn, P), F32)
    rowp = jnp.stack([lam_re[0], lam_im[0], ls[0], lam_re[1], lam_im[1], ls[1], z, z], axis=1)
    colp = jnp.swapaxes(rowp, 1, 2)
    ctr = jnp.swapaxes(c_re, 1, 2)
    cti = jnp.swapaxes(c_im, 1, 2)
    btr = jnp.swapaxes(b_re, 1, 2)
    bti = jnp.swapaxes(b_im, 1, 2)
    blr, bli = b_re, b_im
    dt = d_skip.reshape(Gn, 1, G)
    g3 = lambda a, b: pl.BlockSpec((None, a, b), lambda g: (g, 0, 0))
    return pl.pallas_call(
        _s5gen_body,
        grid=(Gn,),
        in_specs=[g3(P, 8), g3(8, P), g3(P, G), g3(P, G), g3(G, P), g3(G, P), g3(P, G), g3(P, G), g3(1, G)],
        out_specs=[g3(Lc * G, Lc * G + 4 * P), g3(Lc * G, 4 * P), g3(8, 2 * P)],
        out_shape=[_sds((Gn, Lc * G, Lc * G + 4 * P), BF16), _sds((Gn, Lc * G, 4 * P), BF16),
                   _sds((Gn, 8, 2 * P), F32)],
        scratch_shapes=[pltpu.VMEM((Lc * G + 4 * P, Lc * G), F32), pltpu.VMEM((4 * P, Lc * G), F32)],
        compiler_params=_cp(("parallel",)),
        name="s5gen",
    )(colp, rowp, ctr, cti, btr, bti, blr, bli, dt)


def _s5_body(xt_ref, wt_ref, so_ref, ar_ref, y_ref, *scratch, B, nc, ncc, ng):
    P, Lc, G = S5_STATE, S5_CHUNK, S5_GROUP_DIM
    LG = Lc * G
    R = nc * B
    v_refs, h_refs = scratch[:ng], scratch[ng:]
    ucols = []
    for k in range(ng):
        ucol = jnp.concatenate([xt_ref[k * G:(k + 1) * G, l * R:(l + 1) * R] for l in range(Lc)], axis=0)
        v_refs[k][...] = lax.dot_general(ucol, so_ref[k], (((0,), (0,)), ((), ())), preferred_element_type=F32)
        ucols.append(ucol)
    lane = lax.broadcasted_iota(I32, (1, 2 * P), 1)
    isf = lane < P

    def step(i, carry):
        cf = pl.multiple_of(i * B, B)
        cb = pl.multiple_of(jnp.where(i < ncc, ncc - 1 - i, nc + ncc - 1 - i) * B, B)
        out = []
        for k in range(ng):
            hr, hi = carry[k]
            v_ref, h_ref = v_refs[k], h_refs[k]
            a_re, a_im = ar_ref[k, 0:1, :], ar_ref[k, 1:2, :]
            h_ref[pl.ds(cf, B), 0:P] = hr[:, 0:P]
            h_ref[pl.ds(cf, B), 2 * P:3 * P] = hi[:, 0:P]
            h_ref[pl.ds(cb, B), P:2 * P] = hr[:, P:2 * P]
            h_ref[pl.ds(cb, B), 3 * P:4 * P] = hi[:, P:2 * P]
            vr = jnp.where(isf, v_ref[pl.ds(cf, B), 0:2 * P], v_ref[pl.ds(cb, B), 0:2 * P])
            vi = jnp.where(isf, v_ref[pl.ds(cf, B), 2 * P:4 * P], v_ref[pl.ds(cb, B), 2 * P:4 * P])
            out.append((hr * a_re - hi * a_im + vr, hr * a_im + hi * a_re + vi))
        return tuple(out)

    z = jnp.zeros((B, 2 * P), F32)
    lax.fori_loop(0, nc, step, tuple((z, z) for _ in range(ng)))
    for k in range(ng):
        yt = jnp.dot(wt_ref[k, :, 0:LG], ucols[k], preferred_element_type=F32)
        yt = yt + lax.dot_general(wt_ref[k, :, LG:LG + 4 * P], h_refs[k][...].astype(BF16),
                                  (((1,), (1,)), ((), ())), preferred_element_type=F32)
        for t in range(Lc):
            y_ref[k * G:(k + 1) * G, t * R:(t + 1) * R] = yt[t * G:(t + 1) * G, :].astype(y_ref.dtype)


def _s5(xt, wt, so, ar, B, nc, ncc):
    P, Lc, G, Gn = S5_STATE, S5_CHUNK, S5_GROUP_DIM, S5_GROUPS
    N = xt.shape[1]
    R = nc * B
    ng = 2
    g3 = lambda a, b: pl.BlockSpec((ng, a, b), lambda g: (g, 0, 0))
    return pl.pallas_call(
        functools.partial(_s5_body, B=B, nc=nc, ncc=ncc, ng=ng),
        grid=(Gn // ng,),
        in_specs=[pl.BlockSpec((ng * G, N), lambda g: (g, 0)), g3(Lc * G, Lc * G + 4 * P), g3(Lc * G, 4 * P),
                  g3(8, 2 * P)],
        out_specs=pl.BlockSpec((ng * G, N), lambda g: (g, 0)),
        out_shape=_sds((Gn * G, N), BF16),
        scratch_shapes=[pltpu.VMEM((R, 4 * P), F32)] * (2 * ng),
        compiler_params=_cp(("parallel",)),
        name="s5",
    )(xt, wt, so, ar)


def _t2d_body(x_ref, o_ref):
    o_ref[...] = x_ref[...].astype(F32).T.astype(o_ref.dtype)


def _transpose2d(x, tr, name):
    M, C = x.shape
    return pl.pallas_call(
        _t2d_body,
        grid=(M // tr,),
        in_specs=[pl.BlockSpec((tr, C), lambda i: (i, 0))],
        out_specs=pl.BlockSpec((C, tr), lambda i: (0, i)),
        out_shape=_sds((C, M), x.dtype),
        compiler_params=_cp(("parallel",)),
        name=name,
    )(x)


def _untranspose2d(xt, tr, name):
    C, M = xt.shape
    return pl.pallas_call(
        _t2d_body,
        grid=(M // tr,),
        in_specs=[pl.BlockSpec((C, tr), lambda i: (0, i))],
        out_specs=pl.BlockSpec((tr, C), lambda i: (i, 0)),
        out_shape=_sds((M, C), xt.dtype),
        compiler_params=_cp(("parallel",)),
        name=name,
    )(xt)


def _merge_body(x_ref, ya_ref, yf_ref, yb_ref, xc_ref, z_ref, y5_ref, yd_ref, g_ref, gate_ref, sh_ref, sc_ref,
                wa_ref, wb_ref, wc_ref, wd_ref, wo_ref, wglu_ref, bglu_ref, ng_ref, dsk_ref, fg_ref, wrh_ref, wrl_ref,
                o_ref, hm_ref, lg_ref):
    tm, D = o_ref.shape
    nparts = 2 if tm % 32 == 0 else 1
    parts = [slice(p * tm // nparts, (p + 1) * tm // nparts) for p in range(nparts)]
    ys = [yf_ref[r, :].astype(F32) + yb_ref[r, :].astype(F32) + dsk_ref[...] * xc_ref[r, :].astype(F32) for r in parts]
    ys = [y * _silu(z_ref[r, :].astype(F32)) for y, r in zip(ys, parts)]
    ys = [(y * lax.rsqrt(jnp.mean(y * y, axis=-1, keepdims=True) + RMS_EPS) * ng_ref[...]).astype(BF16) for y in ys]
    cs = [jax.nn.gelu(y5_ref[r, :].astype(F32)).astype(BF16) for r in parts]
    glus = [_sigmoid(jnp.dot(c, wglu_ref[...], preferred_element_type=F32) + bglu_ref[...]) for c in cs]
    cs = [(c.astype(F32) * glu).astype(BF16) for c, glu in zip(cs, glus)]

    def gate(r, k):
        return 0.5 * jnp.tanh(g_ref[r, k * D:(k + 1) * D].astype(F32)) + 0.5

    ms = [gate(r, 0) * jnp.dot(ya_ref[r, :], wa_ref[...], preferred_element_type=F32) for r in parts]
    ms = [m + gate(r, 1) * jnp.dot(y, wb_ref[...], preferred_element_type=F32) for m, y, r in zip(ms, ys, parts)]
    ms = [m + gate(r, 2) * jnp.dot(c, wc_ref[...], preferred_element_type=F32) for m, c, r in zip(ms, cs, parts)]
    ms = [m + gate(r, 3) * jnp.dot(yd_ref[r, :], wd_ref[...], preferred_element_type=F32) for m, r in zip(ms, parts)]
    outs = [jnp.dot(m.astype(BF16), wo_ref[...], preferred_element_type=F32) for m in ms]
    xs = [x_ref[r, :] + gate_ref[...] * out for out, r in zip(outs, parts)]
    scale = fg_ref[...] * (1.0 + sc_ref[...])
    hs = [x * lax.rsqrt(jnp.mean(x * x, axis=-1, keepdims=True) + RMS_EPS) * scale + sh_ref[...] for x in xs]
    his = [h.astype(BF16) for h in hs]
    los = [(h - hi.astype(F32)).astype(BF16) for h, hi in zip(hs, his)]
    lgs = [jnp.dot(hi, wrh_ref[...], preferred_element_type=F32) + jnp.dot(lo, wrh_ref[...], preferred_element_type=F32)
           + jnp.dot(hi, wrl_ref[...], preferred_element_type=F32) for hi, lo in zip(his, los)]
    for x, hi, lg, r in zip(xs, his, lgs, parts):
        o_ref[r, :] = x
        hm_ref[r, :] = hi
        lg_ref[r, :] = lg


def _merge(xa, ya, yf, yb, xbc_c, proj, y5, yd, mods3, wa, wb, wc, wd, wo, wglu, bglu, ng, dsk, fg, wr, B, rc):
    N, D = xa.shape
    tm = rc[0]
    rows = lambda w, j=0: pl.BlockSpec((tm, w), lambda i: (i, j))
    full = lambda a: pl.BlockSpec(a.shape, lambda i: (0,) * a.ndim)
    wr_hi = wr.astype(BF16)
    wr_lo = (wr - wr_hi.astype(F32)).astype(BF16)
    return pl.pallas_call(
        _merge_body,
        grid=(N // tm,),
        in_specs=[rows(D), rows(ATTN_Q_DIM), rows(SSD_INNER), rows(SSD_INNER), rows(SSD_INNER),
                  rows(SSD_INNER, PA_Z // SSD_INNER), rows(S5_WIDTH), rows(SC_WIDTH), rows(4 * D, PA_G // (4 * D)),
                  _mod_spec(D, rc, B, 2), _mod_spec(D, rc, B, 3), _mod_spec(D, rc, B, 4),
                  full(wa), full(wb), full(wc), full(wd), full(wo), full(wglu), full(bglu), full(ng), full(dsk),
                  full(fg), full(wr_hi), full(wr_lo)],
        out_specs=[rows(D), rows(D), rows(128)],
        out_shape=[_sds((N, D), F32), _sds((N, D), BF16), _sds((N, 128), F32)],
        compiler_params=_cp(("parallel",), 48),
        name="merge",
    )(xa, ya, yf, yb, xbc_c, proj, y5, yd, proj, mods3, mods3, mods3, wa, wb, wc, wd, wo, wglu, bglu, ng, dsk,
      fg, wr_hi, wr_lo)


def _prefix_excl(mask_f):
    R, T = mask_f.shape
    r = lax.broadcasted_iota(I32, (128, 128), 0)
    c = lax.broadcasted_iota(I32, (128, 128), 1)
    upper = (r <= c).astype(BF16)
    outs = []
    off = jnp.zeros((R, 1), F32)
    for k in range(T // 128):
        blk = mask_f[:, k * 128:(k + 1) * 128]
        inc = jnp.dot(blk.astype(BF16), upper, preferred_element_type=F32)
        outs.append(inc - blk + off)
        off = off + inc[:, 127:128]
    return jnp.concatenate(outs, axis=1)


def _topk_slots(affs, caps):
    E = affs[0].shape[0]

    def step(i, ths):
        out = []
        for a, cap, th in zip(affs, caps, ths):
            cand = th | (jnp.int32(1) << (30 - i))
            cnt = jnp.sum((a >= pltpu.bitcast(cand, F32)).astype(I32), axis=1, keepdims=True)
            out.append(jnp.where(cnt >= cap, cand, th))
        return tuple(out)

    ths = lax.fori_loop(0, 31, step, tuple(jnp.zeros((E, 1), I32) for _ in affs))
    slots = []
    for a, cap, th in zip(affs, caps, ths):
        above = a >= pltpu.bitcast(jnp.maximum(th + 1, 0x00800000), F32)
        tie = (a >= pltpu.bitcast(th, F32)) & jnp.logical_not(above)
        n_above = jnp.sum(above.astype(F32), axis=1, keepdims=True)
        sel = above | (tie & (n_above + _prefix_excl(tie.astype(F32)) < cap))
        slots.append(jnp.where(sel, _prefix_excl(sel.astype(F32)), -1.0))
    return slots


def _router_body(lg_ref, slot_ref, gate_ref, rng_ref, *, Sc, cap_l, cap_c, ctx_out, tb):
    S = lg_ref.shape[0]
    lt = lg_ref[...].T[0:N_EXPERTS, :]
    mx = jnp.max(lt, axis=0, keepdims=True)
    ex = jnp.exp(lt - mx)
    aff = ex / jnp.sum(ex, axis=0, keepdims=True)
    gate_ref[...] = aff
    if ctx_out:
        slot_l, slot_c = _topk_slots([aff[:, Sc:], aff[:, 0:Sc]], [cap_l, cap_c])
        slot_c = jnp.where(slot_c >= 0.0, slot_c + cap_l, -1.0)
    else:
        slot_l, = _topk_slots([aff[:, Sc:]], [cap_l])
        slot_c = jnp.full((N_EXPERTS, Sc), -1.0, F32)
    slot_ref[:, 0:Sc] = slot_c
    slot_ref[:, Sc:] = slot_l
    lane = lax.broadcasted_iota(I32, (1, 128), 1)
    lo_a = jnp.zeros((N_EXPERTS, 128), F32)
    nwin = jnp.zeros((N_EXPERTS, 1), F32)
    for seg0, seg1, base, slots in ((0, Sc, float(cap_l), slot_c), (Sc, S, 0.0, slot_l)):
        lo = jnp.full((N_EXPERTS, 1), base, F32)
        for k in range(seg0 // tb, seg1 // tb):
            blk = slots[:, k * tb - seg0:(k + 1) * tb - seg0]
            hi = lo + jnp.sum((blk >= 0.0).astype(F32), axis=1, keepdims=True)
            lo_al = jnp.floor(lo * (1.0 / SLOT_ALIGN)) * SLOT_ALIGN
            need = jnp.where(hi > lo, jnp.floor((hi - lo_al + (SLOT_WINDOW - 1)) * (1.0 / SLOT_WINDOW)), 0.0)
            nwin = jnp.maximum(nwin, need)
            lo_a = jnp.where(lane == k, lo, lo_a)
            lo = hi
    rng_ref[...] = jnp.where(lane == 127, jnp.max(nwin, axis=0, keepdims=True), lo_a).astype(I32)


def _token_block(S, Sc):
    return math.gcd(math.gcd(Sc, S - Sc), 256)


def _router(logits, B, S, Sc, ctx_out):
    T = S - Sc
    cap_l = EC_CAPACITY * T // N_EXPERTS
    cap_c = EC_CAPACITY * Sc // N_EXPERTS
    tb = _token_block(S, Sc)
    es = pl.BlockSpec((None, N_EXPERTS, S), lambda b_: (b_, 0, 0))
    return pl.pallas_call(
        functools.partial(_router_body, Sc=Sc, cap_l=cap_l, cap_c=cap_c, ctx_out=ctx_out, tb=tb),
        grid=(B,),
        in_specs=[pl.BlockSpec((S, 128), lambda b_: (b_, 0))],
        out_specs=[es, es, pl.BlockSpec((None, N_EXPERTS, 128), lambda b_: (b_, 0, 0))],
        out_shape=[_sds((B, N_EXPERTS, S), F32), _sds((B, N_EXPERTS, S), F32), _sds((B, N_EXPERTS, 128), I32)],
        compiler_params=_cp(("parallel",)),
        name="router",
    )(logits)


def _window_rows(rng_ref, k, j, ncap):
    rows = []
    for e in range(N_EXPERTS):
        lo = rng_ref[e, k]
        lo_al = lax.shift_left(lax.shift_right_logical(lo, SLOT_ALIGN.bit_length() - 1), SLOT_ALIGN.bit_length() - 1)
        rows.append(pl.multiple_of(jnp.minimum(lo_al + j * SLOT_WINDOW, ncap), SLOT_ALIGN))
    return rows


def _window_hits(rows, slot_ref, cols):
    wrow = lax.broadcasted_iota(I32, (SLOT_WINDOW, 1), 0)
    return [(wrow + rows[e]).astype(F32) == slot_ref[e:e + 1, cols] for e in range(N_EXPERTS)]


def _moe_gather_body(rng_ref, hm_ref, slot_ref, gate_ref, xg_ref, gv_ref, *, ncap, tb, nsub):
    kk = pl.program_id(1)
    W = SLOT_WINDOW

    @pl.when(kk == 0)
    def _():
        xg_ref[...] = jnp.zeros_like(xg_ref)
        gv_ref[...] = jnp.zeros_like(gv_ref)

    def make_window(sub):
        k = kk * nsub + sub
        cols = slice(sub * tb, (sub + 1) * tb)

        def window(j, carry):
            rows = _window_rows(rng_ref, k, j, ncap)
            hits = _window_hits(rows, slot_ref, cols)
            hit_all = jnp.concatenate([h.astype(BF16) for h in hits], axis=0)
            xgw = jnp.dot(hit_all, hm_ref[cols, :], preferred_element_type=F32).astype(BF16)
            for e in range(N_EXPERTS):
                r = pl.ds(rows[e], W)
                xg_ref[e, r, :] += xgw[e * W:(e + 1) * W, :]
                gv_ref[e, r, :] += jnp.sum(jnp.where(hits[e], gate_ref[e:e + 1, cols], 0.0), axis=1, keepdims=True)
            return carry

        return window

    windows = [make_window(sub) for sub in range(nsub)]
    for window in windows:
        window(0, 0)
    for window in windows:
        lax.fori_loop(1, rng_ref[0, 127], window, 0)


def _moe_ffn_body(xg_ref, gv_ref, wg_ref, wu_ref, wd_ref, yw_ref, wgs_ref, wus_ref, wds_ref, *, ncap):
    @pl.when(pl.program_id(1) == 0)
    def _():
        wgs_ref[...] = wg_ref[...].astype(BF16)
        wus_ref[...] = wu_ref[...].astype(BF16)
        wds_ref[...] = wd_ref[...].astype(BF16)

    nbp = xg_ref.shape[0]
    for p0 in range(0, nbp, 2):
        ps = range(p0, min(p0 + 2, nbp))
        xg = jnp.concatenate([xg_ref[p, 0:ncap, :] for p in ps], axis=0)
        gv = jnp.concatenate([gv_ref[p, 0:ncap, :] for p in ps], axis=0)
        hid = _silu(jnp.dot(xg, wgs_ref[...], preferred_element_type=F32)) * jnp.dot(
            xg, wus_ref[...], preferred_element_type=F32)
        ye = (jnp.dot(hid.astype(BF16), wds_ref[...], preferred_element_type=F32) * gv).astype(yw_ref.dtype)
        for i, p in enumerate(ps):
            yw_ref[p, 0:ncap, :] = ye[i * ncap:(i + 1) * ncap, :]
            yw_ref[p, ncap:, :] = jnp.zeros((yw_ref.shape[1] - ncap, yw_ref.shape[2]), yw_ref.dtype)


def _moe_scatter_body(rng_ref, slot_ref, yw_ref, x_ref, gl_ref, gc_ref, fg_ref, o_ref,
                      *, ncap, nbc, final, tb, nsub):
    kk = pl.program_id(1)
    W = SLOT_WINDOW
    D = o_ref.shape[1]

    def make_window(sub):
        k = kk * nsub + sub
        cols = slice(sub * tb, (sub + 1) * tb)

        def window(j, acc):
            rows = _window_rows(rng_ref, k, j, ncap)
            hit_all = jnp.concatenate([h.astype(BF16) for h in _window_hits(rows, slot_ref, cols)], axis=0)
            yw = jnp.concatenate([yw_ref[e, pl.ds(rows[e], W), :] for e in range(N_EXPERTS)], axis=0)
            return acc + lax.dot_general(hit_all, yw, (((0,), (0,)), ((), ())), preferred_element_type=F32)

        return window

    def emit(sub, window, acc):
        k = kk * nsub + sub
        cols = slice(sub * tb, (sub + 1) * tb)
        acc = lax.fori_loop(1, rng_ref[0, 127], window, acc)
        x = x_ref[cols, :] + jnp.where(k < nbc, gc_ref[...], gl_ref[...]) * acc
        if final:
            x = x * lax.rsqrt(jnp.mean(x * x, axis=-1, keepdims=True) + RMS_EPS) * fg_ref[...]
        o_ref[cols, :] = x

    windows = [make_window(sub) for sub in range(nsub)]
    zero = jnp.zeros((tb, D), F32)
    if final:
        pl.when(kk >= nbc)(lambda: emit(0, windows[0], windows[0](0, zero)))
    else:
        accs = [window(0, zero) for window in windows]
        for sub in range(nsub):
            emit(sub, windows[sub], accs[sub])


def _experts(hm, slot, gate, rng, wg, wu, wd, li, xres, mods3, B, S, Sc, ctx_out, final_g):
    N, D = hm.shape
    _, E, _, Fd = wg.shape
    T = S - Sc
    tb = _token_block(S, Sc)
    nblk, nbc = S // tb, Sc // tb
    ncap = EC_CAPACITY * T // N_EXPERTS + (EC_CAPACITY * Sc // N_EXPERTS if ctx_out else 0)
    assert ncap % SLOT_ALIGN == 0
    rows = ncap + SLOT_WINDOW
    smem = pl.BlockSpec((None, E, 128), lambda b_, k: (b_, 0, 0), memory_space=pltpu.SMEM)
    nsub = 3 if nblk % 3 == 0 else 1
    es = pl.BlockSpec((None, E, nsub * tb), lambda b_, k: (b_, 0, k))
    xg, gv = pl.pallas_call(
        functools.partial(_moe_gather_body, ncap=ncap, tb=tb, nsub=nsub),
        grid=(B, nblk // nsub),
        in_specs=[smem, pl.BlockSpec((nsub * tb, D), lambda b_, k: (b_ * (nblk // nsub) + k, 0)), es, es],
        out_specs=[pl.BlockSpec((None, E, rows, D), lambda b_, k: (b_, 0, 0, 0)),
                   pl.BlockSpec((None, E, rows, 1), lambda b_, k: (b_, 0, 0, 0))],
        out_shape=[_sds((B, E, rows, D), BF16), _sds((B, E, rows, 1), F32)],
        compiler_params=_cp(("parallel", "arbitrary"), 48),
        name="moe_gather",
    )(rng, hm, slot, gate)
    wspec = lambda a, b: pl.BlockSpec((None, None, a, b), lambda e, b_: (li, e, 0, 0))
    nbp = math.gcd(B, 4)
    yw = pl.pallas_call(
        functools.partial(_moe_ffn_body, ncap=ncap),
        grid=(E, B // nbp),
        in_specs=[pl.BlockSpec((nbp, None, rows, D), lambda e, b_: (b_, e, 0, 0)),
                  pl.BlockSpec((nbp, None, rows, 1), lambda e, b_: (b_, e, 0, 0)),
                  wspec(D, Fd), wspec(D, Fd), wspec(Fd, D)],
        out_specs=pl.BlockSpec((nbp, None, rows, D), lambda e, b_: (b_, e, 0, 0)),
        out_shape=_sds((B, E, rows, D), BF16),
        scratch_shapes=[pltpu.VMEM((D, Fd), BF16), pltpu.VMEM((D, Fd), BF16), pltpu.VMEM((Fd, D), BF16)],
        compiler_params=_cp(("parallel", "arbitrary"), 48),
        name="moe_ffn",
    )(xg, gv, wg, wu, wd)
    final = final_g is not None
    nl = nblk - nbc
    if final:
        nsub = 1
        out_spec = pl.BlockSpec((tb, D), lambda b_, k: (b_ * nl + jnp.maximum(k - nbc, 0), 0))
        out_shape = _sds((B * nl * tb, D), F32)
    else:
        out_spec = pl.BlockSpec((nsub * tb, D), lambda b_, k: (b_ * (nblk // nsub) + k, 0))
        out_shape = _sds((N, D), F32)
        final_g = jnp.ones((1, D), F32)
    nstep = nblk // nsub
    es = pl.BlockSpec((None, E, nsub * tb), lambda b_, k: (b_, 0, k))
    return pl.pallas_call(
        functools.partial(_moe_scatter_body, ncap=ncap, nbc=nbc, final=final, tb=tb, nsub=nsub),
        grid=(B, nstep),
        in_specs=[smem, es, pl.BlockSpec((None, E, rows, D), lambda b_, k: (b_, 0, 0, 0)),
                  pl.BlockSpec((nsub * tb, D), lambda b_, k: (b_ * nstep + k, 0)),
                  pl.BlockSpec((None, 1, D), lambda b_, k: (b_, 0, 5)),
                  pl.BlockSpec((None, 1, D), lambda b_, k: (B, 0, 5)),
                  pl.BlockSpec((1, D), lambda b_, k: (0, 0))],
        out_specs=out_spec,
        out_shape=out_shape,
        compiler_params=_cp(("parallel", "arbitrary"), 48),
        name="moe_scatter",
    )(rng, slot, yw, xres, mods3, mods3, final_g)


def _pack_w_in(w_in):
    D = w_in.shape[0]
    seg = lambda a, n: w_in[:, a:a + n]
    o_q, o_k, o_v, o_z = 0, ATTN_Q_DIM, ATTN_Q_DIM + ATTN_KV_DIM, ATTN_Q_DIM + 2 * ATTN_KV_DIM
    o_xbc = o_z + SSD_INNER
    o_dt = o_xbc + SSD_CONV_DIM
    o_u = o_dt + SSD_HEADS
    o_sb, o_sg, o_sh = o_u + S5_WIDTH, o_u + S5_WIDTH + SC_WIDTH, o_u + S5_WIDTH + 2 * SC_WIDTH
    o_g = o_sh + SC_WIDTH
    w_p = jnp.concatenate([seg(o_g, 4 * D), seg(o_q, ATTN_Q_DIM), seg(o_z, SSD_INNER), seg(o_k, ATTN_KV_DIM),
                           seg(o_v, ATTN_KV_DIM), seg(o_u, S5_WIDTH), seg(o_sb, SC_WIDTH), seg(o_sg, SC_WIDTH),
                           seg(o_sh, SC_WIDTH), seg(o_xbc, SSD_CONV_DIM)], axis=1).astype(BF16)
    w_dt = jnp.pad(seg(o_dt, SSD_HEADS), ((0, 0), (0, 128 - SSD_HEADS))).astype(BF16)
    return w_p, w_dt


def _layer(xa, mods3, lp, ew, li, tabs, B, S, Sc, ctx_out, final_g):
    N, D = xa.shape
    rc = _row_cfg(S, Sc)
    w_p, w_dt = _pack_w_in(lp["w_in"])
    proj_a, dt = _in_proj(xa, lp["norm_mix_g"][None, :], mods3, w_p, w_dt, B, S, Sc)

    ya = _attention(proj_a, tabs, lp["q_norm_g"], lp["k_norm_g"], B, S, Sc)

    xbc_c = _ssdconv(proj_a, lp["ssd_conv_w"], lp["ssd_conv_b"][None, :], B, S, Sc)
    yf, yb = _ssd(xbc_c, dt, lp["ssd_dt_bias"], lp["ssd_a_log"], B, S, Sc)

    Lc, Gn, G = S5_CHUNK, S5_GROUPS, S5_GROUP_DIM
    nc5, ncc5 = S // Lc, Sc // Lc
    w5, so5, ar5 = _s5gen(lp["s5_lambda_re"], lp["s5_lambda_im"], lp["s5_log_step"], lp["s5_b_re"], lp["s5_b_im"],
                          lp["s5_c_re"], lp["s5_c_im"], lp["s5_d"])
    u = proj_a.reshape(B, nc5, Lc, PA_W)[:, :, :, PA_U:PA_U + S5_WIDTH]
    up = jnp.transpose(u, (2, 1, 0, 3)).reshape(N, S5_WIDTH)
    tr5 = nc5 * B
    y5t = _s5(_transpose2d(up, tr5, "s5_in_t"), w5, so5, ar5, B, nc5, ncc5)
    y5p = _untranspose2d(y5t, tr5, "s5_out_t").reshape(Lc, nc5, B, S5_WIDTH)
    y5 = jnp.transpose(y5p, (2, 1, 0, 3)).reshape(N, S5_WIDTH)

    yd = _shortconv(proj_a, lp["sc_conv_w"], B, S, Sc)

    dsk = jnp.repeat(lp["ssd_d"], SSD_HEAD_DIM)[None, :]
    wr_pad = jnp.pad(lp["w_router"], ((0, 0), (0, 128 - N_EXPERTS)))
    x1, hm, logits = _merge(xa, ya, yf, yb, xbc_c, proj_a, y5, yd, mods3,
                            lp["w_br_attn"].astype(BF16), lp["w_br_ssd"].astype(BF16), lp["w_br_s5"].astype(BF16),
                            lp["w_br_sc"].astype(BF16), lp["w_out"].astype(BF16), lp["s5_w_glu"].astype(BF16),
                            lp["s5_b_glu"][None, :], lp["ssd_norm_g"][None, :], dsk,
                            lp["norm_ffn_g"][None, :], wr_pad, B, rc)
    slot, gate, rng = _router(logits, B, S, Sc, ctx_out)
    return _experts(hm, slot, gate, rng, ew[0], ew[1], ew[2], li, x1, mods3, B, S, Sc, ctx_out, final_g)


def kernel(x, c, ctx, c_ctx, w_mod, b_mod, norm_mix_g, norm_ffn_g, w_in, q_norm_g, k_norm_g, ssd_conv_w, ssd_conv_b, ssd_dt_bias, ssd_a_log, ssd_d, ssd_norm_g, s5_lambda_re, s5_lambda_im, s5_log_step, s5_b_re, s5_b_im, s5_c_re, s5_c_im, s5_d, s5_w_glu, s5_b_glu, sc_conv_w, w_br_attn, w_br_ssd, w_br_s5, w_br_sc, w_out, w_router, w_exp_gate, w_exp_up, w_exp_down, final_norm_g):
    B, T, D = x.shape
    Sc = ctx.shape[1]
    S = Sc + T
    depth = w_in.shape[0]
    xa = jnp.concatenate([ctx, x], axis=1).reshape(B * S, D)
    cc = jnp.zeros((16, D), F32).at[0:B].set(c).at[B].set(c_ctx)
    cos, sin = _rope_tables(T, Sc)
    tabs = (jnp.tile(cos, (1, ATTN_HEADS)), jnp.tile(sin, (1, ATTN_HEADS)),
            jnp.tile(cos, (1, ATTN_KV_HEADS)), jnp.tile(sin, (1, ATTN_KV_HEADS)),
            _block_diag_ones(ATTN_Q_DIM), _block_diag_ones(ATTN_KV_DIM))
    stacked = dict(
        w_in=w_in, norm_mix_g=norm_mix_g, norm_ffn_g=norm_ffn_g, q_norm_g=q_norm_g, k_norm_g=k_norm_g,
        ssd_conv_w=ssd_conv_w, ssd_conv_b=ssd_conv_b, ssd_dt_bias=ssd_dt_bias, ssd_a_log=ssd_a_log, ssd_d=ssd_d,
        ssd_norm_g=ssd_norm_g, s5_lambda_re=s5_lambda_re, s5_lambda_im=s5_lambda_im, s5_log_step=s5_log_step,
        s5_b_re=s5_b_re, s5_b_im=s5_b_im, s5_c_re=s5_c_re, s5_c_im=s5_c_im, s5_d=s5_d, s5_w_glu=s5_w_glu,
        s5_b_glu=s5_b_glu, sc_conv_w=sc_conv_w, w_br_attn=w_br_attn, w_br_ssd=w_br_ssd, w_br_s5=w_br_s5,
        w_br_sc=w_br_sc, w_out=w_out, w_router=w_router)
    ew = (w_exp_gate, w_exp_up, w_exp_down)
    mods_all = _mods(cc, w_mod, b_mod[:, None, :])
    for i in range(depth):
        lp = {k: v[i] for k, v in stacked.items()}
        mods3 = mods_all[i].reshape(16, 1, 6 * D)
        last = i == depth - 1
        xa = _layer(xa, mods3, lp, ew, i, tabs, B, S, Sc, ctx_out=not last,
                    final_g=final_norm_g[None, :] if last else None)
    return xa.reshape(B, T, D)
```
